```python
import math
import jax, jax.numpy as jnp
from jax import lax
import numpy as np

D_MODEL = 1024
BATCH = 8
SEQ = 4096
DEPTH = 4

NORM_EPS = 1e-6
DN_HEAD_DIM = 128
DN_WIDTH = D_MODEL // 2
DN_HEADS = DN_WIDTH // DN_HEAD_DIM
DN_CHUNK = 64
DN_CONV = 4
GM_GROUP_DIM = 64
GM_WIDTH = D_MODEL // 4
GM_GROUPS = GM_WIDTH // GM_GROUP_DIM
GM_CHUNK = 128
SW_HEAD_DIM = 64
SW_WIDTH = D_MODEL // 4
SW_HEADS = SW_WIDTH // SW_HEAD_DIM
SW_PATTERNS = ((128, 1), (512, 4), (2048, 16))
SW_BLOCK = 128
ROPE_THETA = 500000.0
ROPE_DIM = SW_HEAD_DIM // 4
MIX_WIDTH = DN_WIDTH + GM_WIDTH + SW_WIDTH
IN_SPLITS = (3 * DN_WIDTH, 4 * DN_WIDTH, 4 * DN_WIDTH + DN_HEADS, 4 * DN_WIDTH + 2 * DN_HEADS,
             4 * DN_WIDTH + 2 * DN_HEADS + 2 * GM_WIDTH)
IN_WIDTH = IN_SPLITS[-1] + len(SW_PATTERNS) * 3 * SW_WIDTH
FFN_HIDDEN = -(-8 * D_MODEL // (3 * 256)) * 256

kernel_name = "hybrid_parallel_heads_decoder"


def rms_norm(x, w):
    xf = x.astype(jnp.float32)
    y = xf * lax.rsqrt(jnp.mean(xf * xf, axis=-1, keepdims=True) + NORM_EPS)
    return (y * w.astype(jnp.float32)).astype(x.dtype)


def layer_norm(x, g, b):
    xf = x.astype(jnp.float32)
    mu = jnp.mean(xf, axis=-1, keepdims=True)
    xc = xf - mu
    var = jnp.mean(xc * xc, axis=-1, keepdims=True)
    return (xc * lax.rsqrt(var + NORM_EPS) * g.astype(jnp.float32) + b.astype(jnp.float32)).astype(x.dtype)


def l2_norm(x):
    return x * lax.rsqrt(jnp.sum(x * x, axis=-1, keepdims=True) + NORM_EPS)


def rotary_tables(seq):
    inv = ROPE_THETA ** (-jnp.arange(0, ROPE_DIM, 2, dtype=jnp.float32) / ROPE_DIM)
    ang = jnp.arange(seq, dtype=jnp.float32)[:, None] * inv[None, :]
    return jnp.cos(ang), jnp.sin(ang)


def apply_partial_rotary(x, cos, sin):
    half = ROPE_DIM // 2
    x1, x2, xp = x[..., :half], x[..., half:ROPE_DIM], x[..., ROPE_DIM:]
    c = cos[None, :, None, :]
    s = sin[None, :, None, :]
    return jnp.concatenate([x1 * c - x2 * s, x2 * c + x1 * s, xp], axis=-1)


def causal_dwconv_silu(x, w):
    k, ch = w.shape
    y = lax.conv_general_dilated(x, w[:, None, :].astype(x.dtype), window_strides=(1,),
                                 padding=[(k - 1, 0)], dimension_numbers=('NWC', 'WIO', 'NWC'),
                                 feature_group_count=ch)
    return jax.nn.silu(y)


def gated_delta_rule(q, k, v, g, beta):
    bsz, h, t, dk = q.shape
    dv = v.shape[-1]
    n = t // DN_CHUNK
    q = q.reshape(bsz, h, n, DN_CHUNK, dk) * (dk ** -0.5)
    k = k.reshape(bsz, h, n, DN_CHUNK, dk)
    v = v.reshape(bsz, h, n, DN_CHUNK, dv)
    beta = beta.reshape(bsz, h, n, DN_CHUNK)
    gcum = jnp.cumsum(g.reshape(bsz, h, n, DN_CHUNK), axis=-1)
    idx = jnp.arange(DN_CHUNK)
    causal = idx[:, None] >= idx[None, :]
    strict = idx[:, None] > idx[None, :]
    decay = jnp.exp(jnp.where(causal, gcum[..., :, None] - gcum[..., None, :], -jnp.inf))
    kb = k * beta[..., None]
    a_low = jnp.where(strict, jnp.einsum('bhnid,bhnjd->bhnij', kb, k) * decay, 0.0)
    eye = jnp.eye(DN_CHUNK, dtype=q.dtype)
    rhs = jnp.concatenate([v * beta[..., None], kb * jnp.exp(gcum)[..., None]], axis=-1)
    sol = lax.linalg.triangular_solve(a_low + eye, rhs, left_side=True, lower=True)
    u, w = sol[..., :dv], sol[..., dv:]
    a_qk = jnp.einsum('bhnid,bhnjd->bhnij', q, k) * decay
    q_dec = q * jnp.exp(gcum)[..., None]
    g_last = gcum[..., -1]
    k_dec = k * jnp.exp(g_last[..., None] - gcum)[..., None]

    def step(state, xs):
        u_c, w_c, qd_c, a_c, kd_c, gl_c = xs
        v_new = u_c - jnp.einsum('bhcd,bhde->bhce', w_c, state)
        o = jnp.einsum('bhcd,bhde->bhce', qd_c, state) + jnp.einsum('bhij,bhje->bhie', a_c, v_new)
        state = state * jnp.exp(gl_c)[..., None, None] + jnp.einsum('bhcd,bhce->bhde', kd_c, v_new)
        return state, o

    xs = (jnp.moveaxis(u, 2, 0), jnp.moveaxis(w, 2, 0), jnp.moveaxis(q_dec, 2, 0),
          jnp.moveaxis(a_qk, 2, 0), jnp.moveaxis(k_dec, 2, 0), jnp.moveaxis(g_last, 2, 0))
    s0 = jnp.zeros((bsz, h, dk, dv), q.dtype)
    _, o = lax.scan(step, s0, xs)
    return jnp.moveaxis(o, 0, 2).reshape(bsz, h, t, dv)


def deltanet_mixer(qkv, z, a, b, conv_w, a_log, dt_bias, out_norm_w):
    bsz, t, _ = qkv.shape
    qkv = causal_dwconv_silu(qkv, conv_w).astype(jnp.float32)
    q, k, v = jnp.split(qkv, 3, axis=-1)
    heads = lambda y: y.reshape(bsz, t, DN_HEADS, DN_HEAD_DIM).transpose(0, 2, 1, 3)
    q, k, v = l2_norm(heads(q)), l2_norm(heads(k)), heads(v)
    g = -jnp.exp(a_log.astype(jnp.float32)) * jax.nn.softplus(a.astype(jnp.float32) + dt_bias.astype(jnp.float32))
    beta = jax.nn.sigmoid(b.astype(jnp.float32))
    o = gated_delta_rule(q, k, v, g.transpose(0, 2, 1), beta.transpose(0, 2, 1))
    o = o.transpose(0, 2, 1, 3)
    zg = z.astype(jnp.float32).reshape(bsz, t, DN_HEADS, DN_HEAD_DIM)
    o = rms_norm(o, out_norm_w) * jax.nn.silu(zg)
    return o.reshape(bsz, t, DN_WIDTH)


def spatial_gating_mixer(uv, ln_g, ln_b, w_s, b_s):
    bsz, t, _ = uv.shape
    zz = jax.nn.gelu(uv.astype(jnp.float32), approximate=False)
    u, v = jnp.split(zz, 2, axis=-1)
    v = layer_norm(v, ln_g, ln_b)
    n = t // GM_CHUNK
    v = v.reshape(bsz, n, GM_CHUNK, GM_GROUPS, GM_GROUP_DIM)
    causal = jnp.tril(jnp.ones((GM_CHUNK, GM_CHUNK), dtype=bool))
    ws = jnp.where(causal, w_s.astype(jnp.float32), 0.0)
    sv = jnp.einsum('gij,bnjgc->bnigc', ws, v) + b_s.astype(jnp.float32).T[None, None, :, :, None]
    return u * sv.reshape(bsz, t, GM_WIDTH)


def dilated_window_attention(q, k, v, dilation, span):
    bsz, t, h, d = q.shape
    length = t // dilation
    nb = -(-length // SW_BLOCK)
    lp = nb * SW_BLOCK

    def to_sub(y):
        y = y.reshape(bsz, length, dilation, h, d).transpose(0, 2, 3, 1, 4)
        y = jnp.pad(y, ((0, 0), (0, 0), (0, 0), (0, lp - length), (0, 0)))
        return y.reshape(bsz, dilation, h, nb, SW_BLOCK, d)

    def with_prev(y):
        prev = jnp.pad(y, ((0, 0), (0, 0), (0, 0), (1, 0), (0, 0), (0, 0)))[:, :, :, :-1]
        return jnp.concatenate([prev, y], axis=-2)

    qs = to_sub(q)
    kk, vv = with_prev(to_sub(k)), with_prev(to_sub(v))
    s = jnp.einsum('brhnie,brhnje->brhnij', qs, kk) * (d ** -0.5)
    blk = jnp.arange(nb)[:, None, None] * SW_BLOCK
    qpos = blk + jnp.arange(SW_BLOCK)[None, :, None]
    kpos = blk - SW_BLOCK + jnp.arange(2 * SW_BLOCK)[None, None, :]
    dist = qpos - kpos
    valid = (dist >= 0) & (dist <= span) & (kpos >= 0)
    s = jnp.where(valid, s, -jnp.inf)
    m = jnp.max(s, axis=-1, keepdims=True)
    p = jnp.exp(s - m)
    l = jnp.sum(p, axis=-1, keepdims=True)
    o = jnp.einsum('brhnij,brhnje->brhnie', p, vv) / l
    lse = (m + jnp.log(l))[..., 0]
    o = o.reshape(bsz, dilation, h, lp, d)[:, :, :, :length]
    o = o.transpose(0, 3, 1, 2, 4).reshape(bsz, t, h, d)
    lse = lse.reshape(bsz, dilation, h, lp)[:, :, :, :length]
    lse = lse.transpose(0, 3, 1, 2).reshape(bsz, t, h)
    return o, lse


def dilated_attention_mixer(qkv, q_norm_w, k_norm_w, cos, sin):
    bsz, t, _ = qkv.shape
    parts = qkv.astype(jnp.float32).reshape(bsz, t, len(SW_PATTERNS), 3, SW_HEADS, SW_HEAD_DIM)
    outs, lses = [], []
    for gi, (window, dilation) in enumerate(SW_PATTERNS):
        q = apply_partial_rotary(rms_norm(parts[:, :, gi, 0], q_norm_w), cos, sin)
        k = apply_partial_rotary(rms_norm(parts[:, :, gi, 1], k_norm_w), cos, sin)
        o, lse = dilated_window_attention(q, k, parts[:, :, gi, 2], dilation, window // dilation)
        outs.append(o)
        lses.append(lse)
    o = jnp.stack(outs, axis=0)
    wts = jax.nn.softmax(jnp.stack(lses, axis=0), axis=0)
    return jnp.sum(wts[..., None] * o, axis=0).reshape(bsz, t, SW_WIDTH)


def _fwd_setup_inputs(seed: int = 0) -> dict:
    key = jax.random.key(seed)
    ks = jax.random.split(key, 20)
    f32 = jnp.float32
    nl = DEPTH

    def nrm(k, shape, scale):
        return jax.random.normal(k, shape, f32) * scale

    dt = jnp.exp(jax.random.uniform(ks[10], (nl, DN_HEADS), f32, math.log(1e-3), math.log(1e-1)))
    return {
        'x': nrm(ks[0], (BATCH, SEQ, D_MODEL), 1.0),
        'c': nrm(ks[1], (BATCH, D_MODEL), 1.0),
        'w_mod': nrm(ks[2], (nl, D_MODEL, 6 * D_MODEL), 0.5 * D_MODEL ** -0.5),
        'b_mod': nrm(ks[3], (nl, 6 * D_MODEL), 0.01),
        'mix_norm_w': 1.0 + nrm(ks[4], (nl, D_MODEL), 0.02),
        'ffn_norm_w': 1.0 + nrm(ks[5], (nl, D_MODEL), 0.02),
        'w_in': nrm(ks[6], (nl, D_MODEL, IN_WIDTH), D_MODEL ** -0.5),
        'w_out': nrm(ks[7], (nl, MIX_WIDTH, D_MODEL), MIX_WIDTH ** -0.5),
        'dn_conv_w': nrm(ks[8], (nl, DN_CONV, 3 * DN_WIDTH), DN_CONV ** -0.5),
        'dn_a_log': jnp.log(jax.random.uniform(ks[9], (nl, DN_HEADS), f32, 1.0, 16.0)),
        'dn_dt_bias': dt + jnp.log(-jnp.expm1(-dt)),
        'dn_out_norm_w': 1.0 + nrm(ks[11], (nl, DN_HEAD_DIM), 0.02),
        'gm_ln_g': 1.0 + nrm(ks[12], (nl, GM_WIDTH), 0.02),
        'gm_ln_b': nrm(ks[13], (nl, GM_WIDTH), 0.02),
        'gm_w_s': nrm(ks[14], (nl, GM_GROUPS, GM_CHUNK, GM_CHUNK), GM_CHUNK ** -0.5),
        'gm_b_s': 1.0 + nrm(ks[15], (nl, GM_GROUPS, GM_CHUNK), 0.01),
        'sw_q_norm_w': 1.0 + nrm(ks[16], (nl, SW_HEAD_DIM), 0.02),
        'sw_k_norm_w': 1.0 + nrm(ks[17], (nl, SW_HEAD_DIM), 0.02),
        'w_ffn_in': nrm(ks[18], (nl, D_MODEL, 2 * FFN_HIDDEN), D_MODEL ** -0.5),
        'w_ffn_out': nrm(ks[19], (nl, FFN_HIDDEN, D_MODEL), FFN_HIDDEN ** -0.5),
    }


def _fwd_reference(x, c, w_mod, b_mod, mix_norm_w, ffn_norm_w, w_in, w_out, dn_conv_w, dn_a_log,
              dn_dt_bias, dn_out_norm_w, gm_ln_g, gm_ln_b, gm_w_s, gm_b_s, sw_q_norm_w,
              sw_k_norm_w, w_ffn_in, w_ffn_out):
    bsz, t, _ = x.shape
    cos, sin = rotary_tables(t)
    c_act = jax.nn.silu(c)
    for layer in range(DEPTH):
        mod = jnp.einsum('bd,de->be', c_act, w_mod[layer]) + b_mod[layer]
        shift1, scale1, gate1, shift2, scale2, gate2 = [m[:, None, :] for m in jnp.split(mod, 6, axis=-1)]
        h = rms_norm(x, mix_norm_w[layer]) * (1.0 + scale1) + shift1
        proj = jnp.einsum('btd,de->bte', h, w_in[layer])
        dn_qkv, dn_z, dn_a, dn_b, gm_uv, sw_qkv = jnp.split(proj, IN_SPLITS, axis=-1)
        y_a = deltanet_mixer(dn_qkv, dn_z, dn_a, dn_b, dn_conv_w[layer], dn_a_log[layer],
                             dn_dt_bias[layer], dn_out_norm_w[layer])
        y_b = spatial_gating_mixer(gm_uv, gm_ln_g[layer], gm_ln_b[layer], gm_w_s[layer], gm_b_s[layer])
        y_c = dilated_attention_mixer(sw_qkv, sw_q_norm_w[layer], sw_k_norm_w[layer], cos, sin)
        y = jnp.concatenate([y_a, y_b, y_c], axis=-1).astype(x.dtype)
        x = x + gate1 * jnp.einsum('bte,ed->btd', y, w_out[layer])
        h = rms_norm(x, ffn_norm_w[layer]) * (1.0 + scale2) + shift2
        gate, up = jnp.split(jnp.einsum('btd,df->btf', h, w_ffn_in[layer]), 2, axis=-1)
        x = x + gate2 * jnp.einsum('btf,fd->btd', jax.nn.silu(gate) * up, w_ffn_out[layer])
    return x


import jax as _jax
import jax.numpy as _jnp

TWIN_FORMAT = 'train_step'
FWD_PARAMS = ['x', 'c', 'w_mod', 'b_mod', 'mix_norm_w', 'ffn_norm_w', 'w_in', 'w_out', 'dn_conv_w', 'dn_a_log', 'dn_dt_bias', 'dn_out_norm_w', 'gm_ln_g', 'gm_ln_b', 'gm_w_s', 'gm_b_s', 'sw_q_norm_w', 'sw_k_norm_w', 'w_ffn_in', 'w_ffn_out']
TWIN_WEIGHTS = ['w_mod', 'b_mod', 'mix_norm_w', 'ffn_norm_w', 'w_in', 'w_out', 'dn_conv_w', 'dn_a_log', 'dn_dt_bias', 'dn_out_norm_w', 'gm_ln_g', 'gm_ln_b', 'gm_w_s', 'gm_b_s', 'sw_q_norm_w', 'sw_k_norm_w', 'w_ffn_in', 'w_ffn_out']
TWIN_DIFF_INPUT = 'x'
TWIN_INPUTS = ['x', 'c', 'w_mod', 'b_mod', 'mix_norm_w', 'ffn_norm_w', 'w_in', 'w_out', 'dn_conv_w', 'dn_a_log', 'dn_dt_bias', 'dn_out_norm_w', 'gm_ln_g', 'gm_ln_b', 'gm_w_s', 'gm_b_s', 'sw_q_norm_w', 'sw_k_norm_w', 'w_ffn_in', 'w_ffn_out', 'loss_target', 'm_w_mod', 'm_b_mod', 'm_mix_norm_w', 'm_ffn_norm_w', 'm_w_in', 'm_w_out', 'm_dn_conv_w', 'm_dn_a_log', 'm_dn_dt_bias', 'm_dn_out_norm_w', 'm_gm_ln_g', 'm_gm_ln_b', 'm_gm_w_s', 'm_gm_b_s', 'm_sw_q_norm_w', 'm_sw_k_norm_w', 'm_w_ffn_in', 'm_w_ffn_out', 'v_w_mod', 'v_b_mod', 'v_mix_norm_w', 'v_ffn_norm_w', 'v_w_in', 'v_w_out', 'v_dn_conv_w', 'v_dn_a_log', 'v_dn_dt_bias', 'v_dn_out_norm_w', 'v_gm_ln_g', 'v_gm_ln_b', 'v_gm_w_s', 'v_gm_b_s', 'v_sw_q_norm_w', 'v_sw_k_norm_w', 'v_w_ffn_in', 'v_w_ffn_out']
TWIN_OUTPUTS = ['loss', 'grad_x', 'grad_w_mod', 'grad_b_mod', 'grad_mix_norm_w', 'grad_ffn_norm_w', 'grad_w_in', 'grad_w_out', 'grad_dn_conv_w', 'grad_dn_a_log', 'grad_dn_dt_bias', 'grad_dn_out_norm_w', 'grad_gm_ln_g', 'grad_gm_ln_b', 'grad_gm_w_s', 'grad_gm_b_s', 'grad_sw_q_norm_w', 'grad_sw_k_norm_w', 'grad_w_ffn_in', 'grad_w_ffn_out', 'delta_w_mod', 'delta_b_mod', 'delta_mix_norm_w', 'delta_ffn_norm_w', 'delta_w_in', 'delta_w_out', 'delta_dn_conv_w', 'delta_dn_a_log', 'delta_dn_dt_bias', 'delta_dn_out_norm_w', 'delta_gm_ln_g', 'delta_gm_ln_b', 'delta_gm_w_s', 'delta_gm_b_s', 'delta_sw_q_norm_w', 'delta_sw_k_norm_w', 'delta_w_ffn_in', 'delta_w_ffn_out', 'new_m_w_mod', 'new_m_b_mod', 'new_m_mix_norm_w', 'new_m_ffn_norm_w', 'new_m_w_in', 'new_m_w_out', 'new_m_dn_conv_w', 'new_m_dn_a_log', 'new_m_dn_dt_bias', 'new_m_dn_out_norm_w', 'new_m_gm_ln_g', 'new_m_gm_ln_b', 'new_m_gm_w_s', 'new_m_gm_b_s', 'new_m_sw_q_norm_w', 'new_m_sw_k_norm_w', 'new_m_w_ffn_in', 'new_m_w_ffn_out', 'new_v_w_mod', 'new_v_b_mod', 'new_v_mix_norm_w', 'new_v_ffn_norm_w', 'new_v_w_in', 'new_v_w_out', 'new_v_dn_conv_w', 'new_v_dn_a_log', 'new_v_dn_dt_bias', 'new_v_dn_out_norm_w', 'new_v_gm_ln_g', 'new_v_gm_ln_b', 'new_v_gm_w_s', 'new_v_gm_b_s', 'new_v_sw_q_norm_w', 'new_v_sw_k_norm_w', 'new_v_w_ffn_in', 'new_v_w_ffn_out']
TWIN_LEAF_KINDS = {'loss': 'loss', 'grad_x': 'grad_x', 'grad_w_mod': 'grad_w', 'grad_b_mod': 'grad_w', 'grad_mix_norm_w': 'grad_w', 'grad_ffn_norm_w': 'grad_w', 'grad_w_in': 'grad_w', 'grad_w_out': 'grad_w', 'grad_dn_conv_w': 'grad_w', 'grad_dn_a_log': 'grad_w', 'grad_dn_dt_bias': 'grad_w', 'grad_dn_out_norm_w': 'grad_w', 'grad_gm_ln_g': 'grad_w', 'grad_gm_ln_b': 'grad_w', 'grad_gm_w_s': 'grad_w', 'grad_gm_b_s': 'grad_w', 'grad_sw_q_norm_w': 'grad_w', 'grad_sw_k_norm_w': 'grad_w', 'grad_w_ffn_in': 'grad_w', 'grad_w_ffn_out': 'grad_w', 'delta_w_mod': 'delta_w', 'delta_b_mod': 'delta_w', 'delta_mix_norm_w': 'delta_w', 'delta_ffn_norm_w': 'delta_w', 'delta_w_in': 'delta_w', 'delta_w_out': 'delta_w', 'delta_dn_conv_w': 'delta_w', 'delta_dn_a_log': 'delta_w', 'delta_dn_dt_bias': 'delta_w', 'delta_dn_out_norm_w': 'delta_w', 'delta_gm_ln_g': 'delta_w', 'delta_gm_ln_b': 'delta_w', 'delta_gm_w_s': 'delta_w', 'delta_gm_b_s': 'delta_w', 'delta_sw_q_norm_w': 'delta_w', 'delta_sw_k_norm_w': 'delta_w', 'delta_w_ffn_in': 'delta_w', 'delta_w_ffn_out': 'delta_w', 'new_m_w_mod': 'new_m', 'new_m_b_mod': 'new_m', 'new_m_mix_norm_w': 'new_m', 'new_m_ffn_norm_w': 'new_m', 'new_m_w_in': 'new_m', 'new_m_w_out': 'new_m', 'new_m_dn_conv_w': 'new_m', 'new_m_dn_a_log': 'new_m', 'new_m_dn_dt_bias': 'new_m', 'new_m_dn_out_norm_w': 'new_m', 'new_m_gm_ln_g': 'new_m', 'new_m_gm_ln_b': 'new_m', 'new_m_gm_w_s': 'new_m', 'new_m_gm_b_s': 'new_m', 'new_m_sw_q_norm_w': 'new_m', 'new_m_sw_k_norm_w': 'new_m', 'new_m_w_ffn_in': 'new_m', 'new_m_w_ffn_out': 'new_m', 'new_v_w_mod': 'new_v', 'new_v_b_mod': 'new_v', 'new_v_mix_norm_w': 'new_v', 'new_v_ffn_norm_w': 'new_v', 'new_v_w_in': 'new_v', 'new_v_w_out': 'new_v', 'new_v_dn_conv_w': 'new_v', 'new_v_dn_a_log': 'new_v', 'new_v_dn_dt_bias': 'new_v', 'new_v_dn_out_norm_w': 'new_v', 'new_v_gm_ln_g': 'new_v', 'new_v_gm_ln_b': 'new_v', 'new_v_gm_w_s': 'new_v', 'new_v_gm_b_s': 'new_v', 'new_v_sw_q_norm_w': 'new_v', 'new_v_sw_k_norm_w': 'new_v', 'new_v_w_ffn_in': 'new_v', 'new_v_w_ffn_out': 'new_v'}


def _forward(args):
    return _fwd_reference(*[args[k] for k in FWD_PARAMS])


def _output_shape():
    out = _jax.eval_shape(lambda: _forward(_fwd_setup_inputs(0)))
    return out.shape, out.dtype

N_MICROBATCH = 1
ADAM_LR = 0.001
ADAM_B1 = 0.9
ADAM_B2 = 0.999
ADAM_EPS = 1e-08
ADAM_WD = 0.01
ADAM_STEP = 10
PER_EXAMPLE_BATCH_AXIS = {'x': 0, 'c': 0, 'loss_target': 0}
SHARED_INPUTS = []
_WEIGHT_DTYPES = {'w_mod': _jnp.float32, 'b_mod': _jnp.float32, 'mix_norm_w': _jnp.float32, 'ffn_norm_w': _jnp.float32, 'w_in': _jnp.float32, 'w_out': _jnp.float32, 'dn_conv_w': _jnp.float32, 'dn_a_log': _jnp.float32, 'dn_dt_bias': _jnp.float32, 'dn_out_norm_w': _jnp.float32, 'gm_ln_g': _jnp.float32, 'gm_ln_b': _jnp.float32, 'gm_w_s': _jnp.float32, 'gm_b_s': _jnp.float32, 'sw_q_norm_w': _jnp.float32, 'sw_k_norm_w': _jnp.float32, 'w_ffn_in': _jnp.float32, 'w_ffn_out': _jnp.float32}
MOMENT_SCALE = {'w_mod': 9.864879e-01, 'b_mod': 2.130676e+00, 'mix_norm_w': 1.331080e+00, 'ffn_norm_w': 3.327552e+00, 'w_in': 1.312533e-01, 'w_out': 2.325267e-01, 'dn_conv_w': 1.553290e-01, 'dn_a_log': 3.292783e+00, 'dn_dt_bias': 3.142189e+00, 'dn_out_norm_w': 6.068110e+00, 'gm_ln_g': 7.803797e-01, 'gm_ln_b': 1.156441e-01, 'gm_w_s': 5.095654e-02, 'gm_b_s': 7.422666e-01, 'sw_q_norm_w': 7.892093e-02, 'sw_k_norm_w': 7.927867e-02, 'w_ffn_in': 7.095720e-02, 'w_ffn_out': 1.011517e-01}


def _to_microbatches(a, axis):
    t = _jnp.moveaxis(a, axis, 0)
    t = t.reshape((N_MICROBATCH, t.shape[0] // N_MICROBATCH) + t.shape[1:])
    return _jnp.moveaxis(t, 1, axis + 1)


def setup_inputs(seed: int = 0) -> dict:
    inp = _fwd_setup_inputs(seed)
    key = _jax.random.fold_in(_jax.random.key(seed), 7919)
    shape, _ = _output_shape()
    out = dict(inp)
    out["loss_target"] = _jax.random.normal(_jax.random.fold_in(key, 0), shape, _jnp.float32)
    for i, name in enumerate(TWIN_WEIGHTS):
        w = inp[name].astype(_jnp.float32)
        if MOMENT_SCALE is None:
            s = _jnp.sqrt(_jnp.mean(_jnp.square(w)) + 1e-30)
        else:
            s = MOMENT_SCALE[name]
        km, kv = _jax.random.split(_jax.random.fold_in(key, i + 1))
        out[name] = w
        out["m_" + name] = s * _jax.random.normal(km, w.shape, _jnp.float32)
        out["v_" + name] = (s * s) * _jax.random.uniform(kv, w.shape, _jnp.float32, 0.5, 1.5)
    if N_MICROBATCH > 1:
        for name, axis in PER_EXAMPLE_BATCH_AXIS.items():
            out[name] = _to_microbatches(out[name], axis)
    return {'x': out['x'], 'c': out['c'], 'w_mod': out['w_mod'], 'b_mod': out['b_mod'], 'mix_norm_w': out['mix_norm_w'], 'ffn_norm_w': out['ffn_norm_w'], 'w_in': out['w_in'], 'w_out': out['w_out'], 'dn_conv_w': out['dn_conv_w'], 'dn_a_log': out['dn_a_log'], 'dn_dt_bias': out['dn_dt_bias'], 'dn_out_norm_w': out['dn_out_norm_w'], 'gm_ln_g': out['gm_ln_g'], 'gm_ln_b': out['gm_ln_b'], 'gm_w_s': out['gm_w_s'], 'gm_b_s': out['gm_b_s'], 'sw_q_norm_w': out['sw_q_norm_w'], 'sw_k_norm_w': out['sw_k_norm_w'], 'w_ffn_in': out['w_ffn_in'], 'w_ffn_out': out['w_ffn_out'], 'loss_target': out['loss_target'], 'm_w_mod': out['m_w_mod'], 'm_b_mod': out['m_b_mod'], 'm_mix_norm_w': out['m_mix_norm_w'], 'm_ffn_norm_w': out['m_ffn_norm_w'], 'm_w_in': out['m_w_in'], 'm_w_out': out['m_w_out'], 'm_dn_conv_w': out['m_dn_conv_w'], 'm_dn_a_log': out['m_dn_a_log'], 'm_dn_dt_bias': out['m_dn_dt_bias'], 'm_dn_out_norm_w': out['m_dn_out_norm_w'], 'm_gm_ln_g': out['m_gm_ln_g'], 'm_gm_ln_b': out['m_gm_ln_b'], 'm_gm_w_s': out['m_gm_w_s'], 'm_gm_b_s': out['m_gm_b_s'], 'm_sw_q_norm_w': out['m_sw_q_norm_w'], 'm_sw_k_norm_w': out['m_sw_k_norm_w'], 'm_w_ffn_in': out['m_w_ffn_in'], 'm_w_ffn_out': out['m_w_ffn_out'], 'v_w_mod': out['v_w_mod'], 'v_b_mod': out['v_b_mod'], 'v_mix_norm_w': out['v_mix_norm_w'], 'v_ffn_norm_w': out['v_ffn_norm_w'], 'v_w_in': out['v_w_in'], 'v_w_out': out['v_w_out'], 'v_dn_conv_w': out['v_dn_conv_w'], 'v_dn_a_log': out['v_dn_a_log'], 'v_dn_dt_bias': out['v_dn_dt_bias'], 'v_dn_out_norm_w': out['v_dn_out_norm_w'], 'v_gm_ln_g': out['v_gm_ln_g'], 'v_gm_ln_b': out['v_gm_ln_b'], 'v_gm_w_s': out['v_gm_w_s'], 'v_gm_b_s': out['v_gm_b_s'], 'v_sw_q_norm_w': out['v_sw_q_norm_w'], 'v_sw_k_norm_w': out['v_sw_k_norm_w'], 'v_w_ffn_in': out['v_w_ffn_in'], 'v_w_ffn_out': out['v_w_ffn_out']}


def _loss(weights, diff, rest, loss_target):
    with _jax.named_scope("forward"):
        args = {**rest, TWIN_DIFF_INPUT: diff, **{k: w.astype(_WEIGHT_DTYPES[k]) for k, w in weights.items()}}
        y = _forward(args)
    with _jax.named_scope("loss_head"):
        err = _jnp.square(y.astype(_jnp.float32) - loss_target)
        return 0.5 * _jnp.sum(_jnp.mean(err, axis=-1)) if err.ndim else 0.5 * err


def _adamw(w, g, m, v):
    m = ADAM_B1 * m + (1.0 - ADAM_B1) * g
    v = ADAM_B2 * v + (1.0 - ADAM_B2) * _jnp.square(g)
    m_hat = m / (1.0 - ADAM_B1 ** ADAM_STEP)
    v_hat = v / (1.0 - ADAM_B2 ** ADAM_STEP)
    delta = -ADAM_LR * (m_hat / (_jnp.sqrt(v_hat) + ADAM_EPS) + ADAM_WD * w)
    return delta, m, v


def reference(x, c, w_mod, b_mod, mix_norm_w, ffn_norm_w, w_in, w_out, dn_conv_w, dn_a_log, dn_dt_bias, dn_out_norm_w, gm_ln_g, gm_ln_b, gm_w_s, gm_b_s, sw_q_norm_w, sw_k_norm_w, w_ffn_in, w_ffn_out, loss_target, m_w_mod, m_b_mod, m_mix_norm_w, m_ffn_norm_w, m_w_in, m_w_out, m_dn_conv_w, m_dn_a_log, m_dn_dt_bias, m_dn_out_norm_w, m_gm_ln_g, m_gm_ln_b, m_gm_w_s, m_gm_b_s, m_sw_q_norm_w, m_sw_k_norm_w, m_w_ffn_in, m_w_ffn_out, v_w_mod, v_b_mod, v_mix_norm_w, v_ffn_norm_w, v_w_in, v_w_out, v_dn_conv_w, v_dn_a_log, v_dn_dt_bias, v_dn_out_norm_w, v_gm_ln_g, v_gm_ln_b, v_gm_w_s, v_gm_b_s, v_sw_q_norm_w, v_sw_k_norm_w, v_w_ffn_in, v_w_ffn_out):
    given = dict(x=x, c=c, w_mod=w_mod, b_mod=b_mod, mix_norm_w=mix_norm_w, ffn_norm_w=ffn_norm_w, w_in=w_in, w_out=w_out, dn_conv_w=dn_conv_w, dn_a_log=dn_a_log, dn_dt_bias=dn_dt_bias, dn_out_norm_w=dn_out_norm_w, gm_ln_g=gm_ln_g, gm_ln_b=gm_ln_b, gm_w_s=gm_w_s, gm_b_s=gm_b_s, sw_q_norm_w=sw_q_norm_w, sw_k_norm_w=sw_k_norm_w, w_ffn_in=w_ffn_in, w_ffn_out=w_ffn_out, loss_target=loss_target, m_w_mod=m_w_mod, m_b_mod=m_b_mod, m_mix_norm_w=m_mix_norm_w, m_ffn_norm_w=m_ffn_norm_w, m_w_in=m_w_in, m_w_out=m_w_out, m_dn_conv_w=m_dn_conv_w, m_dn_a_log=m_dn_a_log, m_dn_dt_bias=m_dn_dt_bias, m_dn_out_norm_w=m_dn_out_norm_w, m_gm_ln_g=m_gm_ln_g, m_gm_ln_b=m_gm_ln_b, m_gm_w_s=m_gm_w_s, m_gm_b_s=m_gm_b_s, m_sw_q_norm_w=m_sw_q_norm_w, m_sw_k_norm_w=m_sw_k_norm_w, m_w_ffn_in=m_w_ffn_in, m_w_ffn_out=m_w_ffn_out, v_w_mod=v_w_mod, v_b_mod=v_b_mod, v_mix_norm_w=v_mix_norm_w, v_ffn_norm_w=v_ffn_norm_w, v_w_in=v_w_in, v_w_out=v_w_out, v_dn_conv_w=v_dn_conv_w, v_dn_a_log=v_dn_a_log, v_dn_dt_bias=v_dn_dt_bias, v_dn_out_norm_w=v_dn_out_norm_w, v_gm_ln_g=v_gm_ln_g, v_gm_ln_b=v_gm_ln_b, v_gm_w_s=v_gm_w_s, v_gm_b_s=v_gm_b_s, v_sw_q_norm_w=v_sw_q_norm_w, v_sw_k_norm_w=v_sw_k_norm_w, v_w_ffn_in=v_w_ffn_in, v_w_ffn_out=v_w_ffn_out)
    weights = {n: given[n] for n in TWIN_WEIGHTS}
    shared = {n: given[n] for n in SHARED_INPUTS}
    per_example = {n: given[n] for n in ['x', 'c']}
    grad_fn = _jax.value_and_grad(_loss, argnums=(0, 1))

    def one_microbatch(ex, loss_target):
        ex = dict(ex)
        diff = ex.pop(TWIN_DIFF_INPUT)
        return grad_fn(weights, diff, {**shared, **ex}, loss_target)

    if N_MICROBATCH == 1:
        loss, (grad_w, grad_x) = one_microbatch(per_example, given["loss_target"])
    else:
        def body(carry, xs):
            loss_sum, grad_sum = carry
            l_k, (gw_k, gx_k) = one_microbatch(xs[0], xs[1])
            with _jax.named_scope("update"):
                return (loss_sum + l_k, _jax.tree.map(_jnp.add, grad_sum, gw_k)), gx_k

        init = (_jnp.zeros((), _jnp.float32), _jax.tree.map(_jnp.zeros_like, weights))
        (loss, grad_w), grad_x = _jax.lax.scan(body, init, (per_example, given["loss_target"]))
    with _jax.named_scope("update"):
        delta_w, new_m, new_v = {}, {}, {}
        for n in TWIN_WEIGHTS:
            delta_w[n], new_m[n], new_v[n] = _adamw(weights[n], grad_w[n], given["m_" + n], given["v_" + n])
    return (loss, grad_x, *[grad_w[n] for n in TWIN_WEIGHTS], *[delta_w[n] for n in TWIN_WEIGHTS],
            *[new_m[n] for n in TWIN_WEIGHTS], *[new_v[n] for n in TWIN_WEIGHTS])
```

```python
import functools
import math

import jax
import jax.numpy as jnp
from jax import lax
from jax.experimental import pallas as pl
from jax.experimental.pallas import tpu as pltpu

f32 = jnp.float32
bf16 = jnp.bfloat16
HI = lax.Precision.HIGHEST
AXES = ("x", "y", "c")
NDEV = 8
SDS = jax.ShapeDtypeStruct

D = 1024
NORM_EPS = 1e-6
DN_W, DN_H, DN_D, DN_C = 512, 4, 128, 64
GM_W, GM_G, GM_C = 256, 4, 128
SW_W, SW_H, SW_D, SW_B = 256, 4, 64, 128
SW_DIL = (1, 4, 16)
SW_SPAN = 128
ROPE_DIM, ROPE_THETA = 16, 500000.0
IN_W = 4872
IN_WA = 4992
FFN = 2816
ADAM_LR, ADAM_B1, ADAM_B2, ADAM_EPS, ADAM_WD, ADAM_STEP = 0.001, 0.9, 0.999, 1e-08, 0.01, 10

VMEM_LIMIT = 52 * 1024 * 1024


def _cp(sem=None):
    return pltpu.CompilerParams(vmem_limit_bytes=VMEM_LIMIT, dimension_semantics=sem)


_DIMS = {"nn": (((1,), (0,)), ((), ())), "nt": (((1,), (1,)), ((), ())), "tn": (((0,), (0,)), ((), ()))}


def _raw_dot(a, b, mode, hi):
    if hi:
        return lax.dot_general(a, b, _DIMS[mode], precision=HI, preferred_element_type=f32)
    return lax.dot_general(a.astype(bf16), b.astype(bf16), _DIMS[mode], preferred_element_type=f32)


@functools.partial(jax.custom_vjp, nondiff_argnums=(2, 3))
def _dot(a, b, mode, hi):
    return _raw_dot(a, b, mode, hi)


def _dot_fwd(a, b, mode, hi):
    return _raw_dot(a, b, mode, hi), (a, b)


def _dot_bwd(mode, hi, res, g):
    a, b = res
    if mode == "nn":
        return _raw_dot(g, b, "nt", hi), _raw_dot(a, g, "tn", hi)
    if mode == "nt":
        return _raw_dot(g, b, "nn", hi), _raw_dot(g, a, "tn", hi)
    return _raw_dot(b, g, "nt", hi), _raw_dot(a, g, "nn", hi)


_dot.defvjp(_dot_fwd, _dot_bwd)


def _iota(shape, dim):
    return lax.broadcasted_iota(jnp.int32, shape, dim)


def _f_modnorm(x, w, scale, shift):
    y = x * lax.rsqrt(jnp.mean(x * x, axis=-1, keepdims=True) + NORM_EPS) * w
    return (y * (1.0 + scale) + shift,)


def _f_resid(x, m, gate):
    return (x + gate * m,)


def _f_swiglu(g, u):
    return (jax.nn.silu(g) * u,)


def _softplus(x):
    return jnp.maximum(x, 0.0) + jnp.log1p(jnp.exp(-jnp.abs(x)))


def _f_dn_act(y, ab, alog, dtb):
    c = jax.nn.silu(y)
    parts = []
    for j in range(3 * DN_H):
        p = c[:, j * DN_D:(j + 1) * DN_D]
        if j < 2 * DN_H:
            p = p * lax.rsqrt(jnp.sum(p * p, axis=-1, keepdims=True) + NORM_EPS)
        parts.append(p)
    lane = _iota(ab.shape, 1)
    g = -jnp.exp(alog) * _softplus(ab + dtb)
    beta = jax.nn.sigmoid(ab)
    gb = jnp.where(lane < DN_H, g, jnp.where(lane < 2 * DN_H, beta, 0.0))
    return jnp.concatenate(parts, axis=1), gb


def _f_dn_out(o, z, wn):
    parts = []
    for h in range(DN_H):
        oh = o[:, h * DN_D:(h + 1) * DN_D]
        zh = z[:, h * DN_D:(h + 1) * DN_D]
        n = oh * lax.rsqrt(jnp.mean(oh * oh, axis=-1, keepdims=True) + NORM_EPS) * wn
        parts.append(n * jax.nn.silu(zh))
    return (jnp.concatenate(parts, axis=1),)


def _gelu(x):
    return 0.5 * x * (1.0 + lax.erf(x * (1.0 / math.sqrt(2.0))))


def _f_gm(u_raw, v_raw, ln_g, ln_b, w_s, b_st):
    u = _gelu(u_raw)
    v = _gelu(v_raw)
    mu = jnp.mean(v, axis=-1, keepdims=True)
    vc = v - mu
    var = jnp.mean(vc * vc, axis=-1, keepdims=True)
    v = vc * lax.rsqrt(var + NORM_EPS) * ln_g + ln_b
    r = _iota((GM_C, GM_C), 0)
    c = _iota((GM_C, GM_C), 1)
    grp = _iota((GM_C, GM_W), 1) // (GM_W // GM_G)
    expand = jnp.where(_iota((GM_C, GM_W), 0) == grp, 1.0, 0.0)
    sv = _dot(b_st, expand, "nn", True)
    for g in range(GM_G):
        wg = jnp.where(r >= c, w_s[g], 0.0)
        sv = sv + jnp.where(grp == g, _dot(wg, v, "nn", True), 0.0)
    return (u * sv,)


def _head_lanes(shape):
    return _iota(shape, 1) // SW_D


def _f_sw_pre(q0, k0, q1, k1, q2, k2, cs, sn, wq, wk):
    r = _iota((SW_W, SW_W), 0)
    c = _iota((SW_W, SW_W), 1)
    same_head = jnp.where(r // SW_D == c // SW_D, 1.0, 0.0)
    hc = c % SW_D
    half = ROPE_DIM // 2
    perm = jnp.where((hc < half) & (r == c + half), -1.0, jnp.where((hc >= half) & (hc < ROPE_DIM) & (r == c - half), 1.0, 0.0))
    tile = jnp.where((_iota((128, SW_W), 1) % SW_D == _iota((128, SW_W), 0)) & (_iota((128, SW_W), 0) < SW_D), 1.0, 0.0)
    wq_full = _dot(wq, tile, "nn", True)[0:1, :]
    wk_full = _dot(wk, tile, "nn", True)[0:1, :]

    def one(t, w):
        ms = _dot(t * t, same_head, "nn", True) * (1.0 / SW_D)
        n = t * lax.rsqrt(ms + NORM_EPS) * w
        return n * cs + _dot(n, perm, "nn", True) * sn

    return one(q0, wq_full), one(k0, wk_full), one(q1, wq_full), one(k1, wk_full), one(q2, wq_full), one(k2, wk_full)


def _f_sw_merge(o0, l0, o1, l1, o2, l2):
    m = jnp.maximum(jnp.maximum(l0, l1), l2)
    e0, e1, e2 = jnp.exp(l0 - m), jnp.exp(l1 - m), jnp.exp(l2 - m)
    return ((e0 * o0 + e1 * o1 + e2 * o2) / (e0 + e1 + e2),)


def _f_attn(q, kp, kc, vp, vc, has_prev):
    kk = jnp.concatenate([kp, kc], axis=0)
    vv = jnp.concatenate([vp, vc], axis=0)
    i = _iota((SW_B, 2 * SW_B), 0)
    j = _iota((SW_B, 2 * SW_B), 1)
    dist = i + SW_B - j
    valid = (dist >= 0) & (dist <= SW_SPAN) & ((j >= SW_B) | has_prev)
    hl = _head_lanes(q.shape)
    o = jnp.zeros(q.shape, f32)
    lse = jnp.zeros(q.shape, f32)
    for h in range(SW_H):
        qh = jnp.where(hl == h, q, 0.0)
        s = _dot(qh, kk, "nt", False) * (SW_D ** -0.5)
        s = jnp.where(valid, s, -1e30)
        m = jnp.max(s, axis=-1, keepdims=True)
        p = jnp.where(valid, jnp.exp(s - m), 0.0)
        l = jnp.sum(p, axis=-1, keepdims=True)
        oh = _dot(p, vv, "nn", False) / l
        o = o + jnp.where(hl == h, oh, 0.0)
        lse = lse + jnp.where(hl == h, m + jnp.log(l), 0.0)
    return o, lse


def _decay_mats(gcol):
    n = gcol.shape[0]
    r = _iota((n, n), 0)
    c = _iota((n, n), 1)
    low = jnp.where(r >= c, 1.0, 0.0)
    gb = jnp.broadcast_to(gcol, (n, n))
    gc_i = _dot(low, gb, "nn", True)
    gc_j = _dot(jnp.ones((n, n), f32), jnp.where(r <= c, gb, 0.0), "nn", True)
    diff = jnp.where(r >= c, gc_i - gc_j, 0.0)
    decay = jnp.where(r >= c, jnp.exp(diff), 0.0)
    gc = jnp.sum(jnp.where(c == 0, gc_i, 0.0), axis=1, keepdims=True)
    return gc, decay, r, c


def _f_dn_prep(q, k, v, gcol, beta):
    del q
    gc, decay, r, c = _decay_mats(gcol)
    kb = k * beta
    a = jnp.where(r > c, _dot(kb, k, "nt", True) * decay, 0.0)
    eye = jnp.where(r == c, 1.0, 0.0)
    inv = eye - a
    pw = a
    for _ in range(5):
        pw = _dot(pw, pw, "nn", True)
        inv = _dot(inv, eye + pw, "nn", True)
    u = _dot(inv, v * beta, "nn", True)
    w = _dot(inv, kb * jnp.exp(gc), "nn", True)
    return u, w


def _f_dn_scan(state, q, k, u, w, gcol):
    gc, decay, _, _ = _decay_mats(gcol)
    q = q * (DN_D ** -0.5)
    a_qk = _dot(q, k, "nt", True) * decay
    g_last = jnp.sum(gcol, axis=0, keepdims=True)
    q_dec = q * jnp.exp(gc)
    k_dec = k * jnp.exp(g_last - gc)
    v_new = u - _dot(w, state, "nn", True)
    o = _dot(q_dec, state, "nn", True) + _dot(a_qk, v_new, "nn", True)
    new_state = state * jnp.exp(g_last) + _dot(k_dec, v_new, "tn", True)
    return o, new_state


def _cspec(tt, w, cb):
    return pl.BlockSpec((tt, w), lambda i, cb=cb: (i, cb))


def _pspec(shape):
    nd = len(shape)
    return pl.BlockSpec(tuple(shape), lambda i, nd=nd: (0,) * nd)


def _ew_fwd(name, f, tiled, params, outs, tt):
    t = tiled[0][0].shape[0]
    nt, npar = len(tiled), len(params)

    def body(*refs):
        tv = [r[...].astype(f32) for r in refs[:nt]]
        pv = [r[...] for r in refs[nt:nt + npar]]
        res = f(*tv, *pv)
        for o, r in zip(refs[nt + npar:], res):
            o[...] = r.astype(o.dtype)

    res = pl.pallas_call(
        body, grid=(t // tt,), name=name,
        in_specs=[_cspec(tt, w, cb) for _, w, cb in tiled] + [_pspec(p.shape) for p in params],
        out_specs=[_cspec(tt, w, 0) for w, _ in outs],
        out_shape=[SDS((t, w), dt) for w, dt in outs],
        compiler_params=_cp(("arbitrary",)),
    )(*[a for a, _, _ in tiled], *params)
    return res


def _ew_bwd(name, f, tiled, params, cots, diff, tt, adds=None):
    t = tiled[0][0].shape[0]
    nt, npar, nc, nd = len(tiled), len(params), len(cots), len(diff)
    adds = adds or [None] * nd
    add_arrs = [a for a in adds if a is not None]
    na = len(add_arrs)
    dwidth = [tiled[k][1] for k, _ in diff]

    def body(*refs):
        tin = refs[:nt]
        pin = refs[nt:nt + npar]
        cin = refs[nt + npar:nt + npar + nc]
        ain = list(refs[nt + npar + nc:nt + npar + nc + na])
        dts = refs[nt + npar + nc + na:nt + npar + nc + na + nd]
        dps = refs[nt + npar + nc + na + nd:]
        tv = [r[...].astype(f32) for r in tin]
        pv = [r[...] for r in pin]

        def g(*dv):
            full = list(tv)
            for n_, (k, _) in enumerate(diff):
                full[k] = dv[n_]
            return tuple(f(*full, *dv[nd:]))

        _, vjp = jax.vjp(g, *[tv[k] for k, _ in diff], *pv)
        grads = vjp(tuple(c[...].astype(f32) for c in cin))
        for n_ in range(nd):
            val = grads[n_]
            if adds[n_] is not None:
                val = val + ain.pop(0)[...].astype(f32)
            dts[n_][...] = val.astype(dts[n_].dtype)

        @pl.when(pl.program_id(0) == 0)
        def _():
            for r in dps:
                r[...] = jnp.zeros(r.shape, f32)

        for r, gp in zip(dps, grads[nd:]):
            r[...] += gp

    res = pl.pallas_call(
        body, grid=(t // tt,), name=name,
        in_specs=[_cspec(tt, w, cb) for _, w, cb in tiled] + [_pspec(p.shape) for p in params]
        + [_cspec(tt, w, cb) for _, w, cb in cots] + [_cspec(tt, a.shape[1], 0) for a in add_arrs],
        out_specs=[_cspec(tt, w, 0) for w in dwidth] + [_pspec(p.shape) for p in params],
        out_shape=[SDS((t, w), dt) for w, (_, dt) in zip(dwidth, diff)] + [SDS(p.shape, f32) for p in params],
        compiler_params=_cp(("arbitrary",)),
    )(*[a for a, _, _ in tiled], *params, *[a for a, _, _ in cots], *add_arrs)
    return res[:nd], res[nd:]


def _mm(name, a, b, mode, tm, tn, out_dtype=f32):
    if mode == "nn":
        (m, k), (k2, n) = a.shape, b.shape
        a_spec = pl.BlockSpec((tm, k), lambda i, j: (i, 0))
        b_spec = pl.BlockSpec((k, tn), lambda i, j: (0, j))
    elif mode == "nt":
        (m, k), (n, k2) = a.shape, b.shape
        a_spec = pl.BlockSpec((tm, k), lambda i, j: (i, 0))
        b_spec = pl.BlockSpec((tn, k), lambda i, j: (j, 0))
    else:
        (k, m), (k2, n) = a.shape, b.shape
        a_spec = pl.BlockSpec((k, tm), lambda i, j: (0, i))
        b_spec = pl.BlockSpec((k, tn), lambda i, j: (0, j))
    assert k == k2 and m % tm == 0 and n % tn == 0, (name, a.shape, b.shape, mode)
    assert a.dtype == bf16 and b.dtype == bf16, name

    def body(a_ref, b_ref, o_ref):
        o_ref[...] = lax.dot_general(a_ref[...], b_ref[...], _DIMS[mode], preferred_element_type=f32).astype(o_ref.dtype)

    return pl.pallas_call(
        body, grid=(m // tm, n // tn), name=name,
        in_specs=[a_spec, b_spec], out_specs=pl.BlockSpec((tm, tn), lambda i, j: (i, j)),
        out_shape=SDS((m, n), out_dtype), compiler_params=_cp(("parallel", "parallel")),
    )(a, b)


CONV_K = 4
CONV_W = 3 * DN_W
HALO = 8


def _conv_fwd(proj, w, tt):
    t = proj.shape[0]
    nb8 = tt // HALO

    def body(x_ref, h_ref, w_ref, y_ref, xe):
        i = pl.program_id(0)
        xe[0:HALO, :] = jnp.where(i == 0, 0.0, h_ref[...])
        xe[HALO:, :] = x_ref[...]
        wv = w_ref[...]
        acc = jnp.zeros((tt, CONV_W), f32)
        for k in range(CONV_K):
            acc = acc + wv[k:k + 1, :] * xe[pl.ds(HALO - (CONV_K - 1) + k, tt), :]
        y_ref[...] = acc

    return pl.pallas_call(
        body, grid=(t // tt,), name="conv_fwd",
        in_specs=[pl.BlockSpec((tt, CONV_W), lambda i: (i, 0)),
                  pl.BlockSpec((HALO, CONV_W), lambda i: (jnp.maximum(i * nb8 - 1, 0), 0)),
                  _pspec(w.shape)],
        out_specs=pl.BlockSpec((tt, CONV_W), lambda i: (i, 0)),
        out_shape=SDS((t, CONV_W), f32),
        scratch_shapes=[pltpu.VMEM((tt + HALO, CONV_W), f32)],
        compiler_params=_cp(("arbitrary",)),
    )(proj, proj, w)


def _conv_bwd(proj, dy, w, tt):
    t = proj.shape[0]
    nb8 = tt // HALO
    last8 = t // HALO - 1
    nsteps = t // tt

    def body(x_ref, h_ref, dy_ref, n_ref, w_ref, dx_ref, dw_ref, xe, dye):
        i = pl.program_id(0)
        xe[0:HALO, :] = jnp.where(i == 0, 0.0, h_ref[...])
        xe[HALO:, :] = x_ref[...]
        dye[0:tt, :] = dy_ref[...]
        dye[tt:, :] = jnp.where(i == nsteps - 1, 0.0, n_ref[...])
        wv = w_ref[...]
        dyv = dy_ref[...]
        acc = jnp.zeros((tt, CONV_W), f32)

        @pl.when(i == 0)
        def _():
            dw_ref[...] = jnp.zeros(dw_ref.shape, f32)

        for k in range(CONV_K):
            acc = acc + wv[k:k + 1, :] * dye[pl.ds(CONV_K - 1 - k, tt), :]
            dw_ref[k:k + 1, :] += jnp.sum(dyv * xe[pl.ds(HALO - (CONV_K - 1) + k, tt), :], axis=0, keepdims=True)
        dx_ref[...] = acc.astype(dx_ref.dtype)

    return pl.pallas_call(
        body, grid=(nsteps,), name="conv_bwd",
        in_specs=[pl.BlockSpec((tt, CONV_W), lambda i: (i, 0)),
                  pl.BlockSpec((HALO, CONV_W), lambda i: (jnp.maximum(i * nb8 - 1, 0), 0)),
                  pl.BlockSpec((tt, CONV_W), lambda i: (i, 0)),
                  pl.BlockSpec((HALO, CONV_W), lambda i: (jnp.minimum((i + 1) * nb8, last8), 0)),
                  _pspec(w.shape)],
        out_specs=[pl.BlockSpec((tt, CONV_W), lambda i: (i, 0)), _pspec(w.shape)],
        out_shape=[SDS((t, CONV_W), bf16), SDS(w.shape, f32)],
        scratch_shapes=[pltpu.VMEM((tt + HALO, CONV_W), f32), pltpu.VMEM((tt + HALO, CONV_W), f32)],
        compiler_params=_cp(("arbitrary",)),
    )(proj, proj, dy, dy, w)


PREP_CHUNKS = 2


def _lane_col(gb, lane_idx):
    return jnp.sum(jnp.where(_iota(gb.shape, 1) == lane_idx, gb, 0.0), axis=1, keepdims=True)


def _dn_prep_fwd(qkv, gb):
    t = qkv.shape[0]
    rows = PREP_CHUNKS * DN_C

    def body(qkv_ref, gb_ref, u_ref, w_ref):
        for ch in range(PREP_CHUNKS):
            rs = slice(ch * DN_C, (ch + 1) * DN_C)
            gbv = gb_ref[rs, :]
            for h in range(DN_H):
                k = qkv_ref[rs, DN_W + h * DN_D:DN_W + (h + 1) * DN_D]
                v = qkv_ref[rs, 2 * DN_W + h * DN_D:2 * DN_W + (h + 1) * DN_D]
                u, w = _f_dn_prep(None, k, v, _lane_col(gbv, h), _lane_col(gbv, DN_H + h))
                u_ref[rs, h * DN_D:(h + 1) * DN_D] = u
                w_ref[rs, h * DN_D:(h + 1) * DN_D] = w

    return pl.pallas_call(
        body, grid=(t // rows,), name="dn_prep_fwd",
        in_specs=[pl.BlockSpec((rows, CONV_W), lambda i: (i, 0)), pl.BlockSpec((rows, 128), lambda i: (i, 0))],
        out_specs=[pl.BlockSpec((rows, DN_W), lambda i: (i, 0))] * 2,
        out_shape=[SDS((t, DN_W), f32)] * 2,
        compiler_params=_cp(("arbitrary",)),
    )(qkv, gb)


def _dn_prep_bwd(qkv, gb, du, dw, dqk1, dgb1):
    t = qkv.shape[0]
    rows = PREP_CHUNKS * DN_C

    def body(qkv_ref, gb_ref, du_ref, dw_ref, dqk1_ref, dgb1_ref, dqkv_ref, dgb_ref):
        for ch in range(PREP_CHUNKS):
            rs = slice(ch * DN_C, (ch + 1) * DN_C)
            gbv = gb_ref[rs, :]
            lane = _iota(gbv.shape, 1)
            dgb = dgb1_ref[rs, :]
            for h in range(DN_H):
                ks = slice(DN_W + h * DN_D, DN_W + (h + 1) * DN_D)
                vs = slice(2 * DN_W + h * DN_D, 2 * DN_W + (h + 1) * DN_D)
                hs = slice(h * DN_D, (h + 1) * DN_D)
                _, vjp = jax.vjp(lambda k, v, g, b: _f_dn_prep(None, k, v, g, b),
                                 qkv_ref[rs, ks], qkv_ref[rs, vs], _lane_col(gbv, h), _lane_col(gbv, DN_H + h))
                dk, dv, dg, db = vjp((du_ref[rs, hs], dw_ref[rs, hs]))
                dqkv_ref[rs, hs] = dqk1_ref[rs, hs]
                dqkv_ref[rs, ks] = dk + dqk1_ref[rs, ks]
                dqkv_ref[rs, vs] = dv
                dgb = dgb + jnp.where(lane == h, dg, 0.0) + jnp.where(lane == DN_H + h, db, 0.0)
            dgb_ref[rs, :] = dgb

    return pl.pallas_call(
        body, grid=(t // rows,), name="dn_prep_bwd",
        in_specs=[pl.BlockSpec((rows, CONV_W), lambda i: (i, 0)), pl.BlockSpec((rows, 128), lambda i: (i, 0)),
                  pl.BlockSpec((rows, DN_W), lambda i: (i, 0)), pl.BlockSpec((rows, DN_W), lambda i: (i, 0)),
                  pl.BlockSpec((rows, 2 * DN_W), lambda i: (i, 0)), pl.BlockSpec((rows, 128), lambda i: (i, 0))],
        out_specs=[pl.BlockSpec((rows, CONV_W), lambda i: (i, 0)), pl.BlockSpec((rows, 128), lambda i: (i, 0))],
        out_shape=[SDS((t, CONV_W), f32), SDS((t, 128), f32)],
        compiler_params=_cp(("arbitrary",)),
    )(qkv, gb, du, dw, dqk1, dgb1)


def _dn_scan_fwd(qkv, gb, u, w):
    t = qkv.shape[0]
    n = t // DN_C

    def body(qkv_ref, gb_ref, u_ref, w_ref, o_ref, s_ref, state):
        @pl.when(pl.program_id(0) == 0)
        def _():
            state[...] = jnp.zeros(state.shape, f32)

        gbv = gb_ref[...]
        for h in range(DN_H):
            hs = slice(h * DN_D, (h + 1) * DN_D)
            st = state[h]
            s_ref[0, h] = st
            o, new = _f_dn_scan(st, qkv_ref[:, hs], qkv_ref[:, DN_W + h * DN_D:DN_W + (h + 1) * DN_D],
                                u_ref[:, hs], w_ref[:, hs], _lane_col(gbv, h))
            o_ref[:, hs] = o
            state[h] = new

    return pl.pallas_call(
        body, grid=(n,), name="dn_scan_fwd",
        in_specs=[pl.BlockSpec((DN_C, 2 * DN_W), lambda i: (i, 0)), pl.BlockSpec((DN_C, 128), lambda i: (i, 0)),
                  pl.BlockSpec((DN_C, DN_W), lambda i: (i, 0)), pl.BlockSpec((DN_C, DN_W), lambda i: (i, 0))],
        out_specs=[pl.BlockSpec((DN_C, DN_W), lambda i: (i, 0)),
                   pl.BlockSpec((1, DN_H, DN_D, DN_D), lambda i: (i, 0, 0, 0))],
        out_shape=[SDS((t, DN_W), f32), SDS((n, DN_H, DN_D, DN_D), f32)],
        scratch_shapes=[pltpu.VMEM((DN_H, DN_D, DN_D), f32)],
        compiler_params=_cp(("arbitrary",)),
    )(qkv, gb, u, w)


def _dn_scan_bwd(qkv, gb, u, w, states, do):
    t = qkv.shape[0]
    n = t // DN_C
    rev = lambda i: (n - 1 - i, 0)

    def body(qkv_ref, gb_ref, u_ref, w_ref, s_ref, do_ref, dqk_ref, du_ref, dw_ref, dgb_ref, dstate):
        @pl.when(pl.program_id(0) == 0)
        def _():
            dstate[...] = jnp.zeros(dstate.shape, f32)

        gbv = gb_ref[...]
        lane = _iota(gbv.shape, 1)
        dgb = jnp.zeros(gbv.shape, f32)
        for h in range(DN_H):
            hs = slice(h * DN_D, (h + 1) * DN_D)
            ks = slice(DN_W + h * DN_D, DN_W + (h + 1) * DN_D)
            _, vjp = jax.vjp(_f_dn_scan, s_ref[0, h], qkv_ref[:, hs], qkv_ref[:, ks], u_ref[:, hs], w_ref[:, hs],
                             _lane_col(gbv, h))
            ds, dq, dk, du, dw, dg = vjp((do_ref[:, hs], dstate[h]))
            dstate[h] = ds
            dqk_ref[:, hs] = dq
            dqk_ref[:, ks] = dk
            du_ref[:, hs] = du
            dw_ref[:, hs] = dw
            dgb = dgb + jnp.where(lane == h, dg, 0.0)
        dgb_ref[...] = dgb

    return pl.pallas_call(
        body, grid=(n,), name="dn_scan_bwd",
        in_specs=[pl.BlockSpec((DN_C, 2 * DN_W), rev), pl.BlockSpec((DN_C, 128), rev),
                  pl.BlockSpec((DN_C, DN_W), rev), pl.BlockSpec((DN_C, DN_W), rev),
                  pl.BlockSpec((1, DN_H, DN_D, DN_D), lambda i: (n - 1 - i, 0, 0, 0)),
                  pl.BlockSpec((DN_C, DN_W), rev)],
        out_specs=[pl.BlockSpec((DN_C, 2 * DN_W), rev), pl.BlockSpec((DN_C, DN_W), rev),
                   pl.BlockSpec((DN_C, DN_W), rev), pl.BlockSpec((DN_C, 128), rev)],
        out_shape=[SDS((t, 2 * DN_W), f32), SDS((t, DN_W), f32), SDS((t, DN_W), f32), SDS((t, 128), f32)],
        scratch_shapes=[pltpu.VMEM((DN_H, DN_D, DN_D), f32)],
        compiler_params=_cp(("arbitrary",)),
    )(qkv, gb, u, w, states, do)


def _sw_attn_fwd(q2, k2, v2, d):
    l = q2.shape[0]
    nb = l // SW_B
    cur = pl.BlockSpec((SW_B, SW_W), lambda r, n: (n, r))
    prev = pl.BlockSpec((SW_B, SW_W), lambda r, n: (jnp.maximum(n - 1, 0), r))

    def body(q_ref, kp_ref, kc_ref, vp_ref, vc_ref, o_ref, l_ref):
        o, lse = _f_attn(q_ref[...], kp_ref[...], kc_ref[...], vp_ref[...], vc_ref[...], pl.program_id(1) > 0)
        o_ref[...] = o
        l_ref[...] = lse

    return pl.pallas_call(
        body, grid=(d, nb), name=f"sw_attn_fwd_d{d}",
        in_specs=[cur, prev, cur, prev, cur], out_specs=[cur, cur],
        out_shape=[SDS(q2.shape, f32)] * 2, compiler_params=_cp(("arbitrary", "arbitrary")),
    )(q2, k2, k2, v2, v2)


def _sw_attn_bwd(q2, k2, v2, do2, dl2, d):
    l = q2.shape[0]
    nb = l // SW_B
    clamp = lambda n: jnp.minimum(n, nb - 1)
    cur = pl.BlockSpec((SW_B, SW_W), lambda r, n: (clamp(n), r))
    prev = pl.BlockSpec((SW_B, SW_W), lambda r, n: (jnp.maximum(clamp(n) - 1, 0), r))
    lag = pl.BlockSpec((SW_B, SW_W), lambda r, n: (jnp.maximum(n - 1, 0), r))

    def body(q_ref, kp_ref, kc_ref, vp_ref, vc_ref, do_ref, dl_ref, dq_ref, dk_ref, dv_ref, dk_hold, dv_hold):
        n = pl.program_id(1)

        @pl.when(n < nb)
        def _():
            has_prev = n > 0
            _, vjp = jax.vjp(lambda q, kp, kc, vp, vc: _f_attn(q, kp, kc, vp, vc, has_prev),
                             q_ref[...], kp_ref[...], kc_ref[...], vp_ref[...], vc_ref[...])
            dq, dkp, dkc, dvp, dvc = vjp((do_ref[...], dl_ref[...]))
            dq_ref[...] = dq
            dk_ref[...] = dk_hold[...] + dkp
            dv_ref[...] = dv_hold[...] + dvp
            dk_hold[...] = dkc
            dv_hold[...] = dvc

        @pl.when(n == nb)
        def _():
            dk_ref[...] = dk_hold[...]
            dv_ref[...] = dv_hold[...]

    return pl.pallas_call(
        body, grid=(d, nb + 1), name=f"sw_attn_bwd_d{d}",
        in_specs=[cur, prev, cur, prev, cur, cur, cur], out_specs=[cur, lag, lag],
        out_shape=[SDS(q2.shape, f32)] * 3,
        scratch_shapes=[pltpu.VMEM((SW_B, SW_W), f32)] * 2,
        compiler_params=_cp(("arbitrary", "arbitrary")),
    )(q2, k2, k2, v2, v2, do2, dl2)


def _loss_head(y, target, tt):
    t = y.shape[0]

    def body(y_ref, t_ref, dy_ref, l_ref):
        @pl.when(pl.program_id(0) == 0)
        def _():
            l_ref[...] = jnp.zeros(l_ref.shape, f32)

        err = y_ref[...] - t_ref[...]
        dy_ref[...] = err * (1.0 / D)
        l_ref[...] += 0.5 * jnp.sum(jnp.sum(err * err, axis=1, keepdims=True) * (1.0 / D), axis=0, keepdims=True)

    return pl.pallas_call(
        body, grid=(t // tt,), name="loss_head",
        in_specs=[pl.BlockSpec((tt, D), lambda i: (i, 0))] * 2,
        out_specs=[pl.BlockSpec((tt, D), lambda i: (i, 0)), pl.BlockSpec((8, 128), lambda i: (0, 0))],
        out_shape=[SDS((t, D), f32), SDS((8, 128), f32)],
        compiler_params=_cp(("arbitrary",)),
    )(y, target)


def _adamw(name, w, m, v, gparts, tr):
    r, c = w.shape
    p = gparts.shape[0]
    assert r % tr == 0, (name, w.shape, tr)

    def body(w_ref, m_ref, v_ref, g_ref, go_ref, d_ref, mo_ref, vo_ref):
        g = g_ref[0].astype(f32)
        for k in range(1, p):
            g = g + g_ref[k].astype(f32)
        wv = w_ref[...]
        mn = ADAM_B1 * m_ref[...] + (1.0 - ADAM_B1) * g
        vn = ADAM_B2 * v_ref[...] + (1.0 - ADAM_B2) * jnp.square(g)
        m_hat = mn / (1.0 - ADAM_B1 ** ADAM_STEP)
        v_hat = vn / (1.0 - ADAM_B2 ** ADAM_STEP)
        go_ref[...] = g
        d_ref[...] = -ADAM_LR * (m_hat / (jnp.sqrt(v_hat) + ADAM_EPS) + ADAM_WD * wv)
        mo_ref[...] = mn
        vo_ref[...] = vn

    spec = pl.BlockSpec((tr, c), lambda i: (i, 0))
    return pl.pallas_call(
        body, grid=(r // tr,), name=name,
        in_specs=[spec, spec, spec, pl.BlockSpec((p, tr, c), lambda i: (0, i, 0))],
        out_specs=[spec] * 4, out_shape=[SDS((r, c), f32)] * 4,
        compiler_params=_cp(("arbitrary",)),
    )(w, m, v, gparts)


def _mod_cols(c_all, w_mod, b_cols):
    nl, _, wc = w_mod.shape

    def body(c_ref, w_ref, b_ref, o_ref):
        o_ref[0] = _raw_dot(jax.nn.silu(c_ref[...]), w_ref[0], "nn", True) + b_ref[0]

    return pl.pallas_call(
        body, grid=(nl,), name="mod_cols",
        in_specs=[pl.BlockSpec((NDEV, D), lambda l: (0, 0)), pl.BlockSpec((1, D, wc), lambda l: (l, 0, 0)),
                  pl.BlockSpec((1, 1, wc), lambda l: (l, 0, 0))],
        out_specs=pl.BlockSpec((1, NDEV, wc), lambda l: (l, 0, 0)),
        out_shape=SDS((nl, NDEV, wc), f32), compiler_params=_cp(("arbitrary",)),
    )(c_all, w_mod, b_cols)


def _wmod_grad(c_all, dmod_cols):
    nl, _, wc = dmod_cols.shape

    def body(c_ref, d_ref, o_ref):
        o_ref[0, 0] = _raw_dot(jax.nn.silu(c_ref[...]), d_ref[0], "tn", True)

    return pl.pallas_call(
        body, grid=(nl,), name="wmod_grad",
        in_specs=[pl.BlockSpec((NDEV, D), lambda l: (0, 0)), pl.BlockSpec((1, NDEV, wc), lambda l: (l, 0, 0))],
        out_specs=pl.BlockSpec((1, 1, D, wc), lambda l: (0, l, 0, 0)),
        out_shape=SDS((1, nl, D, wc), f32), compiler_params=_cp(("arbitrary",)),
    )(c_all, dmod_cols)


def _me_and_peers():
    x, y, c = (lax.axis_index(a) for a in AXES)
    peers = []
    for k in range(1, NDEV):
        px = 1 - x if (k >> 2) & 1 else x
        py = 1 - y if (k >> 1) & 1 else y
        pc = 1 - c if k & 1 else c
        peers.append(((px, py, pc), 4 * px + 2 * py + pc))
    return 4 * x + 2 * y + c, peers


_ANY = pl.BlockSpec(memory_space=pl.ANY)


def _all_gather(name, a):
    def body(a_ref, o_ref, send_sems, recv_sems, local_sem):
        me, peers = _me_and_peers()
        mine = pltpu.make_async_copy(a_ref, o_ref.at[me], local_sem)
        mine.start()
        copies = [pltpu.make_async_remote_copy(a_ref, o_ref.at[me], send_sems.at[k], recv_sems.at[k],
                                               device_id=dev, device_id_type=pl.DeviceIdType.MESH)
                  for k, (dev, _) in enumerate(peers)]
        for cp in copies:
            cp.start()
        for cp in copies:
            cp.wait()
        mine.wait()

    return pl.pallas_call(
        body, name=name, in_specs=[_ANY], out_specs=_ANY, out_shape=SDS((NDEV,) + a.shape, a.dtype),
        scratch_shapes=[pltpu.SemaphoreType.DMA((NDEV - 1,)), pltpu.SemaphoreType.DMA((NDEV - 1,)), pltpu.SemaphoreType.DMA],
        compiler_params=pltpu.CompilerParams(has_side_effects=True),
    )(a)


def _all_to_all(name, a):
    def body(a_ref, o_ref, send_sems, recv_sems, local_sem):
        me, peers = _me_and_peers()
        mine = pltpu.make_async_copy(a_ref.at[me], o_ref.at[me], local_sem)
        mine.start()
        copies = [pltpu.make_async_remote_copy(a_ref.at[pid], o_ref.at[me], send_sems.at[k], recv_sems.at[k],
                                               device_id=dev, device_id_type=pl.DeviceIdType.MESH)
                  for k, (dev, pid) in enumerate(peers)]
        for cp in copies:
            cp.start()
        for cp in copies:
            cp.wait()
        mine.wait()

    return pl.pallas_call(
        body, name=name, in_specs=[_ANY], out_specs=_ANY, out_shape=SDS(a.shape, a.dtype),
        scratch_shapes=[pltpu.SemaphoreType.DMA((NDEV - 1,)), pltpu.SemaphoreType.DMA((NDEV - 1,)), pltpu.SemaphoreType.DMA],
        compiler_params=pltpu.CompilerParams(has_side_effects=True),
    )(a)


CB_Z = 3
CB_GM = 8
CB_SW = 10
CB_AB = 38


def _to_sub(a, d):
    return a.reshape(a.shape[0] // d, d * a.shape[1])


def _from_sub(a, d):
    return a.reshape(a.shape[0] * d, a.shape[1] // d)


def _sw_pre_tiles(proj, cs, sn):
    return [(proj, SW_W, CB_SW + 3 * g + j) for g in range(3) for j in range(2)] + [(cs, SW_W, 0), (sn, SW_W, 0)]


def _layer_fwd(x, p, tabs):
    cs, sn = tabs
    sh1, sc1, g1, sh2, sc2, g2 = (p["mod"][k] for k in range(6))
    (h1,) = _ew_fwd("modnorm1_fwd", _f_modnorm, [(x, D, 0)], [p["mixw"], sc1, sh1], [(D, bf16)], 512)
    proj = _mm("in_proj", h1, p["wi"], "nn", 512, 1664)
    y = _conv_fwd(proj, p["conv"], 256)
    qkv, gb = _ew_fwd("dn_act_fwd", _f_dn_act, [(y, CONV_W, 0), (proj, 128, CB_AB)], [p["alog"], p["dtb"]],
                      [(CONV_W, f32), (128, f32)], 256)
    u, w = _dn_prep_fwd(qkv, gb)
    o, states = _dn_scan_fwd(qkv, gb, u, w)
    (ya,) = _ew_fwd("dn_out_fwd", _f_dn_out, [(o, DN_W, 0), (proj, DN_W, CB_Z)], [p["wn"]], [(DN_W, bf16)], 256)
    (yb,) = _ew_fwd("gm_fwd", _f_gm, [(proj, GM_W, CB_GM), (proj, GM_W, CB_GM + 1)],
                    [p["lng"], p["lnb"], p["ws"], p["bst"]], [(GM_W, bf16)], GM_C)
    qk = _ew_fwd("sw_pre_fwd", _f_sw_pre, _sw_pre_tiles(proj, cs, sn), [p["wq"], p["wk"]], [(SW_W, f32)] * 6, 512)
    ol = []
    for g, d in enumerate(SW_DIL):
        c0 = (CB_SW + 3 * g + 2) * SW_W
        vg = proj[:, c0:c0 + SW_W]
        o2, l2 = _sw_attn_fwd(_to_sub(qk[2 * g], d), _to_sub(qk[2 * g + 1], d), _to_sub(vg, d), d)
        ol += [_from_sub(o2, d), _from_sub(l2, d)]
    (yc,) = _ew_fwd("sw_merge_fwd", _f_sw_merge, [(a, SW_W, 0) for a in ol], [], [(SW_W, bf16)], 512)
    ycat = jnp.concatenate([ya, yb, yc], axis=1)
    m1 = _mm("out_proj", ycat, p["wo"], "nn", 512, 1024)
    (x2,) = _ew_fwd("resid1_fwd", _f_resid, [(x, D, 0), (m1, D, 0)], [g1], [(D, f32)], 512)
    (h2,) = _ew_fwd("modnorm2_fwd", _f_modnorm, [(x2, D, 0)], [p["ffnw"], sc2, sh2], [(D, bf16)], 512)
    gu = _mm("ffn_in", h2, p["wfi"], "nn", 512, 1408)
    (act,) = _ew_fwd("swiglu_fwd", _f_swiglu, [(gu, FFN, 0), (gu, FFN, 1)], [], [(FFN, bf16)], 256)
    m2 = _mm("ffn_out", act, p["wfo"], "nn", 512, 1024)
    (x3,) = _ew_fwd("resid2_fwd", _f_resid, [(x2, D, 0), (m2, D, 0)], [g2], [(D, f32)], 512)
    res = dict(x=x, h1=h1, proj=proj, y=y, qkv=qkv, gb=gb, u=u, w=w, states=states, o=o, qk=list(qk), ol=ol,
               ycat=ycat, m1=m1, x2=x2, h2=h2, gu=gu, act=act, m2=m2)
    return x3, res


def _layer_bwd(dx3, p, r, tabs):
    cs, sn = tabs
    sh1, sc1, g1, sh2, sc2, g2 = (p["mod"][k] for k in range(6))
    proj = r["proj"]
    (dx2a, dm2), (dg2,) = _ew_bwd("resid2_bwd", _f_resid, [(r["x2"], D, 0), (r["m2"], D, 0)], [g2], [(dx3, D, 0)],
                                  [(0, f32), (1, bf16)], 512)
    dact = _mm("ffn_out_dx", dm2, p["wfo"], "nt", 512, 1408)
    dwfo = _mm("ffn_out_dw", r["act"], dm2, "tn", 256, 1024)
    (dgg, dgu), _ = _ew_bwd("swiglu_bwd", _f_swiglu, [(r["gu"], FFN, 0), (r["gu"], FFN, 1)], [], [(dact, FFN, 0)],
                            [(0, bf16), (1, bf16)], 256)
    dgu_cat = jnp.concatenate([dgg, dgu], axis=1)
    dh2 = _mm("ffn_in_dx", dgu_cat, p["wfi"], "nt", 512, 512)
    dwfi = _mm("ffn_in_dw", r["h2"], dgu_cat, "tn", 512, 512)
    (dx2,), (dffnw, dsc2, dsh2) = _ew_bwd("modnorm2_bwd", _f_modnorm, [(r["x2"], D, 0)], [p["ffnw"], sc2, sh2],
                                          [(dh2, D, 0)], [(0, f32)], 512, adds=[dx2a])
    (dxa, dm1), (dg1,) = _ew_bwd("resid1_bwd", _f_resid, [(r["x"], D, 0), (r["m1"], D, 0)], [g1], [(dx2, D, 0)],
                                 [(0, f32), (1, bf16)], 512)
    dycat = _mm("out_proj_dx", dm1, p["wo"], "nt", 512, 1024)
    dwo = _mm("out_proj_dw", r["ycat"], dm1, "tn", 512, 512)
    dol, _ = _ew_bwd("sw_merge_bwd", _f_sw_merge, [(a, SW_W, 0) for a in r["ol"]], [], [(dycat, SW_W, 3)],
                     [(k, f32) for k in range(6)], 512)
    dqk, dvs = [], []
    for g, d in enumerate(SW_DIL):
        c0 = (CB_SW + 3 * g + 2) * SW_W
        vg = proj[:, c0:c0 + SW_W]
        dq2, dk2, dv2 = _sw_attn_bwd(_to_sub(r["qk"][2 * g], d), _to_sub(r["qk"][2 * g + 1], d), _to_sub(vg, d),
                                     _to_sub(dol[2 * g], d), _to_sub(dol[2 * g + 1], d), d)
        dqk += [_from_sub(dq2, d), _from_sub(dk2, d)]
        dvs.append(_from_sub(dv2, d).astype(bf16))
    dqk_raw, (dwq, dwk) = _ew_bwd("sw_pre_bwd", _f_sw_pre, _sw_pre_tiles(proj, cs, sn), [p["wq"], p["wk"]],
                                  [(a, SW_W, 0) for a in dqk], [(k, bf16) for k in range(6)], 512)
    (dgm_u, dgm_v), (dlng, dlnb, dws, dbst) = _ew_bwd(
        "gm_bwd", _f_gm, [(proj, GM_W, CB_GM), (proj, GM_W, CB_GM + 1)], [p["lng"], p["lnb"], p["ws"], p["bst"]],
        [(dycat, GM_W, 2)], [(0, bf16), (1, bf16)], GM_C)
    (do, dz), (dwn,) = _ew_bwd("dn_out_bwd", _f_dn_out, [(r["o"], DN_W, 0), (proj, DN_W, CB_Z)], [p["wn"]],
                               [(dycat, DN_W, 0)], [(0, f32), (1, bf16)], 256)
    dqk1, du, dw, dgb1 = _dn_scan_bwd(r["qkv"], r["gb"], r["u"], r["w"], r["states"], do)
    dqkv, dgb = _dn_prep_bwd(r["qkv"], r["gb"], du, dw, dqk1, dgb1)
    (dy, dab), (dalog, ddtb) = _ew_bwd("dn_act_bwd", _f_dn_act, [(r["y"], CONV_W, 0), (proj, 128, CB_AB)],
                                       [p["alog"], p["dtb"]], [(dqkv, CONV_W, 0), (dgb, 128, 0)],
                                       [(0, f32), (1, bf16)], 256)
    dxc, dconv = _conv_bwd(proj, dy, p["conv"], 256)
    dproj = jnp.concatenate([dxc, dz, dgm_u, dgm_v, dqk_raw[0], dqk_raw[1], dvs[0], dqk_raw[2], dqk_raw[3], dvs[1],
                             dqk_raw[4], dqk_raw[5], dvs[2], dab], axis=1)
    dh1 = _mm("in_proj_dx", dproj, p["wi"], "nt", 512, 512)
    dwi = _mm("in_proj_dw", r["h1"], dproj, "tn", 512, 384)
    (dx,), (dmixw, dsc1, dsh1) = _ew_bwd("modnorm1_bwd", _f_modnorm, [(r["x"], D, 0)], [p["mixw"], sc1, sh1],
                                         [(dh1, D, 0)], [(0, f32)], 512, adds=[dxa])
    grads = dict(wi=dwi, wo=dwo, wfi=dwfi, wfo=dwfo, conv=dconv, mixw=dmixw, ffnw=dffnw,
                 mod=jnp.stack([dsh1, dsc1, dg1, dsh2, dsc2, dg2]), alog=dalog, dtb=ddtb, wn=dwn, lng=dlng, lnb=dlnb,
                 ws=dws, bst=dbst, wq=dwq, wk=dwk)
    return dx, grads


def _rope_tables(t):
    inv = ROPE_THETA ** (-jnp.arange(0, ROPE_DIM, 2, dtype=f32) / ROPE_DIM)
    ang = jnp.arange(t, dtype=f32)[:, None] * inv[None, :]
    cos, sin = jnp.cos(ang), jnp.sin(ang)
    rest = SW_D - ROPE_DIM
    ch = jnp.concatenate([cos, cos, jnp.ones((t, rest), f32)], axis=1)
    sh = jnp.concatenate([sin, sin, jnp.zeros((t, rest), f32)], axis=1)
    return jnp.tile(ch, (1, SW_H)), jnp.tile(sh, (1, SW_H))


def _pad_last(a, n):
    return jnp.pad(a, [(0, 0)] * (a.ndim - 1) + [(0, n - a.shape[-1])])


def _gather_cols(name, w, pad_to):
    nl, r, cs = w.shape
    g = _all_gather(name, _pad_last(w, pad_to))
    return jnp.transpose(g, (1, 2, 0, 3))[..., :cs].reshape(nl, r, NDEV * cs)


def _gather_rows(name, w):
    nl, rs, c = w.shape
    return jnp.transpose(_all_gather(name, w), (1, 0, 2, 3)).reshape(nl, NDEV * rs, c)


def _scatter_cols(name, g, cs, pad_to):
    nl, r, _ = g.shape
    parts = jnp.transpose(_pad_last(g.reshape(nl, r, NDEV, cs), pad_to), (2, 0, 1, 3))
    return _all_to_all(name, parts).reshape(NDEV, nl * r, pad_to)


def _scatter_rows(name, g, rs):
    nl, _, c = g.shape
    parts = jnp.transpose(g.reshape(nl, NDEV, rs, c), (1, 0, 2, 3))
    return _all_to_all(name, parts).reshape(NDEV, nl * rs, c)


_SMALL = ("b_mod", "mix_norm_w", "ffn_norm_w", "dn_a_log", "dn_dt_bias", "dn_out_norm_w", "gm_ln_g", "gm_ln_b",
          "gm_w_s", "gm_b_s", "sw_q_norm_w", "sw_k_norm_w")
PACK_TILE = 800 * 128


def _pack(arrs):
    flat = jnp.concatenate([a.reshape(-1) for a in arrs])
    n = -(-flat.shape[0] // PACK_TILE) * PACK_TILE
    return jnp.pad(flat, (0, n - flat.shape[0])).reshape(n // 128, 128)


def _unpack(buf, like):
    flat, out, off = buf.reshape(-1), [], 0
    for a in like:
        out.append(flat[off:off + a.size].reshape(a.shape))
        off += a.size
    return out


def kernel(x, c, w_mod, b_mod, mix_norm_w, ffn_norm_w, w_in, w_out, dn_conv_w, dn_a_log, dn_dt_bias, dn_out_norm_w, gm_ln_g, gm_ln_b, gm_w_s, gm_b_s, sw_q_norm_w, sw_k_norm_w, w_ffn_in, w_ffn_out, loss_target, m_w_mod, m_b_mod, m_mix_norm_w, m_ffn_norm_w, m_w_in, m_w_out, m_dn_conv_w, m_dn_a_log, m_dn_dt_bias, m_dn_out_norm_w, m_gm_ln_g, m_gm_ln_b, m_gm_w_s, m_gm_b_s, m_sw_q_norm_w, m_sw_k_norm_w, m_w_ffn_in, m_w_ffn_out, v_w_mod, v_b_mod, v_mix_norm_w, v_ffn_norm_w, v_w_in, v_w_out, v_dn_conv_w, v_dn_a_log, v_dn_dt_bias, v_dn_out_norm_w, v_gm_ln_g, v_gm_ln_b, v_gm_w_s, v_gm_b_s, v_sw_q_norm_w, v_sw_k_norm_w, v_w_ffn_in, v_w_ffn_out):
    weights = dict(w_mod=w_mod, b_mod=b_mod, mix_norm_w=mix_norm_w, ffn_norm_w=ffn_norm_w, w_in=w_in, w_out=w_out,
                   dn_conv_w=dn_conv_w, dn_a_log=dn_a_log, dn_dt_bias=dn_dt_bias, dn_out_norm_w=dn_out_norm_w,
                   gm_ln_g=gm_ln_g, gm_ln_b=gm_ln_b, gm_w_s=gm_w_s, gm_b_s=gm_b_s, sw_q_norm_w=sw_q_norm_w,
                   sw_k_norm_w=sw_k_norm_w, w_ffn_in=w_ffn_in, w_ffn_out=w_ffn_out)
    mom = dict(w_mod=m_w_mod, b_mod=m_b_mod, mix_norm_w=m_mix_norm_w, ffn_norm_w=m_ffn_norm_w, w_in=m_w_in,
               w_out=m_w_out, dn_conv_w=m_dn_conv_w, dn_a_log=m_dn_a_log, dn_dt_bias=m_dn_dt_bias,
               dn_out_norm_w=m_dn_out_norm_w, gm_ln_g=m_gm_ln_g, gm_ln_b=m_gm_ln_b, gm_w_s=m_gm_w_s, gm_b_s=m_gm_b_s,
               sw_q_norm_w=m_sw_q_norm_w, sw_k_norm_w=m_sw_k_norm_w, w_ffn_in=m_w_ffn_in, w_ffn_out=m_w_ffn_out)
    var = dict(w_mod=v_w_mod, b_mod=v_b_mod, mix_norm_w=v_mix_norm_w, ffn_norm_w=v_ffn_norm_w, w_in=v_w_in,
               w_out=v_w_out, dn_conv_w=v_dn_conv_w, dn_a_log=v_dn_a_log, dn_dt_bias=v_dn_dt_bias,
               dn_out_norm_w=v_dn_out_norm_w, gm_ln_g=v_gm_ln_g, gm_ln_b=v_gm_ln_b, gm_w_s=v_gm_w_s, gm_b_s=v_gm_b_s,
               sw_q_norm_w=v_sw_q_norm_w, sw_k_norm_w=v_sw_k_norm_w, w_ffn_in=v_w_ffn_in, w_ffn_out=v_w_ffn_out)
    names = list(weights)
    xi, tgt = x[0], loss_target[0]
    t = xi.shape[0]
    nl = w_mod.shape[0]
    ax, ay, ac = (lax.axis_index(a) for a in AXES)
    me = 4 * ax + 2 * ay + ac
    mod_cs = w_mod.shape[2]

    c_all = _all_gather("ag_c", jnp.broadcast_to(c, (NDEV, D)))[:, 0, :]
    b_cols = lax.dynamic_slice_in_dim(b_mod, me * mod_cs, mod_cs, axis=1)[:, None, :]
    modc = _mod_cols(c_all, w_mod, b_cols)
    mod_tx = jnp.pad(jnp.transpose(modc, (1, 0, 2)), ((0, 0), (0, 8 - nl), (0, 0)))
    mod_rx = _all_to_all("a2a_mod", mod_tx)[:, :nl]
    mod = jnp.transpose(mod_rx, (1, 0, 2)).reshape(nl, 6, 1, D)

    wi = _gather_cols("ag_w_in", w_in.astype(bf16), 640)
    wi = jnp.concatenate([wi[..., :2048], wi[..., 2056:], wi[..., 2048:2056], jnp.zeros((nl, D, IN_WA - IN_W), bf16)], axis=-1)
    wo = _gather_rows("ag_w_out", w_out.astype(bf16))
    wfi = _gather_cols("ag_w_ffn_in", w_ffn_in.astype(bf16), 768)
    wfo = _gather_rows("ag_w_ffn_out", w_ffn_out.astype(bf16))
    conv = _gather_cols("ag_conv", dn_conv_w, 256)

    pad128 = lambda a: _pad_last(a, 128)[:, None, :]
    params = dict(
        mod=mod, wi=wi, wo=wo, wfi=wfi, wfo=wfo, conv=conv,
        mixw=mix_norm_w[:, None, :], ffnw=ffn_norm_w[:, None, :], alog=pad128(dn_a_log), dtb=pad128(dn_dt_bias),
        wn=dn_out_norm_w[:, None, :], lng=gm_ln_g[:, None, :], lnb=gm_ln_b[:, None, :], ws=gm_w_s,
        bst=_pad_last(jnp.transpose(gm_b_s, (0, 2, 1)), 128),
        wq=jnp.pad(sw_q_norm_w[:, None, :], ((0, 0), (0, 7), (0, 128 - SW_D))),
        wk=jnp.pad(sw_k_norm_w[:, None, :], ((0, 0), (0, 7), (0, 128 - SW_D))))
    tabs = _rope_tables(t)

    xf, res = lax.scan(lambda xc, p: _layer_fwd(xc, p, tabs), xi, params)
    dy, lpart = _loss_head(xf, tgt, 512)
    loss = lax.psum(lpart[0, 0], AXES)
    dxi, g = lax.scan(lambda dc, pr: _layer_bwd(dc, pr[0], pr[1], tabs), dy, (params, res), reverse=True)

    dmod = g["mod"].reshape(nl, 6 * D)
    small_g = dict(b_mod=dmod, mix_norm_w=g["mixw"][:, 0], ffn_norm_w=g["ffnw"][:, 0], dn_a_log=g["alog"][:, 0, :DN_H],
                   dn_dt_bias=g["dtb"][:, 0, :DN_H], dn_out_norm_w=g["wn"][:, 0], gm_ln_g=g["lng"][:, 0],
                   gm_ln_b=g["lnb"][:, 0], gm_w_s=g["ws"], gm_b_s=jnp.transpose(g["bst"][:, :, :GM_G], (0, 2, 1)),
                   sw_q_norm_w=g["wq"][:, 0, :SW_D], sw_k_norm_w=g["wk"][:, 0, :SW_D])
    parts = _all_gather("ag_small_grads", _pack([small_g[n] for n in _SMALL]))
    like = [weights[n] for n in _SMALL]
    sm = _adamw("adamw_small", _pack(like), _pack([mom[n] for n in _SMALL]), _pack([var[n] for n in _SMALL]), parts, 800)
    out = {n: vals for n, vals in zip(_SMALL, zip(*[_unpack(b, like) for b in sm]))}

    dmod_all = parts[:, :nl * 6 * D // 128, :].reshape(NDEV, nl, 6 * D)
    dmod_cols = jnp.transpose(lax.dynamic_slice_in_dim(dmod_all, me * mod_cs, mod_cs, axis=2), (1, 0, 2))
    gw_mod = _wmod_grad(c_all, dmod_cols).reshape(1, nl * D, mod_cs)
    big = dict(w_mod=(gw_mod, mod_cs, mod_cs))

    dwi = g["wi"]
    dwi = jnp.concatenate([dwi[..., :2048], dwi[..., 4864:4872], dwi[..., 2048:4864]], axis=-1)
    big["w_in"] = (_scatter_cols("a2a_w_in", dwi, w_in.shape[2], 640), w_in.shape[2], 640)
    big["w_ffn_in"] = (_scatter_cols("a2a_w_ffn_in", g["wfi"], w_ffn_in.shape[2], 768), w_ffn_in.shape[2], 768)
    big["dn_conv_w"] = (_scatter_cols("a2a_conv", g["conv"], dn_conv_w.shape[2], 256), dn_conv_w.shape[2], 256)
    big["w_out"] = (_scatter_rows("a2a_w_out", g["wo"], w_out.shape[1]), D, D)
    big["w_ffn_out"] = (_scatter_rows("a2a_w_ffn_out", g["wfo"], w_ffn_out.shape[1]), D, D)
    for n, (gp, cols, padc) in big.items():
        shp = weights[n].shape
        rows = gp.shape[1]
        prep = lambda a: _pad_last(a.reshape(rows, cols), padc)
        tr = 256 if rows % 256 == 0 else rows // 4 if rows % 32 == 0 else rows
        res4 = _adamw("adamw_" + n, prep(weights[n]), prep(mom[n]), prep(var[n]), gp, tr)
        out[n] = tuple(a[:, :cols].reshape(shp) for a in res4)

    return (loss, dxi[None], *[out[n][0] for n in names], *[out[n][1] for n in names],
            *[out[n][2] for n in names], *[out[n][3] for n in names])
```

```python
import functools
import math

import jax
import jax.numpy as jnp
from jax import lax
from jax.experimental import pallas as pl
from jax.experimental.pallas import tpu as pltpu

f32 = jnp.float32
bf16 = jnp.bfloat16
HI = lax.Precision.HIGH
AXES = ("x", "y", "c")
NDEV = 8
SDS = jax.ShapeDtypeStruct

D = 1024
NORM_EPS = 1e-6
DN_W, DN_H, DN_D, DN_C = 512, 4, 128, 64
GM_W, GM_G, GM_C = 256, 4, 128
SW_W, SW_H, SW_D, SW_B = 256, 4, 64, 128
SW_DIL = (1, 4, 16)
SW_SPAN = 128
ROPE_DIM, ROPE_THETA = 16, 500000.0
IN_W = 4872
IN_WA = 4992
FFN = 2816
ADAM_LR, ADAM_B1, ADAM_B2, ADAM_EPS, ADAM_WD, ADAM_STEP = 0.001, 0.9, 0.999, 1e-08, 0.01, 10

VMEM_LIMIT = 52 * 1024 * 1024


def _cp(sem=None):
    return pltpu.CompilerParams(vmem_limit_bytes=VMEM_LIMIT, dimension_semantics=sem)


_DIMS = {"nn": (((1,), (0,)), ((), ())), "nt": (((1,), (1,)), ((), ())), "tn": (((0,), (0,)), ((), ()))}


def _raw_dot(a, b, mode, hi):
    if hi:
        return lax.dot_general(a, b, _DIMS[mode], precision=HI, preferred_element_type=f32)
    return lax.dot_general(a.astype(bf16), b.astype(bf16), _DIMS[mode], preferred_element_type=f32)


@functools.partial(jax.custom_vjp, nondiff_argnums=(2, 3))
def _dot(a, b, mode, hi):
    return _raw_dot(a, b, mode, hi)


def _dot_fwd(a, b, mode, hi):
    return _raw_dot(a, b, mode, hi), (a, b)


def _dot_bwd(mode, hi, res, g):
    a, b = res
    if mode == "nn":
        return _raw_dot(g, b, "nt", hi), _raw_dot(a, g, "tn", hi)
    if mode == "nt":
        return _raw_dot(g, b, "nn", hi), _raw_dot(g, a, "tn", hi)
    return _raw_dot(b, g, "nt", hi), _raw_dot(a, g, "nn", hi)


_dot.defvjp(_dot_fwd, _dot_bwd)


def _iota(shape, dim):
    return lax.broadcasted_iota(jnp.int32, shape, dim)


def _f_modnorm(x, w, scale, shift):
    y = x * lax.rsqrt(jnp.mean(x * x, axis=-1, keepdims=True) + NORM_EPS) * w
    return (y * (1.0 + scale) + shift,)


def _f_resid(x, m, gate):
    return (x + gate * m,)


def _f_swiglu(g, u):
    return (jax.nn.silu(g) * u,)


def _softplus(x):
    return jnp.maximum(x, 0.0) + jnp.log1p(jnp.exp(-jnp.abs(x)))


def _f_dn_act(y, ab, alog, dtb):
    c = jax.nn.silu(y)
    parts = []
    for j in range(3 * DN_H):
        p = c[:, j * DN_D:(j + 1) * DN_D]
        if j < 2 * DN_H:
            p = p * lax.rsqrt(jnp.sum(p * p, axis=-1, keepdims=True) + NORM_EPS)
        parts.append(p)
    lane = _iota(ab.shape, 1)
    g = -jnp.exp(alog) * _softplus(ab + dtb)
    beta = jax.nn.sigmoid(ab)
    gb = jnp.where(lane < DN_H, g, jnp.where(lane < 2 * DN_H, beta, 0.0))
    return jnp.concatenate(parts, axis=1), gb


def _f_dn_out(o, z, wn):
    parts = []
    for h in range(DN_H):
        oh = o[:, h * DN_D:(h + 1) * DN_D]
        zh = z[:, h * DN_D:(h + 1) * DN_D]
        n = oh * lax.rsqrt(jnp.mean(oh * oh, axis=-1, keepdims=True) + NORM_EPS) * wn
        parts.append(n * jax.nn.silu(zh))
    return (jnp.concatenate(parts, axis=1),)


def _gelu(x):
    return 0.5 * x * (1.0 + lax.erf(x * (1.0 / math.sqrt(2.0))))


def _f_gm(u_raw, v_raw, ln_g, ln_b, w_s, b_st):
    u = _gelu(u_raw)
    v = _gelu(v_raw)
    mu = jnp.mean(v, axis=-1, keepdims=True)
    vc = v - mu
    var = jnp.mean(vc * vc, axis=-1, keepdims=True)
    v = vc * lax.rsqrt(var + NORM_EPS) * ln_g + ln_b
    r = _iota((GM_C, GM_C), 0)
    c = _iota((GM_C, GM_C), 1)
    grp = _iota((GM_C, GM_W), 1) // (GM_W // GM_G)
    expand = jnp.where(_iota((GM_C, GM_W), 0) == grp, 1.0, 0.0)
    sv = _dot(b_st, expand, "nn", True)
    for g in range(GM_G):
        wg = jnp.where(r >= c, w_s[g], 0.0)
        sv = sv + jnp.where(grp == g, _dot(wg, v, "nn", True), 0.0)
    return (u * sv,)


def _head_lanes(shape):
    return _iota(shape, 1) // SW_D


def _f_sw_pre(q0, k0, q1, k1, q2, k2, cs, sn, wq, wk):
    r = _iota((SW_W, SW_W), 0)
    c = _iota((SW_W, SW_W), 1)
    same_head = jnp.where(r // SW_D == c // SW_D, 1.0, 0.0)
    hc = c % SW_D
    half = ROPE_DIM // 2
    perm = jnp.where((hc < half) & (r == c + half), -1.0, jnp.where((hc >= half) & (hc < ROPE_DIM) & (r == c - half), 1.0, 0.0))
    tile = jnp.where((_iota((128, SW_W), 1) % SW_D == _iota((128, SW_W), 0)) & (_iota((128, SW_W), 0) < SW_D), 1.0, 0.0)
    wq_full = _dot(wq, tile, "nn", True)[0:1, :]
    wk_full = _dot(wk, tile, "nn", True)[0:1, :]

    def one(t, w):
        ms = _dot(t * t, same_head, "nn", True) * (1.0 / SW_D)
        n = t * lax.rsqrt(ms + NORM_EPS) * w
        return n * cs + _dot(n, perm, "nn", True) * sn

    return one(q0, wq_full), one(k0, wk_full), one(q1, wq_full), one(k1, wk_full), one(q2, wq_full), one(k2, wk_full)


def _f_sw_merge(o0, l0, o1, l1, o2, l2):
    m = jnp.maximum(jnp.maximum(l0, l1), l2)
    e0, e1, e2 = jnp.exp(l0 - m), jnp.exp(l1 - m), jnp.exp(l2 - m)
    return ((e0 * o0 + e1 * o1 + e2 * o2) / (e0 + e1 + e2),)


def _f_attn(q, kp, kc, vp, vc, has_prev):
    kk = jnp.concatenate([kp, kc], axis=0)
    vv = jnp.concatenate([vp, vc], axis=0)
    i = _iota((SW_B, 2 * SW_B), 0)
    j = _iota((SW_B, 2 * SW_B), 1)
    dist = i + SW_B - j
    valid = (dist >= 0) & (dist <= SW_SPAN) & ((j >= SW_B) | has_prev)
    hl = _head_lanes(q.shape)
    o = jnp.zeros(q.shape, f32)
    lse = jnp.zeros(q.shape, f32)
    for h in range(SW_H):
        qh = jnp.where(hl == h, q, 0.0)
        s = _dot(qh, kk, "nt", False) * (SW_D ** -0.5)
        s = jnp.where(valid, s, -1e30)
        m = jnp.max(s, axis=-1, keepdims=True)
        p = jnp.where(valid, jnp.exp(s - m), 0.0)
        l = jnp.sum(p, axis=-1, keepdims=True)
        oh = _dot(p, vv, "nn", False) / l
        o = o + jnp.where(hl == h, oh, 0.0)
        lse = lse + jnp.where(hl == h, m + jnp.log(l), 0.0)
    return o, lse


def _decay_mats(gcol):
    n = gcol.shape[0]
    r = _iota((n, n), 0)
    c = _iota((n, n), 1)
    low = jnp.where(r >= c, 1.0, 0.0)
    gb = jnp.broadcast_to(gcol, (n, n))
    gc_i = _dot(low, gb, "nn", True)
    gc_j = _dot(jnp.ones((n, n), f32), jnp.where(r <= c, gb, 0.0), "nn", True)
    diff = jnp.where(r >= c, gc_i - gc_j, 0.0)
    decay = jnp.where(r >= c, jnp.exp(diff), 0.0)
    gc = jnp.sum(jnp.where(c == 0, gc_i, 0.0), axis=1, keepdims=True)
    return gc, decay, r, c


def _f_dn_pre(k, v, gcol, beta):
    gc, decay, r, c = _decay_mats(gcol)
    kb = k * beta
    a = jnp.where(r > c, _dot(kb, k, "nt", True) * decay, 0.0)
    return a, jnp.concatenate([v * beta, kb * jnp.exp(gc)], axis=1)


def _inv_unit_lower(a):
    n = a.shape[0]
    eye = jnp.where(_iota((n, n), 0) == _iota((n, n), 1), 1.0, 0.0)
    inv = eye - a
    pw = a
    for _ in range(n.bit_length() - 2):
        pw = _raw_dot(pw, pw, "nn", True)
        inv = _raw_dot(inv, eye + pw, "nn", True)
    return inv


def _f_dn_scan(state, q, k, u, w, gcol):
    gc, decay, _, _ = _decay_mats(gcol)
    q = q * (DN_D ** -0.5)
    a_qk = _dot(q, k, "nt", True) * decay
    g_last = jnp.sum(gcol, axis=0, keepdims=True)
    q_dec = q * jnp.exp(gc)
    k_dec = k * jnp.exp(g_last - gc)
    v_new = u - _dot(w, state, "nn", True)
    o = _dot(q_dec, state, "nn", True) + _dot(a_qk, v_new, "nn", True)
    new_state = state * jnp.exp(g_last) + _dot(k_dec, v_new, "tn", True)
    return o, new_state


def _cspec(tt, w, cb):
    return pl.BlockSpec((tt, w), lambda i, cb=cb: (i, cb))


def _pspec(shape):
    nd = len(shape)
    return pl.BlockSpec(tuple(shape), lambda i, nd=nd: (0,) * nd)


def _ew_fwd(name, f, tiled, params, outs, tt):
    t = tiled[0][0].shape[0]
    nt, npar = len(tiled), len(params)

    def body(*refs):
        tv = [r[...].astype(f32) for r in refs[:nt]]
        pv = [r[...] for r in refs[nt:nt + npar]]
        res = f(*tv, *pv)
        for o, r in zip(refs[nt + npar:], res):
            o[...] = r.astype(o.dtype)

    res = pl.pallas_call(
        body, grid=(t // tt,), name=name,
        in_specs=[_cspec(tt, w, cb) for _, w, cb in tiled] + [_pspec(p.shape) for p in params],
        out_specs=[_cspec(tt, w, 0) for w, _ in outs],
        out_shape=[SDS((t, w), dt) for w, dt in outs],
        compiler_params=_cp(("arbitrary",)),
    )(*[a for a, _, _ in tiled], *params)
    return res


def _ew_bwd(name, f, tiled, params, cots, diff, tt, adds=None):
    t = tiled[0][0].shape[0]
    nt, npar, nc, nd = len(tiled), len(params), len(cots), len(diff)
    adds = adds or [None] * nd
    add_arrs = [a for a in adds if a is not None]
    na = len(add_arrs)
    dwidth = [tiled[k][1] for k, _ in diff]

    def body(*refs):
        tin = refs[:nt]
        pin = refs[nt:nt + npar]
        cin = refs[nt + npar:nt + npar + nc]
        ain = list(refs[nt + npar + nc:nt + npar + nc + na])
        dts = refs[nt + npar + nc + na:nt + npar + nc + na + nd]
        dps = refs[nt + npar + nc + na + nd:]
        tv = [r[...].astype(f32) for r in tin]
        pv = [r[...] for r in pin]

        def g(*dv):
            full = list(tv)
            for n_, (k, _) in enumerate(diff):
                full[k] = dv[n_]
            return tuple(f(*full, *dv[nd:]))

        _, vjp = jax.vjp(g, *[tv[k] for k, _ in diff], *pv)
        grads = vjp(tuple(c[...].astype(f32) for c in cin))
        for n_ in range(nd):
            val = grads[n_]
            if adds[n_] is not None:
                val = val + ain.pop(0)[...].astype(f32)
            dts[n_][...] = val.astype(dts[n_].dtype)

        @pl.when(pl.program_id(0) == 0)
        def _():
            for r in dps:
                r[...] = jnp.zeros(r.shape, f32)

        for r, gp in zip(dps, grads[nd:]):
            r[...] += gp

    res = pl.pallas_call(
        body, grid=(t // tt,), name=name,
        in_specs=[_cspec(tt, w, cb) for _, w, cb in tiled] + [_pspec(p.shape) for p in params]
        + [_cspec(tt, w, cb) for _, w, cb in cots] + [_cspec(tt, a.shape[1], 0) for a in add_arrs],
        out_specs=[_cspec(tt, w, 0) for w in dwidth] + [_pspec(p.shape) for p in params],
        out_shape=[SDS((t, w), dt) for w, (_, dt) in zip(dwidth, diff)] + [SDS(p.shape, f32) for p in params],
        compiler_params=_cp(("arbitrary",)),
    )(*[a for a, _, _ in tiled], *params, *[a for a, _, _ in cots], *add_arrs)
    return res[:nd], res[nd:]


def _mm(name, a, b, mode, tm, tn, out_dtype=f32):
    if mode == "nn":
        (m, k), (k2, n) = a.shape, b.shape
        a_spec = pl.BlockSpec((tm, k), lambda i, j: (i, 0))
        b_spec = pl.BlockSpec((k, tn), lambda i, j: (0, j))
    elif mode == "nt":
        (m, k), (n, k2) = a.shape, b.shape
        a_spec = pl.BlockSpec((tm, k), lambda i, j: (i, 0))
        b_spec = pl.BlockSpec((tn, k), lambda i, j: (j, 0))
    else:
        (k, m), (k2, n) = a.shape, b.shape
        a_spec = pl.BlockSpec((k, tm), lambda i, j: (0, i))
        b_spec = pl.BlockSpec((k, tn), lambda i, j: (0, j))
    assert k == k2 and m % tm == 0 and n % tn == 0, (name, a.shape, b.shape, mode)
    assert a.dtype == bf16 and b.dtype == bf16, name

    def body(a_ref, b_ref, o_ref):
        o_ref[...] = lax.dot_general(a_ref[...], b_ref[...], _DIMS[mode], preferred_element_type=f32).astype(o_ref.dtype)

    return pl.pallas_call(
        body, grid=(m // tm, n // tn), name=name,
        in_specs=[a_spec, b_spec], out_specs=pl.BlockSpec((tm, tn), lambda i, j: (i, j)),
        out_shape=SDS((m, n), out_dtype), compiler_params=_cp(("parallel", "parallel")),
    )(a, b)


CONV_K = 4
CONV_W = 3 * DN_W
HALO = 8


def _conv_fwd(proj, w, tt):
    t = proj.shape[0]
    nb8 = tt // HALO

    def body(x_ref, h_ref, w_ref, y_ref, xe):
        i = pl.program_id(0)
        xe[0:HALO, :] = jnp.where(i == 0, 0.0, h_ref[...])
        xe[HALO:, :] = x_ref[...]
        wv = w_ref[...]
        acc = jnp.zeros((tt, CONV_W), f32)
        for k in range(CONV_K):
            acc = acc + wv[k:k + 1, :] * xe[pl.ds(HALO - (CONV_K - 1) + k, tt), :]
        y_ref[...] = acc

    return pl.pallas_call(
        body, grid=(t // tt,), name="conv_fwd",
        in_specs=[pl.BlockSpec((tt, CONV_W), lambda i: (i, 0)),
                  pl.BlockSpec((HALO, CONV_W), lambda i: (jnp.maximum(i * nb8 - 1, 0), 0)),
                  _pspec(w.shape)],
        out_specs=pl.BlockSpec((tt, CONV_W), lambda i: (i, 0)),
        out_shape=SDS((t, CONV_W), f32),
        scratch_shapes=[pltpu.VMEM((tt + HALO, CONV_W), f32)],
        compiler_params=_cp(("arbitrary",)),
    )(proj, proj, w)


def _conv_bwd(proj, dy, w, tt):
    t = proj.shape[0]
    nb8 = tt // HALO
    last8 = t // HALO - 1
    nsteps = t // tt

    def body(x_ref, h_ref, dy_ref, n_ref, w_ref, dx_ref, dw_ref, xe, dye):
        i = pl.program_id(0)
        xe[0:HALO, :] = jnp.where(i == 0, 0.0, h_ref[...])
        xe[HALO:, :] = x_ref[...]
        dye[0:tt, :] = dy_ref[...]
        dye[tt:, :] = jnp.where(i == nsteps - 1, 0.0, n_ref[...])
        wv = w_ref[...]
        dyv = dy_ref[...]
        acc = jnp.zeros((tt, CONV_W), f32)

        @pl.when(i == 0)
        def _():
            dw_ref[...] = jnp.zeros(dw_ref.shape, f32)

        for k in range(CONV_K):
            acc = acc + wv[k:k + 1, :] * dye[pl.ds(CONV_K - 1 - k, tt), :]
            dw_ref[k:k + 1, :] += jnp.sum(dyv * xe[pl.ds(HALO - (CONV_K - 1) + k, tt), :], axis=0, keepdims=True)
        dx_ref[...] = acc.astype(dx_ref.dtype)

    return pl.pallas_call(
        body, grid=(nsteps,), name="conv_bwd",
        in_specs=[pl.BlockSpec((tt, CONV_W), lambda i: (i, 0)),
                  pl.BlockSpec((HALO, CONV_W), lambda i: (jnp.maximum(i * nb8 - 1, 0), 0)),
                  pl.BlockSpec((tt, CONV_W), lambda i: (i, 0)),
                  pl.BlockSpec((HALO, CONV_W), lambda i: (jnp.minimum((i + 1) * nb8, last8), 0)),
                  _pspec(w.shape)],
        out_specs=[pl.BlockSpec((tt, CONV_W), lambda i: (i, 0)), _pspec(w.shape)],
        out_shape=[SDS((t, CONV_W), bf16), SDS(w.shape, f32)],
        scratch_shapes=[pltpu.VMEM((tt + HALO, CONV_W), f32), pltpu.VMEM((tt + HALO, CONV_W), f32)],
        compiler_params=_cp(("arbitrary",)),
    )(proj, proj, dy, dy, w)


PREP_CHUNKS = 4


def _lane_col(gb, lane_idx):
    return jnp.sum(jnp.where(_iota(gb.shape, 1) == lane_idx, gb, 0.0), axis=1, keepdims=True)


def _dn_prep_fwd(qkv, gb):
    t = qkv.shape[0]
    rows = PREP_CHUNKS * DN_C

    def body(qkv_ref, gb_ref, u_ref, w_ref, inv_ref):
        inv_ref[...] = jnp.zeros(inv_ref.shape, f32)
        for ch in range(PREP_CHUNKS):
            rs = slice(ch * DN_C, (ch + 1) * DN_C)
            gbv = gb_ref[rs, :]
            for h in range(DN_H):
                k = qkv_ref[rs, DN_W + h * DN_D:DN_W + (h + 1) * DN_D]
                v = qkv_ref[rs, 2 * DN_W + h * DN_D:2 * DN_W + (h + 1) * DN_D]
                a, rhs = _f_dn_pre(k, v, _lane_col(gbv, h), _lane_col(gbv, DN_H + h))
                inv = _inv_unit_lower(a)
                uw = _raw_dot(inv, rhs, "nn", True)
                u_ref[rs, h * DN_D:(h + 1) * DN_D] = uw[:, :DN_D]
                w_ref[rs, h * DN_D:(h + 1) * DN_D] = uw[:, DN_D:]
                inv_ref[rs, h * DN_D:h * DN_D + DN_C] = inv

    return pl.pallas_call(
        body, grid=(t // rows,), name="dn_prep_fwd",
        in_specs=[pl.BlockSpec((rows, CONV_W), lambda i: (i, 0)), pl.BlockSpec((rows, 128), lambda i: (i, 0))],
        out_specs=[pl.BlockSpec((rows, DN_W), lambda i: (i, 0))] * 3,
        out_shape=[SDS((t, DN_W), f32)] * 3,
        compiler_params=_cp(("arbitrary",)),
    )(qkv, gb)


def _dn_prep_bwd(qkv, gb, inv_all, du, dw, dqk1, dgb1):
    t = qkv.shape[0]
    rows = PREP_CHUNKS * DN_C

    def body(qkv_ref, gb_ref, inv_ref, du_ref, dw_ref, dqk1_ref, dgb1_ref, dqkv_ref, dgb_ref):
        for ch in range(PREP_CHUNKS):
            rs = slice(ch * DN_C, (ch + 1) * DN_C)
            gbv = gb_ref[rs, :]
            lane = _iota(gbv.shape, 1)
            dgb = dgb1_ref[rs, :]
            for h in range(DN_H):
                ks = slice(DN_W + h * DN_D, DN_W + (h + 1) * DN_D)
                vs = slice(2 * DN_W + h * DN_D, 2 * DN_W + (h + 1) * DN_D)
                hs = slice(h * DN_D, (h + 1) * DN_D)
                (_, rhs), vjp = jax.vjp(_f_dn_pre, qkv_ref[rs, ks], qkv_ref[rs, vs], _lane_col(gbv, h),
                                        _lane_col(gbv, DN_H + h))
                inv = inv_ref[rs, h * DN_D:h * DN_D + DN_C]
                uw = _raw_dot(inv, rhs, "nn", True)
                drhs = _raw_dot(inv, jnp.concatenate([du_ref[rs, hs], dw_ref[rs, hs]], axis=1), "tn", True)
                da = -_raw_dot(drhs, uw, "nt", True)
                dk, dv, dg, db = vjp((da, drhs))
                dqkv_ref[rs, hs] = dqk1_ref[rs, hs]
                dqkv_ref[rs, ks] = dk + dqk1_ref[rs, ks]
                dqkv_ref[rs, vs] = dv
                dgb = dgb + jnp.where(lane == h, dg, 0.0) + jnp.where(lane == DN_H + h, db, 0.0)
            dgb_ref[rs, :] = dgb

    return pl.pallas_call(
        body, grid=(t // rows,), name="dn_prep_bwd",
        in_specs=[pl.BlockSpec((rows, CONV_W), lambda i: (i, 0)), pl.BlockSpec((rows, 128), lambda i: (i, 0)),
                  pl.BlockSpec((rows, DN_W), lambda i: (i, 0)),
                  pl.BlockSpec((rows, DN_W), lambda i: (i, 0)), pl.BlockSpec((rows, DN_W), lambda i: (i, 0)),
                  pl.BlockSpec((rows, 2 * DN_W), lambda i: (i, 0)), pl.BlockSpec((rows, 128), lambda i: (i, 0))],
        out_specs=[pl.BlockSpec((rows, CONV_W), lambda i: (i, 0)), pl.BlockSpec((rows, 128), lambda i: (i, 0))],
        out_shape=[SDS((t, CONV_W), f32), SDS((t, 128), f32)],
        compiler_params=_cp(("arbitrary",)),
    )(qkv, gb, inv_all, du, dw, dqk1, dgb1)


def _dn_scan_fwd(qkv, gb, u, w):
    t = qkv.shape[0]
    n = t // DN_C

    def body(qkv_ref, gb_ref, u_ref, w_ref, o_ref, s_ref, state):
        @pl.when(pl.program_id(0) == 0)
        def _():
            state[...] = jnp.zeros(state.shape, f32)

        gbv = gb_ref[...]
        for h in range(DN_H):
            hs = slice(h * DN_D, (h + 1) * DN_D)
            st = state[h]
            s_ref[0, h] = st
            o, new = _f_dn_scan(st, qkv_ref[:, hs], qkv_ref[:, DN_W + h * DN_D:DN_W + (h + 1) * DN_D],
                                u_ref[:, hs], w_ref[:, hs], _lane_col(gbv, h))
            o_ref[:, hs] = o
            state[h] = new

    return pl.pallas_call(
        body, grid=(n,), name="dn_scan_fwd",
        in_specs=[pl.BlockSpec((DN_C, 2 * DN_W), lambda i: (i, 0)), pl.BlockSpec((DN_C, 128), lambda i: (i, 0)),
                  pl.BlockSpec((DN_C, DN_W), lambda i: (i, 0)), pl.BlockSpec((DN_C, DN_W), lambda i: (i, 0))],
        out_specs=[pl.BlockSpec((DN_C, DN_W), lambda i: (i, 0)),
                   pl.BlockSpec((1, DN_H, DN_D, DN_D), lambda i: (i, 0, 0, 0))],
        out_shape=[SDS((t, DN_W), f32), SDS((n, DN_H, DN_D, DN_D), f32)],
        scratch_shapes=[pltpu.VMEM((DN_H, DN_D, DN_D), f32)],
        compiler_params=_cp(("arbitrary",)),
    )(qkv, gb, u, w)


def _dn_scan_bwd(qkv, gb, u, w, states, do):
    t = qkv.shape[0]
    n = t // DN_C
    rev = lambda i: (n - 1 - i, 0)

    def body(qkv_ref, gb_ref, u_ref, w_ref, s_ref, do_ref, dqk_ref, du_ref, dw_ref, dgb_ref, dstate):
        @pl.when(pl.program_id(0) == 0)
        def _():
            dstate[...] = jnp.zeros(dstate.shape, f32)

        gbv = gb_ref[...]
        lane = _iota(gbv.shape, 1)
        dgb = jnp.zeros(gbv.shape, f32)
        for h in range(DN_H):
            hs = slice(h * DN_D, (h + 1) * DN_D)
            ks = slice(DN_W + h * DN_D, DN_W + (h + 1) * DN_D)
            _, vjp = jax.vjp(_f_dn_scan, s_ref[0, h], qkv_ref[:, hs], qkv_ref[:, ks], u_ref[:, hs], w_ref[:, hs],
                             _lane_col(gbv, h))
            ds, dq, dk, du, dw, dg = vjp((do_ref[:, hs], dstate[h]))
            dstate[h] = ds
            dqk_ref[:, hs] = dq
            dqk_ref[:, ks] = dk
            du_ref[:, hs] = du
            dw_ref[:, hs] = dw
            dgb = dgb + jnp.where(lane == h, dg, 0.0)
        dgb_ref[...] = dgb

    return pl.pallas_call(
        body, grid=(n,), name="dn_scan_bwd",
        in_specs=[pl.BlockSpec((DN_C, 2 * DN_W), rev), pl.BlockSpec((DN_C, 128), rev),
                  pl.BlockSpec((DN_C, DN_W), rev), pl.BlockSpec((DN_C, DN_W), rev),
                  pl.BlockSpec((1, DN_H, DN_D, DN_D), lambda i: (n - 1 - i, 0, 0, 0)),
                  pl.BlockSpec((DN_C, DN_W), rev)],
        out_specs=[pl.BlockSpec((DN_C, 2 * DN_W), rev), pl.BlockSpec((DN_C, DN_W), rev),
                   pl.BlockSpec((DN_C, DN_W), rev), pl.BlockSpec((DN_C, 128), rev)],
        out_shape=[SDS((t, 2 * DN_W), f32), SDS((t, DN_W), f32), SDS((t, DN_W), f32), SDS((t, 128), f32)],
        scratch_shapes=[pltpu.VMEM((DN_H, DN_D, DN_D), f32)],
        compiler_params=_cp(("arbitrary",)),
    )(qkv, gb, u, w, states, do)


def _sw_attn_fwd(q2, k2, v2, d):
    l = q2.shape[0]
    nb = l // SW_B
    cur = pl.BlockSpec((SW_B, SW_W), lambda r, n: (n, r))
    prev = pl.BlockSpec((SW_B, SW_W), lambda r, n: (jnp.maximum(n - 1, 0), r))

    def body(q_ref, kp_ref, kc_ref, vp_ref, vc_ref, o_ref, l_ref):
        o, lse = _f_attn(q_ref[...], kp_ref[...], kc_ref[...], vp_ref[...], vc_ref[...], pl.program_id(1) > 0)
        o_ref[...] = o
        l_ref[...] = lse

    return pl.pallas_call(
        body, grid=(d, nb), name=f"sw_attn_fwd_d{d}",
        in_specs=[cur, prev, cur, prev, cur], out_specs=[cur, cur],
        out_shape=[SDS(q2.shape, f32)] * 2, compiler_params=_cp(("arbitrary", "arbitrary")),
    )(q2, k2, k2, v2, v2)


def _sw_attn_bwd(q2, k2, v2, do2, dl2, d):
    l = q2.shape[0]
    nb = l // SW_B
    clamp = lambda n: jnp.minimum(n, nb - 1)
    cur = pl.BlockSpec((SW_B, SW_W), lambda r, n: (clamp(n), r))
    prev = pl.BlockSpec((SW_B, SW_W), lambda r, n: (jnp.maximum(clamp(n) - 1, 0), r))
    lag = pl.BlockSpec((SW_B, SW_W), lambda r, n: (jnp.maximum(n - 1, 0), r))

    def body(q_ref, kp_ref, kc_ref, vp_ref, vc_ref, do_ref, dl_ref, dq_ref, dk_ref, dv_ref, dk_hold, dv_hold):
        n = pl.program_id(1)

        @pl.when(n < nb)
        def _():
            has_prev = n > 0
            _, vjp = jax.vjp(lambda q, kp, kc, vp, vc: _f_attn(q, kp, kc, vp, vc, has_prev),
                             q_ref[...], kp_ref[...], kc_ref[...], vp_ref[...], vc_ref[...])
            dq, dkp, dkc, dvp, dvc = vjp((do_ref[...], dl_ref[...]))
            dq_ref[...] = dq
            dk_ref[...] = dk_hold[...] + dkp
            dv_ref[...] = dv_hold[...] + dvp
            dk_hold[...] = dkc
            dv_hold[...] = dvc

        @pl.when(n == nb)
        def _():
            dk_ref[...] = dk_hold[...]
            dv_ref[...] = dv_hold[...]

    return pl.pallas_call(
        body, grid=(d, nb + 1), name=f"sw_attn_bwd_d{d}",
        in_specs=[cur, prev, cur, prev, cur, cur, cur], out_specs=[cur, lag, lag],
        out_shape=[SDS(q2.shape, f32)] * 3,
        scratch_shapes=[pltpu.VMEM((SW_B, SW_W), f32)] * 2,
        compiler_params=_cp(("arbitrary", "arbitrary")),
    )(q2, k2, k2, v2, v2, do2, dl2)


def _loss_head(y, target, tt):
    t = y.shape[0]

    def body(y_ref, t_ref, dy_ref, l_ref):
        @pl.when(pl.program_id(0) == 0)
        def _():
            l_ref[...] = jnp.zeros(l_ref.shape, f32)

        err = y_ref[...] - t_ref[...]
        dy_ref[...] = err * (1.0 / D)
        l_ref[...] += 0.5 * jnp.sum(jnp.sum(err * err, axis=1, keepdims=True) * (1.0 / D), axis=0, keepdims=True)

    return pl.pallas_call(
        body, grid=(t // tt,), name="loss_head",
        in_specs=[pl.BlockSpec((tt, D), lambda i: (i, 0))] * 2,
        out_specs=[pl.BlockSpec((tt, D), lambda i: (i, 0)), pl.BlockSpec((8, 128), lambda i: (0, 0))],
        out_shape=[SDS((t, D), f32), SDS((8, 128), f32)],
        compiler_params=_cp(("arbitrary",)),
    )(y, target)


def _adamw(name, w, m, v, gparts, tr):
    r, c = w.shape
    p = gparts.shape[0]
    assert r % tr == 0, (name, w.shape, tr)

    def body(w_ref, m_ref, v_ref, g_ref, go_ref, d_ref, mo_ref, vo_ref):
        g = g_ref[0].astype(f32)
        for k in range(1, p):
            g = g + g_ref[k].astype(f32)
        wv = w_ref[...]
        mn = ADAM_B1 * m_ref[...] + (1.0 - ADAM_B1) * g
        vn = ADAM_B2 * v_ref[...] + (1.0 - ADAM_B2) * jnp.square(g)
        m_hat = mn / (1.0 - ADAM_B1 ** ADAM_STEP)
        v_hat = vn / (1.0 - ADAM_B2 ** ADAM_STEP)
        go_ref[...] = g
        d_ref[...] = -ADAM_LR * (m_hat / (jnp.sqrt(v_hat) + ADAM_EPS) + ADAM_WD * wv)
        mo_ref[...] = mn
        vo_ref[...] = vn

    spec = pl.BlockSpec((tr, c), lambda i: (i, 0))
    return pl.pallas_call(
        body, grid=(r // tr,), name=name,
        in_specs=[spec, spec, spec, pl.BlockSpec((p, tr, c), lambda i: (0, i, 0))],
        out_specs=[spec] * 4, out_shape=[SDS((r, c), f32)] * 4,
        compiler_params=_cp(("arbitrary",)),
    )(w, m, v, gparts)


def _mod_cols(c_all, w_mod, b_cols):
    nl, _, wc = w_mod.shape

    def body(c_ref, w_ref, b_ref, o_ref):
        o_ref[0] = _raw_dot(jax.nn.silu(c_ref[...]), w_ref[0], "nn", True) + b_ref[0]

    return pl.pallas_call(
        body, grid=(nl,), name="mod_cols",
        in_specs=[pl.BlockSpec((NDEV, D), lambda l: (0, 0)), pl.BlockSpec((1, D, wc), lambda l: (l, 0, 0)),
                  pl.BlockSpec((1, 1, wc), lambda l: (l, 0, 0))],
        out_specs=pl.BlockSpec((1, NDEV, wc), lambda l: (l, 0, 0)),
        out_shape=SDS((nl, NDEV, wc), f32), compiler_params=_cp(("arbitrary",)),
    )(c_all, w_mod, b_cols)


def _wmod_grad(c_all, dmod_cols):
    nl, _, wc = dmod_cols.shape

    def body(c_ref, d_ref, o_ref):
        o_ref[0, 0] = _raw_dot(jax.nn.silu(c_ref[...]), d_ref[0], "tn", True)

    return pl.pallas_call(
        body, grid=(nl,), name="wmod_grad",
        in_specs=[pl.BlockSpec((NDEV, D), lambda l: (0, 0)), pl.BlockSpec((1, NDEV, wc), lambda l: (l, 0, 0))],
        out_specs=pl.BlockSpec((1, 1, D, wc), lambda l: (0, l, 0, 0)),
        out_shape=SDS((1, nl, D, wc), f32), compiler_params=_cp(("arbitrary",)),
    )(c_all, dmod_cols)


def _me_and_peers():
    x, y, c = (lax.axis_index(a) for a in AXES)
    peers = []
    for k in range(1, NDEV):
        px = 1 - x if (k >> 2) & 1 else x
        py = 1 - y if (k >> 1) & 1 else y
        pc = 1 - c if k & 1 else c
        peers.append(((px, py, pc), 4 * px + 2 * py + pc))
    return 4 * x + 2 * y + c, peers


_ANY = pl.BlockSpec(memory_space=pl.ANY)


def _all_gather(name, a):
    def body(a_ref, o_ref, send_sems, recv_sems, local_sem):
        me, peers = _me_and_peers()
        mine = pltpu.make_async_copy(a_ref, o_ref.at[me], local_sem)
        mine.start()
        copies = [pltpu.make_async_remote_copy(a_ref, o_ref.at[me], send_sems.at[k], recv_sems.at[k],
                                               device_id=dev, device_id_type=pl.DeviceIdType.MESH)
                  for k, (dev, _) in enumerate(peers)]
        for cp in copies:
            cp.start()
        for cp in copies:
            cp.wait()
        mine.wait()

    return pl.pallas_call(
        body, name=name, in_specs=[_ANY], out_specs=_ANY, out_shape=SDS((NDEV,) + a.shape, a.dtype),
        scratch_shapes=[pltpu.SemaphoreType.DMA((NDEV - 1,)), pltpu.SemaphoreType.DMA((NDEV - 1,)), pltpu.SemaphoreType.DMA],
        compiler_params=pltpu.CompilerParams(has_side_effects=True),
    )(a)


def _all_to_all(name, a):
    def body(a_ref, o_ref, send_sems, recv_sems, local_sem):
        me, peers = _me_and_peers()
        mine = pltpu.make_async_copy(a_ref.at[me], o_ref.at[me], local_sem)
        mine.start()
        copies = [pltpu.make_async_remote_copy(a_ref.at[pid], o_ref.at[me], send_sems.at[k], recv_sems.at[k],
                                               device_id=dev, device_id_type=pl.DeviceIdType.MESH)
                  for k, (dev, pid) in enumerate(peers)]
        for cp in copies:
            cp.start()
        for cp in copies:
            cp.wait()
        mine.wait()

    return pl.pallas_call(
        body, name=name, in_specs=[_ANY], out_specs=_ANY, out_shape=SDS(a.shape, a.dtype),
        scratch_shapes=[pltpu.SemaphoreType.DMA((NDEV - 1,)), pltpu.SemaphoreType.DMA((NDEV - 1,)), pltpu.SemaphoreType.DMA],
        compiler_params=pltpu.CompilerParams(has_side_effects=True),
    )(a)


CB_Z = 3
CB_GM = 8
CB_SW = 10
CB_AB = 38


def _to_sub(a, d):
    return a.reshape(a.shape[0] // d, d * a.shape[1])


def _from_sub(a, d):
    return a.reshape(a.shape[0] * d, a.shape[1] // d)


def _sw_pre_tiles(proj, cs, sn):
    return [(proj, SW_W, CB_SW + 3 * g + j) for g in range(3) for j in range(2)] + [(cs, SW_W, 0), (sn, SW_W, 0)]


def _layer_fwd(x, p, tabs):
    cs, sn = tabs
    sh1, sc1, g1, sh2, sc2, g2 = (p["mod"][k] for k in range(6))
    (h1,) = _ew_fwd("modnorm1_fwd", _f_modnorm, [(x, D, 0)], [p["mixw"], sc1, sh1], [(D, bf16)], 512)
    proj = _mm("in_proj", h1, p["wi"], "nn", 512, 1664)
    y = _conv_fwd(proj, p["conv"], 256)
    qkv, gb = _ew_fwd("dn_act_fwd", _f_dn_act, [(y, CONV_W, 0), (proj, 128, CB_AB)], [p["alog"], p["dtb"]],
                      [(CONV_W, f32), (128, f32)], 256)
    u, w, inv = _dn_prep_fwd(qkv, gb)
    o, states = _dn_scan_fwd(qkv, gb, u, w)
    (ya,) = _ew_fwd("dn_out_fwd", _f_dn_out, [(o, DN_W, 0), (proj, DN_W, CB_Z)], [p["wn"]], [(DN_W, bf16)], 256)
    (yb,) = _ew_fwd("gm_fwd", _f_gm, [(proj, GM_W, CB_GM), (proj, GM_W, CB_GM + 1)],
                    [p["lng"], p["lnb"], p["ws"], p["bst"]], [(GM_W, bf16)], GM_C)
    qk = _ew_fwd("sw_pre_fwd", _f_sw_pre, _sw_pre_tiles(proj, cs, sn), [p["wq"], p["wk"]], [(SW_W, f32)] * 6, 512)
    ol = []
    for g, d in enumerate(SW_DIL):
        c0 = (CB_SW + 3 * g + 2) * SW_W
        vg = proj[:, c0:c0 + SW_W]
        o2, l2 = _sw_attn_fwd(_to_sub(qk[2 * g], d), _to_sub(qk[2 * g + 1], d), _to_sub(vg, d), d)
        ol += [_from_sub(o2, d), _from_sub(l2, d)]
    (yc,) = _ew_fwd("sw_merge_fwd", _f_sw_merge, [(a, SW_W, 0) for a in ol], [], [(SW_W, bf16)], 512)
    ycat = jnp.concatenate([ya, yb, yc], axis=1)
    m1 = _mm("out_proj", ycat, p["wo"], "nn", 512, 1024)
    (x2,) = _ew_fwd("resid1_fwd", _f_resid, [(x, D, 0), (m1, D, 0)], [g1], [(D, f32)], 512)
    (h2,) = _ew_fwd("modnorm2_fwd", _f_modnorm, [(x2, D, 0)], [p["ffnw"], sc2, sh2], [(D, bf16)], 512)
    gu = _mm("ffn_in", h2, p["wfi"], "nn", 512, 1408)
    (act,) = _ew_fwd("swiglu_fwd", _f_swiglu, [(gu, FFN, 0), (gu, FFN, 1)], [], [(FFN, bf16)], 256)
    m2 = _mm("ffn_out", act, p["wfo"], "nn", 512, 1024)
    (x3,) = _ew_fwd("resid2_fwd", _f_resid, [(x2, D, 0), (m2, D, 0)], [g2], [(D, f32)], 512)
    res = dict(x=x, h1=h1, proj=proj, y=y, qkv=qkv, gb=gb, u=u, w=w, inv=inv, states=states, o=o, qk=list(qk), ol=ol,
               ycat=ycat, m1=m1, x2=x2, h2=h2, gu=gu, act=act, m2=m2)
    return x3, res


def _layer_bwd(dx3, p, r, tabs):
    cs, sn = tabs
    sh1, sc1, g1, sh2, sc2, g2 = (p["mod"][k] for k in range(6))
    proj = r["proj"]
    (dx2a, dm2), (dg2,) = _ew_bwd("resid2_bwd", _f_resid, [(r["x2"], D, 0), (r["m2"], D, 0)], [g2], [(dx3, D, 0)],
                                  [(0, f32), (1, bf16)], 512)
    dact = _mm("ffn_out_dx", dm2, p["wfo"], "nt", 512, 1408)
    dwfo = _mm("ffn_out_dw", r["act"], dm2, "tn", 256, 1024, bf16)
    (dgg, dgu), _ = _ew_bwd("swiglu_bwd", _f_swiglu, [(r["gu"], FFN, 0), (r["gu"], FFN, 1)], [], [(dact, FFN, 0)],
                            [(0, bf16), (1, bf16)], 256)
    dgu_cat = jnp.concatenate([dgg, dgu], axis=1)
    dh2 = _mm("ffn_in_dx", dgu_cat, p["wfi"], "nt", 512, 512)
    dwfi = _mm("ffn_in_dw", r["h2"], dgu_cat, "tn", 512, 512, bf16)
    (dx2,), (dffnw, dsc2, dsh2) = _ew_bwd("modnorm2_bwd", _f_modnorm, [(r["x2"], D, 0)], [p["ffnw"], sc2, sh2],
                                          [(dh2, D, 0)], [(0, f32)], 512, adds=[dx2a])
    (dxa, dm1), (dg1,) = _ew_bwd("resid1_bwd", _f_resid, [(r["x"], D, 0), (r["m1"], D, 0)], [g1], [(dx2, D, 0)],
                                 [(0, f32), (1, bf16)], 512)
    dycat = _mm("out_proj_dx", dm1, p["wo"], "nt", 512, 1024)
    dwo = _mm("out_proj_dw", r["ycat"], dm1, "tn", 512, 512, bf16)
    dol, _ = _ew_bwd("sw_merge_bwd", _f_sw_merge, [(a, SW_W, 0) for a in r["ol"]], [], [(dycat, SW_W, 3)],
                     [(k, f32) for k in range(6)], 512)
    dqk, dvs = [], []
    for g, d in enumerate(SW_DIL):
        c0 = (CB_SW + 3 * g + 2) * SW_W
        vg = proj[:, c0:c0 + SW_W]
        dq2, dk2, dv2 = _sw_attn_bwd(_to_sub(r["qk"][2 * g], d), _to_sub(r["qk"][2 * g + 1], d), _to_sub(vg, d),
                                     _to_sub(dol[2 * g], d), _to_sub(dol[2 * g + 1], d), d)
        dqk += [_from_sub(dq2, d), _from_sub(dk2, d)]
        dvs.append(_from_sub(dv2, d).astype(bf16))
    dqk_raw, (dwq, dwk) = _ew_bwd("sw_pre_bwd", _f_sw_pre, _sw_pre_tiles(proj, cs, sn), [p["wq"], p["wk"]],
                                  [(a, SW_W, 0) for a in dqk], [(k, bf16) for k in range(6)], 512)
    (dgm_u, dgm_v), (dlng, dlnb, dws, dbst) = _ew_bwd(
        "gm_bwd", _f_gm, [(proj, GM_W, CB_GM), (proj, GM_W, CB_GM + 1)], [p["lng"], p["lnb"], p["ws"], p["bst"]],
        [(dycat, GM_W, 2)], [(0, bf16), (1, bf16)], GM_C)
    (do, dz), (dwn,) = _ew_bwd("dn_out_bwd", _f_dn_out, [(r["o"], DN_W, 0), (proj, DN_W, CB_Z)], [p["wn"]],
                               [(dycat, DN_W, 0)], [(0, f32), (1, bf16)], 256)
    dqk1, du, dw, dgb1 = _dn_scan_bwd(r["qkv"], r["gb"], r["u"], r["w"], r["states"], do)
    dqkv, dgb = _dn_prep_bwd(r["qkv"], r["gb"], r["inv"], du, dw, dqk1, dgb1)
    (dy, dab), (dalog, ddtb) = _ew_bwd("dn_act_bwd", _f_dn_act, [(r["y"], CONV_W, 0), (proj, 128, CB_AB)],
                                       [p["alog"], p["dtb"]], [(dqkv, CONV_W, 0), (dgb, 128, 0)],
                                       [(0, f32), (1, bf16)], 256)
    dxc, dconv = _conv_bwd(proj, dy, p["conv"], 256)
    dproj = jnp.concatenate([dxc, dz, dgm_u, dgm_v, dqk_raw[0], dqk_raw[1], dvs[0], dqk_raw[2], dqk_raw[3], dvs[1],
                             dqk_raw[4], dqk_raw[5], dvs[2], dab], axis=1)
    dh1 = _mm("in_proj_dx", dproj, p["wi"], "nt", 512, 512)
    dwi = _mm("in_proj_dw", r["h1"], dproj, "tn", 512, 384, bf16)
    (dx,), (dmixw, dsc1, dsh1) = _ew_bwd("modnorm1_bwd", _f_modnorm, [(r["x"], D, 0)], [p["mixw"], sc1, sh1],
                                         [(dh1, D, 0)], [(0, f32)], 512, adds=[dxa])
    grads = dict(wi=dwi, wo=dwo, wfi=dwfi, wfo=dwfo, conv=dconv, mixw=dmixw, ffnw=dffnw,
                 mod=jnp.stack([dsh1, dsc1, dg1, dsh2, dsc2, dg2]), alog=dalog, dtb=ddtb, wn=dwn, lng=dlng, lnb=dlnb,
                 ws=dws, bst=dbst, wq=dwq, wk=dwk)
    return dx, grads


def _rope_tables(t):
    inv = ROPE_THETA ** (-jnp.arange(0, ROPE_DIM, 2, dtype=f32) / ROPE_DIM)
    ang = jnp.arange(t, dtype=f32)[:, None] * inv[None, :]
    cos, sin = jnp.cos(ang), jnp.sin(ang)
    rest = SW_D - ROPE_DIM
    ch = jnp.concatenate([cos, cos, jnp.ones((t, rest), f32)], axis=1)
    sh = jnp.concatenate([sin, sin, jnp.zeros((t, rest), f32)], axis=1)
    return jnp.tile(ch, (1, SW_H)), jnp.tile(sh, (1, SW_H))


def _pad_last(a, n):
    return jnp.pad(a, [(0, 0)] * (a.ndim - 1) + [(0, n - a.shape[-1])])


def _gather_cols(name, w, pad_to):
    nl, r, cs = w.shape
    g = _all_gather(name, _pad_last(w, pad_to))
    return jnp.transpose(g, (1, 2, 0, 3))[..., :cs].reshape(nl, r, NDEV * cs)


def _gather_rows(name, w):
    nl, rs, c = w.shape
    return jnp.transpose(_all_gather(name, w), (1, 0, 2, 3)).reshape(nl, NDEV * rs, c)


def _scatter_cols(name, gs, cs, pad_to):
    r = gs[0].shape[0]
    parts = jnp.stack([jnp.transpose(_pad_last(g.reshape(r, NDEV, cs), pad_to), (1, 0, 2)) for g in gs], axis=1)
    return _all_to_all(name, parts).reshape(NDEV, len(gs) * r, pad_to)


def _scatter_rows(name, gs, rs):
    c = gs[0].shape[1]
    parts = jnp.stack([g.reshape(NDEV, rs, c) for g in gs], axis=1)
    return _all_to_all(name, parts).reshape(NDEV, len(gs) * rs, c)


_SMALL = ("b_mod", "mix_norm_w", "ffn_norm_w", "dn_a_log", "dn_dt_bias", "dn_out_norm_w", "gm_ln_g", "gm_ln_b",
          "gm_w_s", "gm_b_s", "sw_q_norm_w", "sw_k_norm_w")
PACK_TILE = 800 * 128


def _pack(arrs):
    flat = jnp.concatenate([a.reshape(-1) for a in arrs])
    n = -(-flat.shape[0] // PACK_TILE) * PACK_TILE
    return jnp.pad(flat, (0, n - flat.shape[0])).reshape(n // 128, 128)


def _unpack(buf, like):
    flat, out, off = buf.reshape(-1), [], 0
    for a in like:
        out.append(flat[off:off + a.size].reshape(a.shape))
        off += a.size
    return out


def kernel(x, c, w_mod, b_mod, mix_norm_w, ffn_norm_w, w_in, w_out, dn_conv_w, dn_a_log, dn_dt_bias, dn_out_norm_w, gm_ln_g, gm_ln_b, gm_w_s, gm_b_s, sw_q_norm_w, sw_k_norm_w, w_ffn_in, w_ffn_out, loss_target, m_w_mod, m_b_mod, m_mix_norm_w, m_ffn_norm_w, m_w_in, m_w_out, m_dn_conv_w, m_dn_a_log, m_dn_dt_bias, m_dn_out_norm_w, m_gm_ln_g, m_gm_ln_b, m_gm_w_s, m_gm_b_s, m_sw_q_norm_w, m_sw_k_norm_w, m_w_ffn_in, m_w_ffn_out, v_w_mod, v_b_mod, v_mix_norm_w, v_ffn_norm_w, v_w_in, v_w_out, v_dn_conv_w, v_dn_a_log, v_dn_dt_bias, v_dn_out_norm_w, v_gm_ln_g, v_gm_ln_b, v_gm_w_s, v_gm_b_s, v_sw_q_norm_w, v_sw_k_norm_w, v_w_ffn_in, v_w_ffn_out):
    weights = dict(w_mod=w_mod, b_mod=b_mod, mix_norm_w=mix_norm_w, ffn_norm_w=ffn_norm_w, w_in=w_in, w_out=w_out,
                   dn_conv_w=dn_conv_w, dn_a_log=dn_a_log, dn_dt_bias=dn_dt_bias, dn_out_norm_w=dn_out_norm_w,
                   gm_ln_g=gm_ln_g, gm_ln_b=gm_ln_b, gm_w_s=gm_w_s, gm_b_s=gm_b_s, sw_q_norm_w=sw_q_norm_w,
                   sw_k_norm_w=sw_k_norm_w, w_ffn_in=w_ffn_in, w_ffn_out=w_ffn_out)
    mom = dict(w_mod=m_w_mod, b_mod=m_b_mod, mix_norm_w=m_mix_norm_w, ffn_norm_w=m_ffn_norm_w, w_in=m_w_in,
               w_out=m_w_out, dn_conv_w=m_dn_conv_w, dn_a_log=m_dn_a_log, dn_dt_bias=m_dn_dt_bias,
               dn_out_norm_w=m_dn_out_norm_w, gm_ln_g=m_gm_ln_g, gm_ln_b=m_gm_ln_b, gm_w_s=m_gm_w_s, gm_b_s=m_gm_b_s,
               sw_q_norm_w=m_sw_q_norm_w, sw_k_norm_w=m_sw_k_norm_w, w_ffn_in=m_w_ffn_in, w_ffn_out=m_w_ffn_out)
    var = dict(w_mod=v_w_mod, b_mod=v_b_mod, mix_norm_w=v_mix_norm_w, ffn_norm_w=v_ffn_norm_w, w_in=v_w_in,
               w_out=v_w_out, dn_conv_w=v_dn_conv_w, dn_a_log=v_dn_a_log, dn_dt_bias=v_dn_dt_bias,
               dn_out_norm_w=v_dn_out_norm_w, gm_ln_g=v_gm_ln_g, gm_ln_b=v_gm_ln_b, gm_w_s=v_gm_w_s, gm_b_s=v_gm_b_s,
               sw_q_norm_w=v_sw_q_norm_w, sw_k_norm_w=v_sw_k_norm_w, w_ffn_in=v_w_ffn_in, w_ffn_out=v_w_ffn_out)
    names = list(weights)
    xi, tgt = x[0], loss_target[0]
    t = xi.shape[0]
    nl = w_mod.shape[0]
    ax, ay, ac = (lax.axis_index(a) for a in AXES)
    me = 4 * ax + 2 * ay + ac
    mod_cs = w_mod.shape[2]

    c_all = _all_gather("ag_c", jnp.broadcast_to(c, (NDEV, D)))[:, 0, :]
    b_cols = lax.dynamic_slice_in_dim(b_mod, me * mod_cs, mod_cs, axis=1)[:, None, :]
    modc = _mod_cols(c_all, w_mod, b_cols)
    mod_tx = jnp.pad(jnp.transpose(modc, (1, 0, 2)), ((0, 0), (0, 8 - nl), (0, 0)))
    mod_rx = _all_to_all("a2a_mod", mod_tx)[:, :nl]
    mod = jnp.transpose(mod_rx, (1, 0, 2)).reshape(nl, 6, 1, D)

    wi = _gather_cols("ag_w_in", w_in.astype(bf16), 640)
    wi = jnp.concatenate([wi[..., :2048], wi[..., 2056:], wi[..., 2048:2056], jnp.zeros((nl, D, IN_WA - IN_W), bf16)], axis=-1)
    wo = _gather_rows("ag_w_out", w_out.astype(bf16))
    wfi = _gather_cols("ag_w_ffn_in", w_ffn_in.astype(bf16), 768)
    wfo = _gather_rows("ag_w_ffn_out", w_ffn_out.astype(bf16))
    conv = _gather_cols("ag_conv", dn_conv_w, 256)

    pad128 = lambda a: _pad_last(a, 128)[:, None, :]
    params = dict(
        mod=mod, wi=wi, wo=wo, wfi=wfi, wfo=wfo, conv=conv,
        mixw=mix_norm_w[:, None, :], ffnw=ffn_norm_w[:, None, :], alog=pad128(dn_a_log), dtb=pad128(dn_dt_bias),
        wn=dn_out_norm_w[:, None, :], lng=gm_ln_g[:, None, :], lnb=gm_ln_b[:, None, :], ws=gm_w_s,
        bst=_pad_last(jnp.transpose(gm_b_s, (0, 2, 1)), 128),
        wq=jnp.pad(sw_q_norm_w[:, None, :], ((0, 0), (0, 7), (0, 128 - SW_D))),
        wk=jnp.pad(sw_k_norm_w[:, None, :], ((0, 0), (0, 7), (0, 128 - SW_D))))
    tabs = _rope_tables(t)

    layer_p = [{k: v[l] for k, v in params.items()} for l in range(nl)]
    xc, res = xi, []
    for l in range(nl):
        xc, r = _layer_fwd(xc, layer_p[l], tabs)
        res.append(r)
    dy, lpart = _loss_head(xc, tgt, 512)
    loss = lax.psum(lpart[0, 0], AXES)
    dxi, gl = dy, [None] * nl
    for l in reversed(range(nl)):
        dxi, gl[l] = _layer_bwd(dxi, layer_p[l], res[l], tabs)
    g = {k: jnp.stack([gl[l][k] for l in range(nl)]) for k in gl[0] if k not in ("wi", "wo", "wfi", "wfo")}

    dmod = g["mod"].reshape(nl, 6 * D)
    small_g = dict(b_mod=dmod, mix_norm_w=g["mixw"][:, 0], ffn_norm_w=g["ffnw"][:, 0], dn_a_log=g["alog"][:, 0, :DN_H],
                   dn_dt_bias=g["dtb"][:, 0, :DN_H], dn_out_norm_w=g["wn"][:, 0], gm_ln_g=g["lng"][:, 0],
                   gm_ln_b=g["lnb"][:, 0], gm_w_s=g["ws"], gm_b_s=jnp.transpose(g["bst"][:, :, :GM_G], (0, 2, 1)),
                   sw_q_norm_w=g["wq"][:, 0, :SW_D], sw_k_norm_w=g["wk"][:, 0, :SW_D])
    parts = _all_gather("ag_small_grads", _pack([small_g[n] for n in _SMALL]))
    like = [weights[n] for n in _SMALL]
    sm = _adamw("adamw_small", _pack(like), _pack([mom[n] for n in _SMALL]), _pack([var[n] for n in _SMALL]), parts, 800)
    out = {n: vals for n, vals in zip(_SMALL, zip(*[_unpack(b, like) for b in sm]))}

    dmod_all = parts[:, :nl * 6 * D // 128, :].reshape(NDEV, nl, 6 * D)
    dmod_cols = jnp.transpose(lax.dynamic_slice_in_dim(dmod_all, me * mod_cs, mod_cs, axis=2), (1, 0, 2))
    gw_mod = _wmod_grad(c_all, dmod_cols).reshape(1, nl * D, mod_cs)
    big = dict(w_mod=(gw_mod, mod_cs, mod_cs))

    dwi = [jnp.concatenate([d[:, :2048], d[:, 4864:4872], d[:, 2048:4864]], axis=-1) for d in (gl[l]["wi"] for l in range(nl))]
    big["w_in"] = (_scatter_cols("a2a_w_in", dwi, w_in.shape[2], 640), w_in.shape[2], 640)
    big["w_ffn_in"] = (_scatter_cols("a2a_w_ffn_in", [gl[l]["wfi"] for l in range(nl)], w_ffn_in.shape[2], 768),
                       w_ffn_in.shape[2], 768)
    big["dn_conv_w"] = (_scatter_cols("a2a_conv", [gl[l]["conv"] for l in range(nl)], dn_conv_w.shape[2], 256),
                        dn_conv_w.shape[2], 256)
    big["w_out"] = (_scatter_rows("a2a_w_out", [gl[l]["wo"] for l in range(nl)], w_out.shape[1]), D, D)
    big["w_ffn_out"] = (_scatter_rows("a2a_w_ffn_out", [gl[l]["wfo"] for l in range(nl)], w_ffn_out.shape[1]), D, D)
    for n, (gp, cols, padc) in big.items():
        shp = weights[n].shape
        rows = gp.shape[1]
        prep = lambda a: _pad_last(a.reshape(rows, cols), padc)
        tr = 256 if rows % 256 == 0 else rows // 4 if rows % 32 == 0 else rows
        res4 = _adamw("adamw_" + n, prep(weights[n]), prep(mom[n]), prep(var[n]), gp, tr)
        out[n] = tuple(a[:, :cols].reshape(shp) for a in res4)

    return (loss, dxi[None], *[out[n][0] for n in names], *[out[n][1] for n in names],
            *[out[n][2] for n in names], *[out[n][3] for n in names])
```

```python
import functools
import math

import jax
import jax.numpy as jnp
from jax import lax
from jax.experimental import pallas as pl
from jax.experimental.pallas import tpu as pltpu

f32 = jnp.float32
bf16 = jnp.bfloat16
HI = lax.Precision.HIGH
AXES = ("x", "y", "c")
NDEV = 8
SDS = jax.ShapeDtypeStruct

D = 1024
NORM_EPS = 1e-6
DN_W, DN_H, DN_D, DN_C = 512, 4, 128, 64
GM_W, GM_G, GM_C = 256, 4, 128
SW_W, SW_H, SW_D, SW_B = 256, 4, 64, 128
SW_DIL = (1, 4, 16)
SW_SPAN = 128
ROPE_DIM, ROPE_THETA = 16, 500000.0
IN_W = 4872
IN_WA = 4992
FFN = 2816
ADAM_LR, ADAM_B1, ADAM_B2, ADAM_EPS, ADAM_WD, ADAM_STEP = 0.001, 0.9, 0.999, 1e-08, 0.01, 10

VMEM_LIMIT = 52 * 1024 * 1024


def _cp(sem=None):
    return pltpu.CompilerParams(vmem_limit_bytes=VMEM_LIMIT, dimension_semantics=sem)


_DIMS = {"nn": (((1,), (0,)), ((), ())), "nt": (((1,), (1,)), ((), ())), "tn": (((0,), (0,)), ((), ()))}


def _raw_dot(a, b, mode, hi):
    if hi:
        return lax.dot_general(a, b, _DIMS[mode], precision=HI, preferred_element_type=f32)
    return lax.dot_general(a.astype(bf16), b.astype(bf16), _DIMS[mode], preferred_element_type=f32)


@functools.partial(jax.custom_vjp, nondiff_argnums=(2, 3))
def _dot(a, b, mode, hi):
    return _raw_dot(a, b, mode, hi)


def _dot_fwd(a, b, mode, hi):
    return _raw_dot(a, b, mode, hi), (a, b)


def _dot_bwd(mode, hi, res, g):
    a, b = res
    if mode == "nn":
        return _raw_dot(g, b, "nt", hi), _raw_dot(a, g, "tn", hi)
    if mode == "nt":
        return _raw_dot(g, b, "nn", hi), _raw_dot(g, a, "tn", hi)
    return _raw_dot(b, g, "nt", hi), _raw_dot(a, g, "nn", hi)


_dot.defvjp(_dot_fwd, _dot_bwd)


def _iota(shape, dim):
    return lax.broadcasted_iota(jnp.int32, shape, dim)


def _f_modnorm(x, w, scale, shift):
    y = x * lax.rsqrt(jnp.mean(x * x, axis=-1, keepdims=True) + NORM_EPS) * w
    return (y * (1.0 + scale) + shift,)


def _f_resid(x, m, gate):
    return (x + gate * m,)


def _f_swiglu(g, u):
    return (jax.nn.silu(g) * u,)


def _softplus(x):
    return jnp.maximum(x, 0.0) + jnp.log1p(jnp.exp(-jnp.abs(x)))


def _f_dn_act(y, ab, alog, dtb):
    c = jax.nn.silu(y)
    parts = []
    for j in range(3 * DN_H):
        p = c[:, j * DN_D:(j + 1) * DN_D]
        if j < 2 * DN_H:
            p = p * lax.rsqrt(jnp.sum(p * p, axis=-1, keepdims=True) + NORM_EPS)
        parts.append(p)
    lane = _iota(ab.shape, 1)
    g = -jnp.exp(alog) * _softplus(ab + dtb)
    beta = jax.nn.sigmoid(ab)
    gb = jnp.where(lane < DN_H, g, jnp.where(lane < 2 * DN_H, beta, 0.0))
    return jnp.concatenate(parts, axis=1), gb


def _f_dn_out(o, z, wn):
    parts = []
    for h in range(DN_H):
        oh = o[:, h * DN_D:(h + 1) * DN_D]
        zh = z[:, h * DN_D:(h + 1) * DN_D]
        n = oh * lax.rsqrt(jnp.mean(oh * oh, axis=-1, keepdims=True) + NORM_EPS) * wn
        parts.append(n * jax.nn.silu(zh))
    return (jnp.concatenate(parts, axis=1),)


def _gelu(x):
    return 0.5 * x * (1.0 + lax.erf(x * (1.0 / math.sqrt(2.0))))


def _f_gm(u_raw, v_raw, ln_g, ln_b, w_s, b_st):
    u = _gelu(u_raw)
    v = _gelu(v_raw)
    mu = jnp.mean(v, axis=-1, keepdims=True)
    vc = v - mu
    var = jnp.mean(vc * vc, axis=-1, keepdims=True)
    v = vc * lax.rsqrt(var + NORM_EPS) * ln_g + ln_b
    r = _iota((GM_C, GM_C), 0)
    c = _iota((GM_C, GM_C), 1)
    grp = _iota((GM_C, GM_W), 1) // (GM_W // GM_G)
    expand = jnp.where(_iota((GM_C, GM_W), 0) == grp, 1.0, 0.0)
    sv = _dot(b_st, expand, "nn", True)
    for g in range(GM_G):
        wg = jnp.where(r >= c, w_s[g], 0.0)
        sv = sv + jnp.where(grp == g, _dot(wg, v, "nn", True), 0.0)
    return (u * sv,)


def _head_lanes(shape):
    return _iota(shape, 1) // SW_D


def _f_sw_pre(q0, k0, q1, k1, q2, k2, cs, sn, wq, wk):
    r = _iota((SW_W, SW_W), 0)
    c = _iota((SW_W, SW_W), 1)
    same_head = jnp.where(r // SW_D == c // SW_D, 1.0, 0.0)
    hc = c % SW_D
    half = ROPE_DIM // 2
    perm = jnp.where((hc < half) & (r == c + half), -1.0, jnp.where((hc >= half) & (hc < ROPE_DIM) & (r == c - half), 1.0, 0.0))
    tile = jnp.where((_iota((128, SW_W), 1) % SW_D == _iota((128, SW_W), 0)) & (_iota((128, SW_W), 0) < SW_D), 1.0, 0.0)
    wq_full = _dot(wq, tile, "nn", True)[0:1, :]
    wk_full = _dot(wk, tile, "nn", True)[0:1, :]

    def one(t, w):
        ms = _dot(t * t, same_head, "nn", False) * (1.0 / SW_D)
        n = t * lax.rsqrt(ms + NORM_EPS) * w
        return n * cs + _dot(n, perm, "nn", False) * sn

    return one(q0, wq_full), one(k0, wk_full), one(q1, wq_full), one(k1, wk_full), one(q2, wq_full), one(k2, wk_full)


def _f_sw_merge(o0, l0, o1, l1, o2, l2):
    m = jnp.maximum(jnp.maximum(l0, l1), l2)
    e0, e1, e2 = jnp.exp(l0 - m), jnp.exp(l1 - m), jnp.exp(l2 - m)
    return ((e0 * o0 + e1 * o1 + e2 * o2) / (e0 + e1 + e2),)


def _f_attn(q, kp, kc, vp, vc, has_prev):
    kk = jnp.concatenate([kp, kc], axis=0)
    vv = jnp.concatenate([vp, vc], axis=0)
    i = _iota((SW_B, 2 * SW_B), 0)
    j = _iota((SW_B, 2 * SW_B), 1)
    dist = i + SW_B - j
    valid = (dist >= 0) & (dist <= SW_SPAN) & ((j >= SW_B) | has_prev)
    hl = _head_lanes(q.shape)
    heads = range(SW_H)
    ss = [_dot(jnp.where(hl == h, q, 0.0), kk, "nt", False) * (SW_D ** -0.5) for h in heads]
    ss = [jnp.where(valid, s, -1e30) for s in ss]
    ms = [jnp.max(s, axis=-1, keepdims=True) for s in ss]
    ps = [jnp.where(valid, jnp.exp(s - m), 0.0) for s, m in zip(ss, ms)]
    ls = [jnp.sum(p, axis=-1, keepdims=True) for p in ps]
    ohs = [_dot(p, vv, "nn", False) * (1.0 / l) for p, l in zip(ps, ls)]
    o = jnp.zeros(q.shape, f32)
    lse = jnp.zeros(q.shape, f32)
    for h in heads:
        o = o + jnp.where(hl == h, ohs[h], 0.0)
        lse = lse + jnp.where(hl == h, ms[h] + jnp.log(ls[h]), 0.0)
    return o, lse


def _lane_col(tile, lane_idx):
    return jnp.sum(jnp.where(_iota(tile.shape, 1) == lane_idx, tile, 0.0), axis=1, keepdims=True)


def _chunk_decays(gbv):
    n = gbv.shape[0]
    r = _iota((n, n), 0)
    c = _iota((n, n), 1)
    gc_all = _dot(jnp.where(r >= c, 1.0, 0.0), gbv, "nn", True)
    gc_rows = gc_all.T
    out = []
    for h in range(DN_H):
        gcol = _lane_col(gc_all, h)
        diff = jnp.where(r >= c, gcol - gc_rows[h:h + 1, :], 0.0)
        out.append((gcol, jnp.where(r >= c, jnp.exp(diff), 0.0)))
    return out, r, c


def _f_dn_pre(ks, vs, gbvs):
    decs, betas = [], []
    for gbv in gbvs:
        d, r, c = _chunk_decays(gbv)
        decs += d
        betas += [_lane_col(gbv, DN_H + h) for h in range(DN_H)]
    kbs = [k * b for k, b in zip(ks, betas)]
    grams = [_dot(kb, k, "nt", True) for kb, k in zip(kbs, ks)]
    mats = [jnp.where(r > c, g * dec[1], 0.0) for g, dec in zip(grams, decs)]
    rhss = [jnp.concatenate([v * b, kb * jnp.exp(dec[0])], axis=1) for v, b, kb, dec in zip(vs, betas, kbs, decs)]
    return mats, rhss


def _inv_unit_lower(mats):
    n = mats[0].shape[0]
    eye = jnp.where(_iota((n, n), 0) == _iota((n, n), 1), 1.0, 0.0)
    invs = [eye - a for a in mats]
    pws = list(mats)
    for _ in range(n.bit_length() - 2):
        pws = [_raw_dot(p, p, "nn", True) for p in pws]
        invs = [_raw_dot(i, eye + p, "nn", True) for i, p in zip(invs, pws)]
    return invs


def _f_dn_scan(states, qs, ks, us, ws, gbv):
    decs, _, _ = _chunk_decays(gbv)
    n = gbv.shape[0]
    last = _iota((n, 1), 0) == n - 1
    g_last = [jnp.sum(jnp.where(last, gc, 0.0), axis=0, keepdims=True) for gc, _ in decs]
    qs = [q * (DN_D ** -0.5) for q in qs]
    a_qk = [_dot(q, k, "nt", True) * dec[1] for q, k, dec in zip(qs, ks, decs)]
    q_dec = [q * jnp.exp(dec[0]) for q, dec in zip(qs, decs)]
    k_dec = [k * jnp.exp(gl - dec[0]) for k, gl, dec in zip(ks, g_last, decs)]
    ws_ = [_dot(w, s, "nn", True) for w, s in zip(ws, states)]
    o_st = [_dot(qd, s, "nn", True) for qd, s in zip(q_dec, states)]
    v_new = [u - x for u, x in zip(us, ws_)]
    o_in = [_dot(a, vn, "nn", True) for a, vn in zip(a_qk, v_new)]
    upd = [_dot(kd, vn, "tn", True) for kd, vn in zip(k_dec, v_new)]
    outs = [x + y for x, y in zip(o_st, o_in)]
    new_states = [s * jnp.exp(gl) + x for s, gl, x in zip(states, g_last, upd)]
    return outs, new_states


def _cspec(tt, w, cb):
    return pl.BlockSpec((tt, w), lambda i, cb=cb: (i, cb))


def _pspec(shape):
    nd = len(shape)
    return pl.BlockSpec(tuple(shape), lambda i, nd=nd: (0,) * nd)


def _ew_fwd(name, f, tiled, params, outs, tt):
    t = tiled[0][0].shape[0]
    nt, npar = len(tiled), len(params)

    def body(*refs):
        tv = [r[...].astype(f32) for r in refs[:nt]]
        pv = [r[...] for r in refs[nt:nt + npar]]
        res = f(*tv, *pv)
        for o, r in zip(refs[nt + npar:], res):
            o[...] = r.astype(o.dtype)

    res = pl.pallas_call(
        body, grid=(t // tt,), name=name,
        in_specs=[_cspec(tt, w, cb) for _, w, cb in tiled] + [_pspec(p.shape) for p in params],
        out_specs=[_cspec(tt, w, 0) for w, _ in outs],
        out_shape=[SDS((t, w), dt) for w, dt in outs],
        compiler_params=_cp(("arbitrary",)),
    )(*[a for a, _, _ in tiled], *params)
    return res


def _ew_bwd(name, f, tiled, params, cots, diff, tt, adds=None):
    t = tiled[0][0].shape[0]
    nt, npar, nc, nd = len(tiled), len(params), len(cots), len(diff)
    adds = adds or [None] * nd
    add_arrs = [a for a in adds if a is not None]
    na = len(add_arrs)
    dwidth = [tiled[k][1] for k, _ in diff]

    def body(*refs):
        tin = refs[:nt]
        pin = refs[nt:nt + npar]
        cin = refs[nt + npar:nt + npar + nc]
        ain = list(refs[nt + npar + nc:nt + npar + nc + na])
        dts = refs[nt + npar + nc + na:nt + npar + nc + na + nd]
        dps = refs[nt + npar + nc + na + nd:]
        tv = [r[...].astype(f32) for r in tin]
        pv = [r[...] for r in pin]

        def g(*dv):
            full = list(tv)
            for n_, (k, _) in enumerate(diff):
                full[k] = dv[n_]
            return tuple(f(*full, *dv[nd:]))

        _, vjp = jax.vjp(g, *[tv[k] for k, _ in diff], *pv)
        grads = vjp(tuple(c[...].astype(f32) for c in cin))
        for n_ in range(nd):
            val = grads[n_]
            if adds[n_] is not None:
                val = val + ain.pop(0)[...].astype(f32)
            dts[n_][...] = val.astype(dts[n_].dtype)

        @pl.when(pl.program_id(0) == 0)
        def _():
            for r in dps:
                r[...] = jnp.zeros(r.shape, f32)

        for r, gp in zip(dps, grads[nd:]):
            r[...] += gp

    res = pl.pallas_call(
        body, grid=(t // tt,), name=name,
        in_specs=[_cspec(tt, w, cb) for _, w, cb in tiled] + [_pspec(p.shape) for p in params]
        + [_cspec(tt, w, cb) for _, w, cb in cots] + [_cspec(tt, a.shape[1], 0) for a in add_arrs],
        out_specs=[_cspec(tt, w, 0) for w in dwidth] + [_pspec(p.shape) for p in params],
        out_shape=[SDS((t, w), dt) for w, (_, dt) in zip(dwidth, diff)] + [SDS(p.shape, f32) for p in params],
        compiler_params=_cp(("arbitrary",)),
    )(*[a for a, _, _ in tiled], *params, *[a for a, _, _ in cots], *add_arrs)
    return res[:nd], res[nd:]


def _mm(name, a, b, mode, tm, tn, out_dtype=f32):
    if mode == "nn":
        (m, k), (k2, n) = a.shape, b.shape
        a_spec = pl.BlockSpec((tm, k), lambda i, j: (i, 0))
        b_spec = pl.BlockSpec((k, tn), lambda i, j: (0, j))
    elif mode == "nt":
        (m, k), (n, k2) = a.shape, b.shape
        a_spec = pl.BlockSpec((tm, k), lambda i, j: (i, 0))
        b_spec = pl.BlockSpec((tn, k), lambda i, j: (j, 0))
    else:
        (k, m), (k2, n) = a.shape, b.shape
        a_spec = pl.BlockSpec((k, tm), lambda i, j: (0, i))
        b_spec = pl.BlockSpec((k, tn), lambda i, j: (0, j))
    assert k == k2 and m % tm == 0 and n % tn == 0, (name, a.shape, b.shape, mode)
    assert a.dtype == bf16 and b.dtype == bf16, name

    def body(a_ref, b_ref, o_ref):
        o_ref[...] = lax.dot_general(a_ref[...], b_ref[...], _DIMS[mode], preferred_element_type=f32).astype(o_ref.dtype)

    return pl.pallas_call(
        body, grid=(m // tm, n // tn), name=name,
        in_specs=[a_spec, b_spec], out_specs=pl.BlockSpec((tm, tn), lambda i, j: (i, j)),
        out_shape=SDS((m, n), out_dtype), compiler_params=_cp(("parallel", "parallel")),
    )(a, b)


CONV_K = 4
CONV_W = 3 * DN_W
HALO = 8


def _conv_fwd(proj, w, tt):
    t = proj.shape[0]
    nb8 = tt // HALO

    def body(x_ref, h_ref, w_ref, y_ref, xe):
        i = pl.program_id(0)
        xe[0:HALO, :] = jnp.where(i == 0, 0.0, h_ref[...])
        xe[HALO:, :] = x_ref[...]
        wv = w_ref[...]
        acc = jnp.zeros((tt, CONV_W), f32)
        for k in range(CONV_K):
            acc = acc + wv[k:k + 1, :] * xe[pl.ds(HALO - (CONV_K - 1) + k, tt), :]
        y_ref[...] = acc

    return pl.pallas_call(
        body, grid=(t // tt,), name="conv_fwd",
        in_specs=[pl.BlockSpec((tt, CONV_W), lambda i: (i, 0)),
                  pl.BlockSpec((HALO, CONV_W), lambda i: (jnp.maximum(i * nb8 - 1, 0), 0)),
                  _pspec(w.shape)],
        out_specs=pl.BlockSpec((tt, CONV_W), lambda i: (i, 0)),
        out_shape=SDS((t, CONV_W), f32),
        scratch_shapes=[pltpu.VMEM((tt + HALO, CONV_W), f32)],
        compiler_params=_cp(("arbitrary",)),
    )(proj, proj, w)


def _conv_bwd(proj, dy, w, tt):
    t = proj.shape[0]
    nb8 = tt // HALO
    last8 = t // HALO - 1
    nsteps = t // tt

    def body(x_ref, h_ref, dy_ref, n_ref, w_ref, dx_ref, dw_ref, xe, dye):
        i = pl.program_id(0)
        xe[0:HALO, :] = jnp.where(i == 0, 0.0, h_ref[...])
        xe[HALO:, :] = x_ref[...]
        dye[0:tt, :] = dy_ref[...]
        dye[tt:, :] = jnp.where(i == nsteps - 1, 0.0, n_ref[...])
        wv = w_ref[...]
        dyv = dy_ref[...]
        acc = jnp.zeros((tt, CONV_W), f32)

        @pl.when(i == 0)
        def _():
            dw_ref[...] = jnp.zeros(dw_ref.shape, f32)

        for k in range(CONV_K):
            acc = acc + wv[k:k + 1, :] * dye[pl.ds(CONV_K - 1 - k, tt), :]
            dw_ref[k:k + 1, :] += jnp.sum(dyv * xe[pl.ds(HALO - (CONV_K - 1) + k, tt), :], axis=0, keepdims=True)
        dx_ref[...] = acc.astype(dx_ref.dtype)

    return pl.pallas_call(
        body, grid=(nsteps,), name="conv_bwd",
        in_specs=[pl.BlockSpec((tt, CONV_W), lambda i: (i, 0)),
                  pl.BlockSpec((HALO, CONV_W), lambda i: (jnp.maximum(i * nb8 - 1, 0), 0)),
                  pl.BlockSpec((tt, CONV_W), lambda i: (i, 0)),
                  pl.BlockSpec((HALO, CONV_W), lambda i: (jnp.minimum((i + 1) * nb8, last8), 0)),
                  _pspec(w.shape)],
        out_specs=[pl.BlockSpec((tt, CONV_W), lambda i: (i, 0)), _pspec(w.shape)],
        out_shape=[SDS((t, CONV_W), bf16), SDS(w.shape, f32)],
        scratch_shapes=[pltpu.VMEM((tt + HALO, CONV_W), f32), pltpu.VMEM((tt + HALO, CONV_W), f32)],
        compiler_params=_cp(("arbitrary",)),
    )(proj, proj, dy, dy, w)


PREP_CHUNKS = 4


def _head_cols(part, h):
    return slice(part * DN_W + h * DN_D, part * DN_W + (h + 1) * DN_D)


def _prep_operands(qkv_ref, gb_ref):
    inst = [(slice(ch * DN_C, (ch + 1) * DN_C), h) for ch in range(PREP_CHUNKS) for h in range(DN_H)]
    ks = [qkv_ref[rs, _head_cols(1, h)] for rs, h in inst]
    vs = [qkv_ref[rs, _head_cols(2, h)] for rs, h in inst]
    gbvs = [gb_ref[ch * DN_C:(ch + 1) * DN_C, :] for ch in range(PREP_CHUNKS)]
    return inst, ks, vs, gbvs


def _dn_prep_fwd(qkv, gb):
    t = qkv.shape[0]
    rows = PREP_CHUNKS * DN_C

    def body(qkv_ref, gb_ref, u_ref, w_ref, inv_ref):
        inv_ref[...] = jnp.zeros(inv_ref.shape, f32)
        inst, ks, vs, gbvs = _prep_operands(qkv_ref, gb_ref)
        mats, rhss = _f_dn_pre(ks, vs, gbvs)
        invs = _inv_unit_lower(mats)
        uws = [_raw_dot(inv, rhs, "nn", True) for inv, rhs in zip(invs, rhss)]
        for (rs, h), inv, uw in zip(inst, invs, uws):
            u_ref[rs, _head_cols(0, h)] = uw[:, :DN_D]
            w_ref[rs, _head_cols(0, h)] = uw[:, DN_D:]
            inv_ref[rs, h * DN_D:h * DN_D + DN_C] = inv

    return pl.pallas_call(
        body, grid=(t // rows,), name="dn_prep_fwd",
        in_specs=[pl.BlockSpec((rows, CONV_W), lambda i: (i, 0)), pl.BlockSpec((rows, 128), lambda i: (i, 0))],
        out_specs=[pl.BlockSpec((rows, DN_W), lambda i: (i, 0))] * 3,
        out_shape=[SDS((t, DN_W), f32)] * 3,
        compiler_params=_cp(("arbitrary",)),
    )(qkv, gb)


def _dn_prep_bwd(qkv, gb, inv_all, du, dw, dqk1, dgb1):
    t = qkv.shape[0]
    rows = PREP_CHUNKS * DN_C

    def body(qkv_ref, gb_ref, inv_ref, du_ref, dw_ref, dqk1_ref, dgb1_ref, dqkv_ref, dgb_ref):
        inst, ks, vs, gbvs = _prep_operands(qkv_ref, gb_ref)
        (_, rhss), vjp = jax.vjp(_f_dn_pre, ks, vs, gbvs)
        invs = [inv_ref[rs, h * DN_D:h * DN_D + DN_C] for rs, h in inst]
        dxs = [jnp.concatenate([du_ref[rs, _head_cols(0, h)], dw_ref[rs, _head_cols(0, h)]], axis=1) for rs, h in inst]
        uws = [_raw_dot(inv, rhs, "nn", True) for inv, rhs in zip(invs, rhss)]
        drhss = [_raw_dot(inv, dx, "tn", True) for inv, dx in zip(invs, dxs)]
        das = [-_raw_dot(dr, uw, "nt", True) for dr, uw in zip(drhss, uws)]
        dks, dvs, dgbvs = vjp((das, drhss))
        for (rs, h), dk, dv in zip(inst, dks, dvs):
            dqkv_ref[rs, _head_cols(0, h)] = dqk1_ref[rs, _head_cols(0, h)]
            dqkv_ref[rs, _head_cols(1, h)] = dk + dqk1_ref[rs, _head_cols(1, h)]
            dqkv_ref[rs, _head_cols(2, h)] = dv
        for ch, dgbv in enumerate(dgbvs):
            rs = slice(ch * DN_C, (ch + 1) * DN_C)
            dgb_ref[rs, :] = dgbv + dgb1_ref[rs, :]

    return pl.pallas_call(
        body, grid=(t // rows,), name="dn_prep_bwd",
        in_specs=[pl.BlockSpec((rows, CONV_W), lambda i: (i, 0)), pl.BlockSpec((rows, 128), lambda i: (i, 0)),
                  pl.BlockSpec((rows, DN_W), lambda i: (i, 0)),
                  pl.BlockSpec((rows, DN_W), lambda i: (i, 0)), pl.BlockSpec((rows, DN_W), lambda i: (i, 0)),
                  pl.BlockSpec((rows, 2 * DN_W), lambda i: (i, 0)), pl.BlockSpec((rows, 128), lambda i: (i, 0))],
        out_specs=[pl.BlockSpec((rows, CONV_W), lambda i: (i, 0)), pl.BlockSpec((rows, 128), lambda i: (i, 0))],
        out_shape=[SDS((t, CONV_W), f32), SDS((t, 128), f32)],
        compiler_params=_cp(("arbitrary",)),
    )(qkv, gb, inv_all, du, dw, dqk1, dgb1)


def _dn_scan_fwd(qkv, gb, u, w):
    t = qkv.shape[0]
    n = t // DN_C
    heads = range(DN_H)

    def body(qkv_ref, gb_ref, u_ref, w_ref, o_ref, s_ref, state):
        @pl.when(pl.program_id(0) == 0)
        def _():
            state[...] = jnp.zeros(state.shape, f32)

        states = [state[h] for h in heads]
        for h in heads:
            s_ref[0, h] = states[h]
        outs, new = _f_dn_scan(states, [qkv_ref[:, _head_cols(0, h)] for h in heads],
                               [qkv_ref[:, _head_cols(1, h)] for h in heads],
                               [u_ref[:, _head_cols(0, h)] for h in heads],
                               [w_ref[:, _head_cols(0, h)] for h in heads], gb_ref[...])
        for h in heads:
            o_ref[:, _head_cols(0, h)] = outs[h]
            state[h] = new[h]

    return pl.pallas_call(
        body, grid=(n,), name="dn_scan_fwd",
        in_specs=[pl.BlockSpec((DN_C, 2 * DN_W), lambda i: (i, 0)), pl.BlockSpec((DN_C, 128), lambda i: (i, 0)),
                  pl.BlockSpec((DN_C, DN_W), lambda i: (i, 0)), pl.BlockSpec((DN_C, DN_W), lambda i: (i, 0))],
        out_specs=[pl.BlockSpec((DN_C, DN_W), lambda i: (i, 0)),
                   pl.BlockSpec((1, DN_H, DN_D, DN_D), lambda i: (i, 0, 0, 0))],
        out_shape=[SDS((t, DN_W), f32), SDS((n, DN_H, DN_D, DN_D), f32)],
        scratch_shapes=[pltpu.VMEM((DN_H, DN_D, DN_D), f32)],
        compiler_params=_cp(("arbitrary",)),
    )(qkv, gb, u, w)


def _dn_scan_bwd(qkv, gb, u, w, states, do):
    t = qkv.shape[0]
    n = t // DN_C
    rev = lambda i: (n - 1 - i, 0)
    heads = range(DN_H)

    def body(qkv_ref, gb_ref, u_ref, w_ref, s_ref, do_ref, dqk_ref, du_ref, dw_ref, dgb_ref, dstate):
        @pl.when(pl.program_id(0) == 0)
        def _():
            dstate[...] = jnp.zeros(dstate.shape, f32)

        _, vjp = jax.vjp(_f_dn_scan, [s_ref[0, h] for h in heads], [qkv_ref[:, _head_cols(0, h)] for h in heads],
                         [qkv_ref[:, _head_cols(1, h)] for h in heads], [u_ref[:, _head_cols(0, h)] for h in heads],
                         [w_ref[:, _head_cols(0, h)] for h in heads], gb_ref[...])
        ds, dq, dk, du, dw, dgbv = vjp(([do_ref[:, _head_cols(0, h)] for h in heads], [dstate[h] for h in heads]))
        for h in heads:
            dstate[h] = ds[h]
            dqk_ref[:, _head_cols(0, h)] = dq[h]
            dqk_ref[:, _head_cols(1, h)] = dk[h]
            du_ref[:, _head_cols(0, h)] = du[h]
            dw_ref[:, _head_cols(0, h)] = dw[h]
        dgb_ref[...] = dgbv

    return pl.pallas_call(
        body, grid=(n,), name="dn_scan_bwd",
        in_specs=[pl.BlockSpec((DN_C, 2 * DN_W), rev), pl.BlockSpec((DN_C, 128), rev),
                  pl.BlockSpec((DN_C, DN_W), rev), pl.BlockSpec((DN_C, DN_W), rev),
                  pl.BlockSpec((1, DN_H, DN_D, DN_D), lambda i: (n - 1 - i, 0, 0, 0)),
                  pl.BlockSpec((DN_C, DN_W), rev)],
        out_specs=[pl.BlockSpec((DN_C, 2 * DN_W), rev), pl.BlockSpec((DN_C, DN_W), rev),
                   pl.BlockSpec((DN_C, DN_W), rev), pl.BlockSpec((DN_C, 128), rev)],
        out_shape=[SDS((t, 2 * DN_W), f32), SDS((t, DN_W), f32), SDS((t, DN_W), f32), SDS((t, 128), f32)],
        scratch_shapes=[pltpu.VMEM((DN_H, DN_D, DN_D), f32)],
        compiler_params=_cp(("arbitrary",)),
    )(qkv, gb, u, w, states, do)


def _sw_attn_fwd(q2, k2, v2, d):
    l = q2.shape[0]
    nb = l // SW_B
    cur = pl.BlockSpec((SW_B, SW_W), lambda r, n: (n, r))
    prev = pl.BlockSpec((SW_B, SW_W), lambda r, n: (jnp.maximum(n - 1, 0), r))

    def body(q_ref, kp_ref, kc_ref, vp_ref, vc_ref, o_ref, l_ref):
        o, lse = _f_attn(q_ref[...], kp_ref[...], kc_ref[...], vp_ref[...], vc_ref[...], pl.program_id(1) > 0)
        o_ref[...] = o
        l_ref[...] = lse

    return pl.pallas_call(
        body, grid=(d, nb), name=f"sw_attn_fwd_d{d}",
        in_specs=[cur, prev, cur, prev, cur], out_specs=[cur, cur],
        out_shape=[SDS(q2.shape, f32)] * 2, compiler_params=_cp(("arbitrary", "arbitrary")),
    )(q2, k2, k2, v2, v2)


def _sw_attn_bwd(q2, k2, v2, do2, dl2, d):
    l = q2.shape[0]
    nb = l // SW_B
    clamp = lambda n: jnp.minimum(n, nb - 1)
    cur = pl.BlockSpec((SW_B, SW_W), lambda r, n: (clamp(n), r))
    prev = pl.BlockSpec((SW_B, SW_W), lambda r, n: (jnp.maximum(clamp(n) - 1, 0), r))
    lag = pl.BlockSpec((SW_B, SW_W), lambda r, n: (jnp.maximum(n - 1, 0), r))

    def body(q_ref, kp_ref, kc_ref, vp_ref, vc_ref, do_ref, dl_ref, dq_ref, dk_ref, dv_ref, dk_hold, dv_hold):
        n = pl.program_id(1)

        @pl.when(n < nb)
        def _():
            has_prev = n > 0
            _, vjp = jax.vjp(lambda q, kp, kc, vp, vc: _f_attn(q, kp, kc, vp, vc, has_prev),
                             q_ref[...], kp_ref[...], kc_ref[...], vp_ref[...], vc_ref[...])
            dq, dkp, dkc, dvp, dvc = vjp((do_ref[...], dl_ref[...]))
            dq_ref[...] = dq
            dk_ref[...] = dk_hold[...] + dkp
            dv_ref[...] = dv_hold[...] + dvp
            dk_hold[...] = dkc
            dv_hold[...] = dvc

        @pl.when(n == nb)
        def _():
            dk_ref[...] = dk_hold[...]
            dv_ref[...] = dv_hold[...]

    return pl.pallas_call(
        body, grid=(d, nb + 1), name=f"sw_attn_bwd_d{d}",
        in_specs=[cur, prev, cur, prev, cur, cur, cur], out_specs=[cur, lag, lag],
        out_shape=[SDS(q2.shape, f32)] * 3,
        scratch_shapes=[pltpu.VMEM((SW_B, SW_W), f32)] * 2,
        compiler_params=_cp(("arbitrary", "arbitrary")),
    )(q2, k2, k2, v2, v2, do2, dl2)


def _loss_head(y, target, tt):
    t = y.shape[0]

    def body(y_ref, t_ref, dy_ref, l_ref):
        @pl.when(pl.program_id(0) == 0)
        def _():
            l_ref[...] = jnp.zeros(l_ref.shape, f32)

        err = y_ref[...] - t_ref[...]
        dy_ref[...] = err * (1.0 / D)
        l_ref[...] += 0.5 * jnp.sum(jnp.sum(err * err, axis=1, keepdims=True) * (1.0 / D), axis=0, keepdims=True)

    return pl.pallas_call(
        body, grid=(t // tt,), name="loss_head",
        in_specs=[pl.BlockSpec((tt, D), lambda i: (i, 0))] * 2,
        out_specs=[pl.BlockSpec((tt, D), lambda i: (i, 0)), pl.BlockSpec((8, 128), lambda i: (0, 0))],
        out_shape=[SDS((t, D), f32), SDS((8, 128), f32)],
        compiler_params=_cp(("arbitrary",)),
    )(y, target)


def _adamw(name, w, m, v, gparts, tr):
    r, c = w.shape
    p = gparts.shape[0]
    assert r % tr == 0, (name, w.shape, tr)

    def body(w_ref, m_ref, v_ref, g_ref, go_ref, d_ref, mo_ref, vo_ref):
        g = g_ref[0].astype(f32)
        for k in range(1, p):
            g = g + g_ref[k].astype(f32)
        wv = w_ref[...]
        mn = ADAM_B1 * m_ref[...] + (1.0 - ADAM_B1) * g
        vn = ADAM_B2 * v_ref[...] + (1.0 - ADAM_B2) * jnp.square(g)
        m_hat = mn / (1.0 - ADAM_B1 ** ADAM_STEP)
        v_hat = vn / (1.0 - ADAM_B2 ** ADAM_STEP)
        go_ref[...] = g
        d_ref[...] = -ADAM_LR * (m_hat / (jnp.sqrt(v_hat) + ADAM_EPS) + ADAM_WD * wv)
        mo_ref[...] = mn
        vo_ref[...] = vn

    spec = pl.BlockSpec((tr, c), lambda i: (i, 0))
    return pl.pallas_call(
        body, grid=(r // tr,), name=name,
        in_specs=[spec, spec, spec, pl.BlockSpec((p, tr, c), lambda i: (0, i, 0))],
        out_specs=[spec] * 4, out_shape=[SDS((r, c), f32)] * 4,
        compiler_params=_cp(("arbitrary",)),
    )(w, m, v, gparts)


def _mod_cols(c_all, w_mod, b_cols):
    nl, _, wc = w_mod.shape

    def body(c_ref, w_ref, b_ref, o_ref):
        o_ref[0] = _raw_dot(jax.nn.silu(c_ref[...]), w_ref[0], "nn", True) + b_ref[0]

    return pl.pallas_call(
        body, grid=(nl,), name="mod_cols",
        in_specs=[pl.BlockSpec((NDEV, D), lambda l: (0, 0)), pl.BlockSpec((1, D, wc), lambda l: (l, 0, 0)),
                  pl.BlockSpec((1, 1, wc), lambda l: (l, 0, 0))],
        out_specs=pl.BlockSpec((1, NDEV, wc), lambda l: (l, 0, 0)),
        out_shape=SDS((nl, NDEV, wc), f32), compiler_params=_cp(("arbitrary",)),
    )(c_all, w_mod, b_cols)


def _wmod_grad(c_all, dmod_cols):
    nl, _, wc = dmod_cols.shape

    def body(c_ref, d_ref, o_ref):
        o_ref[0, 0] = _raw_dot(jax.nn.silu(c_ref[...]), d_ref[0], "tn", True)

    return pl.pallas_call(
        body, grid=(nl,), name="wmod_grad",
        in_specs=[pl.BlockSpec((NDEV, D), lambda l: (0, 0)), pl.BlockSpec((1, NDEV, wc), lambda l: (l, 0, 0))],
        out_specs=pl.BlockSpec((1, 1, D, wc), lambda l: (0, l, 0, 0)),
        out_shape=SDS((1, nl, D, wc), f32), compiler_params=_cp(("arbitrary",)),
    )(c_all, dmod_cols)


def _me_and_peers():
    x, y, c = (lax.axis_index(a) for a in AXES)
    peers = []
    for k in range(1, NDEV):
        px = 1 - x if (k >> 2) & 1 else x
        py = 1 - y if (k >> 1) & 1 else y
        pc = 1 - c if k & 1 else c
        peers.append(((px, py, pc), 4 * px + 2 * py + pc))
    return 4 * x + 2 * y + c, peers


_ANY = pl.BlockSpec(memory_space=pl.ANY)


def _all_gather(name, a):
    def body(a_ref, o_ref, send_sems, recv_sems, local_sem):
        me, peers = _me_and_peers()
        mine = pltpu.make_async_copy(a_ref, o_ref.at[me], local_sem)
        mine.start()
        copies = [pltpu.make_async_remote_copy(a_ref, o_ref.at[me], send_sems.at[k], recv_sems.at[k],
                                               device_id=dev, device_id_type=pl.DeviceIdType.MESH)
                  for k, (dev, _) in enumerate(peers)]
        for cp in copies:
            cp.start()
        for cp in copies:
            cp.wait()
        mine.wait()

    return pl.pallas_call(
        body, name=name, in_specs=[_ANY], out_specs=_ANY, out_shape=SDS((NDEV,) + a.shape, a.dtype),
        scratch_shapes=[pltpu.SemaphoreType.DMA((NDEV - 1,)), pltpu.SemaphoreType.DMA((NDEV - 1,)), pltpu.SemaphoreType.DMA],
        compiler_params=pltpu.CompilerParams(has_side_effects=True),
    )(a)


def _all_to_all(name, a):
    def body(a_ref, o_ref, send_sems, recv_sems, local_sem):
        me, peers = _me_and_peers()
        mine = pltpu.make_async_copy(a_ref.at[me], o_ref.at[me], local_sem)
        mine.start()
        copies = [pltpu.make_async_remote_copy(a_ref.at[pid], o_ref.at[me], send_sems.at[k], recv_sems.at[k],
                                               device_id=dev, device_id_type=pl.DeviceIdType.MESH)
                  for k, (dev, pid) in enumerate(peers)]
        for cp in copies:
            cp.start()
        for cp in copies:
            cp.wait()
        mine.wait()

    return pl.pallas_call(
        body, name=name, in_specs=[_ANY], out_specs=_ANY, out_shape=SDS(a.shape, a.dtype),
        scratch_shapes=[pltpu.SemaphoreType.DMA((NDEV - 1,)), pltpu.SemaphoreType.DMA((NDEV - 1,)), pltpu.SemaphoreType.DMA],
        compiler_params=pltpu.CompilerParams(has_side_effects=True),
    )(a)


_HBM = pl.BlockSpec(memory_space=pltpu.HBM)
_SEM = pl.BlockSpec(memory_space=pltpu.SEMAPHORE)
_FLOW = pltpu.SideEffectType.DATAFLOW_SIDE_EFFECTING


def _exchange_copies(srcs, lands, send_sems, recv_sems, scatter):
    me, peers = _me_and_peers()
    copies = []
    for a, (src, land) in enumerate(zip(srcs, lands)):
        for k, (dev, pid) in enumerate(peers):
            copies.append(pltpu.make_async_remote_copy(
                src.at[pid] if scatter else src, land.at[me], send_sems.at[a * (NDEV - 1) + k],
                recv_sems.at[a * (NDEV - 1) + k], device_id=dev, device_id_type=pl.DeviceIdType.MESH))
    return copies


def _exchange_start(name, srcs, lands, scatter):
    n = len(srcs)
    nsem = n * (NDEV - 1)

    def body(*refs):
        for cp in _exchange_copies(refs[:n], refs[n:2 * n], refs[2 * n], refs[2 * n + 1], scatter):
            cp.start()
        token = refs[-1]
        token[...] = jnp.zeros(token.shape, token.dtype)

    hbm = lambda a: pltpu.with_memory_space_constraint(a, pltpu.HBM)
    res = pl.pallas_call(
        body, name=name,
        out_shape=(pltpu.SemaphoreType.DMA((nsem,)), pltpu.SemaphoreType.DMA((nsem,)),
                   *[pltpu.HBM(a.shape, a.dtype) for a in srcs], *[pltpu.HBM(a.shape, a.dtype) for a in lands],
                   SDS((8, 128), f32)),
        in_specs=[_HBM] * (2 * n), out_specs=(_SEM, _SEM, *([_HBM] * (2 * n)), pl.BlockSpec(memory_space=pltpu.VMEM)),
        input_output_aliases={i: 2 + i for i in range(2 * n)},
        compiler_params=pltpu.CompilerParams(has_side_effects=_FLOW),
    )(*[hbm(a) for a in srcs], *[hbm(a) for a in lands])
    return res[0], res[1], list(res[2:2 + n]), list(res[2 + n:2 + 2 * n]), res[-1]


def _exchange_wait(name, handle, after, scatter):
    send_sems, recv_sems, srcs, lands, _ = handle
    n = len(srcs)

    def body(*refs):
        for cp in _exchange_copies(refs[:n], refs[n:2 * n], refs[2 * n], refs[2 * n + 1], scatter):
            cp.wait_send()
            cp.wait_recv()
        token = refs[-1]
        token[...] = jnp.zeros(token.shape, token.dtype)

    res = pl.pallas_call(
        body, name=name,
        out_shape=(*[pltpu.HBM(a.shape, a.dtype) for a in srcs + lands], SDS((8, 128), f32)),
        in_specs=[_HBM] * (2 * n) + [_SEM, _SEM, _ANY],
        out_specs=(*([_HBM] * (2 * n)), pl.BlockSpec(memory_space=pltpu.VMEM)),
        input_output_aliases={i: i for i in range(2 * n)},
        compiler_params=pltpu.CompilerParams(has_side_effects=_FLOW),
    )(*srcs, *lands, send_sems, recv_sems, after)
    return list(res[n:2 * n]), res[-1]


def _own_slot(a, me):
    return lax.dynamic_update_slice(jnp.zeros((NDEV,) + a.shape, a.dtype), a[None], (me,) + (0,) * a.ndim)


CB_Z = 3
CB_GM = 8
CB_SW = 10
CB_AB = 38


def _to_sub(a, d):
    return a.reshape(a.shape[0] // d, d * a.shape[1])


def _from_sub(a, d):
    return a.reshape(a.shape[0] * d, a.shape[1] // d)


def _sw_pre_tiles(proj, cs, sn):
    return [(proj, SW_W, CB_SW + 3 * g + j) for g in range(3) for j in range(2)] + [(cs, SW_W, 0), (sn, SW_W, 0)]


def _layer_fwd(x, p, tabs):
    cs, sn = tabs
    sh1, sc1, g1, sh2, sc2, g2 = (p["mod"][k] for k in range(6))
    (h1,) = _ew_fwd("modnorm1_fwd", _f_modnorm, [(x, D, 0)], [p["mixw"], sc1, sh1], [(D, bf16)], 512)
    proj = _mm("in_proj", h1, p["wi"], "nn", 512, 1664)
    y = _conv_fwd(proj, p["conv"], 256)
    qkv, gb = _ew_fwd("dn_act_fwd", _f_dn_act, [(y, CONV_W, 0), (proj, 128, CB_AB)], [p["alog"], p["dtb"]],
                      [(CONV_W, f32), (128, f32)], 256)
    u, w, inv = _dn_prep_fwd(qkv, gb)
    o, states = _dn_scan_fwd(qkv, gb, u, w)
    (ya,) = _ew_fwd("dn_out_fwd", _f_dn_out, [(o, DN_W, 0), (proj, DN_W, CB_Z)], [p["wn"]], [(DN_W, bf16)], 256)
    (yb,) = _ew_fwd("gm_fwd", _f_gm, [(proj, GM_W, CB_GM), (proj, GM_W, CB_GM + 1)],
                    [p["lng"], p["lnb"], p["ws"], p["bst"]], [(GM_W, bf16)], GM_C)
    qk = _ew_fwd("sw_pre_fwd", _f_sw_pre, _sw_pre_tiles(proj, cs, sn), [p["wq"], p["wk"]], [(SW_W, f32)] * 6, 512)
    ol = []
    for g, d in enumerate(SW_DIL):
        c0 = (CB_SW + 3 * g + 2) * SW_W
        vg = proj[:, c0:c0 + SW_W]
        o2, l2 = _sw_attn_fwd(_to_sub(qk[2 * g], d), _to_sub(qk[2 * g + 1], d), _to_sub(vg, d), d)
        ol += [_from_sub(o2, d), _from_sub(l2, d)]
    (yc,) = _ew_fwd("sw_merge_fwd", _f_sw_merge, [(a, SW_W, 0) for a in ol], [], [(SW_W, bf16)], 512)
    ycat = jnp.concatenate([ya, yb, yc], axis=1)
    m1 = _mm("out_proj", ycat, p["wo"], "nn", 512, 1024)
    (x2,) = _ew_fwd("resid1_fwd", _f_resid, [(x, D, 0), (m1, D, 0)], [g1], [(D, f32)], 512)
    (h2,) = _ew_fwd("modnorm2_fwd", _f_modnorm, [(x2, D, 0)], [p["ffnw"], sc2, sh2], [(D, bf16)], 512)
    gu = _mm("ffn_in", h2, p["wfi"], "nn", 512, 1408)
    (act,) = _ew_fwd("swiglu_fwd", _f_swiglu, [(gu, FFN, 0), (gu, FFN, 1)], [], [(FFN, bf16)], 256)
    m2 = _mm("ffn_out", act, p["wfo"], "nn", 512, 1024)
    (x3,) = _ew_fwd("resid2_fwd", _f_resid, [(x2, D, 0), (m2, D, 0)], [g2], [(D, f32)], 512)
    res = dict(x=x, h1=h1, proj=proj, y=y, qkv=qkv, gb=gb, u=u, w=w, inv=inv, states=states, o=o, qk=list(qk), ol=ol,
               ycat=ycat, m1=m1, x2=x2, h2=h2, gu=gu, act=act, m2=m2)
    return x3, res


def _layer_bwd(dx3, p, r, tabs):
    cs, sn = tabs
    sh1, sc1, g1, sh2, sc2, g2 = (p["mod"][k] for k in range(6))
    proj = r["proj"]
    (dx2a, dm2), (dg2,) = _ew_bwd("resid2_bwd", _f_resid, [(r["x2"], D, 0), (r["m2"], D, 0)], [g2], [(dx3, D, 0)],
                                  [(0, f32), (1, bf16)], 512)
    dact = _mm("ffn_out_dx", dm2, p["wfo"], "nt", 512, 1408)
    dwfo = _mm("ffn_out_dw", r["act"], dm2, "tn", 256, 1024, bf16)
    (dgg, dgu), _ = _ew_bwd("swiglu_bwd", _f_swiglu, [(r["gu"], FFN, 0), (r["gu"], FFN, 1)], [], [(dact, FFN, 0)],
                            [(0, bf16), (1, bf16)], 256)
    dgu_cat = jnp.concatenate([dgg, dgu], axis=1)
    dh2 = _mm("ffn_in_dx", dgu_cat, p["wfi"], "nt", 512, 512)
    dwfi = _mm("ffn_in_dw", r["h2"], dgu_cat, "tn", 512, 512, bf16)
    (dx2,), (dffnw, dsc2, dsh2) = _ew_bwd("modnorm2_bwd", _f_modnorm, [(r["x2"], D, 0)], [p["ffnw"], sc2, sh2],
                                          [(dh2, D, 0)], [(0, f32)], 512, adds=[dx2a])
    (dxa, dm1), (dg1,) = _ew_bwd("resid1_bwd", _f_resid, [(r["x"], D, 0), (r["m1"], D, 0)], [g1], [(dx2, D, 0)],
                                 [(0, f32), (1, bf16)], 512)
    dycat = _mm("out_proj_dx", dm1, p["wo"], "nt", 512, 1024)
    dwo = _mm("out_proj_dw", r["ycat"], dm1, "tn", 512, 512, bf16)
    dol, _ = _ew_bwd("sw_merge_bwd", _f_sw_merge, [(a, SW_W, 0) for a in r["ol"]], [], [(dycat, SW_W, 3)],
                     [(k, f32) for k in range(6)], 512)
    dqk, dvs = [], []
    for g, d in enumerate(SW_DIL):
        c0 = (CB_SW + 3 * g + 2) * SW_W
        vg = proj[:, c0:c0 + SW_W]
        dq2, dk2, dv2 = _sw_attn_bwd(_to_sub(r["qk"][2 * g], d), _to_sub(r["qk"][2 * g + 1], d), _to_sub(vg, d),
                                     _to_sub(dol[2 * g], d), _to_sub(dol[2 * g + 1], d), d)
        dqk += [_from_sub(dq2, d), _from_sub(dk2, d)]
        dvs.append(_from_sub(dv2, d).astype(bf16))
    dqk_raw, (dwq, dwk) = _ew_bwd("sw_pre_bwd", _f_sw_pre, _sw_pre_tiles(proj, cs, sn), [p["wq"], p["wk"]],
                                  [(a, SW_W, 0) for a in dqk], [(k, bf16) for k in range(6)], 512)
    (dgm_u, dgm_v), (dlng, dlnb, dws, dbst) = _ew_bwd(
        "gm_bwd", _f_gm, [(proj, GM_W, CB_GM), (proj, GM_W, CB_GM + 1)], [p["lng"], p["lnb"], p["ws"], p["bst"]],
        [(dycat, GM_W, 2)], [(0, bf16), (1, bf16)], GM_C)
    (do, dz), (dwn,) = _ew_bwd("dn_out_bwd", _f_dn_out, [(r["o"], DN_W, 0), (proj, DN_W, CB_Z)], [p["wn"]],
                               [(dycat, DN_W, 0)], [(0, f32), (1, bf16)], 256)
    dqk1, du, dw, dgb1 = _dn_scan_bwd(r["qkv"], r["gb"], r["u"], r["w"], r["states"], do)
    dqkv, dgb = _dn_prep_bwd(r["qkv"], r["gb"], r["inv"], du, dw, dqk1, dgb1)
    (dy, dab), (dalog, ddtb) = _ew_bwd("dn_act_bwd", _f_dn_act, [(r["y"], CONV_W, 0), (proj, 128, CB_AB)],
                                       [p["alog"], p["dtb"]], [(dqkv, CONV_W, 0), (dgb, 128, 0)],
                                       [(0, f32), (1, bf16)], 256)
    dxc, dconv = _conv_bwd(proj, dy, p["conv"], 256)
    dproj = jnp.concatenate([dxc, dz, dgm_u, dgm_v, dqk_raw[0], dqk_raw[1], dvs[0], dqk_raw[2], dqk_raw[3], dvs[1],
                             dqk_raw[4], dqk_raw[5], dvs[2], dab], axis=1)
    dh1 = _mm("in_proj_dx", dproj, p["wi"], "nt", 512, 512)
    dwi = _mm("in_proj_dw", r["h1"], dproj, "tn", 512, 384, bf16)
    (dx,), (dmixw, dsc1, dsh1) = _ew_bwd("modnorm1_bwd", _f_modnorm, [(r["x"], D, 0)], [p["mixw"], sc1, sh1],
                                         [(dh1, D, 0)], [(0, f32)], 512, adds=[dxa])
    grads = dict(wi=dwi, wo=dwo, wfi=dwfi, wfo=dwfo, conv=dconv, mixw=dmixw, ffnw=dffnw,
                 mod=jnp.stack([dsh1, dsc1, dg1, dsh2, dsc2, dg2]), alog=dalog, dtb=ddtb, wn=dwn, lng=dlng, lnb=dlnb,
                 ws=dws, bst=dbst, wq=dwq, wk=dwk)
    return dx, grads


def _rope_tables(t):
    inv = ROPE_THETA ** (-jnp.arange(0, ROPE_DIM, 2, dtype=f32) / ROPE_DIM)
    ang = jnp.arange(t, dtype=f32)[:, None] * inv[None, :]
    cos, sin = jnp.cos(ang), jnp.sin(ang)
    rest = SW_D - ROPE_DIM
    ch = jnp.concatenate([cos, cos, jnp.ones((t, rest), f32)], axis=1)
    sh = jnp.concatenate([sin, sin, jnp.zeros((t, rest), f32)], axis=1)
    return jnp.tile(ch, (1, SW_H)), jnp.tile(sh, (1, SW_H))


def _pad_last(a, n):
    return jnp.pad(a, [(0, 0)] * (a.ndim - 1) + [(0, n - a.shape[-1])])


def _gather_cols(name, w, pad_to):
    nl, r, cs = w.shape
    g = _all_gather(name, _pad_last(w, pad_to))
    return jnp.transpose(g, (1, 2, 0, 3))[..., :cs].reshape(nl, r, NDEV * cs)


def _scatter_cols(name, gs, cs, pad_to):
    r = gs[0].shape[0]
    parts = jnp.stack([jnp.transpose(_pad_last(g.reshape(r, NDEV, cs), pad_to), (1, 0, 2)) for g in gs], axis=1)
    return _all_to_all(name, parts).reshape(NDEV, len(gs) * r, pad_to)


_SMALL = ("b_mod", "mix_norm_w", "ffn_norm_w", "dn_a_log", "dn_dt_bias", "dn_out_norm_w", "gm_ln_g", "gm_ln_b",
          "gm_w_s", "gm_b_s", "sw_q_norm_w", "sw_k_norm_w")
PACK_TILE = 800 * 128


def _pack(arrs):
    flat = jnp.concatenate([a.reshape(-1) for a in arrs])
    n = -(-flat.shape[0] // PACK_TILE) * PACK_TILE
    return jnp.pad(flat, (0, n - flat.shape[0])).reshape(n // 128, 128)


def _unpack(buf, like):
    flat, out, off = buf.reshape(-1), [], 0
    for a in like:
        out.append(flat[off:off + a.size].reshape(a.shape))
        off += a.size
    return out


def kernel(x, c, w_mod, b_mod, mix_norm_w, ffn_norm_w, w_in, w_out, dn_conv_w, dn_a_log, dn_dt_bias, dn_out_norm_w, gm_ln_g, gm_ln_b, gm_w_s, gm_b_s, sw_q_norm_w, sw_k_norm_w, w_ffn_in, w_ffn_out, loss_target, m_w_mod, m_b_mod, m_mix_norm_w, m_ffn_norm_w, m_w_in, m_w_out, m_dn_conv_w, m_dn_a_log, m_dn_dt_bias, m_dn_out_norm_w, m_gm_ln_g, m_gm_ln_b, m_gm_w_s, m_gm_b_s, m_sw_q_norm_w, m_sw_k_norm_w, m_w_ffn_in, m_w_ffn_out, v_w_mod, v_b_mod, v_mix_norm_w, v_ffn_norm_w, v_w_in, v_w_out, v_dn_conv_w, v_dn_a_log, v_dn_dt_bias, v_dn_out_norm_w, v_gm_ln_g, v_gm_ln_b, v_gm_w_s, v_gm_b_s, v_sw_q_norm_w, v_sw_k_norm_w, v_w_ffn_in, v_w_ffn_out):
    weights = dict(w_mod=w_mod, b_mod=b_mod, mix_norm_w=mix_norm_w, ffn_norm_w=ffn_norm_w, w_in=w_in, w_out=w_out,
                   dn_conv_w=dn_conv_w, dn_a_log=dn_a_log, dn_dt_bias=dn_dt_bias, dn_out_norm_w=dn_out_norm_w,
                   gm_ln_g=gm_ln_g, gm_ln_b=gm_ln_b, gm_w_s=gm_w_s, gm_b_s=gm_b_s, sw_q_norm_w=sw_q_norm_w,
                   sw_k_norm_w=sw_k_norm_w, w_ffn_in=w_ffn_in, w_ffn_out=w_ffn_out)
    mom = dict(w_mod=m_w_mod, b_mod=m_b_mod, mix_norm_w=m_mix_norm_w, ffn_norm_w=m_ffn_norm_w, w_in=m_w_in,
               w_out=m_w_out, dn_conv_w=m_dn_conv_w, dn_a_log=m_dn_a_log, dn_dt_bias=m_dn_dt_bias,
               dn_out_norm_w=m_dn_out_norm_w, gm_ln_g=m_gm_ln_g, gm_ln_b=m_gm_ln_b, gm_w_s=m_gm_w_s, gm_b_s=m_gm_b_s,
               sw_q_norm_w=m_sw_q_norm_w, sw_k_norm_w=m_sw_k_norm_w, w_ffn_in=m_w_ffn_in, w_ffn_out=m_w_ffn_out)
    var = dict(w_mod=v_w_mod, b_mod=v_b_mod, mix_norm_w=v_mix_norm_w, ffn_norm_w=v_ffn_norm_w, w_in=v_w_in,
               w_out=v_w_out, dn_conv_w=v_dn_conv_w, dn_a_log=v_dn_a_log, dn_dt_bias=v_dn_dt_bias,
               dn_out_norm_w=v_dn_out_norm_w, gm_ln_g=v_gm_ln_g, gm_ln_b=v_gm_ln_b, gm_w_s=v_gm_w_s, gm_b_s=v_gm_b_s,
               sw_q_norm_w=v_sw_q_norm_w, sw_k_norm_w=v_sw_k_norm_w, w_ffn_in=v_w_ffn_in, w_ffn_out=v_w_ffn_out)
    names = list(weights)
    xi, tgt = x[0], loss_target[0]
    t = xi.shape[0]
    nl = w_mod.shape[0]
    ax, ay, ac = (lax.axis_index(a) for a in AXES)
    me = 4 * ax + 2 * ay + ac
    mod_cs = w_mod.shape[2]

    c_all = _all_gather("ag_c", jnp.broadcast_to(c, (NDEV, D)))[:, 0, :]
    b_cols = lax.dynamic_slice_in_dim(b_mod, me * mod_cs, mod_cs, axis=1)[:, None, :]
    modc = _mod_cols(c_all, w_mod, b_cols)
    mod_tx = jnp.pad(jnp.transpose(modc, (1, 0, 2)), ((0, 0), (0, 8 - nl), (0, 0)))
    mod_rx = _all_to_all("a2a_mod", mod_tx)[:, :nl]
    mod = jnp.transpose(mod_rx, (1, 0, 2)).reshape(nl, 6, 1, D)

    shards = [_pad_last(w_in.astype(bf16), 640), w_out.astype(bf16), _pad_last(w_ffn_in.astype(bf16), 768),
              w_ffn_out.astype(bf16)]

    def start_gather(l, shards_l):
        return _exchange_start(f"ag_start_{l}", shards_l, [_own_slot(a, me) for a in shards_l], False)

    def full_weights(landed):
        gi, go, gfi, gfo = landed
        wi = jnp.transpose(gi, (1, 0, 2))[..., :w_in.shape[2]].reshape(D, IN_W)
        wi = jnp.concatenate([wi[:, :2048], wi[:, 2056:], wi[:, 2048:2056], jnp.zeros((D, IN_WA - IN_W), bf16)], axis=-1)
        wfi = jnp.transpose(gfi, (1, 0, 2))[..., :w_ffn_in.shape[2]].reshape(D, 2 * FFN)
        return dict(wi=wi, wo=go.reshape(D, D), wfi=wfi, wfo=gfo.reshape(FFN, D))

    conv = _gather_cols("ag_conv", dn_conv_w, 256)
    pad128 = lambda a: _pad_last(a, 128)[:, None, :]
    params = dict(
        mod=mod, conv=conv,
        mixw=mix_norm_w[:, None, :], ffnw=ffn_norm_w[:, None, :], alog=pad128(dn_a_log), dtb=pad128(dn_dt_bias),
        wn=dn_out_norm_w[:, None, :], lng=gm_ln_g[:, None, :], lnb=gm_ln_b[:, None, :], ws=gm_w_s,
        bst=_pad_last(jnp.transpose(gm_b_s, (0, 2, 1)), 128),
        wq=jnp.pad(sw_q_norm_w[:, None, :], ((0, 0), (0, 7), (0, 128 - SW_D))),
        wk=jnp.pad(sw_k_norm_w[:, None, :], ((0, 0), (0, 7), (0, 128 - SW_D))))
    tabs = _rope_tables(t)

    layer_p, res = [], []
    xc = xi
    handle = start_gather(0, [a[0] for a in shards])
    after = handle[4]
    for l in range(nl):
        landed, tok = _exchange_wait(f"ag_wait_{l}", handle, after, False)
        p = {k: v[l] for k, v in params.items()}
        if l + 1 < nl:
            handle = start_gather(l + 1, [a[l + 1] + tok[0, 0].astype(bf16) for a in shards])
            p["mod"] = p["mod"] + handle[4][0, 0]
        p.update(full_weights(landed))
        layer_p.append(p)
        xc, r = _layer_fwd(xc, p, tabs)
        res.append(r)
        after = xc
    dy, lpart = _loss_head(xc, tgt, 512)
    loss = lax.psum(lpart[0, 0], AXES)

    slot = lax.broadcasted_iota(jnp.int32, (NDEV, 1, 1), 0)
    dxi, gl, handles = dy, [None] * nl, [None] * nl
    for l in reversed(range(nl)):
        dxi, gl[l] = _layer_bwd(dxi, layer_p[l], res[l], tabs)
        d = gl[l]["wi"]
        d = jnp.concatenate([d[:, :2048], d[:, 4864:4872], d[:, 2048:4864]], axis=-1)
        parts = [jnp.transpose(_pad_last(d.reshape(D, NDEV, w_in.shape[2]), 640), (1, 0, 2)),
                 gl[l]["wo"].reshape(NDEV, w_out.shape[1], D),
                 jnp.transpose(_pad_last(gl[l]["wfi"].reshape(D, NDEV, w_ffn_in.shape[2]), 768), (1, 0, 2)),
                 gl[l]["wfo"].reshape(NDEV, w_ffn_out.shape[1], D)]
        handles[l] = _exchange_start(f"a2a_start_{l}", parts, [jnp.where(slot == me, a, jnp.zeros_like(a)) for a in parts], True)
        if l > 0:
            layer_p[l - 1]["mod"] = layer_p[l - 1]["mod"] + handles[l][4][0, 0]
    g = {k: jnp.stack([gl[l][k] for l in range(nl)]) for k in gl[0] if k not in ("wi", "wo", "wfi", "wfo")}

    dmod = g["mod"].reshape(nl, 6 * D)
    small_g = dict(b_mod=dmod, mix_norm_w=g["mixw"][:, 0], ffn_norm_w=g["ffnw"][:, 0], dn_a_log=g["alog"][:, 0, :DN_H],
                   dn_dt_bias=g["dtb"][:, 0, :DN_H], dn_out_norm_w=g["wn"][:, 0], gm_ln_g=g["lng"][:, 0],
                   gm_ln_b=g["lnb"][:, 0], gm_w_s=g["ws"], gm_b_s=jnp.transpose(g["bst"][:, :, :GM_G], (0, 2, 1)),
                   sw_q_norm_w=g["wq"][:, 0, :SW_D], sw_k_norm_w=g["wk"][:, 0, :SW_D])
    parts = _all_gather("ag_small_grads", _pack([small_g[n] for n in _SMALL]) + handles[0][4][0, 0])
    like = [weights[n] for n in _SMALL]
    sm = _adamw("adamw_small", _pack(like), _pack([mom[n] for n in _SMALL]), _pack([var[n] for n in _SMALL]), parts, 800)
    out = {n: vals for n, vals in zip(_SMALL, zip(*[_unpack(b, like) for b in sm]))}

    dmod_all = parts[:, :nl * 6 * D // 128, :].reshape(NDEV, nl, 6 * D)
    dmod_cols = jnp.transpose(lax.dynamic_slice_in_dim(dmod_all, me * mod_cs, mod_cs, axis=2), (1, 0, 2))
    gw_mod = _wmod_grad(c_all, dmod_cols).reshape(1, nl * D, mod_cs)
    big = dict(w_mod=(gw_mod, mod_cs, mod_cs))

    after, landed_g = sm[0], [None] * nl
    for l in reversed(range(nl)):
        landed_g[l], after = _exchange_wait(f"a2a_wait_{l}", handles[l], after, True)
    for i, (n, cols, padc) in enumerate((("w_in", w_in.shape[2], 640), ("w_out", D, D), ("w_ffn_in", w_ffn_in.shape[2], 768),
                                         ("w_ffn_out", D, D))):
        big[n] = (jnp.concatenate([landed_g[l][i] for l in range(nl)], axis=1), cols, padc)
    big["dn_conv_w"] = (_scatter_cols("a2a_conv", [gl[l]["conv"] for l in range(nl)], dn_conv_w.shape[2], 256),
                        dn_conv_w.shape[2], 256)
    for n, (gp, cols, padc) in big.items():
        shp = weights[n].shape
        rows = gp.shape[1]
        prep = lambda a: _pad_last(a.reshape(rows, cols), padc)
        tr = 256 if rows % 256 == 0 else rows // 4 if rows % 32 == 0 else rows
        res4 = _adamw("adamw_" + n, prep(weights[n]), prep(mom[n]), prep(var[n]), gp, tr)
        out[n] = tuple(a[:, :cols].reshape(shp) for a in res4)

    return (loss, dxi[None], *[out[n][0] for n in names], *[out[n][1] for n in names],
            *[out[n][2] for n in names], *[out[n][3] for n in names])
```

```python
import functools
import math

import jax
import jax.numpy as jnp
from jax import lax
from jax.experimental import pallas as pl
from jax.experimental.pallas import tpu as pltpu

f32 = jnp.float32
bf16 = jnp.bfloat16
HI = lax.Precision.HIGH
AXES = ("x", "y", "c")
NDEV = 8
SDS = jax.ShapeDtypeStruct

D = 1024
NORM_EPS = 1e-6
DN_W, DN_H, DN_D, DN_C = 512, 4, 128, 64
GM_W, GM_G, GM_C = 256, 4, 128
SW_W, SW_H, SW_D, SW_B = 256, 4, 64, 128
SW_DIL = (1, 4, 16)
SW_SPAN = 128
ROPE_DIM, ROPE_THETA = 16, 500000.0
IN_W = 4872
IN_WA = 4992
FFN = 2816
ADAM_LR, ADAM_B1, ADAM_B2, ADAM_EPS, ADAM_WD, ADAM_STEP = 0.001, 0.9, 0.999, 1e-08, 0.01, 10

VMEM_LIMIT = 52 * 1024 * 1024


def _cp(sem=None):
    return pltpu.CompilerParams(vmem_limit_bytes=VMEM_LIMIT, dimension_semantics=sem)


_DIMS = {"nn": (((1,), (0,)), ((), ())), "nt": (((1,), (1,)), ((), ())), "tn": (((0,), (0,)), ((), ()))}


def _raw_dot(a, b, mode, hi):
    if hi:
        return lax.dot_general(a, b, _DIMS[mode], precision=HI, preferred_element_type=f32)
    return lax.dot_general(a.astype(bf16), b.astype(bf16), _DIMS[mode], preferred_element_type=f32)


@functools.partial(jax.custom_vjp, nondiff_argnums=(2, 3))
def _dot(a, b, mode, hi):
    return _raw_dot(a, b, mode, hi)


def _dot_fwd(a, b, mode, hi):
    return _raw_dot(a, b, mode, hi), (a, b)


def _dot_bwd(mode, hi, res, g):
    a, b = res
    if mode == "nn":
        return _raw_dot(g, b, "nt", hi), _raw_dot(a, g, "tn", hi)
    if mode == "nt":
        return _raw_dot(g, b, "nn", hi), _raw_dot(g, a, "tn", hi)
    return _raw_dot(b, g, "nt", hi), _raw_dot(a, g, "nn", hi)


_dot.defvjp(_dot_fwd, _dot_bwd)


def _iota(shape, dim):
    return lax.broadcasted_iota(jnp.int32, shape, dim)


def _f_modnorm(x, w, scale, shift):
    y = x * lax.rsqrt(jnp.mean(x * x, axis=-1, keepdims=True) + NORM_EPS) * w
    return (y * (1.0 + scale) + shift,)


def _f_resid(x, m, gate):
    return (x + gate * m,)


def _f_swiglu(gu):
    return (jax.nn.silu(gu[:, :FFN]) * gu[:, FFN:],)


def _softplus(x):
    return jnp.maximum(x, 0.0) + jnp.log1p(jnp.exp(-jnp.abs(x)))


def _f_dn_act(y, ab, alog, dtb):
    c = jax.nn.silu(y)
    parts = []
    for j in range(3 * DN_H):
        p = c[:, j * DN_D:(j + 1) * DN_D]
        if j < 2 * DN_H:
            p = p * lax.rsqrt(jnp.sum(p * p, axis=-1, keepdims=True) + NORM_EPS)
        parts.append(p)
    lane = _iota(ab.shape, 1)
    g = -jnp.exp(alog) * _softplus(ab + dtb)
    beta = jax.nn.sigmoid(ab)
    gb = jnp.where(lane < DN_H, g, jnp.where(lane < 2 * DN_H, beta, 0.0))
    return jnp.concatenate(parts, axis=1), gb


def _f_dn_out(o, z, wn):
    parts = []
    for h in range(DN_H):
        oh = o[:, h * DN_D:(h + 1) * DN_D]
        zh = z[:, h * DN_D:(h + 1) * DN_D]
        n = oh * lax.rsqrt(jnp.mean(oh * oh, axis=-1, keepdims=True) + NORM_EPS) * wn
        parts.append(n * jax.nn.silu(zh))
    return (jnp.concatenate(parts, axis=1),)


def _gelu(x):
    return 0.5 * x * (1.0 + lax.erf(x * (1.0 / math.sqrt(2.0))))


def _f_gm(u_raw, v_raw, ln_g, ln_b, w_s, b_st):
    u = _gelu(u_raw)
    v = _gelu(v_raw)
    mu = jnp.mean(v, axis=-1, keepdims=True)
    vc = v - mu
    var = jnp.mean(vc * vc, axis=-1, keepdims=True)
    v = vc * lax.rsqrt(var + NORM_EPS) * ln_g + ln_b
    r = _iota((GM_C, GM_C), 0)
    c = _iota((GM_C, GM_C), 1)
    grp = _iota((GM_C, GM_W), 1) // (GM_W // GM_G)
    expand = jnp.where(_iota((GM_C, GM_W), 0) == grp, 1.0, 0.0)
    sv = _dot(b_st, expand, "nn", True)
    for g in range(GM_G):
        wg = jnp.where(r >= c, w_s[g], 0.0)
        sv = sv + jnp.where(grp == g, _dot(wg, v, "nn", True), 0.0)
    return (u * sv,)


def _head_lanes(shape):
    return _iota(shape, 1) // SW_D


def _f_sw_pre(q0, k0, q1, k1, q2, k2, cs, sn, wq, wk):
    r = _iota((SW_W, SW_W), 0)
    c = _iota((SW_W, SW_W), 1)
    same_head = jnp.where(r // SW_D == c // SW_D, 1.0, 0.0)
    hc = c % SW_D
    half = ROPE_DIM // 2
    perm = jnp.where((hc < half) & (r == c + half), -1.0, jnp.where((hc >= half) & (hc < ROPE_DIM) & (r == c - half), 1.0, 0.0))
    tile = jnp.where((_iota((128, SW_W), 1) % SW_D == _iota((128, SW_W), 0)) & (_iota((128, SW_W), 0) < SW_D), 1.0, 0.0)
    wq_full = _dot(wq, tile, "nn", True)[0:1, :]
    wk_full = _dot(wk, tile, "nn", True)[0:1, :]

    def one(t, w):
        ms = _dot(t * t, same_head, "nn", False) * (1.0 / SW_D)
        n = t * lax.rsqrt(ms + NORM_EPS) * w
        return n * cs + _dot(n, perm, "nn", False) * sn

    return one(q0, wq_full), one(k0, wk_full), one(q1, wq_full), one(k1, wk_full), one(q2, wq_full), one(k2, wk_full)


def _f_sw_merge(o0, l0, o1, l1, o2, l2):
    m = jnp.maximum(jnp.maximum(l0, l1), l2)
    e0, e1, e2 = jnp.exp(l0 - m), jnp.exp(l1 - m), jnp.exp(l2 - m)
    return ((e0 * o0 + e1 * o1 + e2 * o2) / (e0 + e1 + e2),)


def _f_attn(q, kp, kc, vp, vc, has_prev):
    kk = jnp.concatenate([kp, kc], axis=0)
    vv = jnp.concatenate([vp, vc], axis=0)
    i = _iota((SW_B, 2 * SW_B), 0)
    j = _iota((SW_B, 2 * SW_B), 1)
    dist = i + SW_B - j
    valid = (dist >= 0) & (dist <= SW_SPAN) & ((j >= SW_B) | has_prev)
    hl = _head_lanes(q.shape)
    heads = range(SW_H)
    ss = [_dot(jnp.where(hl == h, q, 0.0), kk, "nt", False) * (SW_D ** -0.5) for h in heads]
    ss = [jnp.where(valid, s, -1e30) for s in ss]
    ms = [jnp.max(s, axis=-1, keepdims=True) for s in ss]
    ps = [jnp.where(valid, jnp.exp(s - m), 0.0) for s, m in zip(ss, ms)]
    ls = [jnp.sum(p, axis=-1, keepdims=True) for p in ps]
    ohs = [_dot(p, vv, "nn", False) * (1.0 / l) for p, l in zip(ps, ls)]
    o = jnp.zeros(q.shape, f32)
    lse = jnp.zeros(q.shape, f32)
    for h in heads:
        o = o + jnp.where(hl == h, ohs[h], 0.0)
        lse = lse + jnp.where(hl == h, ms[h] + jnp.log(ls[h]), 0.0)
    return o, lse


def _lane_col(tile, lane_idx):
    return jnp.sum(jnp.where(_iota(tile.shape, 1) == lane_idx, tile, 0.0), axis=1, keepdims=True)


def _chunk_decays(gbv):
    n = gbv.shape[0]
    r = _iota((n, n), 0)
    c = _iota((n, n), 1)
    gc_all = _dot(jnp.where(r >= c, 1.0, 0.0), gbv, "nn", True)
    gc_rows = gc_all.T
    out = []
    for h in range(DN_H):
        gcol = _lane_col(gc_all, h)
        diff = jnp.where(r >= c, gcol - gc_rows[h:h + 1, :], 0.0)
        out.append((gcol, jnp.where(r >= c, jnp.exp(diff), 0.0)))
    return out, r, c


def _f_dn_pre(ks, vs, gbvs):
    decs, betas = [], []
    for gbv in gbvs:
        d, r, c = _chunk_decays(gbv)
        decs += d
        betas += [_lane_col(gbv, DN_H + h) for h in range(DN_H)]
    kbs = [k * b for k, b in zip(ks, betas)]
    grams = [_dot(kb, k, "nt", True) for kb, k in zip(kbs, ks)]
    mats = [jnp.where(r > c, g * dec[1], 0.0) for g, dec in zip(grams, decs)]
    rhss = [jnp.concatenate([v * b, kb * jnp.exp(dec[0])], axis=1) for v, b, kb, dec in zip(vs, betas, kbs, decs)]
    return mats, rhss


def _inv_unit_lower(mats):
    n = mats[0].shape[0]
    eye = jnp.where(_iota((n, n), 0) == _iota((n, n), 1), 1.0, 0.0)
    invs = [eye - a for a in mats]
    pws = list(mats)
    for _ in range(n.bit_length() - 2):
        pws = [_raw_dot(p, p, "nn", True) for p in pws]
        invs = [_raw_dot(i, eye + p, "nn", True) for i, p in zip(invs, pws)]
    return invs


def _f_dn_scan(states, qs, ks, us, ws, gbv):
    decs, _, _ = _chunk_decays(gbv)
    n = gbv.shape[0]
    last = _iota((n, 1), 0) == n - 1
    g_last = [jnp.sum(jnp.where(last, gc, 0.0), axis=0, keepdims=True) for gc, _ in decs]
    qs = [q * (DN_D ** -0.5) for q in qs]
    a_qk = [_dot(q, k, "nt", True) * dec[1] for q, k, dec in zip(qs, ks, decs)]
    q_dec = [q * jnp.exp(dec[0]) for q, dec in zip(qs, decs)]
    k_dec = [k * jnp.exp(gl - dec[0]) for k, gl, dec in zip(ks, g_last, decs)]
    ws_ = [_dot(w, s, "nn", True) for w, s in zip(ws, states)]
    o_st = [_dot(qd, s, "nn", True) for qd, s in zip(q_dec, states)]
    v_new = [u - x for u, x in zip(us, ws_)]
    o_in = [_dot(a, vn, "nn", True) for a, vn in zip(a_qk, v_new)]
    upd = [_dot(kd, vn, "tn", True) for kd, vn in zip(k_dec, v_new)]
    outs = [x + y for x, y in zip(o_st, o_in)]
    new_states = [s * jnp.exp(gl) + x for s, gl, x in zip(states, g_last, upd)]
    return outs, new_states


def _cspec(tt, w, cb):
    return pl.BlockSpec((tt, w), lambda i, cb=cb: (i, cb))


def _pspec(shape):
    nd = len(shape)
    return pl.BlockSpec(tuple(shape), lambda i, nd=nd: (0,) * nd)


def _ew_fwd(name, f, tiled, params, outs, tt):
    t = tiled[0][0].shape[0]
    nt, npar = len(tiled), len(params)

    def body(*refs):
        tv = [r[...].astype(f32) for r in refs[:nt]]
        pv = [r[...] for r in refs[nt:nt + npar]]
        res = f(*tv, *pv)
        for o, r in zip(refs[nt + npar:], res):
            o[...] = r.astype(o.dtype)

    res = pl.pallas_call(
        body, grid=(t // tt,), name=name,
        in_specs=[_cspec(tt, w, cb) for _, w, cb in tiled] + [_pspec(p.shape) for p in params],
        out_specs=[_cspec(tt, w, 0) for w, _ in outs],
        out_shape=[SDS((t, w), dt) for w, dt in outs],
        compiler_params=_cp(("arbitrary",)),
    )(*[a for a, _, _ in tiled], *params)
    return res


def _ew_bwd(name, f, tiled, params, cots, diff, tt, adds=None):
    t = tiled[0][0].shape[0]
    nt, npar, nc, nd = len(tiled), len(params), len(cots), len(diff)
    adds = adds or [None] * nd
    add_arrs = [a for a in adds if a is not None]
    na = len(add_arrs)
    dwidth = [tiled[k][1] for k, _ in diff]

    def body(*refs):
        tin = refs[:nt]
        pin = refs[nt:nt + npar]
        cin = refs[nt + npar:nt + npar + nc]
        ain = list(refs[nt + npar + nc:nt + npar + nc + na])
        dts = refs[nt + npar + nc + na:nt + npar + nc + na + nd]
        dps = refs[nt + npar + nc + na + nd:]
        tv = [r[...].astype(f32) for r in tin]
        pv = [r[...] for r in pin]

        def g(*dv):
            full = list(tv)
            for n_, (k, _) in enumerate(diff):
                full[k] = dv[n_]
            return tuple(f(*full, *dv[nd:]))

        _, vjp = jax.vjp(g, *[tv[k] for k, _ in diff], *pv)
        grads = vjp(tuple(c[...].astype(f32) for c in cin))
        for n_ in range(nd):
            val = grads[n_]
            if adds[n_] is not None:
                val = val + ain.pop(0)[...].astype(f32)
            dts[n_][...] = val.astype(dts[n_].dtype)

        @pl.when(pl.program_id(0) == 0)
        def _():
            for r in dps:
                r[...] = jnp.zeros(r.shape, f32)

        for r, gp in zip(dps, grads[nd:]):
            r[...] += gp

    res = pl.pallas_call(
        body, grid=(t // tt,), name=name,
        in_specs=[_cspec(tt, w, cb) for _, w, cb in tiled] + [_pspec(p.shape) for p in params]
        + [_cspec(tt, w, cb) for _, w, cb in cots] + [_cspec(tt, a.shape[1], 0) for a in add_arrs],
        out_specs=[_cspec(tt, w, 0) for w in dwidth] + [_pspec(p.shape) for p in params],
        out_shape=[SDS((t, w), dt) for w, (_, dt) in zip(dwidth, diff)] + [SDS(p.shape, f32) for p in params],
        compiler_params=_cp(("arbitrary",)),
    )(*[a for a, _, _ in tiled], *params, *[a for a, _, _ in cots], *add_arrs)
    return res[:nd], res[nd:]


def _mm(name, a, b, mode, tm, tn, out_dtype=f32, b_outer=False):
    ij = (lambda g0, g1: (g1, g0)) if b_outer else (lambda g0, g1: (g0, g1))
    if mode == "nn":
        (m, k), (k2, n) = a.shape, b.shape
        a_spec = pl.BlockSpec((tm, k), lambda g0, g1: (ij(g0, g1)[0], 0))
        b_spec = pl.BlockSpec((k, tn), lambda g0, g1: (0, ij(g0, g1)[1]))
    elif mode == "nt":
        (m, k), (n, k2) = a.shape, b.shape
        a_spec = pl.BlockSpec((tm, k), lambda g0, g1: (ij(g0, g1)[0], 0))
        b_spec = pl.BlockSpec((tn, k), lambda g0, g1: (ij(g0, g1)[1], 0))
    else:
        (k, m), (k2, n) = a.shape, b.shape
        a_spec = pl.BlockSpec((k, tm), lambda g0, g1: (0, ij(g0, g1)[0]))
        b_spec = pl.BlockSpec((k, tn), lambda g0, g1: (0, ij(g0, g1)[1]))
    assert k == k2 and m % tm == 0 and n % tn == 0, (name, a.shape, b.shape, mode)
    assert a.dtype == bf16 and b.dtype == bf16, name

    def body(a_ref, b_ref, o_ref):
        o_ref[...] = lax.dot_general(a_ref[...], b_ref[...], _DIMS[mode], preferred_element_type=f32).astype(o_ref.dtype)

    return pl.pallas_call(
        body, grid=(n // tn, m // tm) if b_outer else (m // tm, n // tn), name=name,
        in_specs=[a_spec, b_spec], out_specs=pl.BlockSpec((tm, tn), lambda g0, g1: ij(g0, g1)),
        out_shape=SDS((m, n), out_dtype), compiler_params=_cp(("parallel", "parallel")),
    )(a, b)


CONV_K = 4
CONV_W = 3 * DN_W
HALO = 8


def _conv_fwd(proj, w, tt):
    t = proj.shape[0]
    nb8 = tt // HALO

    def body(x_ref, h_ref, w_ref, y_ref, xe):
        i = pl.program_id(0)
        xe[0:HALO, :] = jnp.where(i == 0, 0.0, h_ref[...])
        xe[HALO:, :] = x_ref[...]
        wv = w_ref[...]
        acc = jnp.zeros((tt, CONV_W), f32)
        for k in range(CONV_K):
            acc = acc + wv[k:k + 1, :] * xe[pl.ds(HALO - (CONV_K - 1) + k, tt), :]
        y_ref[...] = acc

    return pl.pallas_call(
        body, grid=(t // tt,), name="conv_fwd",
        in_specs=[pl.BlockSpec((tt, CONV_W), lambda i: (i, 0)),
                  pl.BlockSpec((HALO, CONV_W), lambda i: (jnp.maximum(i * nb8 - 1, 0), 0)),
                  _pspec(w.shape)],
        out_specs=pl.BlockSpec((tt, CONV_W), lambda i: (i, 0)),
        out_shape=SDS((t, CONV_W), f32),
        scratch_shapes=[pltpu.VMEM((tt + HALO, CONV_W), f32)],
        compiler_params=_cp(("arbitrary",)),
    )(proj, proj, w)


def _conv_bwd(proj, dy, w, tt):
    t = proj.shape[0]
    nb8 = tt // HALO
    last8 = t // HALO - 1
    nsteps = t // tt

    def body(x_ref, h_ref, dy_ref, n_ref, w_ref, dx_ref, dw_ref, xe, dye):
        i = pl.program_id(0)
        xe[0:HALO, :] = jnp.where(i == 0, 0.0, h_ref[...])
        xe[HALO:, :] = x_ref[...]
        dye[0:tt, :] = dy_ref[...]
        dye[tt:, :] = jnp.where(i == nsteps - 1, 0.0, n_ref[...])
        wv = w_ref[...]
        dyv = dy_ref[...]
        acc = jnp.zeros((tt, CONV_W), f32)

        @pl.when(i == 0)
        def _():
            dw_ref[...] = jnp.zeros(dw_ref.shape, f32)

        for k in range(CONV_K):
            acc = acc + wv[k:k + 1, :] * dye[pl.ds(CONV_K - 1 - k, tt), :]
            dw_ref[k:k + 1, :] += jnp.sum(dyv * xe[pl.ds(HALO - (CONV_K - 1) + k, tt), :], axis=0, keepdims=True)
        dx_ref[...] = acc.astype(dx_ref.dtype)

    return pl.pallas_call(
        body, grid=(nsteps,), name="conv_bwd",
        in_specs=[pl.BlockSpec((tt, CONV_W), lambda i: (i, 0)),
                  pl.BlockSpec((HALO, CONV_W), lambda i: (jnp.maximum(i * nb8 - 1, 0), 0)),
                  pl.BlockSpec((tt, CONV_W), lambda i: (i, 0)),
                  pl.BlockSpec((HALO, CONV_W), lambda i: (jnp.minimum((i + 1) * nb8, last8), 0)),
                  _pspec(w.shape)],
        out_specs=[pl.BlockSpec((tt, CONV_W), lambda i: (i, 0)), _pspec(w.shape)],
        out_shape=[SDS((t, CONV_W), bf16), SDS(w.shape, f32)],
        scratch_shapes=[pltpu.VMEM((tt + HALO, CONV_W), f32), pltpu.VMEM((tt + HALO, CONV_W), f32)],
        compiler_params=_cp(("arbitrary",)),
    )(proj, proj, dy, dy, w)


PREP_CHUNKS = 4


def _head_cols(part, h):
    return slice(part * DN_W + h * DN_D, part * DN_W + (h + 1) * DN_D)


def _prep_operands(qkv_ref, gb_ref):
    inst = [(slice(ch * DN_C, (ch + 1) * DN_C), h) for ch in range(PREP_CHUNKS) for h in range(DN_H)]
    ks = [qkv_ref[rs, _head_cols(1, h)] for rs, h in inst]
    vs = [qkv_ref[rs, _head_cols(2, h)] for rs, h in inst]
    gbvs = [gb_ref[ch * DN_C:(ch + 1) * DN_C, :] for ch in range(PREP_CHUNKS)]
    return inst, ks, vs, gbvs


def _dn_prep_fwd(qkv, gb):
    t = qkv.shape[0]
    rows = PREP_CHUNKS * DN_C

    def body(qkv_ref, gb_ref, u_ref, w_ref, inv_ref):
        inv_ref[...] = jnp.zeros(inv_ref.shape, f32)
        inst, ks, vs, gbvs = _prep_operands(qkv_ref, gb_ref)
        mats, rhss = _f_dn_pre(ks, vs, gbvs)
        invs = _inv_unit_lower(mats)
        uws = [_raw_dot(inv, rhs, "nn", True) for inv, rhs in zip(invs, rhss)]
        for (rs, h), inv, uw in zip(inst, invs, uws):
            u_ref[rs, _head_cols(0, h)] = uw[:, :DN_D]
            w_ref[rs, _head_cols(0, h)] = uw[:, DN_D:]
            inv_ref[rs, h * DN_D:h * DN_D + DN_C] = inv

    return pl.pallas_call(
        body, grid=(t // rows,), name="dn_prep_fwd",
        in_specs=[pl.BlockSpec((rows, CONV_W), lambda i: (i, 0)), pl.BlockSpec((rows, 128), lambda i: (i, 0))],
        out_specs=[pl.BlockSpec((rows, DN_W), lambda i: (i, 0))] * 3,
        out_shape=[SDS((t, DN_W), f32)] * 3,
        compiler_params=_cp(("arbitrary",)),
    )(qkv, gb)


def _dn_prep_bwd(qkv, gb, inv_all, du, dw, dqk1, dgb1):
    t = qkv.shape[0]
    rows = PREP_CHUNKS * DN_C

    def body(qkv_ref, gb_ref, inv_ref, du_ref, dw_ref, dqk1_ref, dgb1_ref, dqkv_ref, dgb_ref):
        inst, ks, vs, gbvs = _prep_operands(qkv_ref, gb_ref)
        (_, rhss), vjp = jax.vjp(_f_dn_pre, ks, vs, gbvs)
        invs = [inv_ref[rs, h * DN_D:h * DN_D + DN_C] for rs, h in inst]
        dxs = [jnp.concatenate([du_ref[rs, _head_cols(0, h)], dw_ref[rs, _head_cols(0, h)]], axis=1) for rs, h in inst]
        uws = [_raw_dot(inv, rhs, "nn", True) for inv, rhs in zip(invs, rhss)]
        drhss = [_raw_dot(inv, dx, "tn", True) for inv, dx in zip(invs, dxs)]
        das = [-_raw_dot(dr, uw, "nt", True) for dr, uw in zip(drhss, uws)]
        dks, dvs, dgbvs = vjp((das, drhss))
        for (rs, h), dk, dv in zip(inst, dks, dvs):
            dqkv_ref[rs, _head_cols(0, h)] = dqk1_ref[rs, _head_cols(0, h)]
            dqkv_ref[rs, _head_cols(1, h)] = dk + dqk1_ref[rs, _head_cols(1, h)]
            dqkv_ref[rs, _head_cols(2, h)] = dv
        for ch, dgbv in enumerate(dgbvs):
            rs = slice(ch * DN_C, (ch + 1) * DN_C)
            dgb_ref[rs, :] = dgbv + dgb1_ref[rs, :]

    return pl.pallas_call(
        body, grid=(t // rows,), name="dn_prep_bwd",
        in_specs=[pl.BlockSpec((rows, CONV_W), lambda i: (i, 0)), pl.BlockSpec((rows, 128), lambda i: (i, 0)),
                  pl.BlockSpec((rows, DN_W), lambda i: (i, 0)),
                  pl.BlockSpec((rows, DN_W), lambda i: (i, 0)), pl.BlockSpec((rows, DN_W), lambda i: (i, 0)),
                  pl.BlockSpec((rows, 2 * DN_W), lambda i: (i, 0)), pl.BlockSpec((rows, 128), lambda i: (i, 0))],
        out_specs=[pl.BlockSpec((rows, CONV_W), lambda i: (i, 0)), pl.BlockSpec((rows, 128), lambda i: (i, 0))],
        out_shape=[SDS((t, CONV_W), f32), SDS((t, 128), f32)],
        compiler_params=_cp(("arbitrary",)),
    )(qkv, gb, inv_all, du, dw, dqk1, dgb1)


def _dn_scan_fwd(qkv, gb, u, w):
    t = qkv.shape[0]
    n = t // DN_C
    heads = range(DN_H)

    def body(qkv_ref, gb_ref, u_ref, w_ref, o_ref, s_ref, state):
        @pl.when(pl.program_id(0) == 0)
        def _():
            state[...] = jnp.zeros(state.shape, f32)

        states = [state[h] for h in heads]
        for h in heads:
            s_ref[0, h] = states[h]
        outs, new = _f_dn_scan(states, [qkv_ref[:, _head_cols(0, h)] for h in heads],
                               [qkv_ref[:, _head_cols(1, h)] for h in heads],
                               [u_ref[:, _head_cols(0, h)] for h in heads],
                               [w_ref[:, _head_cols(0, h)] for h in heads], gb_ref[...])
        for h in heads:
            o_ref[:, _head_cols(0, h)] = outs[h]
            state[h] = new[h]

    return pl.pallas_call(
        body, grid=(n,), name="dn_scan_fwd",
        in_specs=[pl.BlockSpec((DN_C, 2 * DN_W), lambda i: (i, 0)), pl.BlockSpec((DN_C, 128), lambda i: (i, 0)),
                  pl.BlockSpec((DN_C, DN_W), lambda i: (i, 0)), pl.BlockSpec((DN_C, DN_W), lambda i: (i, 0))],
        out_specs=[pl.BlockSpec((DN_C, DN_W), lambda i: (i, 0)),
                   pl.BlockSpec((1, DN_H, DN_D, DN_D), lambda i: (i, 0, 0, 0))],
        out_shape=[SDS((t, DN_W), f32), SDS((n, DN_H, DN_D, DN_D), f32)],
        scratch_shapes=[pltpu.VMEM((DN_H, DN_D, DN_D), f32)],
        compiler_params=_cp(("arbitrary",)),
    )(qkv, gb, u, w)


def _dn_scan_bwd(qkv, gb, u, w, states, do):
    t = qkv.shape[0]
    n = t // DN_C
    rev = lambda i: (n - 1 - i, 0)
    heads = range(DN_H)

    def body(qkv_ref, gb_ref, u_ref, w_ref, s_ref, do_ref, dqk_ref, du_ref, dw_ref, dgb_ref, dstate):
        @pl.when(pl.program_id(0) == 0)
        def _():
            dstate[...] = jnp.zeros(dstate.shape, f32)

        _, vjp = jax.vjp(_f_dn_scan, [s_ref[0, h] for h in heads], [qkv_ref[:, _head_cols(0, h)] for h in heads],
                         [qkv_ref[:, _head_cols(1, h)] for h in heads], [u_ref[:, _head_cols(0, h)] for h in heads],
                         [w_ref[:, _head_cols(0, h)] for h in heads], gb_ref[...])
        ds, dq, dk, du, dw, dgbv = vjp(([do_ref[:, _head_cols(0, h)] for h in heads], [dstate[h] for h in heads]))
        for h in heads:
            dstate[h] = ds[h]
            dqk_ref[:, _head_cols(0, h)] = dq[h]
            dqk_ref[:, _head_cols(1, h)] = dk[h]
            du_ref[:, _head_cols(0, h)] = du[h]
            dw_ref[:, _head_cols(0, h)] = dw[h]
        dgb_ref[...] = dgbv

    return pl.pallas_call(
        body, grid=(n,), name="dn_scan_bwd",
        in_specs=[pl.BlockSpec((DN_C, 2 * DN_W), rev), pl.BlockSpec((DN_C, 128), rev),
                  pl.BlockSpec((DN_C, DN_W), rev), pl.BlockSpec((DN_C, DN_W), rev),
                  pl.BlockSpec((1, DN_H, DN_D, DN_D), lambda i: (n - 1 - i, 0, 0, 0)),
                  pl.BlockSpec((DN_C, DN_W), rev)],
        out_specs=[pl.BlockSpec((DN_C, 2 * DN_W), rev), pl.BlockSpec((DN_C, DN_W), rev),
                   pl.BlockSpec((DN_C, DN_W), rev), pl.BlockSpec((DN_C, 128), rev)],
        out_shape=[SDS((t, 2 * DN_W), f32), SDS((t, DN_W), f32), SDS((t, DN_W), f32), SDS((t, 128), f32)],
        scratch_shapes=[pltpu.VMEM((DN_H, DN_D, DN_D), f32)],
        compiler_params=_cp(("arbitrary",)),
    )(qkv, gb, u, w, states, do)


def _sw_attn_fwd(q2, k2, v2, d):
    l = q2.shape[0]
    nb = l // SW_B
    cur = pl.BlockSpec((SW_B, SW_W), lambda r, n: (n, r))
    prev = pl.BlockSpec((SW_B, SW_W), lambda r, n: (jnp.maximum(n - 1, 0), r))

    def body(q_ref, kp_ref, kc_ref, vp_ref, vc_ref, o_ref, l_ref):
        o, lse = _f_attn(q_ref[...], kp_ref[...], kc_ref[...], vp_ref[...], vc_ref[...], pl.program_id(1) > 0)
        o_ref[...] = o
        l_ref[...] = lse

    return pl.pallas_call(
        body, grid=(d, nb), name=f"sw_attn_fwd_d{d}",
        in_specs=[cur, prev, cur, prev, cur], out_specs=[cur, cur],
        out_shape=[SDS(q2.shape, f32)] * 2, compiler_params=_cp(("arbitrary", "arbitrary")),
    )(q2, k2, k2, v2, v2)


def _sw_attn_bwd(q2, k2, v2, do2, dl2, d):
    l = q2.shape[0]
    nb = l // SW_B
    clamp = lambda n: jnp.minimum(n, nb - 1)
    cur = pl.BlockSpec((SW_B, SW_W), lambda r, n: (clamp(n), r))
    prev = pl.BlockSpec((SW_B, SW_W), lambda r, n: (jnp.maximum(clamp(n) - 1, 0), r))
    lag = pl.BlockSpec((SW_B, SW_W), lambda r, n: (jnp.maximum(n - 1, 0), r))

    def body(q_ref, kp_ref, kc_ref, vp_ref, vc_ref, do_ref, dl_ref, dq_ref, dk_ref, dv_ref, dk_hold, dv_hold):
        n = pl.program_id(1)

        @pl.when(n < nb)
        def _():
            has_prev = n > 0
            _, vjp = jax.vjp(lambda q, kp, kc, vp, vc: _f_attn(q, kp, kc, vp, vc, has_prev),
                             q_ref[...], kp_ref[...], kc_ref[...], vp_ref[...], vc_ref[...])
            dq, dkp, dkc, dvp, dvc = vjp((do_ref[...], dl_ref[...]))
            dq_ref[...] = dq
            dk_ref[...] = dk_hold[...] + dkp
            dv_ref[...] = dv_hold[...] + dvp
            dk_hold[...] = dkc
            dv_hold[...] = dvc

        @pl.when(n == nb)
        def _():
            dk_ref[...] = dk_hold[...]
            dv_ref[...] = dv_hold[...]

    return pl.pallas_call(
        body, grid=(d, nb + 1), name=f"sw_attn_bwd_d{d}",
        in_specs=[cur, prev, cur, prev, cur, cur, cur], out_specs=[cur, lag, lag],
        out_shape=[SDS(q2.shape, f32)] * 3,
        scratch_shapes=[pltpu.VMEM((SW_B, SW_W), f32)] * 2,
        compiler_params=_cp(("arbitrary", "arbitrary")),
    )(q2, k2, k2, v2, v2, do2, dl2)


def _loss_head(y, target, tt):
    t = y.shape[0]

    def body(y_ref, t_ref, dy_ref, l_ref):
        @pl.when(pl.program_id(0) == 0)
        def _():
            l_ref[...] = jnp.zeros(l_ref.shape, f32)

        err = y_ref[...] - t_ref[...]
        dy_ref[...] = err * (1.0 / D)
        l_ref[...] += 0.5 * jnp.sum(jnp.sum(err * err, axis=1, keepdims=True) * (1.0 / D), axis=0, keepdims=True)

    return pl.pallas_call(
        body, grid=(t // tt,), name="loss_head",
        in_specs=[pl.BlockSpec((tt, D), lambda i: (i, 0))] * 2,
        out_specs=[pl.BlockSpec((tt, D), lambda i: (i, 0)), pl.BlockSpec((8, 128), lambda i: (0, 0))],
        out_shape=[SDS((t, D), f32), SDS((8, 128), f32)],
        compiler_params=_cp(("arbitrary",)),
    )(y, target)


def _adamw(name, w, m, v, gparts, tr):
    r, c = w.shape
    p = gparts.shape[0]
    assert r % tr == 0, (name, w.shape, tr)

    def body(w_ref, m_ref, v_ref, g_ref, go_ref, d_ref, mo_ref, vo_ref):
        g = g_ref[0].astype(f32)
        for k in range(1, p):
            g = g + g_ref[k].astype(f32)
        wv = w_ref[...]
        mn = ADAM_B1 * m_ref[...] + (1.0 - ADAM_B1) * g
        vn = ADAM_B2 * v_ref[...] + (1.0 - ADAM_B2) * jnp.square(g)
        m_hat = mn / (1.0 - ADAM_B1 ** ADAM_STEP)
        v_hat = vn / (1.0 - ADAM_B2 ** ADAM_STEP)
        go_ref[...] = g
        d_ref[...] = -ADAM_LR * (m_hat / (jnp.sqrt(v_hat) + ADAM_EPS) + ADAM_WD * wv)
        mo_ref[...] = mn
        vo_ref[...] = vn

    spec = pl.BlockSpec((tr, c), lambda i: (i, 0))
    return pl.pallas_call(
        body, grid=(r // tr,), name=name,
        in_specs=[spec, spec, spec, pl.BlockSpec((p, tr, c), lambda i: (0, i, 0))],
        out_specs=[spec] * 4, out_shape=[SDS((r, c), f32)] * 4,
        compiler_params=_cp(("arbitrary",)),
    )(w, m, v, gparts)


def _mod_cols(c_all, w_mod, b_cols):
    nl, _, wc = w_mod.shape

    def body(c_ref, w_ref, b_ref, o_ref):
        o_ref[0] = _raw_dot(jax.nn.silu(c_ref[...]), w_ref[0], "nn", True) + b_ref[0]

    return pl.pallas_call(
        body, grid=(nl,), name="mod_cols",
        in_specs=[pl.BlockSpec((NDEV, D), lambda l: (0, 0)), pl.BlockSpec((1, D, wc), lambda l: (l, 0, 0)),
                  pl.BlockSpec((1, 1, wc), lambda l: (l, 0, 0))],
        out_specs=pl.BlockSpec((1, NDEV, wc), lambda l: (l, 0, 0)),
        out_shape=SDS((nl, NDEV, wc), f32), compiler_params=_cp(("arbitrary",)),
    )(c_all, w_mod, b_cols)


def _wmod_grad(c_all, dmod_cols):
    nl, _, wc = dmod_cols.shape

    def body(c_ref, d_ref, o_ref):
        o_ref[0, 0] = _raw_dot(jax.nn.silu(c_ref[...]), d_ref[0], "tn", True)

    return pl.pallas_call(
        body, grid=(nl,), name="wmod_grad",
        in_specs=[pl.BlockSpec((NDEV, D), lambda l: (0, 0)), pl.BlockSpec((1, NDEV, wc), lambda l: (l, 0, 0))],
        out_specs=pl.BlockSpec((1, 1, D, wc), lambda l: (0, l, 0, 0)),
        out_shape=SDS((1, nl, D, wc), f32), compiler_params=_cp(("arbitrary",)),
    )(c_all, dmod_cols)


def _me_and_peers():
    x, y, c = (lax.axis_index(a) for a in AXES)
    peers = []
    for k in range(1, NDEV):
        px = 1 - x if (k >> 2) & 1 else x
        py = 1 - y if (k >> 1) & 1 else y
        pc = 1 - c if k & 1 else c
        peers.append(((px, py, pc), 4 * px + 2 * py + pc))
    return 4 * x + 2 * y + c, peers


_ANY = pl.BlockSpec(memory_space=pl.ANY)


def _all_gather(name, a):
    def body(a_ref, o_ref, send_sems, recv_sems, local_sem):
        me, peers = _me_and_peers()
        mine = pltpu.make_async_copy(a_ref, o_ref.at[me], local_sem)
        mine.start()
        copies = [pltpu.make_async_remote_copy(a_ref, o_ref.at[me], send_sems.at[k], recv_sems.at[k],
                                               device_id=dev, device_id_type=pl.DeviceIdType.MESH)
                  for k, (dev, _) in enumerate(peers)]
        for cp in copies:
            cp.start()
        for cp in copies:
            cp.wait()
        mine.wait()

    return pl.pallas_call(
        body, name=name, in_specs=[_ANY], out_specs=_ANY, out_shape=SDS((NDEV,) + a.shape, a.dtype),
        scratch_shapes=[pltpu.SemaphoreType.DMA((NDEV - 1,)), pltpu.SemaphoreType.DMA((NDEV - 1,)), pltpu.SemaphoreType.DMA],
        compiler_params=pltpu.CompilerParams(has_side_effects=True),
    )(a)


def _all_to_all(name, a):
    def body(a_ref, o_ref, send_sems, recv_sems, local_sem):
        me, peers = _me_and_peers()
        mine = pltpu.make_async_copy(a_ref.at[me], o_ref.at[me], local_sem)
        mine.start()
        copies = [pltpu.make_async_remote_copy(a_ref.at[pid], o_ref.at[me], send_sems.at[k], recv_sems.at[k],
                                               device_id=dev, device_id_type=pl.DeviceIdType.MESH)
                  for k, (dev, pid) in enumerate(peers)]
        for cp in copies:
            cp.start()
        for cp in copies:
            cp.wait()
        mine.wait()

    return pl.pallas_call(
        body, name=name, in_specs=[_ANY], out_specs=_ANY, out_shape=SDS(a.shape, a.dtype),
        scratch_shapes=[pltpu.SemaphoreType.DMA((NDEV - 1,)), pltpu.SemaphoreType.DMA((NDEV - 1,)), pltpu.SemaphoreType.DMA],
        compiler_params=pltpu.CompilerParams(has_side_effects=True),
    )(a)


_HBM = pl.BlockSpec(memory_space=pltpu.HBM)
_SEM = pl.BlockSpec(memory_space=pltpu.SEMAPHORE)
_FLOW = pltpu.SideEffectType.DATAFLOW_SIDE_EFFECTING


def _exchange_copies(srcs, lands, send_sems, recv_sems, scatter):
    me, peers = _me_and_peers()
    copies = []
    for a, (src, land) in enumerate(zip(srcs, lands)):
        for k, (dev, pid) in enumerate(peers):
            copies.append(pltpu.make_async_remote_copy(
                src.at[pid] if scatter else src, land.at[me], send_sems.at[a * (NDEV - 1) + k],
                recv_sems.at[a * (NDEV - 1) + k], device_id=dev, device_id_type=pl.DeviceIdType.MESH))
    return copies


def _exchange_start(name, srcs, lands, scatter):
    n = len(srcs)
    nsem = n * (NDEV - 1)

    def body(*refs):
        for cp in _exchange_copies(refs[:n], refs[n:2 * n], refs[2 * n], refs[2 * n + 1], scatter):
            cp.start()
        token = refs[-1]
        token[...] = jnp.zeros(token.shape, token.dtype)

    hbm = lambda a: pltpu.with_memory_space_constraint(a, pltpu.HBM)
    res = pl.pallas_call(
        body, name=name,
        out_shape=(pltpu.SemaphoreType.DMA((nsem,)), pltpu.SemaphoreType.DMA((nsem,)),
                   *[pltpu.HBM(a.shape, a.dtype) for a in srcs], *[pltpu.HBM(a.shape, a.dtype) for a in lands],
                   SDS((8, 128), f32)),
        in_specs=[_HBM] * (2 * n), out_specs=(_SEM, _SEM, *([_HBM] * (2 * n)), pl.BlockSpec(memory_space=pltpu.VMEM)),
        input_output_aliases={i: 2 + i for i in range(2 * n)},
        compiler_params=pltpu.CompilerParams(has_side_effects=_FLOW),
    )(*[hbm(a) for a in srcs], *[hbm(a) for a in lands])
    return res[0], res[1], list(res[2:2 + n]), list(res[2 + n:2 + 2 * n]), res[-1]


def _exchange_wait(name, handle, after, scatter):
    send_sems, recv_sems, srcs, lands, _ = handle
    n = len(srcs)

    def body(*refs):
        for cp in _exchange_copies(refs[:n], refs[n:2 * n], refs[2 * n], refs[2 * n + 1], scatter):
            cp.wait_send()
            cp.wait_recv()
        token = refs[-1]
        token[...] = jnp.zeros(token.shape, token.dtype)

    res = pl.pallas_call(
        body, name=name,
        out_shape=(*[pltpu.HBM(a.shape, a.dtype) for a in srcs + lands], SDS((8, 128), f32)),
        in_specs=[_HBM] * (2 * n) + [_SEM, _SEM, _ANY],
        out_specs=(*([_HBM] * (2 * n)), pl.BlockSpec(memory_space=pltpu.VMEM)),
        input_output_aliases={i: i for i in range(2 * n)},
        compiler_params=pltpu.CompilerParams(has_side_effects=_FLOW),
    )(*srcs, *lands, send_sems, recv_sems, after)
    return list(res[n:2 * n]), res[-1]


def _own_slot(a, me):
    return lax.dynamic_update_slice(jnp.zeros((NDEV,) + a.shape, a.dtype), a[None], (me,) + (0,) * a.ndim)


CB_Z = 3
CB_GM = 8
CB_SW = 10
CB_AB = 38


def _to_sub(a, d):
    return a.reshape(a.shape[0] // d, d * a.shape[1])


def _from_sub(a, d):
    return a.reshape(a.shape[0] * d, a.shape[1] // d)


def _sw_pre_tiles(proj, cs, sn):
    return [(proj, SW_W, CB_SW + 3 * g + j) for g in range(3) for j in range(2)] + [(cs, SW_W, 0), (sn, SW_W, 0)]


def _layer_fwd(x, p, tabs, late_weights):
    cs, sn = tabs
    sh1, sc1, g1, sh2, sc2, g2 = (p["mod"][k] for k in range(6))
    (h1,) = _ew_fwd("modnorm1_fwd", _f_modnorm, [(x, D, 0)], [p["mixw"], sc1, sh1], [(D, bf16)], 512)
    proj = _mm("in_proj", h1, p["wi"], "nn", 512, 1664, b_outer=True)
    y = _conv_fwd(proj, p["conv"], 256)
    qkv, gb = _ew_fwd("dn_act_fwd", _f_dn_act, [(y, CONV_W, 0), (proj, 128, CB_AB)], [p["alog"], p["dtb"]],
                      [(CONV_W, f32), (128, f32)], 256)
    u, w, inv = _dn_prep_fwd(qkv, gb)
    o, states = _dn_scan_fwd(qkv, gb, u, w)
    (ya,) = _ew_fwd("dn_out_fwd", _f_dn_out, [(o, DN_W, 0), (proj, DN_W, CB_Z)], [p["wn"]], [(DN_W, bf16)], 256)
    (yb,) = _ew_fwd("gm_fwd", _f_gm, [(proj, GM_W, CB_GM), (proj, GM_W, CB_GM + 1)],
                    [p["lng"], p["lnb"], p["ws"], p["bst"]], [(GM_W, bf16)], GM_C)
    qk = _ew_fwd("sw_pre_fwd", _f_sw_pre, _sw_pre_tiles(proj, cs, sn), [p["wq"], p["wk"]], [(SW_W, f32)] * 6, 512)
    ol = []
    for g, d in enumerate(SW_DIL):
        c0 = (CB_SW + 3 * g + 2) * SW_W
        vg = proj[:, c0:c0 + SW_W]
        o2, l2 = _sw_attn_fwd(_to_sub(qk[2 * g], d), _to_sub(qk[2 * g + 1], d), _to_sub(vg, d), d)
        ol += [_from_sub(o2, d), _from_sub(l2, d)]
    (yc,) = _ew_fwd("sw_merge_fwd", _f_sw_merge, [(a, SW_W, 0) for a in ol], [], [(SW_W, bf16)], 512)
    ycat = jnp.concatenate([ya, yb, yc], axis=1)
    p.update(late_weights(ycat))
    m1 = _mm("out_proj", ycat, p["wo"], "nn", 512, 1024)
    (x2,) = _ew_fwd("resid1_fwd", _f_resid, [(x, D, 0), (m1, D, 0)], [g1], [(D, f32)], 512)
    (h2,) = _ew_fwd("modnorm2_fwd", _f_modnorm, [(x2, D, 0)], [p["ffnw"], sc2, sh2], [(D, bf16)], 512)
    gu = _mm("ffn_in", h2, p["wfi"], "nn", 512, 1408, b_outer=True)
    (act,) = _ew_fwd("swiglu_fwd", _f_swiglu, [(gu, 2 * FFN, 0)], [], [(FFN, bf16)], 256)
    m2 = _mm("ffn_out", act, p["wfo"], "nn", 512, 1024)
    (x3,) = _ew_fwd("resid2_fwd", _f_resid, [(x2, D, 0), (m2, D, 0)], [g2], [(D, f32)], 512)
    res = dict(x=x, h1=h1, proj=proj, y=y, qkv=qkv, gb=gb, u=u, w=w, inv=inv, states=states, o=o, qk=list(qk), ol=ol,
               ycat=ycat, m1=m1, x2=x2, h2=h2, gu=gu, act=act, m2=m2)
    return x3, res


def _layer_bwd(dx3, p, r, tabs, ffn_grads_ready):
    cs, sn = tabs
    sh1, sc1, g1, sh2, sc2, g2 = (p["mod"][k] for k in range(6))
    proj = r["proj"]
    (dx2a, dm2), (dg2,) = _ew_bwd("resid2_bwd", _f_resid, [(r["x2"], D, 0), (r["m2"], D, 0)], [g2], [(dx3, D, 0)],
                                  [(0, f32), (1, bf16)], 512)
    dact = _mm("ffn_out_dx", dm2, p["wfo"], "nt", 512, 1408, bf16)
    dwfo = _mm("ffn_out_dw", r["act"], dm2, "tn", 256, 1024, bf16)
    (dgu_cat,), _ = _ew_bwd("swiglu_bwd", _f_swiglu, [(r["gu"], 2 * FFN, 0)], [], [(dact, FFN, 0)], [(0, bf16)], 256)
    dh2 = _mm("ffn_in_dx", dgu_cat, p["wfi"], "nt", 512, 512)
    dwfi = _mm("ffn_in_dw", r["h2"], dgu_cat, "tn", 512, 512, bf16)
    g1 = g1 + ffn_grads_ready(dwfi, dwfo)[0, 0]
    (dx2,), (dffnw, dsc2, dsh2) = _ew_bwd("modnorm2_bwd", _f_modnorm, [(r["x2"], D, 0)], [p["ffnw"], sc2, sh2],
                                          [(dh2, D, 0)], [(0, f32)], 512, adds=[dx2a])
    (dxa, dm1), (dg1,) = _ew_bwd("resid1_bwd", _f_resid, [(r["x"], D, 0), (r["m1"], D, 0)], [g1], [(dx2, D, 0)],
                                 [(0, f32), (1, bf16)], 512)
    dycat = _mm("out_proj_dx", dm1, p["wo"], "nt", 512, 1024)
    dwo = _mm("out_proj_dw", r["ycat"], dm1, "tn", 512, 512, bf16)
    dol, _ = _ew_bwd("sw_merge_bwd", _f_sw_merge, [(a, SW_W, 0) for a in r["ol"]], [], [(dycat, SW_W, 3)],
                     [(k, f32) for k in range(6)], 512)
    dqk, dvs = [], []
    for g, d in enumerate(SW_DIL):
        c0 = (CB_SW + 3 * g + 2) * SW_W
        vg = proj[:, c0:c0 + SW_W]
        dq2, dk2, dv2 = _sw_attn_bwd(_to_sub(r["qk"][2 * g], d), _to_sub(r["qk"][2 * g + 1], d), _to_sub(vg, d),
                                     _to_sub(dol[2 * g], d), _to_sub(dol[2 * g + 1], d), d)
        dqk += [_from_sub(dq2, d), _from_sub(dk2, d)]
        dvs.append(_from_sub(dv2, d).astype(bf16))
    dqk_raw, (dwq, dwk) = _ew_bwd("sw_pre_bwd", _f_sw_pre, _sw_pre_tiles(proj, cs, sn), [p["wq"], p["wk"]],
                                  [(a, SW_W, 0) for a in dqk], [(k, bf16) for k in range(6)], 512)
    (dgm_u, dgm_v), (dlng, dlnb, dws, dbst) = _ew_bwd(
        "gm_bwd", _f_gm, [(proj, GM_W, CB_GM), (proj, GM_W, CB_GM + 1)], [p["lng"], p["lnb"], p["ws"], p["bst"]],
        [(dycat, GM_W, 2)], [(0, bf16), (1, bf16)], GM_C)
    (do, dz), (dwn,) = _ew_bwd("dn_out_bwd", _f_dn_out, [(r["o"], DN_W, 0), (proj, DN_W, CB_Z)], [p["wn"]],
                               [(dycat, DN_W, 0)], [(0, f32), (1, bf16)], 256)
    dqk1, du, dw, dgb1 = _dn_scan_bwd(r["qkv"], r["gb"], r["u"], r["w"], r["states"], do)
    dqkv, dgb = _dn_prep_bwd(r["qkv"], r["gb"], r["inv"], du, dw, dqk1, dgb1)
    (dy, dab), (dalog, ddtb) = _ew_bwd("dn_act_bwd", _f_dn_act, [(r["y"], CONV_W, 0), (proj, 128, CB_AB)],
                                       [p["alog"], p["dtb"]], [(dqkv, CONV_W, 0), (dgb, 128, 0)],
                                       [(0, f32), (1, bf16)], 256)
    dxc, dconv = _conv_bwd(proj, dy, p["conv"], 256)
    dproj = jnp.concatenate([dxc, dz, dgm_u, dgm_v, dqk_raw[0], dqk_raw[1], dvs[0], dqk_raw[2], dqk_raw[3], dvs[1],
                             dqk_raw[4], dqk_raw[5], dvs[2], dab], axis=1)
    dh1 = _mm("in_proj_dx", dproj, p["wi"], "nt", 512, 512)
    dwi = _mm("in_proj_dw", r["h1"], dproj, "tn", 512, 384, bf16)
    (dx,), (dmixw, dsc1, dsh1) = _ew_bwd("modnorm1_bwd", _f_modnorm, [(r["x"], D, 0)], [p["mixw"], sc1, sh1],
                                         [(dh1, D, 0)], [(0, f32)], 512, adds=[dxa])
    grads = dict(wi=dwi, wo=dwo, wfi=dwfi, wfo=dwfo, conv=dconv, mixw=dmixw, ffnw=dffnw,
                 mod=jnp.stack([dsh1, dsc1, dg1, dsh2, dsc2, dg2]), alog=dalog, dtb=ddtb, wn=dwn, lng=dlng, lnb=dlnb,
                 ws=dws, bst=dbst, wq=dwq, wk=dwk)
    return dx, grads


def _rope_tables(t):
    inv = ROPE_THETA ** (-jnp.arange(0, ROPE_DIM, 2, dtype=f32) / ROPE_DIM)
    ang = jnp.arange(t, dtype=f32)[:, None] * inv[None, :]
    cos, sin = jnp.cos(ang), jnp.sin(ang)
    rest = SW_D - ROPE_DIM
    ch = jnp.concatenate([cos, cos, jnp.ones((t, rest), f32)], axis=1)
    sh = jnp.concatenate([sin, sin, jnp.zeros((t, rest), f32)], axis=1)
    return jnp.tile(ch, (1, SW_H)), jnp.tile(sh, (1, SW_H))


def _pad_last(a, n):
    return jnp.pad(a, [(0, 0)] * (a.ndim - 1) + [(0, n - a.shape[-1])])


def _gather_cols(name, w, pad_to):
    nl, r, cs = w.shape
    g = _all_gather(name, _pad_last(w, pad_to))
    return jnp.transpose(g, (1, 2, 0, 3))[..., :cs].reshape(nl, r, NDEV * cs)


def _scatter_cols(name, gs, cs, pad_to):
    r = gs[0].shape[0]
    parts = jnp.stack([jnp.transpose(_pad_last(g.reshape(r, NDEV, cs), pad_to), (1, 0, 2)) for g in gs], axis=1)
    return _all_to_all(name, parts).reshape(NDEV, len(gs) * r, pad_to)


_SMALL = ("b_mod", "mix_norm_w", "ffn_norm_w", "dn_a_log", "dn_dt_bias", "dn_out_norm_w", "gm_ln_g", "gm_ln_b",
          "gm_w_s", "gm_b_s", "sw_q_norm_w", "sw_k_norm_w")
PACK_TILE = 800 * 128


def _pack(arrs):
    flat = jnp.concatenate([a.reshape(-1) for a in arrs])
    n = -(-flat.shape[0] // PACK_TILE) * PACK_TILE
    return jnp.pad(flat, (0, n - flat.shape[0])).reshape(n // 128, 128)


def _unpack(buf, like):
    flat, out, off = buf.reshape(-1), [], 0
    for a in like:
        out.append(flat[off:off + a.size].reshape(a.shape))
        off += a.size
    return out


def kernel(x, c, w_mod, b_mod, mix_norm_w, ffn_norm_w, w_in, w_out, dn_conv_w, dn_a_log, dn_dt_bias, dn_out_norm_w, gm_ln_g, gm_ln_b, gm_w_s, gm_b_s, sw_q_norm_w, sw_k_norm_w, w_ffn_in, w_ffn_out, loss_target, m_w_mod, m_b_mod, m_mix_norm_w, m_ffn_norm_w, m_w_in, m_w_out, m_dn_conv_w, m_dn_a_log, m_dn_dt_bias, m_dn_out_norm_w, m_gm_ln_g, m_gm_ln_b, m_gm_w_s, m_gm_b_s, m_sw_q_norm_w, m_sw_k_norm_w, m_w_ffn_in, m_w_ffn_out, v_w_mod, v_b_mod, v_mix_norm_w, v_ffn_norm_w, v_w_in, v_w_out, v_dn_conv_w, v_dn_a_log, v_dn_dt_bias, v_dn_out_norm_w, v_gm_ln_g, v_gm_ln_b, v_gm_w_s, v_gm_b_s, v_sw_q_norm_w, v_sw_k_norm_w, v_w_ffn_in, v_w_ffn_out):
    weights = dict(w_mod=w_mod, b_mod=b_mod, mix_norm_w=mix_norm_w, ffn_norm_w=ffn_norm_w, w_in=w_in, w_out=w_out,
                   dn_conv_w=dn_conv_w, dn_a_log=dn_a_log, dn_dt_bias=dn_dt_bias, dn_out_norm_w=dn_out_norm_w,
                   gm_ln_g=gm_ln_g, gm_ln_b=gm_ln_b, gm_w_s=gm_w_s, gm_b_s=gm_b_s, sw_q_norm_w=sw_q_norm_w,
                   sw_k_norm_w=sw_k_norm_w, w_ffn_in=w_ffn_in, w_ffn_out=w_ffn_out)
    mom = dict(w_mod=m_w_mod, b_mod=m_b_mod, mix_norm_w=m_mix_norm_w, ffn_norm_w=m_ffn_norm_w, w_in=m_w_in,
               w_out=m_w_out, dn_conv_w=m_dn_conv_w, dn_a_log=m_dn_a_log, dn_dt_bias=m_dn_dt_bias,
               dn_out_norm_w=m_dn_out_norm_w, gm_ln_g=m_gm_ln_g, gm_ln_b=m_gm_ln_b, gm_w_s=m_gm_w_s, gm_b_s=m_gm_b_s,
               sw_q_norm_w=m_sw_q_norm_w, sw_k_norm_w=m_sw_k_norm_w, w_ffn_in=m_w_ffn_in, w_ffn_out=m_w_ffn_out)
    var = dict(w_mod=v_w_mod, b_mod=v_b_mod, mix_norm_w=v_mix_norm_w, ffn_norm_w=v_ffn_norm_w, w_in=v_w_in,
               w_out=v_w_out, dn_conv_w=v_dn_conv_w, dn_a_log=v_dn_a_log, dn_dt_bias=v_dn_dt_bias,
               dn_out_norm_w=v_dn_out_norm_w, gm_ln_g=v_gm_ln_g, gm_ln_b=v_gm_ln_b, gm_w_s=v_gm_w_s, gm_b_s=v_gm_b_s,
               sw_q_norm_w=v_sw_q_norm_w, sw_k_norm_w=v_sw_k_norm_w, w_ffn_in=v_w_ffn_in, w_ffn_out=v_w_ffn_out)
    names = list(weights)
    xi, tgt = x[0], loss_target[0]
    t = xi.shape[0]
    nl = w_mod.shape[0]
    ax, ay, ac = (lax.axis_index(a) for a in AXES)
    me = 4 * ax + 2 * ay + ac
    mod_cs = w_mod.shape[2]

    shards_a = [_pad_last(w_in.astype(bf16), 640)]
    shards_b = [w_out.astype(bf16), _pad_last(w_ffn_in.astype(bf16), 768), w_ffn_out.astype(bf16)]

    def start_gather(l, tok):
        ha = _exchange_start(f"ag_start_a{l}", [a[l] + tok for a in shards_a], [_own_slot(a[l], me) for a in shards_a], False)
        hb = _exchange_start(f"ag_start_b{l}", [a[l] + ha[4][0, 0].astype(bf16) for a in shards_b],
                             [_own_slot(a[l], me) for a in shards_b], False)
        return ha, hb

    gather = start_gather(0, jnp.zeros((), bf16))

    c = c + gather[1][4][0, 0]
    c_all = _all_gather("ag_c", jnp.broadcast_to(c, (NDEV, D)))[:, 0, :]
    b_cols = lax.dynamic_slice_in_dim(b_mod, me * mod_cs, mod_cs, axis=1)[:, None, :]
    modc = _mod_cols(c_all, w_mod, b_cols)
    mod_tx = jnp.pad(jnp.transpose(modc, (1, 0, 2)), ((0, 0), (0, 8 - nl), (0, 0)))
    mod_rx = _all_to_all("a2a_mod", mod_tx)[:, :nl]
    mod = jnp.transpose(mod_rx, (1, 0, 2)).reshape(nl, 6, 1, D)

    def w_in_full(landed):
        wi = jnp.transpose(landed[0], (1, 0, 2))[..., :w_in.shape[2]].reshape(D, IN_W)
        return jnp.concatenate([wi[:, :2048], wi[:, 2056:], wi[:, 2048:2056], jnp.zeros((D, IN_WA - IN_W), bf16)], axis=-1)

    def late_full(landed):
        go, gfi, gfo = landed
        wfi = jnp.transpose(gfi, (1, 0, 2))[..., :w_ffn_in.shape[2]].reshape(D, 2 * FFN)
        return dict(wo=go.reshape(D, D), wfi=wfi, wfo=gfo.reshape(FFN, D))

    conv = _gather_cols("ag_conv", dn_conv_w, 256)
    pad128 = lambda a: _pad_last(a, 128)[:, None, :]
    params = dict(
        mod=mod, conv=conv,
        mixw=mix_norm_w[:, None, :], ffnw=ffn_norm_w[:, None, :], alog=pad128(dn_a_log), dtb=pad128(dn_dt_bias),
        wn=dn_out_norm_w[:, None, :], lng=gm_ln_g[:, None, :], lnb=gm_ln_b[:, None, :], ws=gm_w_s,
        bst=_pad_last(jnp.transpose(gm_b_s, (0, 2, 1)), 128),
        wq=jnp.pad(sw_q_norm_w[:, None, :], ((0, 0), (0, 7), (0, 128 - SW_D))),
        wk=jnp.pad(sw_k_norm_w[:, None, :], ((0, 0), (0, 7), (0, 128 - SW_D))))
    tabs = _rope_tables(t)

    layer_p, res = [], []
    xc, after = xi, gather[1][4]
    for l in range(nl):
        ha, hb = gather
        landed_a, tok = _exchange_wait(f"ag_wait_a{l}", ha, after, False)
        p = {k: v[l] for k, v in params.items()}
        p["wi"] = w_in_full(landed_a)
        if l + 1 < nl:
            gather = start_gather(l + 1, tok[0, 0].astype(bf16))
            p["mod"] = p["mod"] + gather[1][4][0, 0]
        late = lambda ycat, hb=hb, l=l: late_full(_exchange_wait(f"ag_wait_b{l}", hb, ycat, False)[0])
        layer_p.append(p)
        xc, r = _layer_fwd(xc, p, tabs, late)
        res.append(r)
        after = xc
    dy, lpart = _loss_head(xc, tgt, 512)
    loss = lax.psum(lpart[0, 0], AXES)

    slot = lax.broadcasted_iota(jnp.int32, (NDEV, 1, 1), 0)

    def start_scatter(name, parts):
        return _exchange_start(name, parts, [jnp.where(slot == me, a, jnp.zeros_like(a)) for a in parts], True)

    dxi, gl, h_ffn, h_mix = dy, [None] * nl, [None] * nl, [None] * nl
    for l in reversed(range(nl)):
        def ffn_ready(dwfi, dwfo, l=l):
            h_ffn[l] = start_scatter(f"a2a_start_f{l}", [
                jnp.transpose(_pad_last(dwfi.reshape(D, NDEV, w_ffn_in.shape[2]), 768), (1, 0, 2)),
                dwfo.reshape(NDEV, w_ffn_out.shape[1], D)])
            return h_ffn[l][4]

        dxi, gl[l] = _layer_bwd(dxi, layer_p[l], res[l], tabs, ffn_ready)
        d = gl[l]["wi"]
        d = jnp.concatenate([d[:, :2048], d[:, 4864:4872], d[:, 2048:4864]], axis=-1)
        h_mix[l] = start_scatter(f"a2a_start_m{l}", [
            jnp.transpose(_pad_last(d.reshape(D, NDEV, w_in.shape[2]), 640), (1, 0, 2)),
            gl[l]["wo"].reshape(NDEV, w_out.shape[1], D)])
        if l > 0:
            layer_p[l - 1]["mod"] = layer_p[l - 1]["mod"] + h_mix[l][4][0, 0]
    g = {k: jnp.stack([gl[l][k] for l in range(nl)]) for k in gl[0] if k not in ("wi", "wo", "wfi", "wfo")}

    dmod = g["mod"].reshape(nl, 6 * D)
    small_g = dict(b_mod=dmod, mix_norm_w=g["mixw"][:, 0], ffn_norm_w=g["ffnw"][:, 0], dn_a_log=g["alog"][:, 0, :DN_H],
                   dn_dt_bias=g["dtb"][:, 0, :DN_H], dn_out_norm_w=g["wn"][:, 0], gm_ln_g=g["lng"][:, 0],
                   gm_ln_b=g["lnb"][:, 0], gm_w_s=g["ws"], gm_b_s=jnp.transpose(g["bst"][:, :, :GM_G], (0, 2, 1)),
                   sw_q_norm_w=g["wq"][:, 0, :SW_D], sw_k_norm_w=g["wk"][:, 0, :SW_D])
    parts = _all_gather("ag_small_grads", _pack([small_g[n] for n in _SMALL]) + h_mix[0][4][0, 0])
    like = [weights[n] for n in _SMALL]
    sm = _adamw("adamw_small", _pack(like), _pack([mom[n] for n in _SMALL]), _pack([var[n] for n in _SMALL]), parts, 800)
    out = {n: vals for n, vals in zip(_SMALL, zip(*[_unpack(b, like) for b in sm]))}

    dmod_all = parts[:, :nl * 6 * D // 128, :].reshape(NDEV, nl, 6 * D)
    dmod_cols = jnp.transpose(lax.dynamic_slice_in_dim(dmod_all, me * mod_cs, mod_cs, axis=2), (1, 0, 2))
    gw_mod = _wmod_grad(c_all, dmod_cols).reshape(1, nl * D, mod_cs)
    big = dict(w_mod=(gw_mod, mod_cs, mod_cs))

    after, land_f, land_m = sm[0], [None] * nl, [None] * nl
    for l in reversed(range(nl)):
        land_f[l], after = _exchange_wait(f"a2a_wait_f{l}", h_ffn[l], after, True)
        land_m[l], after = _exchange_wait(f"a2a_wait_m{l}", h_mix[l], after, True)
    for n, land, i, cols, padc in (("w_in", land_m, 0, w_in.shape[2], 640), ("w_out", land_m, 1, D, D),
                                   ("w_ffn_in", land_f, 0, w_ffn_in.shape[2], 768), ("w_ffn_out", land_f, 1, D, D)):
        big[n] = (jnp.concatenate([land[l][i] for l in range(nl)], axis=1), cols, padc)
    big["dn_conv_w"] = (_scatter_cols("a2a_conv", [gl[l]["conv"] for l in range(nl)], dn_conv_w.shape[2], 256),
                        dn_conv_w.shape[2], 256)
    for n, (gp, cols, padc) in big.items():
        shp = weights[n].shape
        rows = gp.shape[1]
        prep = lambda a: _pad_last(a.reshape(rows, cols), padc)
        tr = 256 if rows % 256 == 0 else rows // 4 if rows % 32 == 0 else rows
        res4 = _adamw("adamw_" + n, prep(weights[n]), prep(mom[n]), prep(var[n]), gp, tr)
        out[n] = tuple(a[:, :cols].reshape(shp) for a in res4)

    return (loss, dxi[None], *[out[n][0] for n in names], *[out[n][1] for n in names],
            *[out[n][2] for n in names], *[out[n][3] for n in names])
```

```python
import functools
import math

import jax
import jax.numpy as jnp
from jax import lax
from jax.experimental import pallas as pl
from jax.experimental.pallas import tpu as pltpu

f32 = jnp.float32
bf16 = jnp.bfloat16
HI = lax.Precision.HIGH
AXES = ("x", "y", "c")
NDEV = 8
SDS = jax.ShapeDtypeStruct

D = 1024
NORM_EPS = 1e-6
DN_W, DN_H, DN_D, DN_C = 512, 4, 128, 64
GM_W, GM_G, GM_C = 256, 4, 128
SW_W, SW_H, SW_D, SW_B = 256, 4, 64, 128
SW_DIL = (1, 4, 16)
SW_SPAN = 128
ROPE_DIM, ROPE_THETA = 16, 500000.0
IN_W = 4872
IN_WA = 4992
FFN = 2816
ADAM_LR, ADAM_B1, ADAM_B2, ADAM_EPS, ADAM_WD, ADAM_STEP = 0.001, 0.9, 0.999, 1e-08, 0.01, 10

VMEM_LIMIT = 52 * 1024 * 1024


def _cp(sem=None):
    return pltpu.CompilerParams(vmem_limit_bytes=VMEM_LIMIT, dimension_semantics=sem)


_DIMS = {"nn": (((1,), (0,)), ((), ())), "nt": (((1,), (1,)), ((), ())), "tn": (((0,), (0,)), ((), ()))}


def _raw_dot(a, b, mode, hi):
    if hi:
        return lax.dot_general(a, b, _DIMS[mode], precision=HI, preferred_element_type=f32)
    return lax.dot_general(a.astype(bf16), b.astype(bf16), _DIMS[mode], preferred_element_type=f32)


@functools.partial(jax.custom_vjp, nondiff_argnums=(2, 3))
def _dot(a, b, mode, hi):
    return _raw_dot(a, b, mode, hi)


def _dot_fwd(a, b, mode, hi):
    return _raw_dot(a, b, mode, hi), (a, b)


def _dot_bwd(mode, hi, res, g):
    a, b = res
    if mode == "nn":
        return _raw_dot(g, b, "nt", hi), _raw_dot(a, g, "tn", hi)
    if mode == "nt":
        return _raw_dot(g, b, "nn", hi), _raw_dot(g, a, "tn", hi)
    return _raw_dot(b, g, "nt", hi), _raw_dot(a, g, "nn", hi)


_dot.defvjp(_dot_fwd, _dot_bwd)


def _iota(shape, dim):
    return lax.broadcasted_iota(jnp.int32, shape, dim)


def _f_modnorm(x, w, scale, shift):
    y = x * lax.rsqrt(jnp.mean(x * x, axis=-1, keepdims=True) + NORM_EPS) * w
    return (y * (1.0 + scale) + shift,)


def _f_resid(x, m, gate):
    return (x + gate * m,)


def _f_swiglu(gu):
    return (jax.nn.silu(gu[:, :FFN]) * gu[:, FFN:],)


def _softplus(x):
    return jnp.maximum(x, 0.0) + jnp.log1p(jnp.exp(-jnp.abs(x)))


def _f_dn_act(y, ab, alog, dtb):
    c = jax.nn.silu(y)
    parts = []
    for j in range(3 * DN_H):
        p = c[:, j * DN_D:(j + 1) * DN_D]
        if j < 2 * DN_H:
            p = p * lax.rsqrt(jnp.sum(p * p, axis=-1, keepdims=True) + NORM_EPS)
        parts.append(p)
    lane = _iota(ab.shape, 1)
    g = -jnp.exp(alog) * _softplus(ab + dtb)
    beta = jax.nn.sigmoid(ab)
    gb = jnp.where(lane < DN_H, g, jnp.where(lane < 2 * DN_H, beta, 0.0))
    return jnp.concatenate(parts, axis=1), gb


def _f_dn_out(o, z, wn):
    parts = []
    for h in range(DN_H):
        oh = o[:, h * DN_D:(h + 1) * DN_D]
        zh = z[:, h * DN_D:(h + 1) * DN_D]
        n = oh * lax.rsqrt(jnp.mean(oh * oh, axis=-1, keepdims=True) + NORM_EPS) * wn
        parts.append(n * jax.nn.silu(zh))
    return (jnp.concatenate(parts, axis=1),)


def _gelu(x):
    return 0.5 * x * (1.0 + lax.erf(x * (1.0 / math.sqrt(2.0))))


def _f_gm(u_raw, v_raw, ln_g, ln_b, w_s, b_st):
    u = _gelu(u_raw)
    v = _gelu(v_raw)
    mu = jnp.mean(v, axis=-1, keepdims=True)
    vc = v - mu
    var = jnp.mean(vc * vc, axis=-1, keepdims=True)
    v = vc * lax.rsqrt(var + NORM_EPS) * ln_g + ln_b
    r = _iota((GM_C, GM_C), 0)
    c = _iota((GM_C, GM_C), 1)
    grp = _iota((GM_C, GM_W), 1) // (GM_W // GM_G)
    expand = jnp.where(_iota((GM_C, GM_W), 0) == grp, 1.0, 0.0)
    sv = _dot(b_st, expand, "nn", True)
    for g in range(GM_G):
        wg = jnp.where(r >= c, w_s[g], 0.0)
        sv = sv + jnp.where(grp == g, _dot(wg, v, "nn", True), 0.0)
    return (u * sv,)


def _head_lanes(shape):
    return _iota(shape, 1) // SW_D


def _f_sw_pre(q0, k0, q1, k1, q2, k2, cs, sn, wq, wk):
    r = _iota((SW_W, SW_W), 0)
    c = _iota((SW_W, SW_W), 1)
    same_head = jnp.where(r // SW_D == c // SW_D, 1.0, 0.0)
    hc = c % SW_D
    half = ROPE_DIM // 2
    perm = jnp.where((hc < half) & (r == c + half), -1.0, jnp.where((hc >= half) & (hc < ROPE_DIM) & (r == c - half), 1.0, 0.0))
    tile = jnp.where((_iota((128, SW_W), 1) % SW_D == _iota((128, SW_W), 0)) & (_iota((128, SW_W), 0) < SW_D), 1.0, 0.0)
    wq_full = _dot(wq, tile, "nn", True)[0:1, :]
    wk_full = _dot(wk, tile, "nn", True)[0:1, :]

    def one(t, w):
        ms = _dot(t * t, same_head, "nn", False) * (1.0 / SW_D)
        n = t * lax.rsqrt(ms + NORM_EPS) * w
        return n * cs + _dot(n, perm, "nn", False) * sn

    return one(q0, wq_full), one(k0, wk_full), one(q1, wq_full), one(k1, wk_full), one(q2, wq_full), one(k2, wk_full)


def _f_sw_merge(o0, l0, o1, l1, o2, l2):
    m = jnp.maximum(jnp.maximum(l0, l1), l2)
    e0, e1, e2 = jnp.exp(l0 - m), jnp.exp(l1 - m), jnp.exp(l2 - m)
    return ((e0 * o0 + e1 * o1 + e2 * o2) / (e0 + e1 + e2),)


def _f_attn(q, kp, kc, vp, vc, has_prev):
    kk = jnp.concatenate([kp, kc], axis=0)
    vv = jnp.concatenate([vp, vc], axis=0)
    i = _iota((SW_B, 2 * SW_B), 0)
    j = _iota((SW_B, 2 * SW_B), 1)
    dist = i + SW_B - j
    valid = (dist >= 0) & (dist <= SW_SPAN) & ((j >= SW_B) | has_prev)
    hl = _head_lanes(q.shape)
    heads = range(SW_H)
    ss = [_dot(jnp.where(hl == h, q, 0.0), kk, "nt", False) * (SW_D ** -0.5) for h in heads]
    ss = [jnp.where(valid, s, -1e30) for s in ss]
    ms = [jnp.max(s, axis=-1, keepdims=True) for s in ss]
    ps = [jnp.where(valid, jnp.exp(s - m), 0.0) for s, m in zip(ss, ms)]
    ls = [jnp.sum(p, axis=-1, keepdims=True) for p in ps]
    ohs = [_dot(p, vv, "nn", False) * (1.0 / l) for p, l in zip(ps, ls)]
    o = jnp.zeros(q.shape, f32)
    lse = jnp.zeros(q.shape, f32)
    for h in heads:
        o = o + jnp.where(hl == h, ohs[h], 0.0)
        lse = lse + jnp.where(hl == h, ms[h] + jnp.log(ls[h]), 0.0)
    return o, lse


def _lane_col(tile, lane_idx):
    return jnp.sum(jnp.where(_iota(tile.shape, 1) == lane_idx, tile, 0.0), axis=1, keepdims=True)


def _chunk_decays(gbv):
    n = gbv.shape[0]
    r = _iota((n, n), 0)
    c = _iota((n, n), 1)
    gc_all = _dot(jnp.where(r >= c, 1.0, 0.0), gbv, "nn", True)
    gc_rows = gc_all.T
    out = []
    for h in range(DN_H):
        gcol = _lane_col(gc_all, h)
        diff = jnp.where(r >= c, gcol - gc_rows[h:h + 1, :], 0.0)
        out.append((gcol, jnp.where(r >= c, jnp.exp(diff), 0.0)))
    return out, r, c


def _f_dn_pre(ks, vs, gbvs):
    decs, betas = [], []
    for gbv in gbvs:
        d, r, c = _chunk_decays(gbv)
        decs += d
        betas += [_lane_col(gbv, DN_H + h) for h in range(DN_H)]
    kbs = [k * b for k, b in zip(ks, betas)]
    grams = [_dot(kb, k, "nt", True) for kb, k in zip(kbs, ks)]
    mats = [jnp.where(r > c, g * dec[1], 0.0) for g, dec in zip(grams, decs)]
    rhss = [jnp.concatenate([v * b, kb * jnp.exp(dec[0])], axis=1) for v, b, kb, dec in zip(vs, betas, kbs, decs)]
    return mats, rhss


def _inv_unit_lower(mats):
    n = mats[0].shape[0]
    eye = jnp.where(_iota((n, n), 0) == _iota((n, n), 1), 1.0, 0.0)
    invs = [eye - a for a in mats]
    pws = list(mats)
    for _ in range(n.bit_length() - 2):
        pws = [_raw_dot(p, p, "nn", True) for p in pws]
        invs = [_raw_dot(i, eye + p, "nn", True) for i, p in zip(invs, pws)]
    return invs


def _f_dn_scan(states, qs, ks, us, ws, gbv):
    decs, _, _ = _chunk_decays(gbv)
    n = gbv.shape[0]
    last = _iota((n, 1), 0) == n - 1
    g_last = [jnp.sum(jnp.where(last, gc, 0.0), axis=0, keepdims=True) for gc, _ in decs]
    qs = [q * (DN_D ** -0.5) for q in qs]
    a_qk = [_dot(q, k, "nt", True) * dec[1] for q, k, dec in zip(qs, ks, decs)]
    q_dec = [q * jnp.exp(dec[0]) for q, dec in zip(qs, decs)]
    k_dec = [k * jnp.exp(gl - dec[0]) for k, gl, dec in zip(ks, g_last, decs)]
    ws_ = [_dot(w, s, "nn", True) for w, s in zip(ws, states)]
    o_st = [_dot(qd, s, "nn", True) for qd, s in zip(q_dec, states)]
    v_new = [u - x for u, x in zip(us, ws_)]
    o_in = [_dot(a, vn, "nn", True) for a, vn in zip(a_qk, v_new)]
    upd = [_dot(kd, vn, "tn", True) for kd, vn in zip(k_dec, v_new)]
    outs = [x + y for x, y in zip(o_st, o_in)]
    new_states = [s * jnp.exp(gl) + x for s, gl, x in zip(states, g_last, upd)]
    return outs, new_states


def _cspec(tt, w, cb):
    return pl.BlockSpec((tt, w), lambda i, cb=cb: (i, cb))


def _pspec(shape):
    nd = len(shape)
    return pl.BlockSpec(tuple(shape), lambda i, nd=nd: (0,) * nd)


def _ew_fwd(name, f, tiled, params, outs, tt):
    t = tiled[0][0].shape[0]
    nt, npar = len(tiled), len(params)

    def body(*refs):
        tv = [r[...].astype(f32) for r in refs[:nt]]
        pv = [r[...] for r in refs[nt:nt + npar]]
        res = f(*tv, *pv)
        for o, r in zip(refs[nt + npar:], res):
            o[...] = r.astype(o.dtype)

    res = pl.pallas_call(
        body, grid=(t // tt,), name=name,
        in_specs=[_cspec(tt, w, cb) for _, w, cb in tiled] + [_pspec(p.shape) for p in params],
        out_specs=[_cspec(tt, w, 0) for w, _ in outs],
        out_shape=[SDS((t, w), dt) for w, dt in outs],
        compiler_params=_cp(("arbitrary",)),
    )(*[a for a, _, _ in tiled], *params)
    return res


def _ew_bwd(name, f, tiled, params, cots, diff, tt, adds=None):
    t = tiled[0][0].shape[0]
    cots = [c if isinstance(c, list) else [c] for c in cots]
    pieces = [pc for c in cots for pc in c]
    nt, npar, nc, nd = len(tiled), len(params), len(pieces), len(diff)
    adds = adds or [None] * nd
    add_arrs = [a for a in adds if a is not None]
    na = len(add_arrs)
    dwidth = [tiled[k][1] for k, _ in diff]

    def body(*refs):
        tin = refs[:nt]
        pin = refs[nt:nt + npar]
        cin = refs[nt + npar:nt + npar + nc]
        ain = list(refs[nt + npar + nc:nt + npar + nc + na])
        dts = refs[nt + npar + nc + na:nt + npar + nc + na + nd]
        dps = refs[nt + npar + nc + na + nd:]
        tv = [r[...].astype(f32) for r in tin]
        pv = [r[...] for r in pin]

        def g(*dv):
            full = list(tv)
            for n_, (k, _) in enumerate(diff):
                full[k] = dv[n_]
            return tuple(f(*full, *dv[nd:]))

        _, vjp = jax.vjp(g, *[tv[k] for k, _ in diff], *pv)
        cin = list(cin)
        cvals = [jnp.concatenate([cin.pop(0)[...].astype(f32) for _ in c], axis=1) if len(c) > 1
                 else cin.pop(0)[...].astype(f32) for c in cots]
        grads = vjp(tuple(cvals))
        for n_ in range(nd):
            val = grads[n_]
            if adds[n_] is not None:
                val = val + ain.pop(0)[...].astype(f32)
            dts[n_][...] = val.astype(dts[n_].dtype)

        @pl.when(pl.program_id(0) == 0)
        def _():
            for r in dps:
                r[...] = jnp.zeros(r.shape, f32)

        for r, gp in zip(dps, grads[nd:]):
            r[...] += gp

    res = pl.pallas_call(
        body, grid=(t // tt,), name=name,
        in_specs=[_cspec(tt, w, cb) for _, w, cb in tiled] + [_pspec(p.shape) for p in params]
        + [_cspec(tt, w, cb) for _, w, cb in pieces] + [_cspec(tt, a.shape[1], 0) for a in add_arrs],
        out_specs=[_cspec(tt, w, 0) for w in dwidth] + [_pspec(p.shape) for p in params],
        out_shape=[SDS((t, w), dt) for w, (_, dt) in zip(dwidth, diff)] + [SDS(p.shape, f32) for p in params],
        compiler_params=_cp(("arbitrary",)),
    )(*[a for a, _, _ in tiled], *params, *[a for a, _, _ in pieces], *add_arrs)
    return res[:nd], res[nd:]


def _mm(name, a, b, mode, tm, tn, out_dtype=f32, b_outer=False):
    ij = (lambda g0, g1: (g1, g0)) if b_outer else (lambda g0, g1: (g0, g1))
    if mode == "nn":
        (m, k), (k2, n) = a.shape, b.shape
        a_spec = pl.BlockSpec((tm, k), lambda g0, g1: (ij(g0, g1)[0], 0))
        b_spec = pl.BlockSpec((k, tn), lambda g0, g1: (0, ij(g0, g1)[1]))
    elif mode == "nt":
        (m, k), (n, k2) = a.shape, b.shape
        a_spec = pl.BlockSpec((tm, k), lambda g0, g1: (ij(g0, g1)[0], 0))
        b_spec = pl.BlockSpec((tn, k), lambda g0, g1: (ij(g0, g1)[1], 0))
    else:
        (k, m), (k2, n) = a.shape, b.shape
        a_spec = pl.BlockSpec((k, tm), lambda g0, g1: (0, ij(g0, g1)[0]))
        b_spec = pl.BlockSpec((k, tn), lambda g0, g1: (0, ij(g0, g1)[1]))
    assert k == k2 and m % tm == 0 and n % tn == 0, (name, a.shape, b.shape, mode)
    assert a.dtype == bf16 and b.dtype == bf16, name

    def body(a_ref, b_ref, o_ref):
        o_ref[...] = lax.dot_general(a_ref[...], b_ref[...], _DIMS[mode], preferred_element_type=f32).astype(o_ref.dtype)

    return pl.pallas_call(
        body, grid=(n // tn, m // tm) if b_outer else (m // tm, n // tn), name=name,
        in_specs=[a_spec, b_spec], out_specs=pl.BlockSpec((tm, tn), lambda g0, g1: ij(g0, g1)),
        out_shape=SDS((m, n), out_dtype), compiler_params=_cp(("parallel", "parallel")),
    )(a, b)


CONV_K = 4
CONV_W = 3 * DN_W
HALO = 8


def _conv_fwd(proj, w, tt):
    t = proj.shape[0]
    nb8 = tt // HALO

    def body(x_ref, h_ref, w_ref, y_ref, xe):
        i = pl.program_id(0)
        xe[0:HALO, :] = jnp.where(i == 0, 0.0, h_ref[...])
        xe[HALO:, :] = x_ref[...]
        wv = w_ref[...]
        acc = jnp.zeros((tt, CONV_W), f32)
        for k in range(CONV_K):
            acc = acc + wv[k:k + 1, :] * xe[pl.ds(HALO - (CONV_K - 1) + k, tt), :]
        y_ref[...] = acc

    return pl.pallas_call(
        body, grid=(t // tt,), name="conv_fwd",
        in_specs=[pl.BlockSpec((tt, CONV_W), lambda i: (i, 0)),
                  pl.BlockSpec((HALO, CONV_W), lambda i: (jnp.maximum(i * nb8 - 1, 0), 0)),
                  _pspec(w.shape)],
        out_specs=pl.BlockSpec((tt, CONV_W), lambda i: (i, 0)),
        out_shape=SDS((t, CONV_W), f32),
        scratch_shapes=[pltpu.VMEM((tt + HALO, CONV_W), f32)],
        compiler_params=_cp(("arbitrary",)),
    )(proj, proj, w)


def _conv_bwd(proj, dy, w, tt):
    t = proj.shape[0]
    nb8 = tt // HALO
    last8 = t // HALO - 1
    nsteps = t // tt

    def body(x_ref, h_ref, dy_ref, n_ref, w_ref, dx_ref, dw_ref, xe, dye):
        i = pl.program_id(0)
        xe[0:HALO, :] = jnp.where(i == 0, 0.0, h_ref[...])
        xe[HALO:, :] = x_ref[...]
        dye[0:tt, :] = dy_ref[...]
        dye[tt:, :] = jnp.where(i == nsteps - 1, 0.0, n_ref[...])
        wv = w_ref[...]
        dyv = dy_ref[...]
        acc = jnp.zeros((tt, CONV_W), f32)

        @pl.when(i == 0)
        def _():
            dw_ref[...] = jnp.zeros(dw_ref.shape, f32)

        for k in range(CONV_K):
            acc = acc + wv[k:k + 1, :] * dye[pl.ds(CONV_K - 1 - k, tt), :]
            dw_ref[k:k + 1, :] += jnp.sum(dyv * xe[pl.ds(HALO - (CONV_K - 1) + k, tt), :], axis=0, keepdims=True)
        dx_ref[...] = acc.astype(dx_ref.dtype)

    return pl.pallas_call(
        body, grid=(nsteps,), name="conv_bwd",
        in_specs=[pl.BlockSpec((tt, CONV_W), lambda i: (i, 0)),
                  pl.BlockSpec((HALO, CONV_W), lambda i: (jnp.maximum(i * nb8 - 1, 0), 0)),
                  pl.BlockSpec((tt, CONV_W), lambda i: (i, 0)),
                  pl.BlockSpec((HALO, CONV_W), lambda i: (jnp.minimum((i + 1) * nb8, last8), 0)),
                  _pspec(w.shape)],
        out_specs=[pl.BlockSpec((tt, CONV_W), lambda i: (i, 0)), _pspec(w.shape)],
        out_shape=[SDS((t, CONV_W), bf16), SDS(w.shape, f32)],
        scratch_shapes=[pltpu.VMEM((tt + HALO, CONV_W), f32), pltpu.VMEM((tt + HALO, CONV_W), f32)],
        compiler_params=_cp(("arbitrary",)),
    )(proj, proj, dy, dy, w)


PREP_CHUNKS = 4


def _head_cols(part, h):
    return slice(part * DN_W + h * DN_D, part * DN_W + (h + 1) * DN_D)


def _prep_operands(qkv_ref, gb_ref):
    inst = [(slice(ch * DN_C, (ch + 1) * DN_C), h) for ch in range(PREP_CHUNKS) for h in range(DN_H)]
    ks = [qkv_ref[rs, _head_cols(1, h)] for rs, h in inst]
    vs = [qkv_ref[rs, _head_cols(2, h)] for rs, h in inst]
    gbvs = [gb_ref[ch * DN_C:(ch + 1) * DN_C, :] for ch in range(PREP_CHUNKS)]
    return inst, ks, vs, gbvs


def _dn_prep_fwd(qkv, gb):
    t = qkv.shape[0]
    rows = PREP_CHUNKS * DN_C

    def body(qkv_ref, gb_ref, u_ref, w_ref, inv_ref):
        inv_ref[...] = jnp.zeros(inv_ref.shape, f32)
        inst, ks, vs, gbvs = _prep_operands(qkv_ref, gb_ref)
        mats, rhss = _f_dn_pre(ks, vs, gbvs)
        invs = _inv_unit_lower(mats)
        uws = [_raw_dot(inv, rhs, "nn", True) for inv, rhs in zip(invs, rhss)]
        for (rs, h), inv, uw in zip(inst, invs, uws):
            u_ref[rs, _head_cols(0, h)] = uw[:, :DN_D]
            w_ref[rs, _head_cols(0, h)] = uw[:, DN_D:]
            inv_ref[rs, h * DN_D:h * DN_D + DN_C] = inv

    return pl.pallas_call(
        body, grid=(t // rows,), name="dn_prep_fwd",
        in_specs=[pl.BlockSpec((rows, CONV_W), lambda i: (i, 0)), pl.BlockSpec((rows, 128), lambda i: (i, 0))],
        out_specs=[pl.BlockSpec((rows, DN_W), lambda i: (i, 0))] * 3,
        out_shape=[SDS((t, DN_W), f32)] * 3,
        compiler_params=_cp(("arbitrary",)),
    )(qkv, gb)


def _dn_prep_bwd(qkv, gb, inv_all, du, dw, dqk1, dgb1):
    t = qkv.shape[0]
    rows = PREP_CHUNKS * DN_C

    def body(qkv_ref, gb_ref, inv_ref, du_ref, dw_ref, dqk1_ref, dgb1_ref, dqkv_ref, dgb_ref):
        inst, ks, vs, gbvs = _prep_operands(qkv_ref, gb_ref)
        (_, rhss), vjp = jax.vjp(_f_dn_pre, ks, vs, gbvs)
        invs = [inv_ref[rs, h * DN_D:h * DN_D + DN_C] for rs, h in inst]
        dxs = [jnp.concatenate([du_ref[rs, _head_cols(0, h)], dw_ref[rs, _head_cols(0, h)]], axis=1) for rs, h in inst]
        uws = [_raw_dot(inv, rhs, "nn", True) for inv, rhs in zip(invs, rhss)]
        drhss = [_raw_dot(inv, dx, "tn", True) for inv, dx in zip(invs, dxs)]
        das = [-_raw_dot(dr, uw, "nt", True) for dr, uw in zip(drhss, uws)]
        dks, dvs, dgbvs = vjp((das, drhss))
        for (rs, h), dk, dv in zip(inst, dks, dvs):
            dqkv_ref[rs, _head_cols(0, h)] = dqk1_ref[rs, _head_cols(0, h)]
            dqkv_ref[rs, _head_cols(1, h)] = dk + dqk1_ref[rs, _head_cols(1, h)]
            dqkv_ref[rs, _head_cols(2, h)] = dv
        for ch, dgbv in enumerate(dgbvs):
            rs = slice(ch * DN_C, (ch + 1) * DN_C)
            dgb_ref[rs, :] = dgbv + dgb1_ref[rs, :]

    return pl.pallas_call(
        body, grid=(t // rows,), name="dn_prep_bwd",
        in_specs=[pl.BlockSpec((rows, CONV_W), lambda i: (i, 0)), pl.BlockSpec((rows, 128), lambda i: (i, 0)),
                  pl.BlockSpec((rows, DN_W), lambda i: (i, 0)),
                  pl.BlockSpec((rows, DN_W), lambda i: (i, 0)), pl.BlockSpec((rows, DN_W), lambda i: (i, 0)),
                  pl.BlockSpec((rows, 2 * DN_W), lambda i: (i, 0)), pl.BlockSpec((rows, 128), lambda i: (i, 0))],
        out_specs=[pl.BlockSpec((rows, CONV_W), lambda i: (i, 0)), pl.BlockSpec((rows, 128), lambda i: (i, 0))],
        out_shape=[SDS((t, CONV_W), f32), SDS((t, 128), f32)],
        compiler_params=_cp(("arbitrary",)),
    )(qkv, gb, inv_all, du, dw, dqk1, dgb1)


def _dn_scan_fwd(qkv, gb, u, w):
    t = qkv.shape[0]
    n = t // DN_C
    heads = range(DN_H)

    def body(qkv_ref, gb_ref, u_ref, w_ref, o_ref, s_ref, state):
        @pl.when(pl.program_id(0) == 0)
        def _():
            state[...] = jnp.zeros(state.shape, f32)

        states = [state[h] for h in heads]
        for h in heads:
            s_ref[0, h] = states[h]
        outs, new = _f_dn_scan(states, [qkv_ref[:, _head_cols(0, h)] for h in heads],
                               [qkv_ref[:, _head_cols(1, h)] for h in heads],
                               [u_ref[:, _head_cols(0, h)] for h in heads],
                               [w_ref[:, _head_cols(0, h)] for h in heads], gb_ref[...])
        for h in heads:
            o_ref[:, _head_cols(0, h)] = outs[h]
            state[h] = new[h]

    return pl.pallas_call(
        body, grid=(n,), name="dn_scan_fwd",
        in_specs=[pl.BlockSpec((DN_C, 2 * DN_W), lambda i: (i, 0)), pl.BlockSpec((DN_C, 128), lambda i: (i, 0)),
                  pl.BlockSpec((DN_C, DN_W), lambda i: (i, 0)), pl.BlockSpec((DN_C, DN_W), lambda i: (i, 0))],
        out_specs=[pl.BlockSpec((DN_C, DN_W), lambda i: (i, 0)),
                   pl.BlockSpec((1, DN_H, DN_D, DN_D), lambda i: (i, 0, 0, 0))],
        out_shape=[SDS((t, DN_W), f32), SDS((n, DN_H, DN_D, DN_D), f32)],
        scratch_shapes=[pltpu.VMEM((DN_H, DN_D, DN_D), f32)],
        compiler_params=_cp(("arbitrary",)),
    )(qkv, gb, u, w)


def _dn_scan_bwd(qkv, gb, u, w, states, do):
    t = qkv.shape[0]
    n = t // DN_C
    rev = lambda i: (n - 1 - i, 0)
    heads = range(DN_H)

    def body(qkv_ref, gb_ref, u_ref, w_ref, s_ref, do_ref, dqk_ref, du_ref, dw_ref, dgb_ref, dstate):
        @pl.when(pl.program_id(0) == 0)
        def _():
            dstate[...] = jnp.zeros(dstate.shape, f32)

        _, vjp = jax.vjp(_f_dn_scan, [s_ref[0, h] for h in heads], [qkv_ref[:, _head_cols(0, h)] for h in heads],
                         [qkv_ref[:, _head_cols(1, h)] for h in heads], [u_ref[:, _head_cols(0, h)] for h in heads],
                         [w_ref[:, _head_cols(0, h)] for h in heads], gb_ref[...])
        ds, dq, dk, du, dw, dgbv = vjp(([do_ref[:, _head_cols(0, h)] for h in heads], [dstate[h] for h in heads]))
        for h in heads:
            dstate[h] = ds[h]
            dqk_ref[:, _head_cols(0, h)] = dq[h]
            dqk_ref[:, _head_cols(1, h)] = dk[h]
            du_ref[:, _head_cols(0, h)] = du[h]
            dw_ref[:, _head_cols(0, h)] = dw[h]
        dgb_ref[...] = dgbv

    return pl.pallas_call(
        body, grid=(n,), name="dn_scan_bwd",
        in_specs=[pl.BlockSpec((DN_C, 2 * DN_W), rev), pl.BlockSpec((DN_C, 128), rev),
                  pl.BlockSpec((DN_C, DN_W), rev), pl.BlockSpec((DN_C, DN_W), rev),
                  pl.BlockSpec((1, DN_H, DN_D, DN_D), lambda i: (n - 1 - i, 0, 0, 0)),
                  pl.BlockSpec((DN_C, DN_W), rev)],
        out_specs=[pl.BlockSpec((DN_C, 2 * DN_W), rev), pl.BlockSpec((DN_C, DN_W), rev),
                   pl.BlockSpec((DN_C, DN_W), rev), pl.BlockSpec((DN_C, 128), rev)],
        out_shape=[SDS((t, 2 * DN_W), f32), SDS((t, DN_W), f32), SDS((t, DN_W), f32), SDS((t, 128), f32)],
        scratch_shapes=[pltpu.VMEM((DN_H, DN_D, DN_D), f32)],
        compiler_params=_cp(("arbitrary",)),
    )(qkv, gb, u, w, states, do)


HALF = SW_W // 2


def _attn_specs(rows, cb, lag, nb):
    def one(c):
        if lag:
            return pl.BlockSpec((rows, HALF), lambda n, r: (jnp.maximum(jnp.minimum(n, nb - 1) - 1, 0), c))
        return pl.BlockSpec((rows, HALF), lambda n, r: (jnp.minimum(n, nb - 1), c))
    return [one(2 * cb), one(2 * cb + 1)]


def _sw_attn_fwd(q, k, vsrc, vcb, d):
    t = q.shape[0]
    rows = SW_B * d
    nb = t // rows
    cur = lambda cb: _attn_specs(rows, cb, False, nb)
    prev = lambda cb: _attn_specs(rows, cb, True, nb)

    def body(ql, qh, kpl, kph, kcl, kch, vpl, vph, vcl, vch, ol, oh, ll, lh):
        rs = pl.ds(pl.program_id(1), SW_B, stride=d) if d > 1 else pl.ds(0, SW_B)
        ld = lambda lo, hi: jnp.concatenate([lo[rs, :], hi[rs, :]], axis=1)
        o, lse = _f_attn(ld(ql, qh), ld(kpl, kph), ld(kcl, kch), ld(vpl, vph), ld(vcl, vch), pl.program_id(0) > 0)
        ol[rs, :] = o[:, :HALF]
        oh[rs, :] = o[:, HALF:]
        ll[rs, :] = lse[:, :HALF]
        lh[rs, :] = lse[:, HALF:]

    out = pl.BlockSpec((rows, HALF), lambda n, r: (n, 0))
    return pl.pallas_call(
        body, grid=(nb, d), name=f"sw_attn_fwd_d{d}",
        in_specs=cur(0) + prev(0) + cur(0) + prev(vcb) + cur(vcb), out_specs=[out] * 4,
        out_shape=[SDS((t, HALF), f32)] * 4, compiler_params=_cp(("arbitrary", "arbitrary")),
    )(q, q, k, k, k, k, vsrc, vsrc, vsrc, vsrc)


def _sw_attn_bwd(q, k, vsrc, vcb, do_l, do_h, dl_l, dl_h, d):
    t = q.shape[0]
    rows = SW_B * d
    nb = t // rows
    cur = lambda cb: _attn_specs(rows, cb, False, nb)
    prev = lambda cb: _attn_specs(rows, cb, True, nb)
    lag = pl.BlockSpec((rows, HALF), lambda n, r: (jnp.maximum(n - 1, 0), 0))
    here = pl.BlockSpec((rows, HALF), lambda n, r: (jnp.minimum(n, nb - 1), 0))

    def body(ql, qh, kpl, kph, kcl, kch, vpl, vph, vcl, vch, dol, doh, dll, dlh,
             dql, dqh, dkl, dkh, dvl, dvh, dk_hold, dv_hold):
        n = pl.program_id(0)
        r = pl.program_id(1)
        rs = pl.ds(r, SW_B, stride=d) if d > 1 else pl.ds(0, SW_B)
        hs = pl.ds(pl.multiple_of(r * SW_B, SW_B), SW_B)
        ld = lambda lo, hi: jnp.concatenate([lo[rs, :], hi[rs, :]], axis=1)

        def put(lo, hi, val):
            lo[rs, :] = val[:, :HALF]
            hi[rs, :] = val[:, HALF:]

        @pl.when(n < nb)
        def _():
            has_prev = n > 0
            _, vjp = jax.vjp(lambda q_, kp, kc, vp, vc: _f_attn(q_, kp, kc, vp, vc, has_prev),
                             ld(ql, qh), ld(kpl, kph), ld(kcl, kch), ld(vpl, vph), ld(vcl, vch))
            dq, dkp, dkc, dvp, dvc = vjp((ld(dol, doh), ld(dll, dlh)))
            put(dql, dqh, dq)
            put(dkl, dkh, dk_hold[hs, :] + dkp)
            put(dvl, dvh, dv_hold[hs, :] + dvp)
            dk_hold[hs, :] = dkc
            dv_hold[hs, :] = dvc

        @pl.when(n == nb)
        def _():
            put(dkl, dkh, dk_hold[hs, :])
            put(dvl, dvh, dv_hold[hs, :])

    return pl.pallas_call(
        body, grid=(nb + 1, d), name=f"sw_attn_bwd_d{d}",
        in_specs=cur(0) + prev(0) + cur(0) + prev(vcb) + cur(vcb) + [here] * 4,
        out_specs=[here, here, lag, lag, lag, lag],
        out_shape=[SDS((t, HALF), f32)] * 6,
        scratch_shapes=[pltpu.VMEM((rows, SW_W), f32)] * 2,
        compiler_params=_cp(("arbitrary", "arbitrary")),
    )(q, q, k, k, k, k, vsrc, vsrc, vsrc, vsrc, do_l, do_h, dl_l, dl_h)


def _loss_head(y, target, tt):
    t = y.shape[0]

    def body(y_ref, t_ref, dy_ref, l_ref):
        @pl.when(pl.program_id(0) == 0)
        def _():
            l_ref[...] = jnp.zeros(l_ref.shape, f32)

        err = y_ref[...] - t_ref[...]
        dy_ref[...] = err * (1.0 / D)
        l_ref[...] += 0.5 * jnp.sum(jnp.sum(err * err, axis=1, keepdims=True) * (1.0 / D), axis=0, keepdims=True)

    return pl.pallas_call(
        body, grid=(t // tt,), name="loss_head",
        in_specs=[pl.BlockSpec((tt, D), lambda i: (i, 0))] * 2,
        out_specs=[pl.BlockSpec((tt, D), lambda i: (i, 0)), pl.BlockSpec((8, 128), lambda i: (0, 0))],
        out_shape=[SDS((t, D), f32), SDS((8, 128), f32)],
        compiler_params=_cp(("arbitrary",)),
    )(y, target)


def _adamw(name, w, m, v, gparts, tr):
    r, c = w.shape
    p = gparts.shape[0]
    assert r % tr == 0, (name, w.shape, tr)

    def body(w_ref, m_ref, v_ref, g_ref, go_ref, d_ref, mo_ref, vo_ref):
        g = g_ref[0].astype(f32)
        for k in range(1, p):
            g = g + g_ref[k].astype(f32)
        wv = w_ref[...]
        mn = ADAM_B1 * m_ref[...] + (1.0 - ADAM_B1) * g
        vn = ADAM_B2 * v_ref[...] + (1.0 - ADAM_B2) * jnp.square(g)
        m_hat = mn / (1.0 - ADAM_B1 ** ADAM_STEP)
        v_hat = vn / (1.0 - ADAM_B2 ** ADAM_STEP)
        go_ref[...] = g
        d_ref[...] = -ADAM_LR * (m_hat / (jnp.sqrt(v_hat) + ADAM_EPS) + ADAM_WD * wv)
        mo_ref[...] = mn
        vo_ref[...] = vn

    spec = pl.BlockSpec((tr, c), lambda i: (i, 0))
    return pl.pallas_call(
        body, grid=(r // tr,), name=name,
        in_specs=[spec, spec, spec, pl.BlockSpec((p, tr, c), lambda i: (0, i, 0))],
        out_specs=[spec] * 4, out_shape=[SDS((r, c), f32)] * 4,
        compiler_params=_cp(("arbitrary",)),
    )(w, m, v, gparts)


def _mod_cols(c_all, w_mod, b_cols):
    nl, _, wc = w_mod.shape

    def body(c_ref, w_ref, b_ref, o_ref):
        o_ref[0] = _raw_dot(jax.nn.silu(c_ref[...]), w_ref[0], "nn", True) + b_ref[0]

    return pl.pallas_call(
        body, grid=(nl,), name="mod_cols",
        in_specs=[pl.BlockSpec((NDEV, D), lambda l: (0, 0)), pl.BlockSpec((1, D, wc), lambda l: (l, 0, 0)),
                  pl.BlockSpec((1, 1, wc), lambda l: (l, 0, 0))],
        out_specs=pl.BlockSpec((1, NDEV, wc), lambda l: (l, 0, 0)),
        out_shape=SDS((nl, NDEV, wc), f32), compiler_params=_cp(("arbitrary",)),
    )(c_all, w_mod, b_cols)


def _wmod_grad(c_all, dmod_cols):
    nl, _, wc = dmod_cols.shape

    def body(c_ref, d_ref, o_ref):
        o_ref[0, 0] = _raw_dot(jax.nn.silu(c_ref[...]), d_ref[0], "tn", True)

    return pl.pallas_call(
        body, grid=(nl,), name="wmod_grad",
        in_specs=[pl.BlockSpec((NDEV, D), lambda l: (0, 0)), pl.BlockSpec((1, NDEV, wc), lambda l: (l, 0, 0))],
        out_specs=pl.BlockSpec((1, 1, D, wc), lambda l: (0, l, 0, 0)),
        out_shape=SDS((1, nl, D, wc), f32), compiler_params=_cp(("arbitrary",)),
    )(c_all, dmod_cols)


def _me_and_peers():
    x, y, c = (lax.axis_index(a) for a in AXES)
    peers = []
    for k in range(1, NDEV):
        px = 1 - x if (k >> 2) & 1 else x
        py = 1 - y if (k >> 1) & 1 else y
        pc = 1 - c if k & 1 else c
        peers.append(((px, py, pc), 4 * px + 2 * py + pc))
    return 4 * x + 2 * y + c, peers


_ANY = pl.BlockSpec(memory_space=pl.ANY)


def _all_gather(name, a):
    def body(a_ref, o_ref, send_sems, recv_sems, local_sem):
        me, peers = _me_and_peers()
        mine = pltpu.make_async_copy(a_ref, o_ref.at[me], local_sem)
        mine.start()
        copies = [pltpu.make_async_remote_copy(a_ref, o_ref.at[me], send_sems.at[k], recv_sems.at[k],
                                               device_id=dev, device_id_type=pl.DeviceIdType.MESH)
                  for k, (dev, _) in enumerate(peers)]
        for cp in copies:
            cp.start()
        for cp in copies:
            cp.wait()
        mine.wait()

    return pl.pallas_call(
        body, name=name, in_specs=[_ANY], out_specs=_ANY, out_shape=SDS((NDEV,) + a.shape, a.dtype),
        scratch_shapes=[pltpu.SemaphoreType.DMA((NDEV - 1,)), pltpu.SemaphoreType.DMA((NDEV - 1,)), pltpu.SemaphoreType.DMA],
        compiler_params=pltpu.CompilerParams(has_side_effects=True),
    )(a)


def _all_to_all(name, a):
    def body(a_ref, o_ref, send_sems, recv_sems, local_sem):
        me, peers = _me_and_peers()
        mine = pltpu.make_async_copy(a_ref.at[me], o_ref.at[me], local_sem)
        mine.start()
        copies = [pltpu.make_async_remote_copy(a_ref.at[pid], o_ref.at[me], send_sems.at[k], recv_sems.at[k],
                                               device_id=dev, device_id_type=pl.DeviceIdType.MESH)
                  for k, (dev, pid) in enumerate(peers)]
        for cp in copies:
            cp.start()
        for cp in copies:
            cp.wait()
        mine.wait()

    return pl.pallas_call(
        body, name=name, in_specs=[_ANY], out_specs=_ANY, out_shape=SDS(a.shape, a.dtype),
        scratch_shapes=[pltpu.SemaphoreType.DMA((NDEV - 1,)), pltpu.SemaphoreType.DMA((NDEV - 1,)), pltpu.SemaphoreType.DMA],
        compiler_params=pltpu.CompilerParams(has_side_effects=True),
    )(a)


_HBM = pl.BlockSpec(memory_space=pltpu.HBM)
_SEM = pl.BlockSpec(memory_space=pltpu.SEMAPHORE)
_FLOW = pltpu.SideEffectType.DATAFLOW_SIDE_EFFECTING


def _exchange_copies(srcs, lands, send_sems, recv_sems, scatter):
    me, peers = _me_and_peers()
    copies = []
    for a, (src, land) in enumerate(zip(srcs, lands)):
        for k, (dev, pid) in enumerate(peers):
            copies.append(pltpu.make_async_remote_copy(
                src.at[pid] if scatter else src, land.at[me], send_sems.at[a * (NDEV - 1) + k],
                recv_sems.at[a * (NDEV - 1) + k], device_id=dev, device_id_type=pl.DeviceIdType.MESH))
    return copies


def _exchange_start(name, srcs, lands, scatter):
    n = len(srcs)
    nsem = n * (NDEV - 1)

    def body(*refs):
        for cp in _exchange_copies(refs[:n], refs[n:2 * n], refs[2 * n], refs[2 * n + 1], scatter):
            cp.start()
        token = refs[-1]
        token[...] = jnp.zeros(token.shape, token.dtype)

    hbm = lambda a: pltpu.with_memory_space_constraint(a, pltpu.HBM)
    res = pl.pallas_call(
        body, name=name,
        out_shape=(pltpu.SemaphoreType.DMA((nsem,)), pltpu.SemaphoreType.DMA((nsem,)),
                   *[pltpu.HBM(a.shape, a.dtype) for a in srcs], *[pltpu.HBM(a.shape, a.dtype) for a in lands],
                   SDS((8, 128), f32)),
        in_specs=[_HBM] * (2 * n), out_specs=(_SEM, _SEM, *([_HBM] * (2 * n)), pl.BlockSpec(memory_space=pltpu.VMEM)),
        input_output_aliases={i: 2 + i for i in range(2 * n)},
        compiler_params=pltpu.CompilerParams(has_side_effects=_FLOW),
    )(*[hbm(a) for a in srcs], *[hbm(a) for a in lands])
    return res[0], res[1], list(res[2:2 + n]), list(res[2 + n:2 + 2 * n]), res[-1]


def _exchange_wait(name, handle, after, scatter):
    send_sems, recv_sems, srcs, lands, _ = handle
    n = len(srcs)

    def body(*refs):
        for cp in _exchange_copies(refs[:n], refs[n:2 * n], refs[2 * n], refs[2 * n + 1], scatter):
            cp.wait_send()
            cp.wait_recv()
        token = refs[-1]
        token[...] = jnp.zeros(token.shape, token.dtype)

    res = pl.pallas_call(
        body, name=name,
        out_shape=(*[pltpu.HBM(a.shape, a.dtype) for a in srcs + lands], SDS((8, 128), f32)),
        in_specs=[_HBM] * (2 * n) + [_SEM, _SEM, _ANY],
        out_specs=(*([_HBM] * (2 * n)), pl.BlockSpec(memory_space=pltpu.VMEM)),
        input_output_aliases={i: i for i in range(2 * n)},
        compiler_params=pltpu.CompilerParams(has_side_effects=_FLOW),
    )(*srcs, *lands, send_sems, recv_sems, after)
    return list(res[n:2 * n]), res[-1]


def _own_slot(a, me):
    return lax.dynamic_update_slice(jnp.zeros((NDEV,) + a.shape, a.dtype), a[None], (me,) + (0,) * a.ndim)


CB_Z = 3
CB_GM = 8
CB_SW = 10
CB_AB = 38


def _sw_pre_tiles(proj, cs, sn):
    return [(proj, SW_W, CB_SW + 3 * g + j) for g in range(3) for j in range(2)] + [(cs, SW_W, 0), (sn, SW_W, 0)]


def _layer_fwd(x, p, tabs, late_weights):
    cs, sn = tabs
    sh1, sc1, g1, sh2, sc2, g2 = (p["mod"][k] for k in range(6))
    (h1,) = _ew_fwd("modnorm1_fwd", _f_modnorm, [(x, D, 0)], [p["mixw"], sc1, sh1], [(D, bf16)], 512)
    proj = _mm("in_proj", h1, p["wi"], "nn", 1024, 1664, b_outer=True)
    y = _conv_fwd(proj, p["conv"], 256)
    qkv, gb = _ew_fwd("dn_act_fwd", _f_dn_act, [(y, CONV_W, 0), (proj, 128, CB_AB)], [p["alog"], p["dtb"]],
                      [(CONV_W, f32), (128, f32)], 256)
    u, w, inv = _dn_prep_fwd(qkv, gb)
    o, states = _dn_scan_fwd(qkv, gb, u, w)
    (ya,) = _ew_fwd("dn_out_fwd", _f_dn_out, [(o, DN_W, 0), (proj, DN_W, CB_Z)], [p["wn"]], [(DN_W, bf16)], 256)
    (yb,) = _ew_fwd("gm_fwd", _f_gm, [(proj, GM_W, CB_GM), (proj, GM_W, CB_GM + 1)],
                    [p["lng"], p["lnb"], p["ws"], p["bst"]], [(GM_W, bf16)], GM_C)
    qk = _ew_fwd("sw_pre_fwd", _f_sw_pre, _sw_pre_tiles(proj, cs, sn), [p["wq"], p["wk"]], [(SW_W, f32)] * 6, 512)
    ol = [[], []]
    for g, d in enumerate(SW_DIL):
        o_l, o_h, l_l, l_h = _sw_attn_fwd(qk[2 * g], qk[2 * g + 1], proj, CB_SW + 3 * g + 2, d)
        ol[0] += [o_l, l_l]
        ol[1] += [o_h, l_h]
    yc = [_ew_fwd(f"sw_merge_fwd_{h}", _f_sw_merge, [(a, HALF, 0) for a in ol[h]], [], [(HALF, bf16)], 512)[0]
          for h in range(2)]
    ycat = jnp.concatenate([ya, yb] + yc, axis=1)
    p.update(late_weights(ycat))
    m1 = _mm("out_proj", ycat, p["wo"], "nn", 1024, 1024)
    (x2,) = _ew_fwd("resid1_fwd", _f_resid, [(x, D, 0), (m1, D, 0)], [g1], [(D, f32)], 512)
    (h2,) = _ew_fwd("modnorm2_fwd", _f_modnorm, [(x2, D, 0)], [p["ffnw"], sc2, sh2], [(D, bf16)], 512)
    gu = _mm("ffn_in", h2, p["wfi"], "nn", 1024, 1408, b_outer=True)
    (act,) = _ew_fwd("swiglu_fwd", _f_swiglu, [(gu, 2 * FFN, 0)], [], [(FFN, bf16)], 256)
    m2 = _mm("ffn_out", act, p["wfo"], "nn", 1024, 1024)
    (x3,) = _ew_fwd("resid2_fwd", _f_resid, [(x2, D, 0), (m2, D, 0)], [g2], [(D, f32)], 512)
    res = dict(x=x, h1=h1, proj=proj, y=y, qkv=qkv, gb=gb, u=u, w=w, inv=inv, states=states, o=o, qk=list(qk), ol=ol,
               ycat=ycat, m1=m1, x2=x2, h2=h2, gu=gu, act=act, m2=m2)
    return x3, res


def _layer_bwd(dx3, p, r, tabs, ffn_grads_ready):
    cs, sn = tabs
    sh1, sc1, g1, sh2, sc2, g2 = (p["mod"][k] for k in range(6))
    proj = r["proj"]
    (dx2a, dm2), (dg2,) = _ew_bwd("resid2_bwd", _f_resid, [(r["x2"], D, 0), (r["m2"], D, 0)], [g2], [(dx3, D, 0)],
                                  [(0, f32), (1, bf16)], 512)
    dact = _mm("ffn_out_dx", dm2, p["wfo"], "nt", 512, FFN, bf16)
    dwfo = _mm("ffn_out_dw", r["act"], dm2, "tn", 256, 1024, bf16)
    (dgu_cat,), _ = _ew_bwd("swiglu_bwd", _f_swiglu, [(r["gu"], 2 * FFN, 0)], [], [(dact, FFN, 0)], [(0, bf16)], 256)
    dh2 = _mm("ffn_in_dx", dgu_cat, p["wfi"], "nt", 512, 1024)
    dwfi = _mm("ffn_in_dw", r["h2"], dgu_cat, "tn", 1024, 512, bf16)
    g1 = g1 + ffn_grads_ready(dwfi, dwfo)[0, 0]
    (dx2,), (dffnw, dsc2, dsh2) = _ew_bwd("modnorm2_bwd", _f_modnorm, [(r["x2"], D, 0)], [p["ffnw"], sc2, sh2],
                                          [(dh2, D, 0)], [(0, f32)], 512, adds=[dx2a])
    (dxa, dm1), (dg1,) = _ew_bwd("resid1_bwd", _f_resid, [(r["x"], D, 0), (r["m1"], D, 0)], [g1], [(dx2, D, 0)],
                                 [(0, f32), (1, bf16)], 512)
    dycat = _mm("out_proj_dx", dm1, p["wo"], "nt", 1024, 1024)
    dwo = _mm("out_proj_dw", r["ycat"], dm1, "tn", 1024, 512, bf16)
    dol = [_ew_bwd(f"sw_merge_bwd_{h}", _f_sw_merge, [(a, HALF, 0) for a in r["ol"][h]], [], [(dycat, HALF, 6 + h)],
                   [(k, f32) for k in range(6)], 512)[0] for h in range(2)]
    dqk, dvs = [], []
    for g, d in enumerate(SW_DIL):
        dq_l, dq_h, dk_l, dk_h, dv_l, dv_h = _sw_attn_bwd(
            r["qk"][2 * g], r["qk"][2 * g + 1], proj, CB_SW + 3 * g + 2,
            dol[0][2 * g], dol[1][2 * g], dol[0][2 * g + 1], dol[1][2 * g + 1], d)
        dqk += [[(dq_l, HALF, 0), (dq_h, HALF, 0)], [(dk_l, HALF, 0), (dk_h, HALF, 0)]]
        dvs += [dv_l.astype(bf16), dv_h.astype(bf16)]
    dqk_raw, (dwq, dwk) = _ew_bwd("sw_pre_bwd", _f_sw_pre, _sw_pre_tiles(proj, cs, sn), [p["wq"], p["wk"]],
                                  dqk, [(k, bf16) for k in range(6)], 512)
    (dgm_u, dgm_v), (dlng, dlnb, dws, dbst) = _ew_bwd(
        "gm_bwd", _f_gm, [(proj, GM_W, CB_GM), (proj, GM_W, CB_GM + 1)], [p["lng"], p["lnb"], p["ws"], p["bst"]],
        [(dycat, GM_W, 2)], [(0, bf16), (1, bf16)], GM_C)
    (do, dz), (dwn,) = _ew_bwd("dn_out_bwd", _f_dn_out, [(r["o"], DN_W, 0), (proj, DN_W, CB_Z)], [p["wn"]],
                               [(dycat, DN_W, 0)], [(0, f32), (1, bf16)], 256)
    dqk1, du, dw, dgb1 = _dn_scan_bwd(r["qkv"], r["gb"], r["u"], r["w"], r["states"], do)
    dqkv, dgb = _dn_prep_bwd(r["qkv"], r["gb"], r["inv"], du, dw, dqk1, dgb1)
    (dy, dab), (dalog, ddtb) = _ew_bwd("dn_act_bwd", _f_dn_act, [(r["y"], CONV_W, 0), (proj, 128, CB_AB)],
                                       [p["alog"], p["dtb"]], [(dqkv, CONV_W, 0), (dgb, 128, 0)],
                                       [(0, f32), (1, bf16)], 256)
    dxc, dconv = _conv_bwd(proj, dy, p["conv"], 256)
    dproj = jnp.concatenate([dxc, dz, dgm_u, dgm_v, dqk_raw[0], dqk_raw[1], dvs[0], dvs[1], dqk_raw[2], dqk_raw[3],
                             dvs[2], dvs[3], dqk_raw[4], dqk_raw[5], dvs[4], dvs[5], dab], axis=1)
    dh1 = _mm("in_proj_dx", dproj, p["wi"], "nt", 512, 1024)
    dwi = _mm("in_proj_dw", r["h1"], dproj, "tn", 1024, 384, bf16)
    (dx,), (dmixw, dsc1, dsh1) = _ew_bwd("modnorm1_bwd", _f_modnorm, [(r["x"], D, 0)], [p["mixw"], sc1, sh1],
                                         [(dh1, D, 0)], [(0, f32)], 512, adds=[dxa])
    grads = dict(wi=dwi, wo=dwo, wfi=dwfi, wfo=dwfo, conv=dconv, mixw=dmixw, ffnw=dffnw,
                 mod=jnp.stack([dsh1, dsc1, dg1, dsh2, dsc2, dg2]), alog=dalog, dtb=ddtb, wn=dwn, lng=dlng, lnb=dlnb,
                 ws=dws, bst=dbst, wq=dwq, wk=dwk)
    return dx, grads


def _rope_tables(t):
    inv = ROPE_THETA ** (-jnp.arange(0, ROPE_DIM, 2, dtype=f32) / ROPE_DIM)
    ang = jnp.arange(t, dtype=f32)[:, None] * inv[None, :]
    cos, sin = jnp.cos(ang), jnp.sin(ang)
    rest = SW_D - ROPE_DIM
    ch = jnp.concatenate([cos, cos, jnp.ones((t, rest), f32)], axis=1)
    sh = jnp.concatenate([sin, sin, jnp.zeros((t, rest), f32)], axis=1)
    return jnp.tile(ch, (1, SW_H)), jnp.tile(sh, (1, SW_H))


def _pad_last(a, n):
    return jnp.pad(a, [(0, 0)] * (a.ndim - 1) + [(0, n - a.shape[-1])])


def _gather_cols(name, w, pad_to):
    nl, r, cs = w.shape
    g = _all_gather(name, _pad_last(w, pad_to))
    return jnp.transpose(g, (1, 2, 0, 3))[..., :cs].reshape(nl, r, NDEV * cs)


def _scatter_cols(name, gs, cs, pad_to):
    r = gs[0].shape[0]
    parts = jnp.stack([jnp.transpose(_pad_last(g.reshape(r, NDEV, cs), pad_to), (1, 0, 2)) for g in gs], axis=1)
    return _all_to_all(name, parts).reshape(NDEV, len(gs) * r, pad_to)


_SMALL = ("b_mod", "mix_norm_w", "ffn_norm_w", "dn_a_log", "dn_dt_bias", "dn_out_norm_w", "gm_ln_g", "gm_ln_b",
          "gm_w_s", "gm_b_s", "sw_q_norm_w", "sw_k_norm_w")
PACK_TILE = 800 * 128


def _pack(arrs):
    flat = jnp.concatenate([a.reshape(-1) for a in arrs])
    n = -(-flat.shape[0] // PACK_TILE) * PACK_TILE
    return jnp.pad(flat, (0, n - flat.shape[0])).reshape(n // 128, 128)


def _unpack(buf, like):
    flat, out, off = buf.reshape(-1), [], 0
    for a in like:
        out.append(flat[off:off + a.size].reshape(a.shape))
        off += a.size
    return out


def kernel(x, c, w_mod, b_mod, mix_norm_w, ffn_norm_w, w_in, w_out, dn_conv_w, dn_a_log, dn_dt_bias, dn_out_norm_w, gm_ln_g, gm_ln_b, gm_w_s, gm_b_s, sw_q_norm_w, sw_k_norm_w, w_ffn_in, w_ffn_out, loss_target, m_w_mod, m_b_mod, m_mix_norm_w, m_ffn_norm_w, m_w_in, m_w_out, m_dn_conv_w, m_dn_a_log, m_dn_dt_bias, m_dn_out_norm_w, m_gm_ln_g, m_gm_ln_b, m_gm_w_s, m_gm_b_s, m_sw_q_norm_w, m_sw_k_norm_w, m_w_ffn_in, m_w_ffn_out, v_w_mod, v_b_mod, v_mix_norm_w, v_ffn_norm_w, v_w_in, v_w_out, v_dn_conv_w, v_dn_a_log, v_dn_dt_bias, v_dn_out_norm_w, v_gm_ln_g, v_gm_ln_b, v_gm_w_s, v_gm_b_s, v_sw_q_norm_w, v_sw_k_norm_w, v_w_ffn_in, v_w_ffn_out):
    weights = dict(w_mod=w_mod, b_mod=b_mod, mix_norm_w=mix_norm_w, ffn_norm_w=ffn_norm_w, w_in=w_in, w_out=w_out,
                   dn_conv_w=dn_conv_w, dn_a_log=dn_a_log, dn_dt_bias=dn_dt_bias, dn_out_norm_w=dn_out_norm_w,
                   gm_ln_g=gm_ln_g, gm_ln_b=gm_ln_b, gm_w_s=gm_w_s, gm_b_s=gm_b_s, sw_q_norm_w=sw_q_norm_w,
                   sw_k_norm_w=sw_k_norm_w, w_ffn_in=w_ffn_in, w_ffn_out=w_ffn_out)
    mom = dict(w_mod=m_w_mod, b_mod=m_b_mod, mix_norm_w=m_mix_norm_w, ffn_norm_w=m_ffn_norm_w, w_in=m_w_in,
               w_out=m_w_out, dn_conv_w=m_dn_conv_w, dn_a_log=m_dn_a_log, dn_dt_bias=m_dn_dt_bias,
               dn_out_norm_w=m_dn_out_norm_w, gm_ln_g=m_gm_ln_g, gm_ln_b=m_gm_ln_b, gm_w_s=m_gm_w_s, gm_b_s=m_gm_b_s,
               sw_q_norm_w=m_sw_q_norm_w, sw_k_norm_w=m_sw_k_norm_w, w_ffn_in=m_w_ffn_in, w_ffn_out=m_w_ffn_out)
    var = dict(w_mod=v_w_mod, b_mod=v_b_mod, mix_norm_w=v_mix_norm_w, ffn_norm_w=v_ffn_norm_w, w_in=v_w_in,
               w_out=v_w_out, dn_conv_w=v_dn_conv_w, dn_a_log=v_dn_a_log, dn_dt_bias=v_dn_dt_bias,
               dn_out_norm_w=v_dn_out_norm_w, gm_ln_g=v_gm_ln_g, gm_ln_b=v_gm_ln_b, gm_w_s=v_gm_w_s, gm_b_s=v_gm_b_s,
               sw_q_norm_w=v_sw_q_norm_w, sw_k_norm_w=v_sw_k_norm_w, w_ffn_in=v_w_ffn_in, w_ffn_out=v_w_ffn_out)
    names = list(weights)
    xi, tgt = x[0], loss_target[0]
    t = xi.shape[0]
    nl = w_mod.shape[0]
    ax, ay, ac = (lax.axis_index(a) for a in AXES)
    me = 4 * ax + 2 * ay + ac
    mod_cs = w_mod.shape[2]

    shards_a = [_pad_last(w_in.astype(bf16), 640)]
    shards_b = [w_out.astype(bf16), _pad_last(w_ffn_in.astype(bf16), 768), w_ffn_out.astype(bf16)]

    def start_gather(l, tok):
        ha = _exchange_start(f"ag_start_a{l}", [a[l] + tok for a in shards_a], [_own_slot(a[l], me) for a in shards_a], False)
        hb = _exchange_start(f"ag_start_b{l}", [a[l] + ha[4][0, 0].astype(bf16) for a in shards_b],
                             [_own_slot(a[l], me) for a in shards_b], False)
        return ha, hb

    gather = start_gather(0, jnp.zeros((), bf16))

    c = c + gather[1][4][0, 0]
    c_all = _all_gather("ag_c", jnp.broadcast_to(c, (NDEV, D)))[:, 0, :]
    b_cols = lax.dynamic_slice_in_dim(b_mod, me * mod_cs, mod_cs, axis=1)[:, None, :]
    modc = _mod_cols(c_all, w_mod, b_cols)
    mod_tx = jnp.pad(jnp.transpose(modc, (1, 0, 2)), ((0, 0), (0, 8 - nl), (0, 0)))
    mod_rx = _all_to_all("a2a_mod", mod_tx)[:, :nl]
    mod = jnp.transpose(mod_rx, (1, 0, 2)).reshape(nl, 6, 1, D)

    def w_in_full(landed):
        wi = jnp.transpose(landed[0], (1, 0, 2))[..., :w_in.shape[2]].reshape(D, IN_W)
        return jnp.concatenate([wi[:, :2048], wi[:, 2056:], wi[:, 2048:2056], jnp.zeros((D, IN_WA - IN_W), bf16)], axis=-1)

    def late_full(landed):
        go, gfi, gfo = landed
        wfi = jnp.transpose(gfi, (1, 0, 2))[..., :w_ffn_in.shape[2]].reshape(D, 2 * FFN)
        return dict(wo=go.reshape(D, D), wfi=wfi, wfo=gfo.reshape(FFN, D))

    conv = _gather_cols("ag_conv", dn_conv_w, 256)
    pad128 = lambda a: _pad_last(a, 128)[:, None, :]
    params = dict(
        mod=mod, conv=conv,
        mixw=mix_norm_w[:, None, :], ffnw=ffn_norm_w[:, None, :], alog=pad128(dn_a_log), dtb=pad128(dn_dt_bias),
        wn=dn_out_norm_w[:, None, :], lng=gm_ln_g[:, None, :], lnb=gm_ln_b[:, None, :], ws=gm_w_s,
        bst=_pad_last(jnp.transpose(gm_b_s, (0, 2, 1)), 128),
        wq=jnp.pad(sw_q_norm_w[:, None, :], ((0, 0), (0, 7), (0, 128 - SW_D))),
        wk=jnp.pad(sw_k_norm_w[:, None, :], ((0, 0), (0, 7), (0, 128 - SW_D))))
    tabs = _rope_tables(t)

    layer_p, res = [], []
    xc, after = xi, gather[1][4]
    for l in range(nl):
        ha, hb = gather
        landed_a, tok = _exchange_wait(f"ag_wait_a{l}", ha, after, False)
        p = {k: v[l] for k, v in params.items()}
        p["wi"] = w_in_full(landed_a)
        if l + 1 < nl:
            gather = start_gather(l + 1, tok[0, 0].astype(bf16))
            p["mod"] = p["mod"] + gather[1][4][0, 0]
        late = lambda ycat, hb=hb, l=l: late_full(_exchange_wait(f"ag_wait_b{l}", hb, ycat, False)[0])
        layer_p.append(p)
        xc, r = _layer_fwd(xc, p, tabs, late)
        res.append(r)
        after = xc
    dy, lpart = _loss_head(xc, tgt, 512)
    loss = lax.psum(lpart[0, 0], AXES)

    slot = lax.broadcasted_iota(jnp.int32, (NDEV, 1, 1), 0)

    def start_scatter(name, parts):
        return _exchange_start(name, parts, [jnp.where(slot == me, a, jnp.zeros_like(a)) for a in parts], True)

    dxi, gl, h_ffn, h_mix = dy, [None] * nl, [None] * nl, [None] * nl
    for l in reversed(range(nl)):
        def ffn_ready(dwfi, dwfo, l=l):
            h_ffn[l] = start_scatter(f"a2a_start_f{l}", [
                jnp.transpose(_pad_last(dwfi.reshape(D, NDEV, w_ffn_in.shape[2]), 768), (1, 0, 2)),
                dwfo.reshape(NDEV, w_ffn_out.shape[1], D)])
            return h_ffn[l][4]

        dxi, gl[l] = _layer_bwd(dxi, layer_p[l], res[l], tabs, ffn_ready)
        d = gl[l]["wi"]
        d = jnp.concatenate([d[:, :2048], d[:, 4864:4872], d[:, 2048:4864]], axis=-1)
        h_mix[l] = start_scatter(f"a2a_start_m{l}", [
            jnp.transpose(_pad_last(d.reshape(D, NDEV, w_in.shape[2]), 640), (1, 0, 2)),
            gl[l]["wo"].reshape(NDEV, w_out.shape[1], D)])
        if l > 0:
            layer_p[l - 1]["mod"] = layer_p[l - 1]["mod"] + h_mix[l][4][0, 0]
    g = {k: jnp.stack([gl[l][k] for l in range(nl)]) for k in gl[0] if k not in ("wi", "wo", "wfi", "wfo")}

    dmod = g["mod"].reshape(nl, 6 * D)
    small_g = dict(b_mod=dmod, mix_norm_w=g["mixw"][:, 0], ffn_norm_w=g["ffnw"][:, 0], dn_a_log=g["alog"][:, 0, :DN_H],
                   dn_dt_bias=g["dtb"][:, 0, :DN_H], dn_out_norm_w=g["wn"][:, 0], gm_ln_g=g["lng"][:, 0],
                   gm_ln_b=g["lnb"][:, 0], gm_w_s=g["ws"], gm_b_s=jnp.transpose(g["bst"][:, :, :GM_G], (0, 2, 1)),
                   sw_q_norm_w=g["wq"][:, 0, :SW_D], sw_k_norm_w=g["wk"][:, 0, :SW_D])
    parts = _all_gather("ag_small_grads", _pack([small_g[n] for n in _SMALL]) + h_mix[0][4][0, 0])
    like = [weights[n] for n in _SMALL]
    sm = _adamw("adamw_small", _pack(like), _pack([mom[n] for n in _SMALL]), _pack([var[n] for n in _SMALL]), parts, 800)
    out = {n: vals for n, vals in zip(_SMALL, zip(*[_unpack(b, like) for b in sm]))}

    dmod_all = parts[:, :nl * 6 * D // 128, :].reshape(NDEV, nl, 6 * D)
    dmod_cols = jnp.transpose(lax.dynamic_slice_in_dim(dmod_all, me * mod_cs, mod_cs, axis=2), (1, 0, 2))
    gw_mod = _wmod_grad(c_all, dmod_cols).reshape(1, nl * D, mod_cs)
    big = dict(w_mod=(gw_mod, mod_cs, mod_cs))

    after, land_f, land_m = sm[0], [None] * nl, [None] * nl
    for l in reversed(range(nl)):
        land_f[l], after = _exchange_wait(f"a2a_wait_f{l}", h_ffn[l], after, True)
        land_m[l], after = _exchange_wait(f"a2a_wait_m{l}", h_mix[l], after, True)
    for n, land, i, cols, padc in (("w_in", land_m, 0, w_in.shape[2], 640), ("w_out", land_m, 1, D, D),
                                   ("w_ffn_in", land_f, 0, w_ffn_in.shape[2], 768), ("w_ffn_out", land_f, 1, D, D)):
        big[n] = (jnp.concatenate([land[l][i] for l in range(nl)], axis=1), cols, padc)
    big["dn_conv_w"] = (_scatter_cols("a2a_conv", [gl[l]["conv"] for l in range(nl)], dn_conv_w.shape[2], 256),
                        dn_conv_w.shape[2], 256)
    for n, (gp, cols, padc) in big.items():
        shp = weights[n].shape
        rows = gp.shape[1]
        prep = lambda a: _pad_last(a.reshape(rows, cols), padc)
        tr = 256 if rows % 256 == 0 else rows // 4 if rows % 32 == 0 else rows
        res4 = _adamw("adamw_" + n, prep(weights[n]), prep(mom[n]), prep(var[n]), gp, tr)
        out[n] = tuple(a[:, :cols].reshape(shp) for a in res4)

    return (loss, dxi[None], *[out[n][0] for n in names], *[out[n][1] for n in names],
            *[out[n][2] for n in names], *[out[n][3] for n in names])
```

```python
import functools
import math

import jax
import jax.numpy as jnp
from jax import lax
from jax.experimental import pallas as pl
from jax.experimental.pallas import tpu as pltpu

f32 = jnp.float32
bf16 = jnp.bfloat16
HI = lax.Precision.HIGH
AXES = ("x", "y", "c")
NDEV = 8
SDS = jax.ShapeDtypeStruct

D = 1024
NORM_EPS = 1e-6
DN_W, DN_H, DN_D, DN_C = 512, 4, 128, 64
GM_W, GM_G, GM_C = 256, 4, 128
SW_W, SW_H, SW_D, SW_B = 256, 4, 64, 128
SW_DIL = (1, 4, 16)
SW_SPAN = 128
ROPE_DIM, ROPE_THETA = 16, 500000.0
IN_W = 4872
IN_WA = 4992
FFN = 2816
ADAM_LR, ADAM_B1, ADAM_B2, ADAM_EPS, ADAM_WD, ADAM_STEP = 0.001, 0.9, 0.999, 1e-08, 0.01, 10

VMEM_LIMIT = 52 * 1024 * 1024


def _cp(sem=None):
    return pltpu.CompilerParams(vmem_limit_bytes=VMEM_LIMIT, dimension_semantics=sem)


_DIMS = {"nn": (((1,), (0,)), ((), ())), "nt": (((1,), (1,)), ((), ())), "tn": (((0,), (0,)), ((), ()))}


def _raw_dot(a, b, mode, hi):
    if hi:
        return lax.dot_general(a, b, _DIMS[mode], precision=HI, preferred_element_type=f32)
    return lax.dot_general(a.astype(bf16), b.astype(bf16), _DIMS[mode], preferred_element_type=f32)


@functools.partial(jax.custom_vjp, nondiff_argnums=(2, 3))
def _dot(a, b, mode, hi):
    return _raw_dot(a, b, mode, hi)


def _dot_fwd(a, b, mode, hi):
    return _raw_dot(a, b, mode, hi), (a, b)


def _dot_bwd(mode, hi, res, g):
    a, b = res
    if mode == "nn":
        return _raw_dot(g, b, "nt", hi), _raw_dot(a, g, "tn", hi)
    if mode == "nt":
        return _raw_dot(g, b, "nn", hi), _raw_dot(g, a, "tn", hi)
    return _raw_dot(b, g, "nt", hi), _raw_dot(a, g, "nn", hi)


_dot.defvjp(_dot_fwd, _dot_bwd)


def _iota(shape, dim):
    return lax.broadcasted_iota(jnp.int32, shape, dim)


def _f_modnorm(x, w, scale, shift):
    y = x * lax.rsqrt(jnp.mean(x * x, axis=-1, keepdims=True) + NORM_EPS) * w
    return (y * (1.0 + scale) + shift,)


def _f_resid(x, m, gate):
    return (x + gate * m,)


def _f_swiglu(gu):
    return (jax.nn.silu(gu[:, :FFN]) * gu[:, FFN:],)


def _softplus(x):
    return jnp.maximum(x, 0.0) + jnp.log1p(jnp.exp(-jnp.abs(x)))


def _f_dn_act(y, ab, alog, dtb):
    c = jax.nn.silu(y)
    parts = []
    for j in range(3 * DN_H):
        p = c[:, j * DN_D:(j + 1) * DN_D]
        if j < 2 * DN_H:
            p = p * lax.rsqrt(jnp.sum(p * p, axis=-1, keepdims=True) + NORM_EPS)
        parts.append(p)
    lane = _iota(ab.shape, 1)
    g = -jnp.exp(alog) * _softplus(ab + dtb)
    beta = jax.nn.sigmoid(ab)
    gb = jnp.where(lane < DN_H, g, jnp.where(lane < 2 * DN_H, beta, 0.0))
    return jnp.concatenate(parts, axis=1), gb


def _f_dn_out(o, z, wn):
    parts = []
    for h in range(DN_H):
        oh = o[:, h * DN_D:(h + 1) * DN_D]
        zh = z[:, h * DN_D:(h + 1) * DN_D]
        n = oh * lax.rsqrt(jnp.mean(oh * oh, axis=-1, keepdims=True) + NORM_EPS) * wn
        parts.append(n * jax.nn.silu(zh))
    return (jnp.concatenate(parts, axis=1),)


def _gelu(x):
    return 0.5 * x * (1.0 + lax.erf(x * (1.0 / math.sqrt(2.0))))


def _f_gm(u_raw, v_raw, ln_g, ln_b, w_s, b_st):
    u = _gelu(u_raw)
    v = _gelu(v_raw)
    mu = jnp.mean(v, axis=-1, keepdims=True)
    vc = v - mu
    var = jnp.mean(vc * vc, axis=-1, keepdims=True)
    v = vc * lax.rsqrt(var + NORM_EPS) * ln_g + ln_b
    r = _iota((GM_C, GM_C), 0)
    c = _iota((GM_C, GM_C), 1)
    grp = _iota((GM_C, GM_W), 1) // (GM_W // GM_G)
    expand = jnp.where(_iota((GM_C, GM_W), 0) == grp, 1.0, 0.0)
    sv = _dot(b_st, expand, "nn", True)
    for g in range(GM_G):
        wg = jnp.where(r >= c, w_s[g], 0.0)
        sv = sv + jnp.where(grp == g, _dot(wg, v, "nn", True), 0.0)
    return (u * sv,)


def _head_lanes(shape):
    return _iota(shape, 1) // SW_D


def _f_sw_pre(q0, k0, q1, k1, q2, k2, cs, sn, wq, wk):
    r = _iota((SW_W, SW_W), 0)
    c = _iota((SW_W, SW_W), 1)
    same_head = jnp.where(r // SW_D == c // SW_D, 1.0, 0.0)
    hc = c % SW_D
    half = ROPE_DIM // 2
    perm = jnp.where((hc < half) & (r == c + half), -1.0, jnp.where((hc >= half) & (hc < ROPE_DIM) & (r == c - half), 1.0, 0.0))
    tile = jnp.where((_iota((128, SW_W), 1) % SW_D == _iota((128, SW_W), 0)) & (_iota((128, SW_W), 0) < SW_D), 1.0, 0.0)
    wq_full = _dot(wq, tile, "nn", True)[0:1, :]
    wk_full = _dot(wk, tile, "nn", True)[0:1, :]

    def one(t, w):
        ms = _dot(t * t, same_head, "nn", False) * (1.0 / SW_D)
        n = t * lax.rsqrt(ms + NORM_EPS) * w
        return n * cs + _dot(n, perm, "nn", False) * sn

    return one(q0, wq_full), one(k0, wk_full), one(q1, wq_full), one(k1, wk_full), one(q2, wq_full), one(k2, wk_full)


def _f_sw_merge(o0, l0, o1, l1, o2, l2):
    m = jnp.maximum(jnp.maximum(l0, l1), l2)
    e0, e1, e2 = jnp.exp(l0 - m), jnp.exp(l1 - m), jnp.exp(l2 - m)
    return ((e0 * o0 + e1 * o1 + e2 * o2) / (e0 + e1 + e2),)


def _f_attn(q, kp, kc, vp, vc, has_prev):
    kk = jnp.concatenate([kp, kc], axis=0)
    vv = jnp.concatenate([vp, vc], axis=0)
    i = _iota((SW_B, 2 * SW_B), 0)
    j = _iota((SW_B, 2 * SW_B), 1)
    dist = i + SW_B - j
    valid = (dist >= 0) & (dist <= SW_SPAN) & ((j >= SW_B) | has_prev)
    hl = _head_lanes(q.shape)
    heads = range(SW_H)
    ss = [_dot(jnp.where(hl == h, q, 0.0), kk, "nt", False) * (SW_D ** -0.5) for h in heads]
    ss = [jnp.where(valid, s, -1e30) for s in ss]
    ms = [jnp.max(s, axis=-1, keepdims=True) for s in ss]
    ps = [jnp.where(valid, jnp.exp(s - m), 0.0) for s, m in zip(ss, ms)]
    ls = [jnp.sum(p, axis=-1, keepdims=True) for p in ps]
    ohs = [_dot(p, vv, "nn", False) * (1.0 / l) for p, l in zip(ps, ls)]
    o = jnp.zeros(q.shape, f32)
    lse = jnp.zeros(q.shape, f32)
    for h in heads:
        o = o + jnp.where(hl == h, ohs[h], 0.0)
        lse = lse + jnp.where(hl == h, ms[h] + jnp.log(ls[h]), 0.0)
    return o, lse


def _lane_col(tile, lane_idx):
    return jnp.sum(jnp.where(_iota(tile.shape, 1) == lane_idx, tile, 0.0), axis=1, keepdims=True)


def _chunk_decays(gbv):
    n = gbv.shape[0]
    r = _iota((n, n), 0)
    c = _iota((n, n), 1)
    gc_all = _dot(jnp.where(r >= c, 1.0, 0.0), gbv, "nn", True)
    gc_rows = gc_all.T
    out = []
    for h in range(DN_H):
        gcol = _lane_col(gc_all, h)
        diff = jnp.where(r >= c, gcol - gc_rows[h:h + 1, :], 0.0)
        out.append((gcol, jnp.where(r >= c, jnp.exp(diff), 0.0)))
    return out, r, c


def _f_dn_pre(ks, vs, gbvs):
    decs, betas = [], []
    for gbv in gbvs:
        d, r, c = _chunk_decays(gbv)
        decs += d
        betas += [_lane_col(gbv, DN_H + h) for h in range(DN_H)]
    kbs = [k * b for k, b in zip(ks, betas)]
    grams = [_dot(kb, k, "nt", True) for kb, k in zip(kbs, ks)]
    mats = [jnp.where(r > c, g * dec[1], 0.0) for g, dec in zip(grams, decs)]
    rhss = [jnp.concatenate([v * b, kb * jnp.exp(dec[0])], axis=1) for v, b, kb, dec in zip(vs, betas, kbs, decs)]
    return mats, rhss


def _inv_unit_lower(mats):
    n = mats[0].shape[0]
    eye = jnp.where(_iota((n, n), 0) == _iota((n, n), 1), 1.0, 0.0)
    invs = [eye - a for a in mats]
    pws = list(mats)
    for _ in range(n.bit_length() - 2):
        pws = [_raw_dot(p, p, "nn", True) for p in pws]
        invs = [_raw_dot(i, eye + p, "nn", True) for i, p in zip(invs, pws)]
    return invs


def _f_dn_scan(states, qs, ks, us, ws, gbv):
    decs, _, _ = _chunk_decays(gbv)
    n = gbv.shape[0]
    last = _iota((n, 1), 0) == n - 1
    g_last = [jnp.sum(jnp.where(last, gc, 0.0), axis=0, keepdims=True) for gc, _ in decs]
    qs = [q * (DN_D ** -0.5) for q in qs]
    a_qk = [_dot(q, k, "nt", True) * dec[1] for q, k, dec in zip(qs, ks, decs)]
    q_dec = [q * jnp.exp(dec[0]) for q, dec in zip(qs, decs)]
    k_dec = [k * jnp.exp(gl - dec[0]) for k, gl, dec in zip(ks, g_last, decs)]
    ws_ = [_dot(w, s, "nn", True) for w, s in zip(ws, states)]
    o_st = [_dot(qd, s, "nn", True) for qd, s in zip(q_dec, states)]
    v_new = [u - x for u, x in zip(us, ws_)]
    o_in = [_dot(a, vn, "nn", True) for a, vn in zip(a_qk, v_new)]
    upd = [_dot(kd, vn, "tn", True) for kd, vn in zip(k_dec, v_new)]
    outs = [x + y for x, y in zip(o_st, o_in)]
    new_states = [s * jnp.exp(gl) + x for s, gl, x in zip(states, g_last, upd)]
    return outs, new_states


def _cspec(tt, w, cb):
    return pl.BlockSpec((tt, w), lambda i, cb=cb: (i, cb))


def _pspec(shape):
    nd = len(shape)
    return pl.BlockSpec(tuple(shape), lambda i, nd=nd: (0,) * nd)


def _ew_fwd(name, f, tiled, params, outs, tt):
    t = tiled[0][0].shape[0]
    nt, npar = len(tiled), len(params)

    def body(*refs):
        tv = [r[...].astype(f32) for r in refs[:nt]]
        pv = [r[...] for r in refs[nt:nt + npar]]
        res = f(*tv, *pv)
        for o, r in zip(refs[nt + npar:], res):
            o[...] = r.astype(o.dtype)

    res = pl.pallas_call(
        body, grid=(t // tt,), name=name,
        in_specs=[_cspec(tt, w, cb) for _, w, cb in tiled] + [_pspec(p.shape) for p in params],
        out_specs=[_cspec(tt, w, 0) for w, _ in outs],
        out_shape=[SDS((t, w), dt) for w, dt in outs],
        compiler_params=_cp(("arbitrary",)),
    )(*[a for a, _, _ in tiled], *params)
    return res


def _ew_bwd(name, f, tiled, params, cots, diff, tt, adds=None):
    t = tiled[0][0].shape[0]
    cots = [c if isinstance(c, list) else [c] for c in cots]
    pieces = [pc for c in cots for pc in c]
    nt, npar, nc, nd = len(tiled), len(params), len(pieces), len(diff)
    adds = adds or [None] * nd
    add_arrs = [a for a in adds if a is not None]
    na = len(add_arrs)
    dwidth = [tiled[k][1] for k, _ in diff]

    def body(*refs):
        tin = refs[:nt]
        pin = refs[nt:nt + npar]
        cin = refs[nt + npar:nt + npar + nc]
        ain = list(refs[nt + npar + nc:nt + npar + nc + na])
        dts = refs[nt + npar + nc + na:nt + npar + nc + na + nd]
        dps = refs[nt + npar + nc + na + nd:]
        tv = [r[...].astype(f32) for r in tin]
        pv = [r[...] for r in pin]

        def g(*dv):
            full = list(tv)
            for n_, (k, _) in enumerate(diff):
                full[k] = dv[n_]
            return tuple(f(*full, *dv[nd:]))

        _, vjp = jax.vjp(g, *[tv[k] for k, _ in diff], *pv)
        cin = list(cin)
        cvals = [jnp.concatenate([cin.pop(0)[...].astype(f32) for _ in c], axis=1) if len(c) > 1
                 else cin.pop(0)[...].astype(f32) for c in cots]
        grads = vjp(tuple(cvals))
        for n_ in range(nd):
            val = grads[n_]
            if adds[n_] is not None:
                val = val + ain.pop(0)[...].astype(f32)
            dts[n_][...] = val.astype(dts[n_].dtype)

        @pl.when(pl.program_id(0) == 0)
        def _():
            for r in dps:
                r[...] = jnp.zeros(r.shape, f32)

        for r, gp in zip(dps, grads[nd:]):
            r[...] += gp

    res = pl.pallas_call(
        body, grid=(t // tt,), name=name,
        in_specs=[_cspec(tt, w, cb) for _, w, cb in tiled] + [_pspec(p.shape) for p in params]
        + [_cspec(tt, w, cb) for _, w, cb in pieces] + [_cspec(tt, a.shape[1], 0) for a in add_arrs],
        out_specs=[_cspec(tt, w, 0) for w in dwidth] + [_pspec(p.shape) for p in params],
        out_shape=[SDS((t, w), dt) for w, (_, dt) in zip(dwidth, diff)] + [SDS(p.shape, f32) for p in params],
        compiler_params=_cp(("arbitrary",)),
    )(*[a for a, _, _ in tiled], *params, *[a for a, _, _ in pieces], *add_arrs)
    return res[:nd], res[nd:]


def _mm(name, a, b, mode, tm, tn, out_dtype=f32, b_outer=False):
    ij = (lambda g0, g1: (g1, g0)) if b_outer else (lambda g0, g1: (g0, g1))
    if mode == "nn":
        (m, k), (k2, n) = a.shape, b.shape
        a_spec = pl.BlockSpec((tm, k), lambda g0, g1: (ij(g0, g1)[0], 0))
        b_spec = pl.BlockSpec((k, tn), lambda g0, g1: (0, ij(g0, g1)[1]))
    elif mode == "nt":
        (m, k), (n, k2) = a.shape, b.shape
        a_spec = pl.BlockSpec((tm, k), lambda g0, g1: (ij(g0, g1)[0], 0))
        b_spec = pl.BlockSpec((tn, k), lambda g0, g1: (ij(g0, g1)[1], 0))
    else:
        (k, m), (k2, n) = a.shape, b.shape
        a_spec = pl.BlockSpec((k, tm), lambda g0, g1: (0, ij(g0, g1)[0]))
        b_spec = pl.BlockSpec((k, tn), lambda g0, g1: (0, ij(g0, g1)[1]))
    assert k == k2 and m % tm == 0 and n % tn == 0, (name, a.shape, b.shape, mode)
    assert a.dtype == bf16 and b.dtype == bf16, name

    def body(a_ref, b_ref, o_ref):
        o_ref[...] = lax.dot_general(a_ref[...], b_ref[...], _DIMS[mode], preferred_element_type=f32).astype(o_ref.dtype)

    return pl.pallas_call(
        body, grid=(n // tn, m // tm) if b_outer else (m // tm, n // tn), name=name,
        in_specs=[a_spec, b_spec], out_specs=pl.BlockSpec((tm, tn), lambda g0, g1: ij(g0, g1)),
        out_shape=SDS((m, n), out_dtype), compiler_params=_cp(("parallel", "parallel")),
    )(a, b)


CONV_K = 4
CONV_W = 3 * DN_W
HALO = 8


def _conv_fwd(proj, w, tt):
    t = proj.shape[0]
    nb8 = tt // HALO

    def body(x_ref, h_ref, w_ref, y_ref, xe):
        i = pl.program_id(0)
        xe[0:HALO, :] = jnp.where(i == 0, 0.0, h_ref[...])
        xe[HALO:, :] = x_ref[...]
        wv = w_ref[...]
        acc = jnp.zeros((tt, CONV_W), f32)
        for k in range(CONV_K):
            acc = acc + wv[k:k + 1, :] * xe[pl.ds(HALO - (CONV_K - 1) + k, tt), :]
        y_ref[...] = acc

    return pl.pallas_call(
        body, grid=(t // tt,), name="conv_fwd",
        in_specs=[pl.BlockSpec((tt, CONV_W), lambda i: (i, 0)),
                  pl.BlockSpec((HALO, CONV_W), lambda i: (jnp.maximum(i * nb8 - 1, 0), 0)),
                  _pspec(w.shape)],
        out_specs=pl.BlockSpec((tt, CONV_W), lambda i: (i, 0)),
        out_shape=SDS((t, CONV_W), f32),
        scratch_shapes=[pltpu.VMEM((tt + HALO, CONV_W), f32)],
        compiler_params=_cp(("arbitrary",)),
    )(proj, proj, w)


def _conv_bwd(proj, dy, w, tt):
    t = proj.shape[0]
    nb8 = tt // HALO
    last8 = t // HALO - 1
    nsteps = t // tt

    def body(x_ref, h_ref, dy_ref, n_ref, w_ref, dx_ref, dw_ref, xe, dye):
        i = pl.program_id(0)
        xe[0:HALO, :] = jnp.where(i == 0, 0.0, h_ref[...])
        xe[HALO:, :] = x_ref[...]
        dye[0:tt, :] = dy_ref[...]
        dye[tt:, :] = jnp.where(i == nsteps - 1, 0.0, n_ref[...])
        wv = w_ref[...]
        dyv = dy_ref[...]
        acc = jnp.zeros((tt, CONV_W), f32)

        @pl.when(i == 0)
        def _():
            dw_ref[...] = jnp.zeros(dw_ref.shape, f32)

        for k in range(CONV_K):
            acc = acc + wv[k:k + 1, :] * dye[pl.ds(CONV_K - 1 - k, tt), :]
            dw_ref[k:k + 1, :] += jnp.sum(dyv * xe[pl.ds(HALO - (CONV_K - 1) + k, tt), :], axis=0, keepdims=True)
        dx_ref[...] = acc.astype(dx_ref.dtype)

    return pl.pallas_call(
        body, grid=(nsteps,), name="conv_bwd",
        in_specs=[pl.BlockSpec((tt, CONV_W), lambda i: (i, 0)),
                  pl.BlockSpec((HALO, CONV_W), lambda i: (jnp.maximum(i * nb8 - 1, 0), 0)),
                  pl.BlockSpec((tt, CONV_W), lambda i: (i, 0)),
                  pl.BlockSpec((HALO, CONV_W), lambda i: (jnp.minimum((i + 1) * nb8, last8), 0)),
                  _pspec(w.shape)],
        out_specs=[pl.BlockSpec((tt, CONV_W), lambda i: (i, 0)), _pspec(w.shape)],
        out_shape=[SDS((t, CONV_W), bf16), SDS(w.shape, f32)],
        scratch_shapes=[pltpu.VMEM((tt + HALO, CONV_W), f32), pltpu.VMEM((tt + HALO, CONV_W), f32)],
        compiler_params=_cp(("arbitrary",)),
    )(proj, proj, dy, dy, w)


PREP_CHUNKS = 4


def _head_cols(part, h):
    return slice(part * DN_W + h * DN_D, part * DN_W + (h + 1) * DN_D)


def _prep_operands(qkv_ref, gb_ref):
    inst = [(slice(ch * DN_C, (ch + 1) * DN_C), h) for ch in range(PREP_CHUNKS) for h in range(DN_H)]
    ks = [qkv_ref[rs, _head_cols(1, h)] for rs, h in inst]
    vs = [qkv_ref[rs, _head_cols(2, h)] for rs, h in inst]
    gbvs = [gb_ref[ch * DN_C:(ch + 1) * DN_C, :] for ch in range(PREP_CHUNKS)]
    return inst, ks, vs, gbvs


def _dn_prep_fwd(qkv, gb):
    t = qkv.shape[0]
    rows = PREP_CHUNKS * DN_C

    def body(qkv_ref, gb_ref, u_ref, w_ref, inv_ref):
        inv_ref[...] = jnp.zeros(inv_ref.shape, f32)
        inst, ks, vs, gbvs = _prep_operands(qkv_ref, gb_ref)
        mats, rhss = _f_dn_pre(ks, vs, gbvs)
        invs = _inv_unit_lower(mats)
        uws = [_raw_dot(inv, rhs, "nn", True) for inv, rhs in zip(invs, rhss)]
        for (rs, h), inv, uw in zip(inst, invs, uws):
            u_ref[rs, _head_cols(0, h)] = uw[:, :DN_D]
            w_ref[rs, _head_cols(0, h)] = uw[:, DN_D:]
            inv_ref[rs, h * DN_D:h * DN_D + DN_C] = inv

    return pl.pallas_call(
        body, grid=(t // rows,), name="dn_prep_fwd",
        in_specs=[pl.BlockSpec((rows, CONV_W), lambda i: (i, 0)), pl.BlockSpec((rows, 128), lambda i: (i, 0))],
        out_specs=[pl.BlockSpec((rows, DN_W), lambda i: (i, 0))] * 3,
        out_shape=[SDS((t, DN_W), f32)] * 3,
        compiler_params=_cp(("arbitrary",)),
    )(qkv, gb)


def _dn_prep_bwd(qkv, gb, inv_all, du, dw, dqk1, dgb1):
    t = qkv.shape[0]
    rows = PREP_CHUNKS * DN_C

    def body(qkv_ref, gb_ref, inv_ref, du_ref, dw_ref, dqk1_ref, dgb1_ref, dqkv_ref, dgb_ref):
        inst, ks, vs, gbvs = _prep_operands(qkv_ref, gb_ref)
        (_, rhss), vjp = jax.vjp(_f_dn_pre, ks, vs, gbvs)
        invs = [inv_ref[rs, h * DN_D:h * DN_D + DN_C] for rs, h in inst]
        dxs = [jnp.concatenate([du_ref[rs, _head_cols(0, h)], dw_ref[rs, _head_cols(0, h)]], axis=1) for rs, h in inst]
        uws = [_raw_dot(inv, rhs, "nn", True) for inv, rhs in zip(invs, rhss)]
        drhss = [_raw_dot(inv, dx, "tn", True) for inv, dx in zip(invs, dxs)]
        das = [-_raw_dot(dr, uw, "nt", True) for dr, uw in zip(drhss, uws)]
        dks, dvs, dgbvs = vjp((das, drhss))
        for (rs, h), dk, dv in zip(inst, dks, dvs):
            dqkv_ref[rs, _head_cols(0, h)] = dqk1_ref[rs, _head_cols(0, h)]
            dqkv_ref[rs, _head_cols(1, h)] = dk + dqk1_ref[rs, _head_cols(1, h)]
            dqkv_ref[rs, _head_cols(2, h)] = dv
        for ch, dgbv in enumerate(dgbvs):
            rs = slice(ch * DN_C, (ch + 1) * DN_C)
            dgb_ref[rs, :] = dgbv + dgb1_ref[rs, :]

    return pl.pallas_call(
        body, grid=(t // rows,), name="dn_prep_bwd",
        in_specs=[pl.BlockSpec((rows, CONV_W), lambda i: (i, 0)), pl.BlockSpec((rows, 128), lambda i: (i, 0)),
                  pl.BlockSpec((rows, DN_W), lambda i: (i, 0)),
                  pl.BlockSpec((rows, DN_W), lambda i: (i, 0)), pl.BlockSpec((rows, DN_W), lambda i: (i, 0)),
                  pl.BlockSpec((rows, 2 * DN_W), lambda i: (i, 0)), pl.BlockSpec((rows, 128), lambda i: (i, 0))],
        out_specs=[pl.BlockSpec((rows, CONV_W), lambda i: (i, 0)), pl.BlockSpec((rows, 128), lambda i: (i, 0))],
        out_shape=[SDS((t, CONV_W), f32), SDS((t, 128), f32)],
        compiler_params=_cp(("arbitrary",)),
    )(qkv, gb, inv_all, du, dw, dqk1, dgb1)


def _dn_scan_fwd(qkv, gb, u, w):
    t = qkv.shape[0]
    n = t // DN_C
    heads = range(DN_H)

    def body(qkv_ref, gb_ref, u_ref, w_ref, o_ref, s_ref, state):
        @pl.when(pl.program_id(0) == 0)
        def _():
            state[...] = jnp.zeros(state.shape, f32)

        states = [state[h] for h in heads]
        for h in heads:
            s_ref[0, h] = states[h]
        outs, new = _f_dn_scan(states, [qkv_ref[:, _head_cols(0, h)] for h in heads],
                               [qkv_ref[:, _head_cols(1, h)] for h in heads],
                               [u_ref[:, _head_cols(0, h)] for h in heads],
                               [w_ref[:, _head_cols(0, h)] for h in heads], gb_ref[...])
        for h in heads:
            o_ref[:, _head_cols(0, h)] = outs[h]
            state[h] = new[h]

    return pl.pallas_call(
        body, grid=(n,), name="dn_scan_fwd",
        in_specs=[pl.BlockSpec((DN_C, 2 * DN_W), lambda i: (i, 0)), pl.BlockSpec((DN_C, 128), lambda i: (i, 0)),
                  pl.BlockSpec((DN_C, DN_W), lambda i: (i, 0)), pl.BlockSpec((DN_C, DN_W), lambda i: (i, 0))],
        out_specs=[pl.BlockSpec((DN_C, DN_W), lambda i: (i, 0)),
                   pl.BlockSpec((1, DN_H, DN_D, DN_D), lambda i: (i, 0, 0, 0))],
        out_shape=[SDS((t, DN_W), f32), SDS((n, DN_H, DN_D, DN_D), f32)],
        scratch_shapes=[pltpu.VMEM((DN_H, DN_D, DN_D), f32)],
        compiler_params=_cp(("arbitrary",)),
    )(qkv, gb, u, w)


def _dn_scan_bwd(qkv, gb, u, w, states, do):
    t = qkv.shape[0]
    n = t // DN_C
    rev = lambda i: (n - 1 - i, 0)
    heads = range(DN_H)

    def body(qkv_ref, gb_ref, u_ref, w_ref, s_ref, do_ref, dqk_ref, du_ref, dw_ref, dgb_ref, dstate):
        @pl.when(pl.program_id(0) == 0)
        def _():
            dstate[...] = jnp.zeros(dstate.shape, f32)

        _, vjp = jax.vjp(_f_dn_scan, [s_ref[0, h] for h in heads], [qkv_ref[:, _head_cols(0, h)] for h in heads],
                         [qkv_ref[:, _head_cols(1, h)] for h in heads], [u_ref[:, _head_cols(0, h)] for h in heads],
                         [w_ref[:, _head_cols(0, h)] for h in heads], gb_ref[...])
        ds, dq, dk, du, dw, dgbv = vjp(([do_ref[:, _head_cols(0, h)] for h in heads], [dstate[h] for h in heads]))
        for h in heads:
            dstate[h] = ds[h]
            dqk_ref[:, _head_cols(0, h)] = dq[h]
            dqk_ref[:, _head_cols(1, h)] = dk[h]
            du_ref[:, _head_cols(0, h)] = du[h]
            dw_ref[:, _head_cols(0, h)] = dw[h]
        dgb_ref[...] = dgbv

    return pl.pallas_call(
        body, grid=(n,), name="dn_scan_bwd",
        in_specs=[pl.BlockSpec((DN_C, 2 * DN_W), rev), pl.BlockSpec((DN_C, 128), rev),
                  pl.BlockSpec((DN_C, DN_W), rev), pl.BlockSpec((DN_C, DN_W), rev),
                  pl.BlockSpec((1, DN_H, DN_D, DN_D), lambda i: (n - 1 - i, 0, 0, 0)),
                  pl.BlockSpec((DN_C, DN_W), rev)],
        out_specs=[pl.BlockSpec((DN_C, 2 * DN_W), rev), pl.BlockSpec((DN_C, DN_W), rev),
                   pl.BlockSpec((DN_C, DN_W), rev), pl.BlockSpec((DN_C, 128), rev)],
        out_shape=[SDS((t, 2 * DN_W), f32), SDS((t, DN_W), f32), SDS((t, DN_W), f32), SDS((t, 128), f32)],
        scratch_shapes=[pltpu.VMEM((DN_H, DN_D, DN_D), f32)],
        compiler_params=_cp(("arbitrary",)),
    )(qkv, gb, u, w, states, do)


HALF = SW_W // 2


def _attn_specs(rows, cb, lag, nb):
    def one(c):
        if lag:
            return pl.BlockSpec((rows, HALF), lambda n, r: (jnp.maximum(jnp.minimum(n, nb - 1) - 1, 0), c))
        return pl.BlockSpec((rows, HALF), lambda n, r: (jnp.minimum(n, nb - 1), c))
    return [one(2 * cb), one(2 * cb + 1)]


def _sw_attn_fwd(q, k, vsrc, vcb, d):
    t = q.shape[0]
    rows = SW_B * d
    nb = t // rows
    cur = lambda cb: _attn_specs(rows, cb, False, nb)
    prev = lambda cb: _attn_specs(rows, cb, True, nb)

    def body(ql, qh, kpl, kph, kcl, kch, vpl, vph, vcl, vch, ol, oh, ll, lh):
        rs = pl.ds(pl.program_id(1), SW_B, stride=d) if d > 1 else pl.ds(0, SW_B)
        ld = lambda lo, hi: jnp.concatenate([lo[rs, :], hi[rs, :]], axis=1)
        o, lse = _f_attn(ld(ql, qh), ld(kpl, kph), ld(kcl, kch), ld(vpl, vph), ld(vcl, vch), pl.program_id(0) > 0)
        ol[rs, :] = o[:, :HALF]
        oh[rs, :] = o[:, HALF:]
        ll[rs, :] = lse[:, :HALF]
        lh[rs, :] = lse[:, HALF:]

    out = pl.BlockSpec((rows, HALF), lambda n, r: (n, 0))
    return pl.pallas_call(
        body, grid=(nb, d), name=f"sw_attn_fwd_d{d}",
        in_specs=cur(0) + prev(0) + cur(0) + prev(vcb) + cur(vcb), out_specs=[out] * 4,
        out_shape=[SDS((t, HALF), f32)] * 4, compiler_params=_cp(("arbitrary", "arbitrary")),
    )(q, q, k, k, k, k, vsrc, vsrc, vsrc, vsrc)


def _sw_attn_bwd(q, k, vsrc, vcb, do_l, do_h, dl_l, dl_h, d):
    t = q.shape[0]
    rows = SW_B * d
    nb = t // rows
    cur = lambda cb: _attn_specs(rows, cb, False, nb)
    prev = lambda cb: _attn_specs(rows, cb, True, nb)
    lag = pl.BlockSpec((rows, HALF), lambda n, r: (jnp.maximum(n - 1, 0), 0))
    here = pl.BlockSpec((rows, HALF), lambda n, r: (jnp.minimum(n, nb - 1), 0))

    def body(ql, qh, kpl, kph, kcl, kch, vpl, vph, vcl, vch, dol, doh, dll, dlh,
             dql, dqh, dkl, dkh, dvl, dvh, dk_hold, dv_hold):
        n = pl.program_id(0)
        r = pl.program_id(1)
        rs = pl.ds(r, SW_B, stride=d) if d > 1 else pl.ds(0, SW_B)
        hs = pl.ds(pl.multiple_of(r * SW_B, SW_B), SW_B)
        ld = lambda lo, hi: jnp.concatenate([lo[rs, :], hi[rs, :]], axis=1)

        def put(lo, hi, val):
            lo[rs, :] = val[:, :HALF]
            hi[rs, :] = val[:, HALF:]

        @pl.when(n < nb)
        def _():
            has_prev = n > 0
            _, vjp = jax.vjp(lambda q_, kp, kc, vp, vc: _f_attn(q_, kp, kc, vp, vc, has_prev),
                             ld(ql, qh), ld(kpl, kph), ld(kcl, kch), ld(vpl, vph), ld(vcl, vch))
            dq, dkp, dkc, dvp, dvc = vjp((ld(dol, doh), ld(dll, dlh)))
            put(dql, dqh, dq)
            put(dkl, dkh, dk_hold[hs, :] + dkp)
            put(dvl, dvh, dv_hold[hs, :] + dvp)
            dk_hold[hs, :] = dkc
            dv_hold[hs, :] = dvc

        @pl.when(n == nb)
        def _():
            put(dkl, dkh, dk_hold[hs, :])
            put(dvl, dvh, dv_hold[hs, :])

    return pl.pallas_call(
        body, grid=(nb + 1, d), name=f"sw_attn_bwd_d{d}",
        in_specs=cur(0) + prev(0) + cur(0) + prev(vcb) + cur(vcb) + [here] * 4,
        out_specs=[here, here, lag, lag, lag, lag],
        out_shape=[SDS((t, HALF), f32)] * 6,
        scratch_shapes=[pltpu.VMEM((rows, SW_W), f32)] * 2,
        compiler_params=_cp(("arbitrary", "arbitrary")),
    )(q, q, k, k, k, k, vsrc, vsrc, vsrc, vsrc, do_l, do_h, dl_l, dl_h)


def _loss_head(y, target, tt):
    t = y.shape[0]

    def body(y_ref, t_ref, dy_ref, l_ref):
        @pl.when(pl.program_id(0) == 0)
        def _():
            l_ref[...] = jnp.zeros(l_ref.shape, f32)

        err = y_ref[...] - t_ref[...]
        dy_ref[...] = err * (1.0 / D)
        l_ref[...] += 0.5 * jnp.sum(jnp.sum(err * err, axis=1, keepdims=True) * (1.0 / D), axis=0, keepdims=True)

    return pl.pallas_call(
        body, grid=(t // tt,), name="loss_head",
        in_specs=[pl.BlockSpec((tt, D), lambda i: (i, 0))] * 2,
        out_specs=[pl.BlockSpec((tt, D), lambda i: (i, 0)), pl.BlockSpec((8, 128), lambda i: (0, 0))],
        out_shape=[SDS((t, D), f32), SDS((8, 128), f32)],
        compiler_params=_cp(("arbitrary",)),
    )(y, target)


def _adamw(name, w, m, v, gparts, tr):
    r, c = w.shape
    p = gparts.shape[0]
    assert r % tr == 0, (name, w.shape, tr)

    def body(w_ref, m_ref, v_ref, g_ref, go_ref, d_ref, mo_ref, vo_ref):
        g = g_ref[0].astype(f32)
        for k in range(1, p):
            g = g + g_ref[k].astype(f32)
        wv = w_ref[...]
        mn = ADAM_B1 * m_ref[...] + (1.0 - ADAM_B1) * g
        vn = ADAM_B2 * v_ref[...] + (1.0 - ADAM_B2) * jnp.square(g)
        m_hat = mn / (1.0 - ADAM_B1 ** ADAM_STEP)
        v_hat = vn / (1.0 - ADAM_B2 ** ADAM_STEP)
        go_ref[...] = g
        d_ref[...] = -ADAM_LR * (m_hat / (jnp.sqrt(v_hat) + ADAM_EPS) + ADAM_WD * wv)
        mo_ref[...] = mn
        vo_ref[...] = vn

    spec = pl.BlockSpec((tr, c), lambda i: (i, 0))
    return pl.pallas_call(
        body, grid=(r // tr,), name=name,
        in_specs=[spec, spec, spec, pl.BlockSpec((p, tr, c), lambda i: (0, i, 0))],
        out_specs=[spec] * 4, out_shape=[SDS((r, c), f32)] * 4,
        compiler_params=_cp(("arbitrary",)),
    )(w, m, v, gparts)


def _mod_cols(c_all, w_mod, b_cols):
    nl, _, wc = w_mod.shape

    def body(c_ref, w_ref, b_ref, o_ref):
        o_ref[0] = _raw_dot(jax.nn.silu(c_ref[...]), w_ref[0], "nn", True) + b_ref[0]

    return pl.pallas_call(
        body, grid=(nl,), name="mod_cols",
        in_specs=[pl.BlockSpec((NDEV, D), lambda l: (0, 0)), pl.BlockSpec((1, D, wc), lambda l: (l, 0, 0)),
                  pl.BlockSpec((1, 1, wc), lambda l: (l, 0, 0))],
        out_specs=pl.BlockSpec((1, NDEV, wc), lambda l: (l, 0, 0)),
        out_shape=SDS((nl, NDEV, wc), f32), compiler_params=_cp(("arbitrary",)),
    )(c_all, w_mod, b_cols)


def _wmod_grad(c_all, dmod_cols):
    nl, _, wc = dmod_cols.shape

    def body(c_ref, d_ref, o_ref):
        o_ref[0, 0] = _raw_dot(jax.nn.silu(c_ref[...]), d_ref[0], "tn", True)

    return pl.pallas_call(
        body, grid=(nl,), name="wmod_grad",
        in_specs=[pl.BlockSpec((NDEV, D), lambda l: (0, 0)), pl.BlockSpec((1, NDEV, wc), lambda l: (l, 0, 0))],
        out_specs=pl.BlockSpec((1, 1, D, wc), lambda l: (0, l, 0, 0)),
        out_shape=SDS((1, nl, D, wc), f32), compiler_params=_cp(("arbitrary",)),
    )(c_all, dmod_cols)


def _me_and_peers():
    x, y, c = (lax.axis_index(a) for a in AXES)
    peers = []
    for k in range(1, NDEV):
        px = 1 - x if (k >> 2) & 1 else x
        py = 1 - y if (k >> 1) & 1 else y
        pc = 1 - c if k & 1 else c
        peers.append(((px, py, pc), 4 * px + 2 * py + pc))
    return 4 * x + 2 * y + c, peers


_ANY = pl.BlockSpec(memory_space=pl.ANY)


def _all_gather(name, arrs):
    n = len(arrs)

    def body(*refs):
        ins, outs, token = refs[:n], refs[n:2 * n], refs[2 * n]
        send_sems, recv_sems, local_sems = refs[2 * n + 1:]
        me, peers = _me_and_peers()
        mine = [pltpu.make_async_copy(ins[a], outs[a].at[me], local_sems.at[a]) for a in range(n)]
        copies = [pltpu.make_async_remote_copy(ins[a], outs[a].at[me], send_sems.at[a * (NDEV - 1) + k],
                                               recv_sems.at[a * (NDEV - 1) + k], device_id=dev,
                                               device_id_type=pl.DeviceIdType.MESH)
                  for a in range(n) for k, (dev, _) in enumerate(peers)]
        for cp in mine + copies:
            cp.start()
        token[...] = jnp.zeros(token.shape, token.dtype)
        for cp in copies + mine:
            cp.wait()

    res = pl.pallas_call(
        body, name=name, in_specs=[_ANY] * n, out_specs=[_ANY] * n + [pl.BlockSpec(memory_space=pltpu.VMEM)],
        out_shape=[SDS((NDEV,) + a.shape, a.dtype) for a in arrs] + [SDS((8, 128), f32)],
        scratch_shapes=[pltpu.SemaphoreType.DMA((n * (NDEV - 1),)), pltpu.SemaphoreType.DMA((n * (NDEV - 1),)),
                        pltpu.SemaphoreType.DMA((n,))],
        compiler_params=pltpu.CompilerParams(has_side_effects=True),
    )(*arrs)
    return list(res[:n]), res[n]


def _all_to_all(name, a):
    def body(a_ref, o_ref, send_sems, recv_sems, local_sem):
        me, peers = _me_and_peers()
        mine = pltpu.make_async_copy(a_ref.at[me], o_ref.at[me], local_sem)
        mine.start()
        copies = [pltpu.make_async_remote_copy(a_ref.at[pid], o_ref.at[me], send_sems.at[k], recv_sems.at[k],
                                               device_id=dev, device_id_type=pl.DeviceIdType.MESH)
                  for k, (dev, pid) in enumerate(peers)]
        for cp in copies:
            cp.start()
        for cp in copies:
            cp.wait()
        mine.wait()

    return pl.pallas_call(
        body, name=name, in_specs=[_ANY], out_specs=_ANY, out_shape=SDS(a.shape, a.dtype),
        scratch_shapes=[pltpu.SemaphoreType.DMA((NDEV - 1,)), pltpu.SemaphoreType.DMA((NDEV - 1,)), pltpu.SemaphoreType.DMA],
        compiler_params=pltpu.CompilerParams(has_side_effects=True),
    )(a)


_HBM = pl.BlockSpec(memory_space=pltpu.HBM)
_SEM = pl.BlockSpec(memory_space=pltpu.SEMAPHORE)
_FLOW = pltpu.SideEffectType.DATAFLOW_SIDE_EFFECTING


def _exchange_copies(srcs, lands, send_sems, recv_sems, scatter):
    me, peers = _me_and_peers()
    copies = []
    for a, (src, land) in enumerate(zip(srcs, lands)):
        for k, (dev, pid) in enumerate(peers):
            copies.append(pltpu.make_async_remote_copy(
                src.at[pid] if scatter else src, land.at[me], send_sems.at[a * (NDEV - 1) + k],
                recv_sems.at[a * (NDEV - 1) + k], device_id=dev, device_id_type=pl.DeviceIdType.MESH))
    return copies


def _exchange_start(name, srcs, lands, scatter):
    n = len(srcs)
    nsem = n * (NDEV - 1)

    def body(*refs):
        for cp in _exchange_copies(refs[:n], refs[n:2 * n], refs[2 * n], refs[2 * n + 1], scatter):
            cp.start()
        token = refs[-1]
        token[...] = jnp.zeros(token.shape, token.dtype)

    hbm = lambda a: pltpu.with_memory_space_constraint(a, pltpu.HBM)
    res = pl.pallas_call(
        body, name=name,
        out_shape=(pltpu.SemaphoreType.DMA((nsem,)), pltpu.SemaphoreType.DMA((nsem,)),
                   *[pltpu.HBM(a.shape, a.dtype) for a in srcs], *[pltpu.HBM(a.shape, a.dtype) for a in lands],
                   SDS((8, 128), f32)),
        in_specs=[_HBM] * (2 * n), out_specs=(_SEM, _SEM, *([_HBM] * (2 * n)), pl.BlockSpec(memory_space=pltpu.VMEM)),
        input_output_aliases={i: 2 + i for i in range(2 * n)},
        compiler_params=pltpu.CompilerParams(has_side_effects=_FLOW),
    )(*[hbm(a) for a in srcs], *[hbm(a) for a in lands])
    return res[0], res[1], list(res[2:2 + n]), list(res[2 + n:2 + 2 * n]), res[-1]


def _exchange_wait(name, handle, after, scatter):
    send_sems, recv_sems, srcs, lands, _ = handle
    n = len(srcs)

    def body(*refs):
        for cp in _exchange_copies(refs[:n], refs[n:2 * n], refs[2 * n], refs[2 * n + 1], scatter):
            cp.wait_send()
            cp.wait_recv()
        token = refs[-1]
        token[...] = jnp.zeros(token.shape, token.dtype)

    res = pl.pallas_call(
        body, name=name,
        out_shape=(*[pltpu.HBM(a.shape, a.dtype) for a in srcs + lands], SDS((8, 128), f32)),
        in_specs=[_HBM] * (2 * n) + [_SEM, _SEM, _ANY],
        out_specs=(*([_HBM] * (2 * n)), pl.BlockSpec(memory_space=pltpu.VMEM)),
        input_output_aliases={i: i for i in range(2 * n)},
        compiler_params=pltpu.CompilerParams(has_side_effects=_FLOW),
    )(*srcs, *lands, send_sems, recv_sems, after)
    return list(res[n:2 * n]), res[-1]


def _own_slot(a, me):
    return lax.dynamic_update_slice(jnp.zeros((NDEV,) + a.shape, a.dtype), a[None], (me,) + (0,) * a.ndim)


CB_Z = 3
CB_GM = 8
CB_SW = 10
CB_AB = 38


def _sw_pre_tiles(proj, cs, sn):
    return [(proj, SW_W, CB_SW + 3 * g + j) for g in range(3) for j in range(2)] + [(cs, SW_W, 0), (sn, SW_W, 0)]


def _layer_fwd(x, p, tabs, late_weights):
    cs, sn = tabs
    sh1, sc1, g1, sh2, sc2, g2 = (p["mod"][k] for k in range(6))
    (h1,) = _ew_fwd("modnorm1_fwd", _f_modnorm, [(x, D, 0)], [p["mixw"], sc1, sh1], [(D, bf16)], 512)
    proj = _mm("in_proj", h1, p["wi"], "nn", 1024, 1664, b_outer=True)
    y = _conv_fwd(proj, p["conv"], 256)
    qkv, gb = _ew_fwd("dn_act_fwd", _f_dn_act, [(y, CONV_W, 0), (proj, 128, CB_AB)], [p["alog"], p["dtb"]],
                      [(CONV_W, f32), (128, f32)], 256)
    u, w, inv = _dn_prep_fwd(qkv, gb)
    o, states = _dn_scan_fwd(qkv, gb, u, w)
    (ya,) = _ew_fwd("dn_out_fwd", _f_dn_out, [(o, DN_W, 0), (proj, DN_W, CB_Z)], [p["wn"]], [(DN_W, bf16)], 256)
    (yb,) = _ew_fwd("gm_fwd", _f_gm, [(proj, GM_W, CB_GM), (proj, GM_W, CB_GM + 1)],
                    [p["lng"], p["lnb"], p["ws"], p["bst"]], [(GM_W, bf16)], GM_C)
    qk = _ew_fwd("sw_pre_fwd", _f_sw_pre, _sw_pre_tiles(proj, cs, sn), [p["wq"], p["wk"]], [(SW_W, f32)] * 6, 512)
    ol = [[], []]
    for g, d in enumerate(SW_DIL):
        o_l, o_h, l_l, l_h = _sw_attn_fwd(qk[2 * g], qk[2 * g + 1], proj, CB_SW + 3 * g + 2, d)
        ol[0] += [o_l, l_l]
        ol[1] += [o_h, l_h]
    yc = [_ew_fwd(f"sw_merge_fwd_{h}", _f_sw_merge, [(a, HALF, 0) for a in ol[h]], [], [(HALF, bf16)], 512)[0]
          for h in range(2)]
    ycat = jnp.concatenate([ya, yb] + yc, axis=1)
    p.update(late_weights(ycat))
    m1 = _mm("out_proj", ycat, p["wo"], "nn", 1024, 1024)
    (x2,) = _ew_fwd("resid1_fwd", _f_resid, [(x, D, 0), (m1, D, 0)], [g1], [(D, f32)], 512)
    (h2,) = _ew_fwd("modnorm2_fwd", _f_modnorm, [(x2, D, 0)], [p["ffnw"], sc2, sh2], [(D, bf16)], 512)
    gu = _mm("ffn_in", h2, p["wfi"], "nn", 1024, 1408, b_outer=True)
    (act,) = _ew_fwd("swiglu_fwd", _f_swiglu, [(gu, 2 * FFN, 0)], [], [(FFN, bf16)], 256)
    m2 = _mm("ffn_out", act, p["wfo"], "nn", 1024, 1024)
    (x3,) = _ew_fwd("resid2_fwd", _f_resid, [(x2, D, 0), (m2, D, 0)], [g2], [(D, f32)], 512)
    res = dict(x=x, h1=h1, proj=proj, y=y, qkv=qkv, gb=gb, u=u, w=w, inv=inv, states=states, o=o, qk=list(qk), ol=ol,
               ycat=ycat, m1=m1, x2=x2, h2=h2, gu=gu, act=act, m2=m2)
    return x3, res


def _layer_bwd(dx3, p, r, tabs, ffn_grads_ready):
    cs, sn = tabs
    sh1, sc1, g1, sh2, sc2, g2 = (p["mod"][k] for k in range(6))
    proj = r["proj"]
    (dx2a, dm2), (dg2,) = _ew_bwd("resid2_bwd", _f_resid, [(r["x2"], D, 0), (r["m2"], D, 0)], [g2], [(dx3, D, 0)],
                                  [(0, f32), (1, bf16)], 512)
    dact = _mm("ffn_out_dx", dm2, p["wfo"], "nt", 512, FFN, bf16)
    dwfo = _mm("ffn_out_dw", r["act"], dm2, "tn", 256, 1024, bf16)
    (dgu_cat,), _ = _ew_bwd("swiglu_bwd", _f_swiglu, [(r["gu"], 2 * FFN, 0)], [], [(dact, FFN, 0)], [(0, bf16)], 256)
    dh2 = _mm("ffn_in_dx", dgu_cat, p["wfi"], "nt", 512, 1024)
    dwfi = _mm("ffn_in_dw", r["h2"], dgu_cat, "tn", 1024, 512, bf16)
    g1 = g1 + ffn_grads_ready(dwfi, dwfo)[0, 0]
    (dx2,), (dffnw, dsc2, dsh2) = _ew_bwd("modnorm2_bwd", _f_modnorm, [(r["x2"], D, 0)], [p["ffnw"], sc2, sh2],
                                          [(dh2, D, 0)], [(0, f32)], 512, adds=[dx2a])
    (dxa, dm1), (dg1,) = _ew_bwd("resid1_bwd", _f_resid, [(r["x"], D, 0), (r["m1"], D, 0)], [g1], [(dx2, D, 0)],
                                 [(0, f32), (1, bf16)], 512)
    dycat = _mm("out_proj_dx", dm1, p["wo"], "nt", 1024, 1024)
    dwo = _mm("out_proj_dw", r["ycat"], dm1, "tn", 1024, 512, bf16)
    dol = [_ew_bwd(f"sw_merge_bwd_{h}", _f_sw_merge, [(a, HALF, 0) for a in r["ol"][h]], [], [(dycat, HALF, 6 + h)],
                   [(k, f32) for k in range(6)], 512)[0] for h in range(2)]
    dqk, dvs = [], []
    for g, d in enumerate(SW_DIL):
        dq_l, dq_h, dk_l, dk_h, dv_l, dv_h = _sw_attn_bwd(
            r["qk"][2 * g], r["qk"][2 * g + 1], proj, CB_SW + 3 * g + 2,
            dol[0][2 * g], dol[1][2 * g], dol[0][2 * g + 1], dol[1][2 * g + 1], d)
        dqk += [[(dq_l, HALF, 0), (dq_h, HALF, 0)], [(dk_l, HALF, 0), (dk_h, HALF, 0)]]
        dvs += [dv_l.astype(bf16), dv_h.astype(bf16)]
    dqk_raw, (dwq, dwk) = _ew_bwd("sw_pre_bwd", _f_sw_pre, _sw_pre_tiles(proj, cs, sn), [p["wq"], p["wk"]],
                                  dqk, [(k, bf16) for k in range(6)], 512)
    (dgm_u, dgm_v), (dlng, dlnb, dws, dbst) = _ew_bwd(
        "gm_bwd", _f_gm, [(proj, GM_W, CB_GM), (proj, GM_W, CB_GM + 1)], [p["lng"], p["lnb"], p["ws"], p["bst"]],
        [(dycat, GM_W, 2)], [(0, bf16), (1, bf16)], GM_C)
    (do, dz), (dwn,) = _ew_bwd("dn_out_bwd", _f_dn_out, [(r["o"], DN_W, 0), (proj, DN_W, CB_Z)], [p["wn"]],
                               [(dycat, DN_W, 0)], [(0, f32), (1, bf16)], 256)
    dqk1, du, dw, dgb1 = _dn_scan_bwd(r["qkv"], r["gb"], r["u"], r["w"], r["states"], do)
    dqkv, dgb = _dn_prep_bwd(r["qkv"], r["gb"], r["inv"], du, dw, dqk1, dgb1)
    (dy, dab), (dalog, ddtb) = _ew_bwd("dn_act_bwd", _f_dn_act, [(r["y"], CONV_W, 0), (proj, 128, CB_AB)],
                                       [p["alog"], p["dtb"]], [(dqkv, CONV_W, 0), (dgb, 128, 0)],
                                       [(0, f32), (1, bf16)], 256)
    dxc, dconv = _conv_bwd(proj, dy, p["conv"], 256)
    dproj = jnp.concatenate([dxc, dz, dgm_u, dgm_v, dqk_raw[0], dqk_raw[1], dvs[0], dvs[1], dqk_raw[2], dqk_raw[3],
                             dvs[2], dvs[3], dqk_raw[4], dqk_raw[5], dvs[4], dvs[5], dab], axis=1)
    dh1 = _mm("in_proj_dx", dproj, p["wi"], "nt", 512, 1024)
    dwi = _mm("in_proj_dw", r["h1"], dproj, "tn", 1024, 384, bf16)
    (dx,), (dmixw, dsc1, dsh1) = _ew_bwd("modnorm1_bwd", _f_modnorm, [(r["x"], D, 0)], [p["mixw"], sc1, sh1],
                                         [(dh1, D, 0)], [(0, f32)], 512, adds=[dxa])
    grads = dict(wi=dwi, wo=dwo, wfi=dwfi, wfo=dwfo, conv=dconv, mixw=dmixw, ffnw=dffnw,
                 mod=jnp.stack([dsh1, dsc1, dg1, dsh2, dsc2, dg2]), alog=dalog, dtb=ddtb, wn=dwn, lng=dlng, lnb=dlnb,
                 ws=dws, bst=dbst, wq=dwq, wk=dwk)
    return dx, grads


def _rope_tables(t):
    inv = ROPE_THETA ** (-jnp.arange(0, ROPE_DIM, 2, dtype=f32) / ROPE_DIM)
    ang = jnp.arange(t, dtype=f32)[:, None] * inv[None, :]
    cos, sin = jnp.cos(ang), jnp.sin(ang)
    rest = SW_D - ROPE_DIM
    ch = jnp.concatenate([cos, cos, jnp.ones((t, rest), f32)], axis=1)
    sh = jnp.concatenate([sin, sin, jnp.zeros((t, rest), f32)], axis=1)
    return jnp.tile(ch, (1, SW_H)), jnp.tile(sh, (1, SW_H))


def _pad_last(a, n):
    return jnp.pad(a, [(0, 0)] * (a.ndim - 1) + [(0, n - a.shape[-1])])


def _scatter_cols(name, gs, cs, pad_to):
    r = gs[0].shape[0]
    parts = jnp.stack([jnp.transpose(_pad_last(g.reshape(r, NDEV, cs), pad_to), (1, 0, 2)) for g in gs], axis=1)
    return _all_to_all(name, parts).reshape(NDEV, len(gs) * r, pad_to)


_SMALL = ("b_mod", "mix_norm_w", "ffn_norm_w", "dn_a_log", "dn_dt_bias", "dn_out_norm_w", "gm_ln_g", "gm_ln_b",
          "gm_w_s", "gm_b_s", "sw_q_norm_w", "sw_k_norm_w")
PACK_ROWS = 800


def _pack_rows(a):
    return -(-a.size // 1024) * 8


def _pack(arrs):
    parts = [jnp.pad(a.reshape(-1), (0, _pack_rows(a) * 128 - a.size)).reshape(_pack_rows(a), 128) for a in arrs]
    rows = sum(p.shape[0] for p in parts)
    parts.append(jnp.zeros((-(-rows // PACK_ROWS) * PACK_ROWS - rows, 128), parts[0].dtype))
    return jnp.concatenate(parts, axis=0)


def _unpack(buf, like):
    out, off = [], 0
    for a in like:
        rows = _pack_rows(a)
        flat = buf[off:off + rows].reshape(-1)
        out.append((flat if flat.size == a.size else flat[:a.size]).reshape(a.shape))
        off += rows
    return out


def kernel(x, c, w_mod, b_mod, mix_norm_w, ffn_norm_w, w_in, w_out, dn_conv_w, dn_a_log, dn_dt_bias, dn_out_norm_w, gm_ln_g, gm_ln_b, gm_w_s, gm_b_s, sw_q_norm_w, sw_k_norm_w, w_ffn_in, w_ffn_out, loss_target, m_w_mod, m_b_mod, m_mix_norm_w, m_ffn_norm_w, m_w_in, m_w_out, m_dn_conv_w, m_dn_a_log, m_dn_dt_bias, m_dn_out_norm_w, m_gm_ln_g, m_gm_ln_b, m_gm_w_s, m_gm_b_s, m_sw_q_norm_w, m_sw_k_norm_w, m_w_ffn_in, m_w_ffn_out, v_w_mod, v_b_mod, v_mix_norm_w, v_ffn_norm_w, v_w_in, v_w_out, v_dn_conv_w, v_dn_a_log, v_dn_dt_bias, v_dn_out_norm_w, v_gm_ln_g, v_gm_ln_b, v_gm_w_s, v_gm_b_s, v_sw_q_norm_w, v_sw_k_norm_w, v_w_ffn_in, v_w_ffn_out):
    weights = dict(w_mod=w_mod, b_mod=b_mod, mix_norm_w=mix_norm_w, ffn_norm_w=ffn_norm_w, w_in=w_in, w_out=w_out,
                   dn_conv_w=dn_conv_w, dn_a_log=dn_a_log, dn_dt_bias=dn_dt_bias, dn_out_norm_w=dn_out_norm_w,
                   gm_ln_g=gm_ln_g, gm_ln_b=gm_ln_b, gm_w_s=gm_w_s, gm_b_s=gm_b_s, sw_q_norm_w=sw_q_norm_w,
                   sw_k_norm_w=sw_k_norm_w, w_ffn_in=w_ffn_in, w_ffn_out=w_ffn_out)
    mom = dict(w_mod=m_w_mod, b_mod=m_b_mod, mix_norm_w=m_mix_norm_w, ffn_norm_w=m_ffn_norm_w, w_in=m_w_in,
               w_out=m_w_out, dn_conv_w=m_dn_conv_w, dn_a_log=m_dn_a_log, dn_dt_bias=m_dn_dt_bias,
               dn_out_norm_w=m_dn_out_norm_w, gm_ln_g=m_gm_ln_g, gm_ln_b=m_gm_ln_b, gm_w_s=m_gm_w_s, gm_b_s=m_gm_b_s,
               sw_q_norm_w=m_sw_q_norm_w, sw_k_norm_w=m_sw_k_norm_w, w_ffn_in=m_w_ffn_in, w_ffn_out=m_w_ffn_out)
    var = dict(w_mod=v_w_mod, b_mod=v_b_mod, mix_norm_w=v_mix_norm_w, ffn_norm_w=v_ffn_norm_w, w_in=v_w_in,
               w_out=v_w_out, dn_conv_w=v_dn_conv_w, dn_a_log=v_dn_a_log, dn_dt_bias=v_dn_dt_bias,
               dn_out_norm_w=v_dn_out_norm_w, gm_ln_g=v_gm_ln_g, gm_ln_b=v_gm_ln_b, gm_w_s=v_gm_w_s, gm_b_s=v_gm_b_s,
               sw_q_norm_w=v_sw_q_norm_w, sw_k_norm_w=v_sw_k_norm_w, w_ffn_in=v_w_ffn_in, w_ffn_out=v_w_ffn_out)
    names = list(weights)
    xi, tgt = x[0], loss_target[0]
    t = xi.shape[0]
    nl = w_mod.shape[0]
    ax, ay, ac = (lax.axis_index(a) for a in AXES)
    me = 4 * ax + 2 * ay + ac
    mod_cs = w_mod.shape[2]

    shards_a = [_pad_last(w_in.astype(bf16), 640)]
    shards_b = [w_out.astype(bf16), _pad_last(w_ffn_in.astype(bf16), 768), w_ffn_out.astype(bf16)]

    def start_gather(l, which, tok):
        shards_l = [a[l] + tok.astype(bf16) for a in (shards_a if which == "a" else shards_b)]
        return _exchange_start(f"ag_start_{which}{l}", shards_l, [_own_slot(a, me) for a in shards_l], False)

    gather_a = start_gather(0, "a", jnp.zeros((), f32))

    c = c + gather_a[4][0, 0]
    (c_g, conv_g), _ = _all_gather("ag_c_conv", [jnp.broadcast_to(c, (NDEV, D)), _pad_last(dn_conv_w, 256)])
    c_all = c_g[:, 0, :]
    conv = jnp.transpose(conv_g, (1, 2, 0, 3))[..., :dn_conv_w.shape[2]].reshape(nl, CONV_K, CONV_W)
    b_cols = lax.dynamic_slice_in_dim(b_mod, me * mod_cs, mod_cs, axis=1)[:, None, :]
    modc = _mod_cols(c_all, w_mod, b_cols)
    mod_tx = jnp.pad(jnp.transpose(modc, (1, 0, 2)), ((0, 0), (0, 8 - nl), (0, 0)))
    mod_rx = _all_to_all("a2a_mod", mod_tx)[:, :nl]
    mod = jnp.transpose(mod_rx, (1, 0, 2)).reshape(nl, 6, 1, D)

    def w_in_full(landed):
        wi = jnp.transpose(landed[0], (1, 0, 2))[..., :w_in.shape[2]].reshape(D, IN_W)
        return jnp.concatenate([wi[:, :2048], wi[:, 2056:], wi[:, 2048:2056], jnp.zeros((D, IN_WA - IN_W), bf16)], axis=-1)

    def late_full(landed):
        go, gfi, gfo = landed
        wfi = jnp.transpose(gfi, (1, 0, 2))[..., :w_ffn_in.shape[2]].reshape(D, 2 * FFN)
        return dict(wo=go.reshape(D, D), wfi=wfi, wfo=gfo.reshape(FFN, D))

    pad128 = lambda a: _pad_last(a, 128)[:, None, :]
    params = dict(
        mod=mod, conv=conv,
        mixw=mix_norm_w[:, None, :], ffnw=ffn_norm_w[:, None, :], alog=pad128(dn_a_log), dtb=pad128(dn_dt_bias),
        wn=dn_out_norm_w[:, None, :], lng=gm_ln_g[:, None, :], lnb=gm_ln_b[:, None, :], ws=gm_w_s,
        bst=_pad_last(jnp.transpose(gm_b_s, (0, 2, 1)), 128),
        wq=jnp.pad(sw_q_norm_w[:, None, :], ((0, 0), (0, 7), (0, 128 - SW_D))),
        wk=jnp.pad(sw_k_norm_w[:, None, :], ((0, 0), (0, 7), (0, 128 - SW_D))))
    tabs = _rope_tables(t)

    layer_p, res = [], []
    xc, after, gather_b = xi, mod, None
    for l in range(nl):
        landed_a, tok = _exchange_wait(f"ag_wait_a{l}", gather_a, after, False)
        p = {k: v[l] for k, v in params.items()}
        p["wi"] = w_in_full(landed_a)
        hb = gather_b if l else start_gather(0, "b", tok[0, 0])
        tok = tok if l else hb[4]
        if l + 1 < nl:
            gather_a = start_gather(l + 1, "a", tok[0, 0])
            gather_b = start_gather(l + 1, "b", gather_a[4][0, 0])
            tok = gather_b[4]
        p["mod"] = p["mod"] + tok[0, 0]
        late = lambda ycat, hb=hb, l=l: late_full(_exchange_wait(f"ag_wait_b{l}", hb, ycat, False)[0])
        layer_p.append(p)
        xc, r = _layer_fwd(xc, p, tabs, late)
        res.append(r)
        after = xc
    dy, lpart = _loss_head(xc, tgt, 512)
    loss = lax.psum(lpart[0, 0], AXES)

    slot = lax.broadcasted_iota(jnp.int32, (NDEV, 1, 1), 0)

    def start_scatter(name, parts):
        return _exchange_start(name, parts, [jnp.where(slot == me, a, jnp.zeros_like(a)) for a in parts], True)

    dxi, gl, h_ffn, h_mix = dy, [None] * nl, [None] * nl, [None] * nl
    for l in reversed(range(nl)):
        def ffn_ready(dwfi, dwfo, l=l):
            h_ffn[l] = start_scatter(f"a2a_start_f{l}", [
                jnp.transpose(_pad_last(dwfi.reshape(D, NDEV, w_ffn_in.shape[2]), 768), (1, 0, 2)),
                dwfo.reshape(NDEV, w_ffn_out.shape[1], D)])
            return h_ffn[l][4]

        dxi, gl[l] = _layer_bwd(dxi, layer_p[l], res[l], tabs, ffn_ready)
        d = gl[l]["wi"]
        d = jnp.concatenate([d[:, :2048], d[:, 4864:4872], d[:, 2048:4864]], axis=-1)
        mix_parts = [jnp.transpose(_pad_last(d.reshape(D, NDEV, w_in.shape[2]), 640), (1, 0, 2)),
                     gl[l]["wo"].reshape(NDEV, w_out.shape[1], D)]
        if l > 0:
            h_mix[l] = start_scatter(f"a2a_start_m{l}", mix_parts)
            layer_p[l - 1]["mod"] = layer_p[l - 1]["mod"] + h_mix[l][4][0, 0]
    g = {k: jnp.stack([gl[l][k] for l in range(nl)]) for k in gl[0] if k not in ("wi", "wo", "wfi", "wfo")}

    dmod = g["mod"].reshape(nl, 6 * D)
    small_g = dict(b_mod=dmod, mix_norm_w=g["mixw"][:, 0], ffn_norm_w=g["ffnw"][:, 0], dn_a_log=g["alog"][:, 0, :DN_H],
                   dn_dt_bias=g["dtb"][:, 0, :DN_H], dn_out_norm_w=g["wn"][:, 0], gm_ln_g=g["lng"][:, 0],
                   gm_ln_b=g["lnb"][:, 0], gm_w_s=g["ws"], gm_b_s=jnp.transpose(g["bst"][:, :, :GM_G], (0, 2, 1)),
                   sw_q_norm_w=g["wq"][:, 0, :SW_D], sw_k_norm_w=g["wk"][:, 0, :SW_D])
    (parts,), tok = _all_gather("ag_small_grads", [_pack([small_g[n] for n in _SMALL])])
    h_mix[0] = start_scatter("a2a_start_m0", [a + tok[0, 0].astype(bf16) for a in mix_parts])
    like = [weights[n] for n in _SMALL]
    sm = _adamw("adamw_small", _pack(like) + h_mix[0][4][0, 0], _pack([mom[n] for n in _SMALL]),
                _pack([var[n] for n in _SMALL]), parts, PACK_ROWS)
    out = {n: vals for n, vals in zip(_SMALL, zip(*[_unpack(b, like) for b in sm]))}

    dmod_all = parts[:, :nl * 6 * D // 128, :].reshape(NDEV, nl, 6 * D)
    dmod_cols = jnp.transpose(lax.dynamic_slice_in_dim(dmod_all, me * mod_cs, mod_cs, axis=2), (1, 0, 2))
    gw_mod = _wmod_grad(c_all, dmod_cols).reshape(1, nl * D, mod_cs)

    def adamw_sharded(n, gp, cols, padc):
        shp = weights[n].shape
        rows = gp.shape[1]
        prep = lambda a: _pad_last(a.reshape(rows, cols), padc)
        tr = 256 if rows % 256 == 0 else rows // 4 if rows % 32 == 0 else rows
        res4 = _adamw("adamw_" + n, prep(weights[n]), prep(mom[n]), prep(var[n]), gp, tr)
        out[n] = tuple(a[:, :cols].reshape(shp) for a in res4)

    adamw_sharded("w_mod", gw_mod, mod_cs, mod_cs)
    adamw_sharded("dn_conv_w", _scatter_cols("a2a_conv", [gl[l]["conv"] for l in range(nl)], dn_conv_w.shape[2], 256),
                  dn_conv_w.shape[2], 256)
    after, land_f, land_m = sm[0], [None] * nl, [None] * nl
    for l in reversed(range(nl)):
        land_f[l], after = _exchange_wait(f"a2a_wait_f{l}", h_ffn[l], after, True)
        if l > 0:
            land_m[l], after = _exchange_wait(f"a2a_wait_m{l}", h_mix[l], after, True)
    adamw_sharded("w_ffn_in", jnp.concatenate([land_f[l][0] for l in range(nl)], axis=1), w_ffn_in.shape[2], 768)
    adamw_sharded("w_ffn_out", jnp.concatenate([land_f[l][1] for l in range(nl)], axis=1), D, D)
    both = out["w_ffn_in"][1][0, 0, :1] + out["w_ffn_out"][1][0, 0, :1]
    land_m[0], _ = _exchange_wait("a2a_wait_m0", h_mix[0], both, True)
    adamw_sharded("w_in", jnp.concatenate([land_m[l][0] for l in range(nl)], axis=1), w_in.shape[2], 640)
    adamw_sharded("w_out", jnp.concatenate([land_m[l][1] for l in range(nl)], axis=1), D, D)

    return (loss, dxi[None], *[out[n][0] for n in names], *[out[n][1] for n in names],
            *[out[n][2] for n in names], *[out[n][3] for n in names])
```

```python
import functools
import math

import jax
import jax.numpy as jnp
from jax import lax
from jax.experimental import pallas as pl
from jax.experimental.pallas import tpu as pltpu

f32 = jnp.float32
bf16 = jnp.bfloat16
HI = lax.Precision.HIGH
AXES = ("x", "y", "c")
NDEV = 8
SDS = jax.ShapeDtypeStruct

D = 1024
NORM_EPS = 1e-6
DN_W, DN_H, DN_D, DN_C = 512, 4, 128, 64
GM_W, GM_G, GM_C = 256, 4, 128
SW_W, SW_H, SW_D, SW_B = 256, 4, 64, 128
SW_DIL = (1, 4, 16)
SW_SPAN = 128
ROPE_DIM, ROPE_THETA = 16, 500000.0
IN_W = 4872
IN_WA = 4992
FFN = 2816
ADAM_LR, ADAM_B1, ADAM_B2, ADAM_EPS, ADAM_WD, ADAM_STEP = 0.001, 0.9, 0.999, 1e-08, 0.01, 10

VMEM_LIMIT = 52 * 1024 * 1024


def _cp(sem=None):
    return pltpu.CompilerParams(vmem_limit_bytes=VMEM_LIMIT, dimension_semantics=sem)


_DIMS = {"nn": (((1,), (0,)), ((), ())), "nt": (((1,), (1,)), ((), ())), "tn": (((0,), (0,)), ((), ()))}


def _raw_dot(a, b, mode, hi):
    if hi:
        return lax.dot_general(a, b, _DIMS[mode], precision=HI, preferred_element_type=f32)
    return lax.dot_general(a.astype(bf16), b.astype(bf16), _DIMS[mode], preferred_element_type=f32)


@functools.partial(jax.custom_vjp, nondiff_argnums=(2, 3))
def _dot(a, b, mode, hi):
    return _raw_dot(a, b, mode, hi)


def _dot_fwd(a, b, mode, hi):
    return _raw_dot(a, b, mode, hi), (a, b)


def _dot_bwd(mode, hi, res, g):
    a, b = res
    if mode == "nn":
        return _raw_dot(g, b, "nt", hi), _raw_dot(a, g, "tn", hi)
    if mode == "nt":
        return _raw_dot(g, b, "nn", hi), _raw_dot(g, a, "tn", hi)
    return _raw_dot(b, g, "nt", hi), _raw_dot(a, g, "nn", hi)


_dot.defvjp(_dot_fwd, _dot_bwd)


def _iota(shape, dim):
    return lax.broadcasted_iota(jnp.int32, shape, dim)


def _f_modnorm(x, w, scale, shift):
    y = x * lax.rsqrt(jnp.mean(x * x, axis=-1, keepdims=True) + NORM_EPS) * w
    return (y * (1.0 + scale) + shift,)


def _f_resid(x, m, gate):
    return (x + gate * m,)


def _f_swiglu(gu):
    return (jax.nn.silu(gu[:, :FFN]) * gu[:, FFN:],)


def _softplus(x):
    return jnp.maximum(x, 0.0) + jnp.log1p(jnp.exp(-jnp.abs(x)))


def _f_dn_act(y, ab, alog, dtb):
    c = jax.nn.silu(y)
    parts = []
    for j in range(3 * DN_H):
        p = c[:, j * DN_D:(j + 1) * DN_D]
        if j < 2 * DN_H:
            p = p * lax.rsqrt(jnp.sum(p * p, axis=-1, keepdims=True) + NORM_EPS)
        parts.append(p)
    lane = _iota(ab.shape, 1)
    g = -jnp.exp(alog) * _softplus(ab + dtb)
    beta = jax.nn.sigmoid(ab)
    gb = jnp.where(lane < DN_H, g, jnp.where(lane < 2 * DN_H, beta, 0.0))
    return jnp.concatenate(parts, axis=1), gb


def _f_dn_out(o, z, wn):
    parts = []
    for h in range(DN_H):
        oh = o[:, h * DN_D:(h + 1) * DN_D]
        zh = z[:, h * DN_D:(h + 1) * DN_D]
        n = oh * lax.rsqrt(jnp.mean(oh * oh, axis=-1, keepdims=True) + NORM_EPS) * wn
        parts.append(n * jax.nn.silu(zh))
    return (jnp.concatenate(parts, axis=1),)


def _gelu(x):
    return 0.5 * x * (1.0 + lax.erf(x * (1.0 / math.sqrt(2.0))))


def _f_gm(u_raw, v_raw, ln_g, ln_b, w_s, b_st):
    u = _gelu(u_raw)
    v = _gelu(v_raw)
    mu = jnp.mean(v, axis=-1, keepdims=True)
    vc = v - mu
    var = jnp.mean(vc * vc, axis=-1, keepdims=True)
    v = vc * lax.rsqrt(var + NORM_EPS) * ln_g + ln_b
    r = _iota((GM_C, GM_C), 0)
    c = _iota((GM_C, GM_C), 1)
    grp = _iota((GM_C, GM_W), 1) // (GM_W // GM_G)
    expand = jnp.where(_iota((GM_C, GM_W), 0) == grp, 1.0, 0.0)
    sv = _dot(b_st, expand, "nn", True)
    for g in range(GM_G):
        wg = jnp.where(r >= c, w_s[g], 0.0)
        sv = sv + jnp.where(grp == g, _dot(wg, v, "nn", True), 0.0)
    return (u * sv,)


def _head_lanes(shape):
    return _iota(shape, 1) // SW_D


def _f_sw_pre(q0, k0, q1, k1, q2, k2, cs, sn, wq, wk):
    r = _iota((SW_W, SW_W), 0)
    c = _iota((SW_W, SW_W), 1)
    same_head = jnp.where(r // SW_D == c // SW_D, 1.0, 0.0)
    hc = c % SW_D
    half = ROPE_DIM // 2
    perm = jnp.where((hc < half) & (r == c + half), -1.0, jnp.where((hc >= half) & (hc < ROPE_DIM) & (r == c - half), 1.0, 0.0))
    tile = jnp.where((_iota((128, SW_W), 1) % SW_D == _iota((128, SW_W), 0)) & (_iota((128, SW_W), 0) < SW_D), 1.0, 0.0)
    wq_full = _dot(wq, tile, "nn", True)[0:1, :]
    wk_full = _dot(wk, tile, "nn", True)[0:1, :]

    def one(t, w):
        ms = _dot(t * t, same_head, "nn", False) * (1.0 / SW_D)
        n = t * lax.rsqrt(ms + NORM_EPS) * w
        return n * cs + _dot(n, perm, "nn", False) * sn

    return one(q0, wq_full), one(k0, wk_full), one(q1, wq_full), one(k1, wk_full), one(q2, wq_full), one(k2, wk_full)


def _f_sw_merge(o0, l0, o1, l1, o2, l2):
    m = jnp.maximum(jnp.maximum(l0, l1), l2)
    e0, e1, e2 = jnp.exp(l0 - m), jnp.exp(l1 - m), jnp.exp(l2 - m)
    return ((e0 * o0 + e1 * o1 + e2 * o2) / (e0 + e1 + e2),)


def _f_attn(q, kp, kc, vp, vc, has_prev):
    kk = jnp.concatenate([kp, kc], axis=0)
    vv = jnp.concatenate([vp, vc], axis=0)
    i = _iota((SW_B, 2 * SW_B), 0)
    j = _iota((SW_B, 2 * SW_B), 1)
    dist = i + SW_B - j
    valid = (dist >= 0) & (dist <= SW_SPAN) & ((j >= SW_B) | has_prev)
    hl = _head_lanes(q.shape)
    heads = range(SW_H)
    ss = [_dot(jnp.where(hl == h, q, 0.0), kk, "nt", False) * (SW_D ** -0.5) for h in heads]
    ss = [jnp.where(valid, s, -1e30) for s in ss]
    ms = [jnp.max(s, axis=-1, keepdims=True) for s in ss]
    ps = [jnp.where(valid, jnp.exp(s - m), 0.0) for s, m in zip(ss, ms)]
    ls = [jnp.sum(p, axis=-1, keepdims=True) for p in ps]
    ohs = [_dot(p, vv, "nn", False) * (1.0 / l) for p, l in zip(ps, ls)]
    o = jnp.zeros(q.shape, f32)
    lse = jnp.zeros(q.shape, f32)
    for h in heads:
        o = o + jnp.where(hl == h, ohs[h], 0.0)
        lse = lse + jnp.where(hl == h, ms[h] + jnp.log(ls[h]), 0.0)
    return o, lse


def _lane_col(tile, lane_idx):
    return jnp.sum(jnp.where(_iota(tile.shape, 1) == lane_idx, tile, 0.0), axis=1, keepdims=True)


def _chunk_decays(gbv):
    n = gbv.shape[0]
    r = _iota((n, n), 0)
    c = _iota((n, n), 1)
    gc_all = _dot(jnp.where(r >= c, 1.0, 0.0), gbv, "nn", True)
    gc_rows = gc_all.T
    out = []
    for h in range(DN_H):
        gcol = _lane_col(gc_all, h)
        diff = jnp.where(r >= c, gcol - gc_rows[h:h + 1, :], 0.0)
        out.append((gcol, jnp.where(r >= c, jnp.exp(diff), 0.0)))
    return out, r, c


def _f_dn_pre(ks, vs, gbvs):
    decs, betas = [], []
    for gbv in gbvs:
        d, r, c = _chunk_decays(gbv)
        decs += d
        betas += [_lane_col(gbv, DN_H + h) for h in range(DN_H)]
    kbs = [k * b for k, b in zip(ks, betas)]
    grams = [_dot(kb, k, "nt", True) for kb, k in zip(kbs, ks)]
    mats = [jnp.where(r > c, g * dec[1], 0.0) for g, dec in zip(grams, decs)]
    rhss = [jnp.concatenate([v * b, kb * jnp.exp(dec[0])], axis=1) for v, b, kb, dec in zip(vs, betas, kbs, decs)]
    return mats, rhss


def _inv_unit_lower(mats):
    n = mats[0].shape[0]
    eye = jnp.where(_iota((n, n), 0) == _iota((n, n), 1), 1.0, 0.0)
    invs = [eye - a for a in mats]
    pws = list(mats)
    for _ in range(n.bit_length() - 2):
        pws = [_raw_dot(p, p, "nn", True) for p in pws]
        invs = [_raw_dot(i, eye + p, "nn", True) for i, p in zip(invs, pws)]
    return invs


def _f_dn_scan(states, qs, ks, us, ws, gbv):
    decs, _, _ = _chunk_decays(gbv)
    n = gbv.shape[0]
    last = _iota((n, 1), 0) == n - 1
    g_last = [jnp.sum(jnp.where(last, gc, 0.0), axis=0, keepdims=True) for gc, _ in decs]
    qs = [q * (DN_D ** -0.5) for q in qs]
    a_qk = [_dot(q, k, "nt", True) * dec[1] for q, k, dec in zip(qs, ks, decs)]
    q_dec = [q * jnp.exp(dec[0]) for q, dec in zip(qs, decs)]
    k_dec = [k * jnp.exp(gl - dec[0]) for k, gl, dec in zip(ks, g_last, decs)]
    ws_ = [_dot(w, s, "nn", True) for w, s in zip(ws, states)]
    o_st = [_dot(qd, s, "nn", True) for qd, s in zip(q_dec, states)]
    v_new = [u - x for u, x in zip(us, ws_)]
    o_in = [_dot(a, vn, "nn", True) for a, vn in zip(a_qk, v_new)]
    upd = [_dot(kd, vn, "tn", True) for kd, vn in zip(k_dec, v_new)]
    outs = [x + y for x, y in zip(o_st, o_in)]
    new_states = [s * jnp.exp(gl) + x for s, gl, x in zip(states, g_last, upd)]
    return outs, new_states


def _cspec(tt, w, cb):
    return pl.BlockSpec((tt, w), lambda i, cb=cb: (i, cb))


def _pspec(shape):
    nd = len(shape)
    return pl.BlockSpec(tuple(shape), lambda i, nd=nd: (0,) * nd)


def _ew_fwd(name, f, tiled, params, outs, tt):
    t = tiled[0][0].shape[0]
    nt, npar = len(tiled), len(params)

    def body(*refs):
        tv = [r[...].astype(f32) for r in refs[:nt]]
        pv = [r[...] for r in refs[nt:nt + npar]]
        res = f(*tv, *pv)
        for o, r in zip(refs[nt + npar:], res):
            o[...] = r.astype(o.dtype)

    res = pl.pallas_call(
        body, grid=(t // tt,), name=name,
        in_specs=[_cspec(tt, w, cb) for _, w, cb in tiled] + [_pspec(p.shape) for p in params],
        out_specs=[_cspec(tt, w, 0) for w, _ in outs],
        out_shape=[SDS((t, w), dt) for w, dt in outs],
        compiler_params=_cp(("arbitrary",)),
    )(*[a for a, _, _ in tiled], *params)
    return res


def _ew_bwd(name, f, tiled, params, cots, diff, tt, adds=None):
    t = tiled[0][0].shape[0]
    cots = [c if isinstance(c, list) else [c] for c in cots]
    pieces = [pc for c in cots for pc in c]
    nt, npar, nc, nd = len(tiled), len(params), len(pieces), len(diff)
    adds = adds or [None] * nd
    add_arrs = [a for a in adds if a is not None]
    na = len(add_arrs)
    dwidth = [tiled[k][1] for k, _ in diff]

    def body(*refs):
        tin = refs[:nt]
        pin = refs[nt:nt + npar]
        cin = refs[nt + npar:nt + npar + nc]
        ain = list(refs[nt + npar + nc:nt + npar + nc + na])
        dts = refs[nt + npar + nc + na:nt + npar + nc + na + nd]
        dps = refs[nt + npar + nc + na + nd:]
        tv = [r[...].astype(f32) for r in tin]
        pv = [r[...] for r in pin]

        def g(*dv):
            full = list(tv)
            for n_, (k, _) in enumerate(diff):
                full[k] = dv[n_]
            return tuple(f(*full, *dv[nd:]))

        _, vjp = jax.vjp(g, *[tv[k] for k, _ in diff], *pv)
        cin = list(cin)
        cvals = [jnp.concatenate([cin.pop(0)[...].astype(f32) for _ in c], axis=1) if len(c) > 1
                 else cin.pop(0)[...].astype(f32) for c in cots]
        grads = vjp(tuple(cvals))
        for n_ in range(nd):
            val = grads[n_]
            if adds[n_] is not None:
                val = val + ain.pop(0)[...].astype(f32)
            dts[n_][...] = val.astype(dts[n_].dtype)

        @pl.when(pl.program_id(0) == 0)
        def _():
            for r in dps:
                r[...] = jnp.zeros(r.shape, f32)

        for r, gp in zip(dps, grads[nd:]):
            r[...] += gp

    res = pl.pallas_call(
        body, grid=(t // tt,), name=name,
        in_specs=[_cspec(tt, w, cb) for _, w, cb in tiled] + [_pspec(p.shape) for p in params]
        + [_cspec(tt, w, cb) for _, w, cb in pieces] + [_cspec(tt, a.shape[1], 0) for a in add_arrs],
        out_specs=[_cspec(tt, w, 0) for w in dwidth] + [_pspec(p.shape) for p in params],
        out_shape=[SDS((t, w), dt) for w, (_, dt) in zip(dwidth, diff)] + [SDS(p.shape, f32) for p in params],
        compiler_params=_cp(("arbitrary",)),
    )(*[a for a, _, _ in tiled], *params, *[a for a, _, _ in pieces], *add_arrs)
    return res[:nd], res[nd:]


def _mm(name, a, b, mode, tm, tn, out_dtype=f32, b_outer=False):
    ij = (lambda g0, g1: (g1, g0)) if b_outer else (lambda g0, g1: (g0, g1))
    if mode == "nn":
        (m, k), (k2, n) = a.shape, b.shape
        a_spec = pl.BlockSpec((tm, k), lambda g0, g1: (ij(g0, g1)[0], 0))
        b_spec = pl.BlockSpec((k, tn), lambda g0, g1: (0, ij(g0, g1)[1]))
    elif mode == "nt":
        (m, k), (n, k2) = a.shape, b.shape
        a_spec = pl.BlockSpec((tm, k), lambda g0, g1: (ij(g0, g1)[0], 0))
        b_spec = pl.BlockSpec((tn, k), lambda g0, g1: (ij(g0, g1)[1], 0))
    else:
        (k, m), (k2, n) = a.shape, b.shape
        a_spec = pl.BlockSpec((k, tm), lambda g0, g1: (0, ij(g0, g1)[0]))
        b_spec = pl.BlockSpec((k, tn), lambda g0, g1: (0, ij(g0, g1)[1]))
    assert k == k2 and m % tm == 0 and n % tn == 0, (name, a.shape, b.shape, mode)
    assert a.dtype == bf16 and b.dtype == bf16, name

    def body(a_ref, b_ref, o_ref):
        o_ref[...] = lax.dot_general(a_ref[...], b_ref[...], _DIMS[mode], preferred_element_type=f32).astype(o_ref.dtype)

    return pl.pallas_call(
        body, grid=(n // tn, m // tm) if b_outer else (m // tm, n // tn), name=name,
        in_specs=[a_spec, b_spec], out_specs=pl.BlockSpec((tm, tn), lambda g0, g1: ij(g0, g1)),
        out_shape=SDS((m, n), out_dtype), compiler_params=_cp(("parallel", "parallel")),
    )(a, b)


CONV_K = 4
CONV_W = 3 * DN_W
HALO = 8


def _conv_fwd(proj, w, tt):
    t = proj.shape[0]
    nb8 = tt // HALO

    def body(x_ref, h_ref, w_ref, y_ref, xe):
        i = pl.program_id(0)
        xe[0:HALO, :] = jnp.where(i == 0, 0.0, h_ref[...])
        xe[HALO:, :] = x_ref[...]
        wv = w_ref[...]
        acc = jnp.zeros((tt, CONV_W), f32)
        for k in range(CONV_K):
            acc = acc + wv[k:k + 1, :] * xe[pl.ds(HALO - (CONV_K - 1) + k, tt), :]
        y_ref[...] = acc

    return pl.pallas_call(
        body, grid=(t // tt,), name="conv_fwd",
        in_specs=[pl.BlockSpec((tt, CONV_W), lambda i: (i, 0)),
                  pl.BlockSpec((HALO, CONV_W), lambda i: (jnp.maximum(i * nb8 - 1, 0), 0)),
                  _pspec(w.shape)],
        out_specs=pl.BlockSpec((tt, CONV_W), lambda i: (i, 0)),
        out_shape=SDS((t, CONV_W), f32),
        scratch_shapes=[pltpu.VMEM((tt + HALO, CONV_W), f32)],
        compiler_params=_cp(("arbitrary",)),
    )(proj, proj, w)


def _conv_bwd(proj, dy, w, tt):
    t = proj.shape[0]
    nb8 = tt // HALO
    last8 = t // HALO - 1
    nsteps = t // tt

    def body(x_ref, h_ref, dy_ref, n_ref, w_ref, dx_ref, dw_ref, xe, dye):
        i = pl.program_id(0)
        xe[0:HALO, :] = jnp.where(i == 0, 0.0, h_ref[...])
        xe[HALO:, :] = x_ref[...]
        dye[0:tt, :] = dy_ref[...]
        dye[tt:, :] = jnp.where(i == nsteps - 1, 0.0, n_ref[...])
        wv = w_ref[...]
        dyv = dy_ref[...]
        acc = jnp.zeros((tt, CONV_W), f32)

        @pl.when(i == 0)
        def _():
            dw_ref[...] = jnp.zeros(dw_ref.shape, f32)

        for k in range(CONV_K):
            acc = acc + wv[k:k + 1, :] * dye[pl.ds(CONV_K - 1 - k, tt), :]
            dw_ref[k:k + 1, :] += jnp.sum(dyv * xe[pl.ds(HALO - (CONV_K - 1) + k, tt), :], axis=0, keepdims=True)
        dx_ref[...] = acc.astype(dx_ref.dtype)

    return pl.pallas_call(
        body, grid=(nsteps,), name="conv_bwd",
        in_specs=[pl.BlockSpec((tt, CONV_W), lambda i: (i, 0)),
                  pl.BlockSpec((HALO, CONV_W), lambda i: (jnp.maximum(i * nb8 - 1, 0), 0)),
                  pl.BlockSpec((tt, CONV_W), lambda i: (i, 0)),
                  pl.BlockSpec((HALO, CONV_W), lambda i: (jnp.minimum((i + 1) * nb8, last8), 0)),
                  _pspec(w.shape)],
        out_specs=[pl.BlockSpec((tt, CONV_W), lambda i: (i, 0)), _pspec(w.shape)],
        out_shape=[SDS((t, CONV_W), bf16), SDS(w.shape, f32)],
        scratch_shapes=[pltpu.VMEM((tt + HALO, CONV_W), f32), pltpu.VMEM((tt + HALO, CONV_W), f32)],
        compiler_params=_cp(("arbitrary",)),
    )(proj, proj, dy, dy, w)


PREP_CHUNKS = 4


def _head_cols(part, h):
    return slice(part * DN_W + h * DN_D, part * DN_W + (h + 1) * DN_D)


def _prep_operands(qkv_ref, gb_ref):
    inst = [(slice(ch * DN_C, (ch + 1) * DN_C), h) for ch in range(PREP_CHUNKS) for h in range(DN_H)]
    ks = [qkv_ref[rs, _head_cols(1, h)] for rs, h in inst]
    vs = [qkv_ref[rs, _head_cols(2, h)] for rs, h in inst]
    gbvs = [gb_ref[ch * DN_C:(ch + 1) * DN_C, :] for ch in range(PREP_CHUNKS)]
    return inst, ks, vs, gbvs


def _dn_prep_fwd(qkv, gb):
    t = qkv.shape[0]
    rows = PREP_CHUNKS * DN_C

    def body(qkv_ref, gb_ref, u_ref, w_ref, inv_ref):
        inv_ref[...] = jnp.zeros(inv_ref.shape, f32)
        inst, ks, vs, gbvs = _prep_operands(qkv_ref, gb_ref)
        mats, rhss = _f_dn_pre(ks, vs, gbvs)
        invs = _inv_unit_lower(mats)
        uws = [_raw_dot(inv, rhs, "nn", True) for inv, rhs in zip(invs, rhss)]
        for (rs, h), inv, uw in zip(inst, invs, uws):
            u_ref[rs, _head_cols(0, h)] = uw[:, :DN_D]
            w_ref[rs, _head_cols(0, h)] = uw[:, DN_D:]
            inv_ref[rs, h * DN_D:h * DN_D + DN_C] = inv

    return pl.pallas_call(
        body, grid=(t // rows,), name="dn_prep_fwd",
        in_specs=[pl.BlockSpec((rows, CONV_W), lambda i: (i, 0)), pl.BlockSpec((rows, 128), lambda i: (i, 0))],
        out_specs=[pl.BlockSpec((rows, DN_W), lambda i: (i, 0))] * 3,
        out_shape=[SDS((t, DN_W), f32)] * 3,
        compiler_params=_cp(("arbitrary",)),
    )(qkv, gb)


def _dn_prep_bwd(qkv, gb, inv_all, du, dw, dqk1, dgb1):
    t = qkv.shape[0]
    rows = PREP_CHUNKS * DN_C

    def body(qkv_ref, gb_ref, inv_ref, du_ref, dw_ref, dqk1_ref, dgb1_ref, dqkv_ref, dgb_ref):
        inst, ks, vs, gbvs = _prep_operands(qkv_ref, gb_ref)
        (_, rhss), vjp = jax.vjp(_f_dn_pre, ks, vs, gbvs)
        invs = [inv_ref[rs, h * DN_D:h * DN_D + DN_C] for rs, h in inst]
        dxs = [jnp.concatenate([du_ref[rs, _head_cols(0, h)], dw_ref[rs, _head_cols(0, h)]], axis=1) for rs, h in inst]
        uws = [_raw_dot(inv, rhs, "nn", True) for inv, rhs in zip(invs, rhss)]
        drhss = [_raw_dot(inv, dx, "tn", True) for inv, dx in zip(invs, dxs)]
        das = [-_raw_dot(dr, uw, "nt", True) for dr, uw in zip(drhss, uws)]
        dks, dvs, dgbvs = vjp((das, drhss))
        for (rs, h), dk, dv in zip(inst, dks, dvs):
            dqkv_ref[rs, _head_cols(0, h)] = dqk1_ref[rs, _head_cols(0, h)]
            dqkv_ref[rs, _head_cols(1, h)] = dk + dqk1_ref[rs, _head_cols(1, h)]
            dqkv_ref[rs, _head_cols(2, h)] = dv
        for ch, dgbv in enumerate(dgbvs):
            rs = slice(ch * DN_C, (ch + 1) * DN_C)
            dgb_ref[rs, :] = dgbv + dgb1_ref[rs, :]

    return pl.pallas_call(
        body, grid=(t // rows,), name="dn_prep_bwd",
        in_specs=[pl.BlockSpec((rows, CONV_W), lambda i: (i, 0)), pl.BlockSpec((rows, 128), lambda i: (i, 0)),
                  pl.BlockSpec((rows, DN_W), lambda i: (i, 0)),
                  pl.BlockSpec((rows, DN_W), lambda i: (i, 0)), pl.BlockSpec((rows, DN_W), lambda i: (i, 0)),
                  pl.BlockSpec((rows, 2 * DN_W), lambda i: (i, 0)), pl.BlockSpec((rows, 128), lambda i: (i, 0))],
        out_specs=[pl.BlockSpec((rows, CONV_W), lambda i: (i, 0)), pl.BlockSpec((rows, 128), lambda i: (i, 0))],
        out_shape=[SDS((t, CONV_W), f32), SDS((t, 128), f32)],
        compiler_params=_cp(("arbitrary",)),
    )(qkv, gb, inv_all, du, dw, dqk1, dgb1)


def _dn_scan_fwd(qkv, gb, u, w):
    t = qkv.shape[0]
    n = t // DN_C
    heads = range(DN_H)

    def body(qkv_ref, gb_ref, u_ref, w_ref, o_ref, s_ref, state):
        @pl.when(pl.program_id(0) == 0)
        def _():
            state[...] = jnp.zeros(state.shape, f32)

        states = [state[h] for h in heads]
        for h in heads:
            s_ref[0, h] = states[h]
        outs, new = _f_dn_scan(states, [qkv_ref[:, _head_cols(0, h)] for h in heads],
                               [qkv_ref[:, _head_cols(1, h)] for h in heads],
                               [u_ref[:, _head_cols(0, h)] for h in heads],
                               [w_ref[:, _head_cols(0, h)] for h in heads], gb_ref[...])
        for h in heads:
            o_ref[:, _head_cols(0, h)] = outs[h]
            state[h] = new[h]

    return pl.pallas_call(
        body, grid=(n,), name="dn_scan_fwd",
        in_specs=[pl.BlockSpec((DN_C, 2 * DN_W), lambda i: (i, 0)), pl.BlockSpec((DN_C, 128), lambda i: (i, 0)),
                  pl.BlockSpec((DN_C, DN_W), lambda i: (i, 0)), pl.BlockSpec((DN_C, DN_W), lambda i: (i, 0))],
        out_specs=[pl.BlockSpec((DN_C, DN_W), lambda i: (i, 0)),
                   pl.BlockSpec((1, DN_H, DN_D, DN_D), lambda i: (i, 0, 0, 0))],
        out_shape=[SDS((t, DN_W), f32), SDS((n, DN_H, DN_D, DN_D), f32)],
        scratch_shapes=[pltpu.VMEM((DN_H, DN_D, DN_D), f32)],
        compiler_params=_cp(("arbitrary",)),
    )(qkv, gb, u, w)


def _dn_scan_bwd(qkv, gb, u, w, states, do):
    t = qkv.shape[0]
    n = t // DN_C
    rev = lambda i: (n - 1 - i, 0)
    heads = range(DN_H)

    def body(qkv_ref, gb_ref, u_ref, w_ref, s_ref, do_ref, dqk_ref, du_ref, dw_ref, dgb_ref, dstate):
        @pl.when(pl.program_id(0) == 0)
        def _():
            dstate[...] = jnp.zeros(dstate.shape, f32)

        _, vjp = jax.vjp(_f_dn_scan, [s_ref[0, h] for h in heads], [qkv_ref[:, _head_cols(0, h)] for h in heads],
                         [qkv_ref[:, _head_cols(1, h)] for h in heads], [u_ref[:, _head_cols(0, h)] for h in heads],
                         [w_ref[:, _head_cols(0, h)] for h in heads], gb_ref[...])
        ds, dq, dk, du, dw, dgbv = vjp(([do_ref[:, _head_cols(0, h)] for h in heads], [dstate[h] for h in heads]))
        for h in heads:
            dstate[h] = ds[h]
            dqk_ref[:, _head_cols(0, h)] = dq[h]
            dqk_ref[:, _head_cols(1, h)] = dk[h]
            du_ref[:, _head_cols(0, h)] = du[h]
            dw_ref[:, _head_cols(0, h)] = dw[h]
        dgb_ref[...] = dgbv

    return pl.pallas_call(
        body, grid=(n,), name="dn_scan_bwd",
        in_specs=[pl.BlockSpec((DN_C, 2 * DN_W), rev), pl.BlockSpec((DN_C, 128), rev),
                  pl.BlockSpec((DN_C, DN_W), rev), pl.BlockSpec((DN_C, DN_W), rev),
                  pl.BlockSpec((1, DN_H, DN_D, DN_D), lambda i: (n - 1 - i, 0, 0, 0)),
                  pl.BlockSpec((DN_C, DN_W), rev)],
        out_specs=[pl.BlockSpec((DN_C, 2 * DN_W), rev), pl.BlockSpec((DN_C, DN_W), rev),
                   pl.BlockSpec((DN_C, DN_W), rev), pl.BlockSpec((DN_C, 128), rev)],
        out_shape=[SDS((t, 2 * DN_W), f32), SDS((t, DN_W), f32), SDS((t, DN_W), f32), SDS((t, 128), f32)],
        scratch_shapes=[pltpu.VMEM((DN_H, DN_D, DN_D), f32)],
        compiler_params=_cp(("arbitrary",)),
    )(qkv, gb, u, w, states, do)


HALF = SW_W // 2


def _attn_specs(rows, cb, lag, nb):
    def one(c):
        if lag:
            return pl.BlockSpec((rows, HALF), lambda n, r: (jnp.maximum(jnp.minimum(n, nb - 1) - 1, 0), c))
        return pl.BlockSpec((rows, HALF), lambda n, r: (jnp.minimum(n, nb - 1), c))
    return [one(2 * cb), one(2 * cb + 1)]


def _sw_attn_fwd(q, k, vsrc, vcb, d):
    t = q.shape[0]
    rows = SW_B * d
    nb = t // rows
    cur = lambda cb: _attn_specs(rows, cb, False, nb)
    prev = lambda cb: _attn_specs(rows, cb, True, nb)

    def body(ql, qh, kpl, kph, kcl, kch, vpl, vph, vcl, vch, ol, oh, ll, lh):
        rs = pl.ds(pl.program_id(1), SW_B, stride=d) if d > 1 else pl.ds(0, SW_B)
        ld = lambda lo, hi: jnp.concatenate([lo[rs, :], hi[rs, :]], axis=1)
        o, lse = _f_attn(ld(ql, qh), ld(kpl, kph), ld(kcl, kch), ld(vpl, vph), ld(vcl, vch), pl.program_id(0) > 0)
        ol[rs, :] = o[:, :HALF]
        oh[rs, :] = o[:, HALF:]
        ll[rs, :] = lse[:, :HALF]
        lh[rs, :] = lse[:, HALF:]

    out = pl.BlockSpec((rows, HALF), lambda n, r: (n, 0))
    return pl.pallas_call(
        body, grid=(nb, d), name=f"sw_attn_fwd_d{d}",
        in_specs=cur(0) + prev(0) + cur(0) + prev(vcb) + cur(vcb), out_specs=[out] * 4,
        out_shape=[SDS((t, HALF), f32)] * 4, compiler_params=_cp(("arbitrary", "arbitrary")),
    )(q, q, k, k, k, k, vsrc, vsrc, vsrc, vsrc)


def _sw_attn_bwd(q, k, vsrc, vcb, do_l, do_h, dl_l, dl_h, d):
    t = q.shape[0]
    rows = SW_B * d
    nb = t // rows
    cur = lambda cb: _attn_specs(rows, cb, False, nb)
    prev = lambda cb: _attn_specs(rows, cb, True, nb)
    lag = pl.BlockSpec((rows, HALF), lambda n, r: (jnp.maximum(n - 1, 0), 0))
    here = pl.BlockSpec((rows, HALF), lambda n, r: (jnp.minimum(n, nb - 1), 0))

    def body(ql, qh, kpl, kph, kcl, kch, vpl, vph, vcl, vch, dol, doh, dll, dlh,
             dql, dqh, dkl, dkh, dvl, dvh, dk_hold, dv_hold):
        n = pl.program_id(0)
        r = pl.program_id(1)
        rs = pl.ds(r, SW_B, stride=d) if d > 1 else pl.ds(0, SW_B)
        hs = pl.ds(pl.multiple_of(r * SW_B, SW_B), SW_B)
        ld = lambda lo, hi: jnp.concatenate([lo[rs, :], hi[rs, :]], axis=1)

        def put(lo, hi, val):
            lo[rs, :] = val[:, :HALF]
            hi[rs, :] = val[:, HALF:]

        @pl.when(n < nb)
        def _():
            has_prev = n > 0
            _, vjp = jax.vjp(lambda q_, kp, kc, vp, vc: _f_attn(q_, kp, kc, vp, vc, has_prev),
                             ld(ql, qh), ld(kpl, kph), ld(kcl, kch), ld(vpl, vph), ld(vcl, vch))
            dq, dkp, dkc, dvp, dvc = vjp((ld(dol, doh), ld(dll, dlh)))
            put(dql, dqh, dq)
            put(dkl, dkh, dk_hold[hs, :] + dkp)
            put(dvl, dvh, dv_hold[hs, :] + dvp)
            dk_hold[hs, :] = dkc
            dv_hold[hs, :] = dvc

        @pl.when(n == nb)
        def _():
            put(dkl, dkh, dk_hold[hs, :])
            put(dvl, dvh, dv_hold[hs, :])

    return pl.pallas_call(
        body, grid=(nb + 1, d), name=f"sw_attn_bwd_d{d}",
        in_specs=cur(0) + prev(0) + cur(0) + prev(vcb) + cur(vcb) + [here] * 4,
        out_specs=[here, here, lag, lag, lag, lag],
        out_shape=[SDS((t, HALF), f32)] * 6,
        scratch_shapes=[pltpu.VMEM((rows, SW_W), f32)] * 2,
        compiler_params=_cp(("arbitrary", "arbitrary")),
    )(q, q, k, k, k, k, vsrc, vsrc, vsrc, vsrc, do_l, do_h, dl_l, dl_h)


def _loss_head(y, target, tt):
    t = y.shape[0]

    def body(y_ref, t_ref, dy_ref, l_ref):
        @pl.when(pl.program_id(0) == 0)
        def _():
            l_ref[...] = jnp.zeros(l_ref.shape, f32)

        err = y_ref[...] - t_ref[...]
        dy_ref[...] = err * (1.0 / D)
        l_ref[...] += 0.5 * jnp.sum(jnp.sum(err * err, axis=1, keepdims=True) * (1.0 / D), axis=0, keepdims=True)

    return pl.pallas_call(
        body, grid=(t // tt,), name="loss_head",
        in_specs=[pl.BlockSpec((tt, D), lambda i: (i, 0))] * 2,
        out_specs=[pl.BlockSpec((tt, D), lambda i: (i, 0)), pl.BlockSpec((8, 128), lambda i: (0, 0))],
        out_shape=[SDS((t, D), f32), SDS((8, 128), f32)],
        compiler_params=_cp(("arbitrary",)),
    )(y, target)


def _adamw(name, w, m, v, gparts, tr):
    r, c = w.shape
    p = gparts.shape[0]
    assert r % tr == 0, (name, w.shape, tr)

    def body(w_ref, m_ref, v_ref, g_ref, go_ref, d_ref, mo_ref, vo_ref):
        g = g_ref[0].astype(f32)
        for k in range(1, p):
            g = g + g_ref[k].astype(f32)
        wv = w_ref[...]
        mn = ADAM_B1 * m_ref[...] + (1.0 - ADAM_B1) * g
        vn = ADAM_B2 * v_ref[...] + (1.0 - ADAM_B2) * jnp.square(g)
        m_hat = mn / (1.0 - ADAM_B1 ** ADAM_STEP)
        v_hat = vn / (1.0 - ADAM_B2 ** ADAM_STEP)
        go_ref[...] = g
        d_ref[...] = -ADAM_LR * (m_hat / (jnp.sqrt(v_hat) + ADAM_EPS) + ADAM_WD * wv)
        mo_ref[...] = mn
        vo_ref[...] = vn

    spec = pl.BlockSpec((tr, c), lambda i: (i, 0))
    return pl.pallas_call(
        body, grid=(r // tr,), name=name,
        in_specs=[spec, spec, spec, pl.BlockSpec((p, tr, c), lambda i: (0, i, 0))],
        out_specs=[spec] * 4, out_shape=[SDS((r, c), f32)] * 4,
        compiler_params=_cp(("arbitrary",)),
    )(w, m, v, gparts)


def _mod_cols(c_all, w_mod, b_cols):
    nl, _, wc = w_mod.shape

    def body(c_ref, w_ref, b_ref, o_ref):
        o_ref[0] = _raw_dot(jax.nn.silu(c_ref[...]), w_ref[0], "nn", True) + b_ref[0]

    return pl.pallas_call(
        body, grid=(nl,), name="mod_cols",
        in_specs=[pl.BlockSpec((NDEV, D), lambda l: (0, 0)), pl.BlockSpec((1, D, wc), lambda l: (l, 0, 0)),
                  pl.BlockSpec((1, 1, wc), lambda l: (l, 0, 0))],
        out_specs=pl.BlockSpec((1, NDEV, wc), lambda l: (l, 0, 0)),
        out_shape=SDS((nl, NDEV, wc), f32), compiler_params=_cp(("arbitrary",)),
    )(c_all, w_mod, b_cols)


def _wmod_grad(c_all, dmod_cols):
    nl, _, wc = dmod_cols.shape

    def body(c_ref, d_ref, o_ref):
        o_ref[0, 0] = _raw_dot(jax.nn.silu(c_ref[...]), d_ref[0], "tn", True)

    return pl.pallas_call(
        body, grid=(nl,), name="wmod_grad",
        in_specs=[pl.BlockSpec((NDEV, D), lambda l: (0, 0)), pl.BlockSpec((1, NDEV, wc), lambda l: (l, 0, 0))],
        out_specs=pl.BlockSpec((1, 1, D, wc), lambda l: (0, l, 0, 0)),
        out_shape=SDS((1, nl, D, wc), f32), compiler_params=_cp(("arbitrary",)),
    )(c_all, dmod_cols)


def _me_and_peers():
    x, y, c = (lax.axis_index(a) for a in AXES)
    peers = []
    for k in range(1, NDEV):
        px = 1 - x if (k >> 2) & 1 else x
        py = 1 - y if (k >> 1) & 1 else y
        pc = 1 - c if k & 1 else c
        peers.append(((px, py, pc), 4 * px + 2 * py + pc))
    return 4 * x + 2 * y + c, peers


_ANY = pl.BlockSpec(memory_space=pl.ANY)


def _all_gather(name, arrs):
    n = len(arrs)

    def body(*refs):
        ins, outs, token = refs[:n], refs[n:2 * n], refs[2 * n]
        send_sems, recv_sems, local_sems = refs[2 * n + 1:]
        me, peers = _me_and_peers()
        mine = [pltpu.make_async_copy(ins[a], outs[a].at[me], local_sems.at[a]) for a in range(n)]
        copies = [pltpu.make_async_remote_copy(ins[a], outs[a].at[me], send_sems.at[a * (NDEV - 1) + k],
                                               recv_sems.at[a * (NDEV - 1) + k], device_id=dev,
                                               device_id_type=pl.DeviceIdType.MESH)
                  for a in range(n) for k, (dev, _) in enumerate(peers)]
        for cp in mine + copies:
            cp.start()
        token[...] = jnp.zeros(token.shape, token.dtype)
        for cp in copies + mine:
            cp.wait()

    res = pl.pallas_call(
        body, name=name, in_specs=[_ANY] * n, out_specs=[_ANY] * n + [pl.BlockSpec(memory_space=pltpu.VMEM)],
        out_shape=[SDS((NDEV,) + a.shape, a.dtype) for a in arrs] + [SDS((8, 128), f32)],
        scratch_shapes=[pltpu.SemaphoreType.DMA((n * (NDEV - 1),)), pltpu.SemaphoreType.DMA((n * (NDEV - 1),)),
                        pltpu.SemaphoreType.DMA((n,))],
        compiler_params=pltpu.CompilerParams(has_side_effects=True),
    )(*arrs)
    return list(res[:n]), res[n]


def _all_to_all(name, a):
    def body(a_ref, o_ref, send_sems, recv_sems, local_sem):
        me, peers = _me_and_peers()
        mine = pltpu.make_async_copy(a_ref.at[me], o_ref.at[me], local_sem)
        mine.start()
        copies = [pltpu.make_async_remote_copy(a_ref.at[pid], o_ref.at[me], send_sems.at[k], recv_sems.at[k],
                                               device_id=dev, device_id_type=pl.DeviceIdType.MESH)
                  for k, (dev, pid) in enumerate(peers)]
        for cp in copies:
            cp.start()
        for cp in copies:
            cp.wait()
        mine.wait()

    return pl.pallas_call(
        body, name=name, in_specs=[_ANY], out_specs=_ANY, out_shape=SDS(a.shape, a.dtype),
        scratch_shapes=[pltpu.SemaphoreType.DMA((NDEV - 1,)), pltpu.SemaphoreType.DMA((NDEV - 1,)), pltpu.SemaphoreType.DMA],
        compiler_params=pltpu.CompilerParams(has_side_effects=True),
    )(a)


_HBM = pl.BlockSpec(memory_space=pltpu.HBM)
_SEM = pl.BlockSpec(memory_space=pltpu.SEMAPHORE)
_FLOW = pltpu.SideEffectType.DATAFLOW_SIDE_EFFECTING


def _exchange_copies(srcs, lands, send_sems, recv_sems, scatter):
    me, peers = _me_and_peers()
    copies = []
    for a, (src, land) in enumerate(zip(srcs, lands)):
        for k, (dev, pid) in enumerate(peers):
            copies.append(pltpu.make_async_remote_copy(
                src.at[pid] if scatter else src, land.at[me], send_sems.at[a * (NDEV - 1) + k],
                recv_sems.at[a * (NDEV - 1) + k], device_id=dev, device_id_type=pl.DeviceIdType.MESH))
    return copies


def _exchange_start(name, srcs, lands, scatter):
    n = len(srcs)
    nsem = n * (NDEV - 1)

    def body(*refs):
        for cp in _exchange_copies(refs[:n], refs[n:2 * n], refs[2 * n], refs[2 * n + 1], scatter):
            cp.start()
        token = refs[-1]
        token[...] = jnp.zeros(token.shape, token.dtype)

    hbm = lambda a: pltpu.with_memory_space_constraint(a, pltpu.HBM)
    res = pl.pallas_call(
        body, name=name,
        out_shape=(pltpu.SemaphoreType.DMA((nsem,)), pltpu.SemaphoreType.DMA((nsem,)),
                   *[pltpu.HBM(a.shape, a.dtype) for a in srcs], *[pltpu.HBM(a.shape, a.dtype) for a in lands],
                   SDS((8, 128), f32)),
        in_specs=[_HBM] * (2 * n), out_specs=(_SEM, _SEM, *([_HBM] * (2 * n)), pl.BlockSpec(memory_space=pltpu.VMEM)),
        input_output_aliases={i: 2 + i for i in range(2 * n)},
        compiler_params=pltpu.CompilerParams(has_side_effects=_FLOW),
    )(*[hbm(a) for a in srcs], *[hbm(a) for a in lands])
    return res[0], res[1], list(res[2:2 + n]), list(res[2 + n:2 + 2 * n]), res[-1]


def _exchange_wait(name, handle, after, scatter):
    send_sems, recv_sems, srcs, lands, _ = handle
    n = len(srcs)

    def body(*refs):
        for cp in _exchange_copies(refs[:n], refs[n:2 * n], refs[2 * n], refs[2 * n + 1], scatter):
            cp.wait_send()
            cp.wait_recv()
        token = refs[-1]
        token[...] = jnp.zeros(token.shape, token.dtype)

    res = pl.pallas_call(
        body, name=name,
        out_shape=(*[pltpu.HBM(a.shape, a.dtype) for a in srcs + lands], SDS((8, 128), f32)),
        in_specs=[_HBM] * (2 * n) + [_SEM, _SEM, _ANY],
        out_specs=(*([_HBM] * (2 * n)), pl.BlockSpec(memory_space=pltpu.VMEM)),
        input_output_aliases={i: i for i in range(2 * n)},
        compiler_params=pltpu.CompilerParams(has_side_effects=_FLOW),
    )(*srcs, *lands, send_sems, recv_sems, after)
    return list(res[n:2 * n]), res[-1]


def _own_slot(a, me):
    return lax.dynamic_update_slice(jnp.zeros((NDEV,) + a.shape, a.dtype), a[None], (me,) + (0,) * a.ndim)


CB_Z = 3
CB_GM = 8
CB_SW = 10
CB_AB = 38


def _sw_pre_tiles(proj, cs, sn):
    return [(proj, SW_W, CB_SW + 3 * g + j) for g in range(3) for j in range(2)] + [(cs, SW_W, 0), (sn, SW_W, 0)]


def _layer_fwd(x, p, tabs, late_weights):
    cs, sn = tabs
    sh1, sc1, g1, sh2, sc2, g2 = (p["mod"][k] for k in range(6))
    (h1,) = _ew_fwd("modnorm1_fwd", _f_modnorm, [(x, D, 0)], [p["mixw"], sc1, sh1], [(D, bf16)], 512)
    proj = _mm("in_proj", h1, p["wi"], "nn", 1024, 1664, b_outer=True)
    y = _conv_fwd(proj, p["conv"], 256)
    qkv, gb = _ew_fwd("dn_act_fwd", _f_dn_act, [(y, CONV_W, 0), (proj, 128, CB_AB)], [p["alog"], p["dtb"]],
                      [(CONV_W, f32), (128, f32)], 256)
    u, w, inv = _dn_prep_fwd(qkv, gb)
    o, states = _dn_scan_fwd(qkv, gb, u, w)
    (ya,) = _ew_fwd("dn_out_fwd", _f_dn_out, [(o, DN_W, 0), (proj, DN_W, CB_Z)], [p["wn"]], [(DN_W, bf16)], 256)
    (yb,) = _ew_fwd("gm_fwd", _f_gm, [(proj, GM_W, CB_GM), (proj, GM_W, CB_GM + 1)],
                    [p["lng"], p["lnb"], p["ws"], p["bst"]], [(GM_W, bf16)], GM_C)
    qk = _ew_fwd("sw_pre_fwd", _f_sw_pre, _sw_pre_tiles(proj, cs, sn), [p["wq"], p["wk"]], [(SW_W, f32)] * 6, 512)
    ol = [[], []]
    for g, d in enumerate(SW_DIL):
        o_l, o_h, l_l, l_h = _sw_attn_fwd(qk[2 * g], qk[2 * g + 1], proj, CB_SW + 3 * g + 2, d)
        ol[0] += [o_l, l_l]
        ol[1] += [o_h, l_h]
    yc = [_ew_fwd(f"sw_merge_fwd_{h}", _f_sw_merge, [(a, HALF, 0) for a in ol[h]], [], [(HALF, bf16)], 512)[0]
          for h in range(2)]
    ycat = jnp.concatenate([ya, yb] + yc, axis=1)
    p.update(late_weights(ycat))
    m1 = _mm("out_proj", ycat, p["wo"], "nn", 1024, 1024)
    (x2,) = _ew_fwd("resid1_fwd", _f_resid, [(x, D, 0), (m1, D, 0)], [g1], [(D, f32)], 512)
    (h2,) = _ew_fwd("modnorm2_fwd", _f_modnorm, [(x2, D, 0)], [p["ffnw"], sc2, sh2], [(D, bf16)], 512)
    gu = _mm("ffn_in", h2, p["wfi"], "nn", 1024, 1408, bf16, b_outer=True)
    (act,) = _ew_fwd("swiglu_fwd", _f_swiglu, [(gu, 2 * FFN, 0)], [], [(FFN, bf16)], 256)
    m2 = _mm("ffn_out", act, p["wfo"], "nn", 1024, 1024)
    (x3,) = _ew_fwd("resid2_fwd", _f_resid, [(x2, D, 0), (m2, D, 0)], [g2], [(D, f32)], 512)
    res = dict(x=x, h1=h1, proj=proj, y=y, qkv=qkv, gb=gb, u=u, w=w, inv=inv, states=states, o=o, qk=list(qk), ol=ol,
               ycat=ycat, m1=m1, x2=x2, h2=h2, gu=gu, act=act, m2=m2)
    return x3, res


def _layer_bwd(dx3, p, r, tabs, ffn_grads_ready):
    cs, sn = tabs
    sh1, sc1, g1, sh2, sc2, g2 = (p["mod"][k] for k in range(6))
    proj = r["proj"]
    (dx2a, dm2), (dg2,) = _ew_bwd("resid2_bwd", _f_resid, [(r["x2"], D, 0), (r["m2"], D, 0)], [g2], [(dx3, D, 0)],
                                  [(0, f32), (1, bf16)], 512)
    dact = _mm("ffn_out_dx", dm2, p["wfo"], "nt", 512, FFN, bf16)
    dwfo = _mm("ffn_out_dw", r["act"], dm2, "tn", 256, 1024, bf16)
    (dgu_cat,), _ = _ew_bwd("swiglu_bwd", _f_swiglu, [(r["gu"], 2 * FFN, 0)], [], [(dact, FFN, 0)], [(0, bf16)], 256)
    dh2 = _mm("ffn_in_dx", dgu_cat, p["wfi"], "nt", 512, 1024)
    dwfi = _mm("ffn_in_dw", r["h2"], dgu_cat, "tn", 1024, 512, bf16)
    g1 = g1 + ffn_grads_ready(dwfi, dwfo)[0, 0]
    (dx2,), (dffnw, dsc2, dsh2) = _ew_bwd("modnorm2_bwd", _f_modnorm, [(r["x2"], D, 0)], [p["ffnw"], sc2, sh2],
                                          [(dh2, D, 0)], [(0, f32)], 512, adds=[dx2a])
    (dxa, dm1), (dg1,) = _ew_bwd("resid1_bwd", _f_resid, [(r["x"], D, 0), (r["m1"], D, 0)], [g1], [(dx2, D, 0)],
                                 [(0, f32), (1, bf16)], 512)
    dycat = _mm("out_proj_dx", dm1, p["wo"], "nt", 1024, 1024)
    dwo = _mm("out_proj_dw", r["ycat"], dm1, "tn", 1024, 512, bf16)
    dol = [_ew_bwd(f"sw_merge_bwd_{h}", _f_sw_merge, [(a, HALF, 0) for a in r["ol"][h]], [], [(dycat, HALF, 6 + h)],
                   [(k, f32) for k in range(6)], 512)[0] for h in range(2)]
    dqk, dvs = [], []
    for g, d in enumerate(SW_DIL):
        dq_l, dq_h, dk_l, dk_h, dv_l, dv_h = _sw_attn_bwd(
            r["qk"][2 * g], r["qk"][2 * g + 1], proj, CB_SW + 3 * g + 2,
            dol[0][2 * g], dol[1][2 * g], dol[0][2 * g + 1], dol[1][2 * g + 1], d)
        dqk += [[(dq_l, HALF, 0), (dq_h, HALF, 0)], [(dk_l, HALF, 0), (dk_h, HALF, 0)]]
        dvs += [dv_l.astype(bf16), dv_h.astype(bf16)]
    dqk_raw, (dwq, dwk) = _ew_bwd("sw_pre_bwd", _f_sw_pre, _sw_pre_tiles(proj, cs, sn), [p["wq"], p["wk"]],
                                  dqk, [(k, bf16) for k in range(6)], 512)
    (dgm_u, dgm_v), (dlng, dlnb, dws, dbst) = _ew_bwd(
        "gm_bwd", _f_gm, [(proj, GM_W, CB_GM), (proj, GM_W, CB_GM + 1)], [p["lng"], p["lnb"], p["ws"], p["bst"]],
        [(dycat, GM_W, 2)], [(0, bf16), (1, bf16)], GM_C)
    (do, dz), (dwn,) = _ew_bwd("dn_out_bwd", _f_dn_out, [(r["o"], DN_W, 0), (proj, DN_W, CB_Z)], [p["wn"]],
                               [(dycat, DN_W, 0)], [(0, f32), (1, bf16)], 256)
    dqk1, du, dw, dgb1 = _dn_scan_bwd(r["qkv"], r["gb"], r["u"], r["w"], r["states"], do)
    dqkv, dgb = _dn_prep_bwd(r["qkv"], r["gb"], r["inv"], du, dw, dqk1, dgb1)
    (dy, dab), (dalog, ddtb) = _ew_bwd("dn_act_bwd", _f_dn_act, [(r["y"], CONV_W, 0), (proj, 128, CB_AB)],
                                       [p["alog"], p["dtb"]], [(dqkv, CONV_W, 0), (dgb, 128, 0)],
                                       [(0, f32), (1, bf16)], 256)
    dxc, dconv = _conv_bwd(proj, dy, p["conv"], 256)
    dproj = jnp.concatenate([dxc, dz, dgm_u, dgm_v, dqk_raw[0], dqk_raw[1], dvs[0], dvs[1], dqk_raw[2], dqk_raw[3],
                             dvs[2], dvs[3], dqk_raw[4], dqk_raw[5], dvs[4], dvs[5], dab], axis=1)
    dh1 = _mm("in_proj_dx", dproj, p["wi"], "nt", 512, 1024)
    dwi = _mm("in_proj_dw", r["h1"], dproj, "tn", 1024, 384, bf16)
    (dx,), (dmixw, dsc1, dsh1) = _ew_bwd("modnorm1_bwd", _f_modnorm, [(r["x"], D, 0)], [p["mixw"], sc1, sh1],
                                         [(dh1, D, 0)], [(0, f32)], 512, adds=[dxa])
    grads = dict(wi=dwi, wo=dwo, wfi=dwfi, wfo=dwfo, conv=dconv, mixw=dmixw, ffnw=dffnw,
                 mod=jnp.stack([dsh1, dsc1, dg1, dsh2, dsc2, dg2]), alog=dalog, dtb=ddtb, wn=dwn, lng=dlng, lnb=dlnb,
                 ws=dws, bst=dbst, wq=dwq, wk=dwk)
    return dx, grads


def _rope_tables(t):
    inv = ROPE_THETA ** (-jnp.arange(0, ROPE_DIM, 2, dtype=f32) / ROPE_DIM)
    ang = jnp.arange(t, dtype=f32)[:, None] * inv[None, :]
    cos, sin = jnp.cos(ang), jnp.sin(ang)
    rest = SW_D - ROPE_DIM
    ch = jnp.concatenate([cos, cos, jnp.ones((t, rest), f32)], axis=1)
    sh = jnp.concatenate([sin, sin, jnp.zeros((t, rest), f32)], axis=1)
    return jnp.tile(ch, (1, SW_H)), jnp.tile(sh, (1, SW_H))


def _pad_last(a, n):
    return jnp.pad(a, [(0, 0)] * (a.ndim - 1) + [(0, n - a.shape[-1])])


def _scatter_cols(name, gs, cs, pad_to):
    r = gs[0].shape[0]
    parts = jnp.stack([jnp.transpose(_pad_last(g.reshape(r, NDEV, cs), pad_to), (1, 0, 2)) for g in gs], axis=1)
    return _all_to_all(name, parts).reshape(NDEV, len(gs) * r, pad_to)


_SMALL = ("b_mod", "mix_norm_w", "ffn_norm_w", "dn_a_log", "dn_dt_bias", "dn_out_norm_w", "gm_ln_g", "gm_ln_b",
          "gm_w_s", "gm_b_s", "sw_q_norm_w", "sw_k_norm_w")
PACK_ROWS = 800


def _pack_rows(a):
    return -(-a.size // 1024) * 8


def _pack(arrs):
    parts = [jnp.pad(a.reshape(-1), (0, _pack_rows(a) * 128 - a.size)).reshape(_pack_rows(a), 128) for a in arrs]
    rows = sum(p.shape[0] for p in parts)
    parts.append(jnp.zeros((-(-rows // PACK_ROWS) * PACK_ROWS - rows, 128), parts[0].dtype))
    return jnp.concatenate(parts, axis=0)


def _unpack(buf, like):
    out, off = [], 0
    for a in like:
        rows = _pack_rows(a)
        flat = buf[off:off + rows].reshape(-1)
        out.append((flat if flat.size == a.size else flat[:a.size]).reshape(a.shape))
        off += rows
    return out


def kernel(x, c, w_mod, b_mod, mix_norm_w, ffn_norm_w, w_in, w_out, dn_conv_w, dn_a_log, dn_dt_bias, dn_out_norm_w, gm_ln_g, gm_ln_b, gm_w_s, gm_b_s, sw_q_norm_w, sw_k_norm_w, w_ffn_in, w_ffn_out, loss_target, m_w_mod, m_b_mod, m_mix_norm_w, m_ffn_norm_w, m_w_in, m_w_out, m_dn_conv_w, m_dn_a_log, m_dn_dt_bias, m_dn_out_norm_w, m_gm_ln_g, m_gm_ln_b, m_gm_w_s, m_gm_b_s, m_sw_q_norm_w, m_sw_k_norm_w, m_w_ffn_in, m_w_ffn_out, v_w_mod, v_b_mod, v_mix_norm_w, v_ffn_norm_w, v_w_in, v_w_out, v_dn_conv_w, v_dn_a_log, v_dn_dt_bias, v_dn_out_norm_w, v_gm_ln_g, v_gm_ln_b, v_gm_w_s, v_gm_b_s, v_sw_q_norm_w, v_sw_k_norm_w, v_w_ffn_in, v_w_ffn_out):
    weights = dict(w_mod=w_mod, b_mod=b_mod, mix_norm_w=mix_norm_w, ffn_norm_w=ffn_norm_w, w_in=w_in, w_out=w_out,
                   dn_conv_w=dn_conv_w, dn_a_log=dn_a_log, dn_dt_bias=dn_dt_bias, dn_out_norm_w=dn_out_norm_w,
                   gm_ln_g=gm_ln_g, gm_ln_b=gm_ln_b, gm_w_s=gm_w_s, gm_b_s=gm_b_s, sw_q_norm_w=sw_q_norm_w,
                   sw_k_norm_w=sw_k_norm_w, w_ffn_in=w_ffn_in, w_ffn_out=w_ffn_out)
    mom = dict(w_mod=m_w_mod, b_mod=m_b_mod, mix_norm_w=m_mix_norm_w, ffn_norm_w=m_ffn_norm_w, w_in=m_w_in,
               w_out=m_w_out, dn_conv_w=m_dn_conv_w, dn_a_log=m_dn_a_log, dn_dt_bias=m_dn_dt_bias,
               dn_out_norm_w=m_dn_out_norm_w, gm_ln_g=m_gm_ln_g, gm_ln_b=m_gm_ln_b, gm_w_s=m_gm_w_s, gm_b_s=m_gm_b_s,
               sw_q_norm_w=m_sw_q_norm_w, sw_k_norm_w=m_sw_k_norm_w, w_ffn_in=m_w_ffn_in, w_ffn_out=m_w_ffn_out)
    var = dict(w_mod=v_w_mod, b_mod=v_b_mod, mix_norm_w=v_mix_norm_w, ffn_norm_w=v_ffn_norm_w, w_in=v_w_in,
               w_out=v_w_out, dn_conv_w=v_dn_conv_w, dn_a_log=v_dn_a_log, dn_dt_bias=v_dn_dt_bias,
               dn_out_norm_w=v_dn_out_norm_w, gm_ln_g=v_gm_ln_g, gm_ln_b=v_gm_ln_b, gm_w_s=v_gm_w_s, gm_b_s=v_gm_b_s,
               sw_q_norm_w=v_sw_q_norm_w, sw_k_norm_w=v_sw_k_norm_w, w_ffn_in=v_w_ffn_in, w_ffn_out=v_w_ffn_out)
    names = list(weights)
    xi, tgt = x[0], loss_target[0]
    t = xi.shape[0]
    nl = w_mod.shape[0]
    ax, ay, ac = (lax.axis_index(a) for a in AXES)
    me = 4 * ax + 2 * ay + ac
    mod_cs = w_mod.shape[2]

    shards_a = [_pad_last(w_in.astype(bf16), 640)]
    shards_b = [w_out.astype(bf16), _pad_last(w_ffn_in.astype(bf16), 768), w_ffn_out.astype(bf16)]

    def start_gather(l, which, tok):
        shards_l = [a[l] + tok.astype(bf16) for a in (shards_a if which == "a" else shards_b)]
        return _exchange_start(f"ag_start_{which}{l}", shards_l, [_own_slot(a, me) for a in shards_l], False)

    gather_a = start_gather(0, "a", jnp.zeros((), f32))

    c = c + gather_a[4][0, 0]
    (c_g, conv_g), _ = _all_gather("ag_c_conv", [jnp.broadcast_to(c, (NDEV, D)), _pad_last(dn_conv_w, 256)])
    c_all = c_g[:, 0, :]
    conv = jnp.transpose(conv_g, (1, 2, 0, 3))[..., :dn_conv_w.shape[2]].reshape(nl, CONV_K, CONV_W)
    b_cols = lax.dynamic_slice_in_dim(b_mod, me * mod_cs, mod_cs, axis=1)[:, None, :]
    modc = _mod_cols(c_all, w_mod, b_cols)
    mod_tx = jnp.pad(jnp.transpose(modc, (1, 0, 2)), ((0, 0), (0, 8 - nl), (0, 0)))
    mod_rx = _all_to_all("a2a_mod", mod_tx)[:, :nl]
    mod = jnp.transpose(mod_rx, (1, 0, 2)).reshape(nl, 6, 1, D)

    def w_in_full(landed):
        wi = jnp.transpose(landed[0], (1, 0, 2))[..., :w_in.shape[2]].reshape(D, IN_W)
        return jnp.concatenate([wi[:, :2048], wi[:, 2056:], wi[:, 2048:2056], jnp.zeros((D, IN_WA - IN_W), bf16)], axis=-1)

    def late_full(landed):
        go, gfi, gfo = landed
        wfi = jnp.transpose(gfi, (1, 0, 2))[..., :w_ffn_in.shape[2]].reshape(D, 2 * FFN)
        return dict(wo=go.reshape(D, D), wfi=wfi, wfo=gfo.reshape(FFN, D))

    pad128 = lambda a: _pad_last(a, 128)[:, None, :]
    params = dict(
        mod=mod, conv=conv,
        mixw=mix_norm_w[:, None, :], ffnw=ffn_norm_w[:, None, :], alog=pad128(dn_a_log), dtb=pad128(dn_dt_bias),
        wn=dn_out_norm_w[:, None, :], lng=gm_ln_g[:, None, :], lnb=gm_ln_b[:, None, :], ws=gm_w_s,
        bst=_pad_last(jnp.transpose(gm_b_s, (0, 2, 1)), 128),
        wq=jnp.pad(sw_q_norm_w[:, None, :], ((0, 0), (0, 7), (0, 128 - SW_D))),
        wk=jnp.pad(sw_k_norm_w[:, None, :], ((0, 0), (0, 7), (0, 128 - SW_D))))
    tabs = _rope_tables(t)

    layer_p, res = [], []
    xc, after, gather_b = xi, mod, None
    for l in range(nl):
        landed_a, tok = _exchange_wait(f"ag_wait_a{l}", gather_a, after, False)
        p = {k: v[l] for k, v in params.items()}
        p["wi"] = w_in_full(landed_a)
        hb = gather_b if l else start_gather(0, "b", tok[0, 0])
        tok = tok if l else hb[4]
        if l + 1 < nl:
            gather_a = start_gather(l + 1, "a", tok[0, 0])
            gather_b = start_gather(l + 1, "b", gather_a[4][0, 0])
            tok = gather_b[4]
        p["mod"] = p["mod"] + tok[0, 0]
        late = lambda ycat, hb=hb, l=l: late_full(_exchange_wait(f"ag_wait_b{l}", hb, ycat, False)[0])
        layer_p.append(p)
        xc, r = _layer_fwd(xc, p, tabs, late)
        res.append(r)
        after = xc
    dy, lpart = _loss_head(xc, tgt, 512)
    loss = lax.psum(lpart[0, 0], AXES)

    slot = lax.broadcasted_iota(jnp.int32, (NDEV, 1, 1), 0)

    def start_scatter(name, parts):
        return _exchange_start(name, parts, [jnp.where(slot == me, a, jnp.zeros_like(a)) for a in parts], True)

    dxi, gl, h_ffn, h_mix = dy, [None] * nl, [None] * nl, [None] * nl
    for l in reversed(range(nl)):
        def ffn_ready(dwfi, dwfo, l=l):
            h_ffn[l] = start_scatter(f"a2a_start_f{l}", [
                jnp.transpose(_pad_last(dwfi.reshape(D, NDEV, w_ffn_in.shape[2]), 768), (1, 0, 2)),
                dwfo.reshape(NDEV, w_ffn_out.shape[1], D)])
            return h_ffn[l][4]

        dxi, gl[l] = _layer_bwd(dxi, layer_p[l], res[l], tabs, ffn_ready)
        d = gl[l]["wi"]
        d = jnp.concatenate([d[:, :2048], d[:, 4864:4872], d[:, 2048:4864]], axis=-1)
        mix_parts = [jnp.transpose(_pad_last(d.reshape(D, NDEV, w_in.shape[2]), 640), (1, 0, 2)),
                     gl[l]["wo"].reshape(NDEV, w_out.shape[1], D)]
        if l > 0:
            h_mix[l] = start_scatter(f"a2a_start_m{l}", mix_parts)
            layer_p[l - 1]["mod"] = layer_p[l - 1]["mod"] + h_mix[l][4][0, 0]
    g = {k: jnp.stack([gl[l][k] for l in range(nl)]) for k in gl[0] if k not in ("wi", "wo", "wfi", "wfo")}

    dmod = g["mod"].reshape(nl, 6 * D)
    small_g = dict(b_mod=dmod, mix_norm_w=g["mixw"][:, 0], ffn_norm_w=g["ffnw"][:, 0], dn_a_log=g["alog"][:, 0, :DN_H],
                   dn_dt_bias=g["dtb"][:, 0, :DN_H], dn_out_norm_w=g["wn"][:, 0], gm_ln_g=g["lng"][:, 0],
                   gm_ln_b=g["lnb"][:, 0], gm_w_s=g["ws"], gm_b_s=jnp.transpose(g["bst"][:, :, :GM_G], (0, 2, 1)),
                   sw_q_norm_w=g["wq"][:, 0, :SW_D], sw_k_norm_w=g["wk"][:, 0, :SW_D])
    (parts,), tok = _all_gather("ag_small_grads", [_pack([small_g[n] for n in _SMALL])])
    h_mix[0] = start_scatter("a2a_start_m0", [a + tok[0, 0].astype(bf16) for a in mix_parts])
    like = [weights[n] for n in _SMALL]
    sm = _adamw("adamw_small", _pack(like) + h_mix[0][4][0, 0], _pack([mom[n] for n in _SMALL]),
                _pack([var[n] for n in _SMALL]), parts, PACK_ROWS)
    out = {n: vals for n, vals in zip(_SMALL, zip(*[_unpack(b, like) for b in sm]))}

    dmod_all = parts[:, :nl * 6 * D // 128, :].reshape(NDEV, nl, 6 * D)
    dmod_cols = jnp.transpose(lax.dynamic_slice_in_dim(dmod_all, me * mod_cs, mod_cs, axis=2), (1, 0, 2))
    gw_mod = _wmod_grad(c_all, dmod_cols).reshape(1, nl * D, mod_cs)

    def adamw_sharded(n, gp, cols, padc):
        shp = weights[n].shape
        rows = gp.shape[1]
        prep = lambda a: _pad_last(a.reshape(rows, cols), padc)
        tr = 256 if rows % 256 == 0 else rows // 4 if rows % 32 == 0 else rows
        res4 = _adamw("adamw_" + n, prep(weights[n]), prep(mom[n]), prep(var[n]), gp, tr)
        out[n] = tuple(a[:, :cols].reshape(shp) for a in res4)

    adamw_sharded("w_mod", gw_mod, mod_cs, mod_cs)
    adamw_sharded("dn_conv_w", _scatter_cols("a2a_conv", [gl[l]["conv"] for l in range(nl)], dn_conv_w.shape[2], 256),
                  dn_conv_w.shape[2], 256)
    after, land_f, land_m = sm[0], [None] * nl, [None] * nl
    for l in reversed(range(nl)):
        land_f[l], after = _exchange_wait(f"a2a_wait_f{l}", h_ffn[l], after, True)
        if l > 0:
            land_m[l], after = _exchange_wait(f"a2a_wait_m{l}", h_mix[l], after, True)
    adamw_sharded("w_ffn_in", jnp.concatenate([land_f[l][0] for l in range(nl)], axis=1), w_ffn_in.shape[2], 768)
    adamw_sharded("w_ffn_out", jnp.concatenate([land_f[l][1] for l in range(nl)], axis=1), D, D)
    both = out["w_ffn_in"][1][0, 0, :1] + out["w_ffn_out"][1][0, 0, :1]
    land_m[0], _ = _exchange_wait("a2a_wait_m0", h_mix[0], both, True)
    adamw_sharded("w_in", jnp.concatenate([land_m[l][0] for l in range(nl)], axis=1), w_in.shape[2], 640)
    adamw_sharded("w_out", jnp.concatenate([land_m[l][1] for l in range(nl)], axis=1), D, D)

    return (loss, dxi[None], *[out[n][0] for n in names], *[out[n][1] for n in names],
            *[out[n][2] for n in names], *[out[n][3] for n in names])
```

```python
import functools
import math

import jax
import jax.numpy as jnp
from jax import lax
from jax.experimental import pallas as pl
from jax.experimental.pallas import tpu as pltpu

f32 = jnp.float32
bf16 = jnp.bfloat16
HI = lax.Precision.HIGH
AXES = ("x", "y", "c")
NDEV = 8
SDS = jax.ShapeDtypeStruct

D = 1024
NORM_EPS = 1e-6
DN_W, DN_H, DN_D, DN_C = 512, 4, 128, 64
GM_W, GM_G, GM_C = 256, 4, 128
SW_W, SW_H, SW_D, SW_B = 256, 4, 64, 128
SW_DIL = (1, 4, 16)
SW_SPAN = 128
ROPE_DIM, ROPE_THETA = 16, 500000.0
IN_W = 4872
IN_WA = 4992
FFN = 2816
ADAM_LR, ADAM_B1, ADAM_B2, ADAM_EPS, ADAM_WD, ADAM_STEP = 0.001, 0.9, 0.999, 1e-08, 0.01, 10

VMEM_LIMIT = 52 * 1024 * 1024


def _cp(sem=None):
    return pltpu.CompilerParams(vmem_limit_bytes=VMEM_LIMIT, dimension_semantics=sem)


_DIMS = {"nn": (((1,), (0,)), ((), ())), "nt": (((1,), (1,)), ((), ())), "tn": (((0,), (0,)), ((), ()))}


def _raw_dot(a, b, mode, hi):
    if hi:
        return lax.dot_general(a, b, _DIMS[mode], precision=HI, preferred_element_type=f32)
    return lax.dot_general(a.astype(bf16), b.astype(bf16), _DIMS[mode], preferred_element_type=f32)


@functools.partial(jax.custom_vjp, nondiff_argnums=(2, 3))
def _dot(a, b, mode, hi):
    return _raw_dot(a, b, mode, hi)


def _dot_fwd(a, b, mode, hi):
    return _raw_dot(a, b, mode, hi), (a, b)


def _dot_bwd(mode, hi, res, g):
    a, b = res
    if mode == "nn":
        return _raw_dot(g, b, "nt", hi), _raw_dot(a, g, "tn", hi)
    if mode == "nt":
        return _raw_dot(g, b, "nn", hi), _raw_dot(g, a, "tn", hi)
    return _raw_dot(b, g, "nt", hi), _raw_dot(a, g, "nn", hi)


_dot.defvjp(_dot_fwd, _dot_bwd)


def _iota(shape, dim):
    return lax.broadcasted_iota(jnp.int32, shape, dim)


def _f_modnorm(x, w, scale, shift):
    y = x * lax.rsqrt(jnp.mean(x * x, axis=-1, keepdims=True) + NORM_EPS) * w
    return (y * (1.0 + scale) + shift,)


def _f_resid(x, m, gate):
    return (x + gate * m,)


def _f_swiglu(gu):
    return (jax.nn.silu(gu[:, :FFN]) * gu[:, FFN:],)


def _softplus(x):
    return jnp.maximum(x, 0.0) + jnp.log1p(jnp.exp(-jnp.abs(x)))


def _f_dn_act(y, ab, alog, dtb):
    c = jax.nn.silu(y)
    parts = []
    for j in range(3 * DN_H):
        p = c[:, j * DN_D:(j + 1) * DN_D]
        if j < 2 * DN_H:
            p = p * lax.rsqrt(jnp.sum(p * p, axis=-1, keepdims=True) + NORM_EPS)
        parts.append(p)
    lane = _iota(ab.shape, 1)
    g = -jnp.exp(alog) * _softplus(ab + dtb)
    beta = jax.nn.sigmoid(ab)
    gb = jnp.where(lane < DN_H, g, jnp.where(lane < 2 * DN_H, beta, 0.0))
    return jnp.concatenate(parts, axis=1), gb


def _f_dn_out(o, z, wn):
    parts = []
    for h in range(DN_H):
        oh = o[:, h * DN_D:(h + 1) * DN_D]
        zh = z[:, h * DN_D:(h + 1) * DN_D]
        n = oh * lax.rsqrt(jnp.mean(oh * oh, axis=-1, keepdims=True) + NORM_EPS) * wn
        parts.append(n * jax.nn.silu(zh))
    return (jnp.concatenate(parts, axis=1),)


def _gelu(x):
    return 0.5 * x * (1.0 + lax.erf(x * (1.0 / math.sqrt(2.0))))


def _f_gm(u_raw, v_raw, ln_g, ln_b, w_s, b_st):
    u = _gelu(u_raw)
    v = _gelu(v_raw)
    mu = jnp.mean(v, axis=-1, keepdims=True)
    vc = v - mu
    var = jnp.mean(vc * vc, axis=-1, keepdims=True)
    v = vc * lax.rsqrt(var + NORM_EPS) * ln_g + ln_b
    r = _iota((GM_C, GM_C), 0)
    c = _iota((GM_C, GM_C), 1)
    grp = _iota((GM_C, GM_W), 1) // (GM_W // GM_G)
    expand = jnp.where(_iota((GM_C, GM_W), 0) == grp, 1.0, 0.0)
    sv = _dot(b_st, expand, "nn", True)
    for g in range(GM_G):
        wg = jnp.where(r >= c, w_s[g], 0.0)
        sv = sv + jnp.where(grp == g, _dot(wg, v, "nn", True), 0.0)
    return (u * sv,)


def _head_lanes(shape):
    return _iota(shape, 1) // SW_D


def _f_sw_pre(q0, k0, q1, k1, q2, k2, cs, sn, wq, wk):
    r = _iota((SW_W, SW_W), 0)
    c = _iota((SW_W, SW_W), 1)
    same_head = jnp.where(r // SW_D == c // SW_D, 1.0, 0.0)
    hc = c % SW_D
    half = ROPE_DIM // 2
    perm = jnp.where((hc < half) & (r == c + half), -1.0, jnp.where((hc >= half) & (hc < ROPE_DIM) & (r == c - half), 1.0, 0.0))
    tile = jnp.where((_iota((128, SW_W), 1) % SW_D == _iota((128, SW_W), 0)) & (_iota((128, SW_W), 0) < SW_D), 1.0, 0.0)
    wq_full = _dot(wq, tile, "nn", True)[0:1, :]
    wk_full = _dot(wk, tile, "nn", True)[0:1, :]

    def one(t, w):
        ms = _dot(t * t, same_head, "nn", False) * (1.0 / SW_D)
        n = t * lax.rsqrt(ms + NORM_EPS) * w
        return n * cs + _dot(n, perm, "nn", False) * sn

    return one(q0, wq_full), one(k0, wk_full), one(q1, wq_full), one(k1, wk_full), one(q2, wq_full), one(k2, wk_full)


def _f_sw_merge(o0, l0, o1, l1, o2, l2):
    m = jnp.maximum(jnp.maximum(l0, l1), l2)
    e0, e1, e2 = jnp.exp(l0 - m), jnp.exp(l1 - m), jnp.exp(l2 - m)
    return ((e0 * o0 + e1 * o1 + e2 * o2) / (e0 + e1 + e2),)


def _f_attn(q, kp, kc, vp, vc, has_prev):
    kk = jnp.concatenate([kp, kc], axis=0)
    vv = jnp.concatenate([vp, vc], axis=0)
    i = _iota((SW_B, 2 * SW_B), 0)
    j = _iota((SW_B, 2 * SW_B), 1)
    dist = i + SW_B - j
    valid = (dist >= 0) & (dist <= SW_SPAN) & ((j >= SW_B) | has_prev)
    hl = _head_lanes(q.shape)
    heads = range(SW_H)
    ss = [_dot(jnp.where(hl == h, q, 0.0), kk, "nt", False) * (SW_D ** -0.5) for h in heads]
    ss = [jnp.where(valid, s, -1e30) for s in ss]
    ms = [jnp.max(s, axis=-1, keepdims=True) for s in ss]
    ps = [jnp.where(valid, jnp.exp(s - m), 0.0) for s, m in zip(ss, ms)]
    ls = [jnp.sum(p, axis=-1, keepdims=True) for p in ps]
    ohs = [_dot(p, vv, "nn", False) * (1.0 / l) for p, l in zip(ps, ls)]
    o = jnp.zeros(q.shape, f32)
    lse = jnp.zeros(q.shape, f32)
    for h in heads:
        o = o + jnp.where(hl == h, ohs[h], 0.0)
        lse = lse + jnp.where(hl == h, ms[h] + jnp.log(ls[h]), 0.0)
    return o, lse


def _lane_col(tile, lane_idx):
    return jnp.sum(jnp.where(_iota(tile.shape, 1) == lane_idx, tile, 0.0), axis=1, keepdims=True)


def _chunk_decays(gbv):
    n = gbv.shape[0]
    r = _iota((n, n), 0)
    c = _iota((n, n), 1)
    gc_all = _dot(jnp.where(r >= c, 1.0, 0.0), gbv, "nn", True)
    gc_rows = gc_all.T
    out = []
    for h in range(DN_H):
        gcol = _lane_col(gc_all, h)
        diff = jnp.where(r >= c, gcol - gc_rows[h:h + 1, :], 0.0)
        out.append((gcol, jnp.where(r >= c, jnp.exp(diff), 0.0)))
    return out, r, c


def _f_dn_pre(ks, vs, gbvs):
    decs, betas = [], []
    for gbv in gbvs:
        d, r, c = _chunk_decays(gbv)
        decs += d
        betas += [_lane_col(gbv, DN_H + h) for h in range(DN_H)]
    kbs = [k * b for k, b in zip(ks, betas)]
    grams = [_dot(kb, k, "nt", True) for kb, k in zip(kbs, ks)]
    mats = [jnp.where(r > c, g * dec[1], 0.0) for g, dec in zip(grams, decs)]
    rhss = [jnp.concatenate([v * b, kb * jnp.exp(dec[0])], axis=1) for v, b, kb, dec in zip(vs, betas, kbs, decs)]
    return mats, rhss


def _inv_unit_lower(mats):
    n = mats[0].shape[0]
    eye = jnp.where(_iota((n, n), 0) == _iota((n, n), 1), 1.0, 0.0)
    invs = [eye - a for a in mats]
    pws = list(mats)
    for _ in range(n.bit_length() - 2):
        pws = [_raw_dot(p, p, "nn", True) for p in pws]
        invs = [_raw_dot(i, eye + p, "nn", True) for i, p in zip(invs, pws)]
    return invs


def _f_dn_scan(states, qs, ks, us, ws, gbv):
    decs, _, _ = _chunk_decays(gbv)
    n = gbv.shape[0]
    last = _iota((n, 1), 0) == n - 1
    g_last = [jnp.sum(jnp.where(last, gc, 0.0), axis=0, keepdims=True) for gc, _ in decs]
    qs = [q * (DN_D ** -0.5) for q in qs]
    a_qk = [_dot(q, k, "nt", True) * dec[1] for q, k, dec in zip(qs, ks, decs)]
    q_dec = [q * jnp.exp(dec[0]) for q, dec in zip(qs, decs)]
    k_dec = [k * jnp.exp(gl - dec[0]) for k, gl, dec in zip(ks, g_last, decs)]
    ws_ = [_dot(w, s, "nn", False) for w, s in zip(ws, states)]
    o_st = [_dot(qd, s, "nn", False) for qd, s in zip(q_dec, states)]
    v_new = [u - x for u, x in zip(us, ws_)]
    o_in = [_dot(a, vn, "nn", False) for a, vn in zip(a_qk, v_new)]
    upd = [_dot(kd, vn, "tn", False) for kd, vn in zip(k_dec, v_new)]
    outs = [x + y for x, y in zip(o_st, o_in)]
    new_states = [s * jnp.exp(gl) + x for s, gl, x in zip(states, g_last, upd)]
    return outs, new_states


def _cspec(tt, w, cb):
    return pl.BlockSpec((tt, w), lambda i, cb=cb: (i, cb))


def _pspec(shape):
    nd = len(shape)
    return pl.BlockSpec(tuple(shape), lambda i, nd=nd: (0,) * nd)


def _ew_fwd(name, f, tiled, params, outs, tt):
    t = tiled[0][0].shape[0]
    nt, npar = len(tiled), len(params)

    def body(*refs):
        tv = [r[...].astype(f32) for r in refs[:nt]]
        pv = [r[...] for r in refs[nt:nt + npar]]
        res = f(*tv, *pv)
        for o, r in zip(refs[nt + npar:], res):
            o[...] = r.astype(o.dtype)

    res = pl.pallas_call(
        body, grid=(t // tt,), name=name,
        in_specs=[_cspec(tt, w, cb) for _, w, cb in tiled] + [_pspec(p.shape) for p in params],
        out_specs=[_cspec(tt, w, 0) for w, _ in outs],
        out_shape=[SDS((t, w), dt) for w, dt in outs],
        compiler_params=_cp(("arbitrary",)),
    )(*[a for a, _, _ in tiled], *params)
    return res


def _ew_bwd(name, f, tiled, params, cots, diff, tt, adds=None):
    t = tiled[0][0].shape[0]
    cots = [c if isinstance(c, list) else [c] for c in cots]
    pieces = [pc for c in cots for pc in c]
    nt, npar, nc, nd = len(tiled), len(params), len(pieces), len(diff)
    adds = adds or [None] * nd
    add_arrs = [a for a in adds if a is not None]
    na = len(add_arrs)
    dwidth = [tiled[k][1] for k, _ in diff]

    def body(*refs):
        tin = refs[:nt]
        pin = refs[nt:nt + npar]
        cin = refs[nt + npar:nt + npar + nc]
        ain = list(refs[nt + npar + nc:nt + npar + nc + na])
        dts = refs[nt + npar + nc + na:nt + npar + nc + na + nd]
        dps = refs[nt + npar + nc + na + nd:]
        tv = [r[...].astype(f32) for r in tin]
        pv = [r[...] for r in pin]

        def g(*dv):
            full = list(tv)
            for n_, (k, _) in enumerate(diff):
                full[k] = dv[n_]
            return tuple(f(*full, *dv[nd:]))

        _, vjp = jax.vjp(g, *[tv[k] for k, _ in diff], *pv)
        cin = list(cin)
        cvals = [jnp.concatenate([cin.pop(0)[...].astype(f32) for _ in c], axis=1) if len(c) > 1
                 else cin.pop(0)[...].astype(f32) for c in cots]
        grads = vjp(tuple(cvals))
        for n_ in range(nd):
            val = grads[n_]
            if adds[n_] is not None:
                val = val + ain.pop(0)[...].astype(f32)
            dts[n_][...] = val.astype(dts[n_].dtype)

        @pl.when(pl.program_id(0) == 0)
        def _():
            for r in dps:
                r[...] = jnp.zeros(r.shape, f32)

        for r, gp in zip(dps, grads[nd:]):
            r[...] += gp

    res = pl.pallas_call(
        body, grid=(t // tt,), name=name,
        in_specs=[_cspec(tt, w, cb) for _, w, cb in tiled] + [_pspec(p.shape) for p in params]
        + [_cspec(tt, w, cb) for _, w, cb in pieces] + [_cspec(tt, a.shape[1], 0) for a in add_arrs],
        out_specs=[_cspec(tt, w, 0) for w in dwidth] + [_pspec(p.shape) for p in params],
        out_shape=[SDS((t, w), dt) for w, (_, dt) in zip(dwidth, diff)] + [SDS(p.shape, f32) for p in params],
        compiler_params=_cp(("arbitrary",)),
    )(*[a for a, _, _ in tiled], *params, *[a for a, _, _ in pieces], *add_arrs)
    return res[:nd], res[nd:]


def _mm(name, a, b, mode, tm, tn, out_dtype=f32, b_outer=False):
    ij = (lambda g0, g1: (g1, g0)) if b_outer else (lambda g0, g1: (g0, g1))
    if mode == "nn":
        (m, k), (k2, n) = a.shape, b.shape
        a_spec = pl.BlockSpec((tm, k), lambda g0, g1: (ij(g0, g1)[0], 0))
        b_spec = pl.BlockSpec((k, tn), lambda g0, g1: (0, ij(g0, g1)[1]))
    elif mode == "nt":
        (m, k), (n, k2) = a.shape, b.shape
        a_spec = pl.BlockSpec((tm, k), lambda g0, g1: (ij(g0, g1)[0], 0))
        b_spec = pl.BlockSpec((tn, k), lambda g0, g1: (ij(g0, g1)[1], 0))
    else:
        (k, m), (k2, n) = a.shape, b.shape
        a_spec = pl.BlockSpec((k, tm), lambda g0, g1: (0, ij(g0, g1)[0]))
        b_spec = pl.BlockSpec((k, tn), lambda g0, g1: (0, ij(g0, g1)[1]))
    assert k == k2 and m % tm == 0 and n % tn == 0, (name, a.shape, b.shape, mode)
    assert a.dtype == bf16 and b.dtype == bf16, name

    def body(a_ref, b_ref, o_ref):
        o_ref[...] = lax.dot_general(a_ref[...], b_ref[...], _DIMS[mode], preferred_element_type=f32).astype(o_ref.dtype)

    return pl.pallas_call(
        body, grid=(n // tn, m // tm) if b_outer else (m // tm, n // tn), name=name,
        in_specs=[a_spec, b_spec], out_specs=pl.BlockSpec((tm, tn), lambda g0, g1: ij(g0, g1)),
        out_shape=SDS((m, n), out_dtype), compiler_params=_cp(("parallel", "parallel")),
    )(a, b)


CONV_K = 4
CONV_W = 3 * DN_W
HALO = 8


def _conv_fwd(proj, w, tt):
    t = proj.shape[0]
    nb8 = tt // HALO

    def body(x_ref, h_ref, w_ref, y_ref, xe):
        i = pl.program_id(0)
        xe[0:HALO, :] = jnp.where(i == 0, 0.0, h_ref[...])
        xe[HALO:, :] = x_ref[...]
        wv = w_ref[...]
        acc = jnp.zeros((tt, CONV_W), f32)
        for k in range(CONV_K):
            acc = acc + wv[k:k + 1, :] * xe[pl.ds(HALO - (CONV_K - 1) + k, tt), :]
        y_ref[...] = acc

    return pl.pallas_call(
        body, grid=(t // tt,), name="conv_fwd",
        in_specs=[pl.BlockSpec((tt, CONV_W), lambda i: (i, 0)),
                  pl.BlockSpec((HALO, CONV_W), lambda i: (jnp.maximum(i * nb8 - 1, 0), 0)),
                  _pspec(w.shape)],
        out_specs=pl.BlockSpec((tt, CONV_W), lambda i: (i, 0)),
        out_shape=SDS((t, CONV_W), f32),
        scratch_shapes=[pltpu.VMEM((tt + HALO, CONV_W), f32)],
        compiler_params=_cp(("arbitrary",)),
    )(proj, proj, w)


def _conv_bwd(proj, dy, w, tt):
    t = proj.shape[0]
    nb8 = tt // HALO
    last8 = t // HALO - 1
    nsteps = t // tt

    def body(x_ref, h_ref, dy_ref, n_ref, w_ref, dx_ref, dw_ref, xe, dye):
        i = pl.program_id(0)
        xe[0:HALO, :] = jnp.where(i == 0, 0.0, h_ref[...])
        xe[HALO:, :] = x_ref[...]
        dye[0:tt, :] = dy_ref[...]
        dye[tt:, :] = jnp.where(i == nsteps - 1, 0.0, n_ref[...])
        wv = w_ref[...]
        dyv = dy_ref[...]
        acc = jnp.zeros((tt, CONV_W), f32)

        @pl.when(i == 0)
        def _():
            dw_ref[...] = jnp.zeros(dw_ref.shape, f32)

        for k in range(CONV_K):
            acc = acc + wv[k:k + 1, :] * dye[pl.ds(CONV_K - 1 - k, tt), :]
            dw_ref[k:k + 1, :] += jnp.sum(dyv * xe[pl.ds(HALO - (CONV_K - 1) + k, tt), :], axis=0, keepdims=True)
        dx_ref[...] = acc.astype(dx_ref.dtype)

    return pl.pallas_call(
        body, grid=(nsteps,), name="conv_bwd",
        in_specs=[pl.BlockSpec((tt, CONV_W), lambda i: (i, 0)),
                  pl.BlockSpec((HALO, CONV_W), lambda i: (jnp.maximum(i * nb8 - 1, 0), 0)),
                  pl.BlockSpec((tt, CONV_W), lambda i: (i, 0)),
                  pl.BlockSpec((HALO, CONV_W), lambda i: (jnp.minimum((i + 1) * nb8, last8), 0)),
                  _pspec(w.shape)],
        out_specs=[pl.BlockSpec((tt, CONV_W), lambda i: (i, 0)), _pspec(w.shape)],
        out_shape=[SDS((t, CONV_W), bf16), SDS(w.shape, f32)],
        scratch_shapes=[pltpu.VMEM((tt + HALO, CONV_W), f32), pltpu.VMEM((tt + HALO, CONV_W), f32)],
        compiler_params=_cp(("arbitrary",)),
    )(proj, proj, dy, dy, w)


PREP_CHUNKS = 4


def _head_cols(part, h):
    return slice(part * DN_W + h * DN_D, part * DN_W + (h + 1) * DN_D)


def _prep_operands(qkv_ref, gb_ref):
    inst = [(slice(ch * DN_C, (ch + 1) * DN_C), h) for ch in range(PREP_CHUNKS) for h in range(DN_H)]
    ks = [qkv_ref[rs, _head_cols(1, h)] for rs, h in inst]
    vs = [qkv_ref[rs, _head_cols(2, h)] for rs, h in inst]
    gbvs = [gb_ref[ch * DN_C:(ch + 1) * DN_C, :] for ch in range(PREP_CHUNKS)]
    return inst, ks, vs, gbvs


def _dn_prep_fwd(qkv, gb):
    t = qkv.shape[0]
    rows = PREP_CHUNKS * DN_C

    def body(qkv_ref, gb_ref, u_ref, w_ref, inv_ref):
        inv_ref[...] = jnp.zeros(inv_ref.shape, f32)
        inst, ks, vs, gbvs = _prep_operands(qkv_ref, gb_ref)
        mats, rhss = _f_dn_pre(ks, vs, gbvs)
        invs = _inv_unit_lower(mats)
        uws = [_raw_dot(inv, rhs, "nn", True) for inv, rhs in zip(invs, rhss)]
        for (rs, h), inv, uw in zip(inst, invs, uws):
            u_ref[rs, _head_cols(0, h)] = uw[:, :DN_D]
            w_ref[rs, _head_cols(0, h)] = uw[:, DN_D:]
            inv_ref[rs, h * DN_D:h * DN_D + DN_C] = inv

    return pl.pallas_call(
        body, grid=(t // rows,), name="dn_prep_fwd",
        in_specs=[pl.BlockSpec((rows, CONV_W), lambda i: (i, 0)), pl.BlockSpec((rows, 128), lambda i: (i, 0))],
        out_specs=[pl.BlockSpec((rows, DN_W), lambda i: (i, 0))] * 3,
        out_shape=[SDS((t, DN_W), f32)] * 3,
        compiler_params=_cp(("arbitrary",)),
    )(qkv, gb)


def _dn_prep_bwd(qkv, gb, inv_all, du, dw, dqk1, dgb1):
    t = qkv.shape[0]
    rows = PREP_CHUNKS * DN_C

    def body(qkv_ref, gb_ref, inv_ref, du_ref, dw_ref, dqk1_ref, dgb1_ref, dqkv_ref, dgb_ref):
        inst, ks, vs, gbvs = _prep_operands(qkv_ref, gb_ref)
        (_, rhss), vjp = jax.vjp(_f_dn_pre, ks, vs, gbvs)
        invs = [inv_ref[rs, h * DN_D:h * DN_D + DN_C] for rs, h in inst]
        dxs = [jnp.concatenate([du_ref[rs, _head_cols(0, h)], dw_ref[rs, _head_cols(0, h)]], axis=1) for rs, h in inst]
        uws = [_raw_dot(inv, rhs, "nn", True) for inv, rhs in zip(invs, rhss)]
        drhss = [_raw_dot(inv, dx, "tn", True) for inv, dx in zip(invs, dxs)]
        das = [-_raw_dot(dr, uw, "nt", True) for dr, uw in zip(drhss, uws)]
        dks, dvs, dgbvs = vjp((das, drhss))
        for (rs, h), dk, dv in zip(inst, dks, dvs):
            dqkv_ref[rs, _head_cols(0, h)] = dqk1_ref[rs, _head_cols(0, h)]
            dqkv_ref[rs, _head_cols(1, h)] = dk + dqk1_ref[rs, _head_cols(1, h)]
            dqkv_ref[rs, _head_cols(2, h)] = dv
        for ch, dgbv in enumerate(dgbvs):
            rs = slice(ch * DN_C, (ch + 1) * DN_C)
            dgb_ref[rs, :] = dgbv + dgb1_ref[rs, :]

    return pl.pallas_call(
        body, grid=(t // rows,), name="dn_prep_bwd",
        in_specs=[pl.BlockSpec((rows, CONV_W), lambda i: (i, 0)), pl.BlockSpec((rows, 128), lambda i: (i, 0)),
                  pl.BlockSpec((rows, DN_W), lambda i: (i, 0)),
                  pl.BlockSpec((rows, DN_W), lambda i: (i, 0)), pl.BlockSpec((rows, DN_W), lambda i: (i, 0)),
                  pl.BlockSpec((rows, 2 * DN_W), lambda i: (i, 0)), pl.BlockSpec((rows, 128), lambda i: (i, 0))],
        out_specs=[pl.BlockSpec((rows, CONV_W), lambda i: (i, 0)), pl.BlockSpec((rows, 128), lambda i: (i, 0))],
        out_shape=[SDS((t, CONV_W), f32), SDS((t, 128), f32)],
        compiler_params=_cp(("arbitrary",)),
    )(qkv, gb, inv_all, du, dw, dqk1, dgb1)


def _dn_scan_fwd(qkv, gb, u, w):
    t = qkv.shape[0]
    n = t // DN_C
    heads = range(DN_H)

    def body(qkv_ref, gb_ref, u_ref, w_ref, o_ref, s_ref, state):
        @pl.when(pl.program_id(0) == 0)
        def _():
            state[...] = jnp.zeros(state.shape, f32)

        states = [state[h] for h in heads]
        for h in heads:
            s_ref[0, h] = states[h]
        outs, new = _f_dn_scan(states, [qkv_ref[:, _head_cols(0, h)] for h in heads],
                               [qkv_ref[:, _head_cols(1, h)] for h in heads],
                               [u_ref[:, _head_cols(0, h)] for h in heads],
                               [w_ref[:, _head_cols(0, h)] for h in heads], gb_ref[...])
        for h in heads:
            o_ref[:, _head_cols(0, h)] = outs[h]
            state[h] = new[h]

    return pl.pallas_call(
        body, grid=(n,), name="dn_scan_fwd",
        in_specs=[pl.BlockSpec((DN_C, 2 * DN_W), lambda i: (i, 0)), pl.BlockSpec((DN_C, 128), lambda i: (i, 0)),
                  pl.BlockSpec((DN_C, DN_W), lambda i: (i, 0)), pl.BlockSpec((DN_C, DN_W), lambda i: (i, 0))],
        out_specs=[pl.BlockSpec((DN_C, DN_W), lambda i: (i, 0)),
                   pl.BlockSpec((1, DN_H, DN_D, DN_D), lambda i: (i, 0, 0, 0))],
        out_shape=[SDS((t, DN_W), f32), SDS((n, DN_H, DN_D, DN_D), f32)],
        scratch_shapes=[pltpu.VMEM((DN_H, DN_D, DN_D), f32)],
        compiler_params=_cp(("arbitrary",)),
    )(qkv, gb, u, w)


def _dn_scan_bwd(qkv, gb, u, w, states, do):
    t = qkv.shape[0]
    n = t // DN_C
    rev = lambda i: (n - 1 - i, 0)
    heads = range(DN_H)

    def body(qkv_ref, gb_ref, u_ref, w_ref, s_ref, do_ref, dqk_ref, du_ref, dw_ref, dgb_ref, dstate):
        @pl.when(pl.program_id(0) == 0)
        def _():
            dstate[...] = jnp.zeros(dstate.shape, f32)

        _, vjp = jax.vjp(_f_dn_scan, [s_ref[0, h] for h in heads], [qkv_ref[:, _head_cols(0, h)] for h in heads],
                         [qkv_ref[:, _head_cols(1, h)] for h in heads], [u_ref[:, _head_cols(0, h)] for h in heads],
                         [w_ref[:, _head_cols(0, h)] for h in heads], gb_ref[...])
        ds, dq, dk, du, dw, dgbv = vjp(([do_ref[:, _head_cols(0, h)] for h in heads], [dstate[h] for h in heads]))
        for h in heads:
            dstate[h] = ds[h]
            dqk_ref[:, _head_cols(0, h)] = dq[h]
            dqk_ref[:, _head_cols(1, h)] = dk[h]
            du_ref[:, _head_cols(0, h)] = du[h]
            dw_ref[:, _head_cols(0, h)] = dw[h]
        dgb_ref[...] = dgbv

    return pl.pallas_call(
        body, grid=(n,), name="dn_scan_bwd",
        in_specs=[pl.BlockSpec((DN_C, 2 * DN_W), rev), pl.BlockSpec((DN_C, 128), rev),
                  pl.BlockSpec((DN_C, DN_W), rev), pl.BlockSpec((DN_C, DN_W), rev),
                  pl.BlockSpec((1, DN_H, DN_D, DN_D), lambda i: (n - 1 - i, 0, 0, 0)),
                  pl.BlockSpec((DN_C, DN_W), rev)],
        out_specs=[pl.BlockSpec((DN_C, 2 * DN_W), rev), pl.BlockSpec((DN_C, DN_W), rev),
                   pl.BlockSpec((DN_C, DN_W), rev), pl.BlockSpec((DN_C, 128), rev)],
        out_shape=[SDS((t, 2 * DN_W), f32), SDS((t, DN_W), f32), SDS((t, DN_W), f32), SDS((t, 128), f32)],
        scratch_shapes=[pltpu.VMEM((DN_H, DN_D, DN_D), f32)],
        compiler_params=_cp(("arbitrary",)),
    )(qkv, gb, u, w, states, do)


HALF = SW_W // 2


def _attn_specs(rows, cb, lag, nb):
    def one(c):
        if lag:
            return pl.BlockSpec((rows, HALF), lambda n, r: (jnp.maximum(jnp.minimum(n, nb - 1) - 1, 0), c))
        return pl.BlockSpec((rows, HALF), lambda n, r: (jnp.minimum(n, nb - 1), c))
    return [one(2 * cb), one(2 * cb + 1)]


def _sw_attn_fwd(q, k, vsrc, vcb, d):
    t = q.shape[0]
    rows = SW_B * d
    nb = t // rows
    cur = lambda cb: _attn_specs(rows, cb, False, nb)
    prev = lambda cb: _attn_specs(rows, cb, True, nb)

    def body(ql, qh, kpl, kph, kcl, kch, vpl, vph, vcl, vch, ol, oh, ll, lh):
        rs = pl.ds(pl.program_id(1), SW_B, stride=d) if d > 1 else pl.ds(0, SW_B)
        ld = lambda lo, hi: jnp.concatenate([lo[rs, :], hi[rs, :]], axis=1)
        o, lse = _f_attn(ld(ql, qh), ld(kpl, kph), ld(kcl, kch), ld(vpl, vph), ld(vcl, vch), pl.program_id(0) > 0)
        ol[rs, :] = o[:, :HALF]
        oh[rs, :] = o[:, HALF:]
        ll[rs, :] = lse[:, :HALF]
        lh[rs, :] = lse[:, HALF:]

    out = pl.BlockSpec((rows, HALF), lambda n, r: (n, 0))
    return pl.pallas_call(
        body, grid=(nb, d), name=f"sw_attn_fwd_d{d}",
        in_specs=cur(0) + prev(0) + cur(0) + prev(vcb) + cur(vcb), out_specs=[out] * 4,
        out_shape=[SDS((t, HALF), f32)] * 4, compiler_params=_cp(("arbitrary", "arbitrary")),
    )(q, q, k, k, k, k, vsrc, vsrc, vsrc, vsrc)


def _sw_attn_bwd(q, k, vsrc, vcb, do_l, do_h, dl_l, dl_h, d):
    t = q.shape[0]
    rows = SW_B * d
    nb = t // rows
    cur = lambda cb: _attn_specs(rows, cb, False, nb)
    prev = lambda cb: _attn_specs(rows, cb, True, nb)
    lag = pl.BlockSpec((rows, HALF), lambda n, r: (jnp.maximum(n - 1, 0), 0))
    here = pl.BlockSpec((rows, HALF), lambda n, r: (jnp.minimum(n, nb - 1), 0))

    def body(ql, qh, kpl, kph, kcl, kch, vpl, vph, vcl, vch, dol, doh, dll, dlh,
             dql, dqh, dkl, dkh, dvl, dvh, dk_hold, dv_hold):
        n = pl.program_id(0)
        r = pl.program_id(1)
        rs = pl.ds(r, SW_B, stride=d) if d > 1 else pl.ds(0, SW_B)
        hs = pl.ds(pl.multiple_of(r * SW_B, SW_B), SW_B)
        ld = lambda lo, hi: jnp.concatenate([lo[rs, :], hi[rs, :]], axis=1)

        def put(lo, hi, val):
            lo[rs, :] = val[:, :HALF]
            hi[rs, :] = val[:, HALF:]

        @pl.when(n < nb)
        def _():
            has_prev = n > 0
            _, vjp = jax.vjp(lambda q_, kp, kc, vp, vc: _f_attn(q_, kp, kc, vp, vc, has_prev),
                             ld(ql, qh), ld(kpl, kph), ld(kcl, kch), ld(vpl, vph), ld(vcl, vch))
            dq, dkp, dkc, dvp, dvc = vjp((ld(dol, doh), ld(dll, dlh)))
            put(dql, dqh, dq)
            put(dkl, dkh, dk_hold[hs, :] + dkp)
            put(dvl, dvh, dv_hold[hs, :] + dvp)
            dk_hold[hs, :] = dkc
            dv_hold[hs, :] = dvc

        @pl.when(n == nb)
        def _():
            put(dkl, dkh, dk_hold[hs, :])
            put(dvl, dvh, dv_hold[hs, :])

    return pl.pallas_call(
        body, grid=(nb + 1, d), name=f"sw_attn_bwd_d{d}",
        in_specs=cur(0) + prev(0) + cur(0) + prev(vcb) + cur(vcb) + [here] * 4,
        out_specs=[here, here, lag, lag, lag, lag],
        out_shape=[SDS((t, HALF), f32)] * 6,
        scratch_shapes=[pltpu.VMEM((rows, SW_W), f32)] * 2,
        compiler_params=_cp(("arbitrary", "arbitrary")),
    )(q, q, k, k, k, k, vsrc, vsrc, vsrc, vsrc, do_l, do_h, dl_l, dl_h)


def _loss_head(y, target, tt):
    t = y.shape[0]

    def body(y_ref, t_ref, dy_ref, l_ref):
        @pl.when(pl.program_id(0) == 0)
        def _():
            l_ref[...] = jnp.zeros(l_ref.shape, f32)

        err = y_ref[...] - t_ref[...]
        dy_ref[...] = err * (1.0 / D)
        l_ref[...] += 0.5 * jnp.sum(jnp.sum(err * err, axis=1, keepdims=True) * (1.0 / D), axis=0, keepdims=True)

    return pl.pallas_call(
        body, grid=(t // tt,), name="loss_head",
        in_specs=[pl.BlockSpec((tt, D), lambda i: (i, 0))] * 2,
        out_specs=[pl.BlockSpec((tt, D), lambda i: (i, 0)), pl.BlockSpec((8, 128), lambda i: (0, 0))],
        out_shape=[SDS((t, D), f32), SDS((8, 128), f32)],
        compiler_params=_cp(("arbitrary",)),
    )(y, target)


def _adamw(name, w, m, v, gparts, tr):
    r, c = w.shape
    p = gparts.shape[0]
    assert r % tr == 0, (name, w.shape, tr)

    def body(w_ref, m_ref, v_ref, g_ref, go_ref, d_ref, mo_ref, vo_ref):
        g = g_ref[0].astype(f32)
        for k in range(1, p):
            g = g + g_ref[k].astype(f32)
        wv = w_ref[...]
        mn = ADAM_B1 * m_ref[...] + (1.0 - ADAM_B1) * g
        vn = ADAM_B2 * v_ref[...] + (1.0 - ADAM_B2) * jnp.square(g)
        m_hat = mn / (1.0 - ADAM_B1 ** ADAM_STEP)
        v_hat = vn / (1.0 - ADAM_B2 ** ADAM_STEP)
        go_ref[...] = g
        d_ref[...] = -ADAM_LR * (m_hat / (jnp.sqrt(v_hat) + ADAM_EPS) + ADAM_WD * wv)
        mo_ref[...] = mn
        vo_ref[...] = vn

    spec = pl.BlockSpec((tr, c), lambda i: (i, 0))
    return pl.pallas_call(
        body, grid=(r // tr,), name=name,
        in_specs=[spec, spec, spec, pl.BlockSpec((p, tr, c), lambda i: (0, i, 0))],
        out_specs=[spec] * 4, out_shape=[SDS((r, c), f32)] * 4,
        compiler_params=_cp(("arbitrary",)),
    )(w, m, v, gparts)


def _mod_cols(c_all, w_mod, b_cols):
    nl, _, wc = w_mod.shape

    def body(c_ref, w_ref, b_ref, o_ref):
        o_ref[0] = _raw_dot(jax.nn.silu(c_ref[...]), w_ref[0], "nn", True) + b_ref[0]

    return pl.pallas_call(
        body, grid=(nl,), name="mod_cols",
        in_specs=[pl.BlockSpec((NDEV, D), lambda l: (0, 0)), pl.BlockSpec((1, D, wc), lambda l: (l, 0, 0)),
                  pl.BlockSpec((1, 1, wc), lambda l: (l, 0, 0))],
        out_specs=pl.BlockSpec((1, NDEV, wc), lambda l: (l, 0, 0)),
        out_shape=SDS((nl, NDEV, wc), f32), compiler_params=_cp(("arbitrary",)),
    )(c_all, w_mod, b_cols)


def _wmod_grad(c_all, dmod_cols):
    nl, _, wc = dmod_cols.shape

    def body(c_ref, d_ref, o_ref):
        o_ref[0, 0] = _raw_dot(jax.nn.silu(c_ref[...]), d_ref[0], "tn", True)

    return pl.pallas_call(
        body, grid=(nl,), name="wmod_grad",
        in_specs=[pl.BlockSpec((NDEV, D), lambda l: (0, 0)), pl.BlockSpec((1, NDEV, wc), lambda l: (l, 0, 0))],
        out_specs=pl.BlockSpec((1, 1, D, wc), lambda l: (0, l, 0, 0)),
        out_shape=SDS((1, nl, D, wc), f32), compiler_params=_cp(("arbitrary",)),
    )(c_all, dmod_cols)


def _me_and_peers():
    x, y, c = (lax.axis_index(a) for a in AXES)
    peers = []
    for k in range(1, NDEV):
        px = 1 - x if (k >> 2) & 1 else x
        py = 1 - y if (k >> 1) & 1 else y
        pc = 1 - c if k & 1 else c
        peers.append(((px, py, pc), 4 * px + 2 * py + pc))
    return 4 * x + 2 * y + c, peers


_ANY = pl.BlockSpec(memory_space=pl.ANY)


def _all_gather(name, arrs):
    n = len(arrs)

    def body(*refs):
        ins, outs, token = refs[:n], refs[n:2 * n], refs[2 * n]
        send_sems, recv_sems, local_sems = refs[2 * n + 1:]
        me, peers = _me_and_peers()
        mine = [pltpu.make_async_copy(ins[a], outs[a].at[me], local_sems.at[a]) for a in range(n)]
        copies = [pltpu.make_async_remote_copy(ins[a], outs[a].at[me], send_sems.at[a * (NDEV - 1) + k],
                                               recv_sems.at[a * (NDEV - 1) + k], device_id=dev,
                                               device_id_type=pl.DeviceIdType.MESH)
                  for a in range(n) for k, (dev, _) in enumerate(peers)]
        for cp in mine + copies:
            cp.start()
        token[...] = jnp.zeros(token.shape, token.dtype)
        for cp in copies + mine:
            cp.wait()

    res = pl.pallas_call(
        body, name=name, in_specs=[_ANY] * n, out_specs=[_ANY] * n + [pl.BlockSpec(memory_space=pltpu.VMEM)],
        out_shape=[SDS((NDEV,) + a.shape, a.dtype) for a in arrs] + [SDS((8, 128), f32)],
        scratch_shapes=[pltpu.SemaphoreType.DMA((n * (NDEV - 1),)), pltpu.SemaphoreType.DMA((n * (NDEV - 1),)),
                        pltpu.SemaphoreType.DMA((n,))],
        compiler_params=pltpu.CompilerParams(has_side_effects=True),
    )(*arrs)
    return list(res[:n]), res[n]


def _all_to_all(name, a):
    def body(a_ref, o_ref, send_sems, recv_sems, local_sem):
        me, peers = _me_and_peers()
        mine = pltpu.make_async_copy(a_ref.at[me], o_ref.at[me], local_sem)
        mine.start()
        copies = [pltpu.make_async_remote_copy(a_ref.at[pid], o_ref.at[me], send_sems.at[k], recv_sems.at[k],
                                               device_id=dev, device_id_type=pl.DeviceIdType.MESH)
                  for k, (dev, pid) in enumerate(peers)]
        for cp in copies:
            cp.start()
        for cp in copies:
            cp.wait()
        mine.wait()

    return pl.pallas_call(
        body, name=name, in_specs=[_ANY], out_specs=_ANY, out_shape=SDS(a.shape, a.dtype),
        scratch_shapes=[pltpu.SemaphoreType.DMA((NDEV - 1,)), pltpu.SemaphoreType.DMA((NDEV - 1,)), pltpu.SemaphoreType.DMA],
        compiler_params=pltpu.CompilerParams(has_side_effects=True),
    )(a)


_HBM = pl.BlockSpec(memory_space=pltpu.HBM)
_SEM = pl.BlockSpec(memory_space=pltpu.SEMAPHORE)
_FLOW = pltpu.SideEffectType.DATAFLOW_SIDE_EFFECTING


def _exchange_copies(srcs, lands, send_sems, recv_sems, scatter):
    me, peers = _me_and_peers()
    copies = []
    for a, (src, land) in enumerate(zip(srcs, lands)):
        for k, (dev, pid) in enumerate(peers):
            copies.append(pltpu.make_async_remote_copy(
                src.at[pid] if scatter else src, land.at[me], send_sems.at[a * (NDEV - 1) + k],
                recv_sems.at[a * (NDEV - 1) + k], device_id=dev, device_id_type=pl.DeviceIdType.MESH))
    return copies


def _exchange_start(name, srcs, lands, scatter):
    n = len(srcs)
    nsem = n * (NDEV - 1)

    def body(*refs):
        for cp in _exchange_copies(refs[:n], refs[n:2 * n], refs[2 * n], refs[2 * n + 1], scatter):
            cp.start()
        token = refs[-1]
        token[...] = jnp.zeros(token.shape, token.dtype)

    hbm = lambda a: pltpu.with_memory_space_constraint(a, pltpu.HBM)
    res = pl.pallas_call(
        body, name=name,
        out_shape=(pltpu.SemaphoreType.DMA((nsem,)), pltpu.SemaphoreType.DMA((nsem,)),
                   *[pltpu.HBM(a.shape, a.dtype) for a in srcs], *[pltpu.HBM(a.shape, a.dtype) for a in lands],
                   SDS((8, 128), f32)),
        in_specs=[_HBM] * (2 * n), out_specs=(_SEM, _SEM, *([_HBM] * (2 * n)), pl.BlockSpec(memory_space=pltpu.VMEM)),
        input_output_aliases={i: 2 + i for i in range(2 * n)},
        compiler_params=pltpu.CompilerParams(has_side_effects=_FLOW),
    )(*[hbm(a) for a in srcs], *[hbm(a) for a in lands])
    return res[0], res[1], list(res[2:2 + n]), list(res[2 + n:2 + 2 * n]), res[-1]


def _exchange_wait(name, handle, after, scatter):
    send_sems, recv_sems, srcs, lands, _ = handle
    n = len(srcs)

    def body(*refs):
        for cp in _exchange_copies(refs[:n], refs[n:2 * n], refs[2 * n], refs[2 * n + 1], scatter):
            cp.wait_send()
            cp.wait_recv()
        token = refs[-1]
        token[...] = jnp.zeros(token.shape, token.dtype)

    res = pl.pallas_call(
        body, name=name,
        out_shape=(*[pltpu.HBM(a.shape, a.dtype) for a in srcs + lands], SDS((8, 128), f32)),
        in_specs=[_HBM] * (2 * n) + [_SEM, _SEM, _ANY],
        out_specs=(*([_HBM] * (2 * n)), pl.BlockSpec(memory_space=pltpu.VMEM)),
        input_output_aliases={i: i for i in range(2 * n)},
        compiler_params=pltpu.CompilerParams(has_side_effects=_FLOW),
    )(*srcs, *lands, send_sems, recv_sems, after)
    return list(res[n:2 * n]), res[-1]


def _own_slot(a, me):
    return lax.dynamic_update_slice(jnp.zeros((NDEV,) + a.shape, a.dtype), a[None], (me,) + (0,) * a.ndim)


CB_Z = 3
CB_GM = 8
CB_SW = 10
CB_AB = 38


def _sw_pre_tiles(proj, cs, sn):
    return [(proj, SW_W, CB_SW + 3 * g + j) for g in range(3) for j in range(2)] + [(cs, SW_W, 0), (sn, SW_W, 0)]


def _layer_fwd(x, p, tabs, late_weights):
    cs, sn = tabs
    sh1, sc1, g1, sh2, sc2, g2 = (p["mod"][k] for k in range(6))
    (h1,) = _ew_fwd("modnorm1_fwd", _f_modnorm, [(x, D, 0)], [p["mixw"], sc1, sh1], [(D, bf16)], 512)
    proj = _mm("in_proj", h1, p["wi"], "nn", 1024, 1664, b_outer=True)
    y = _conv_fwd(proj, p["conv"], 256)
    qkv, gb = _ew_fwd("dn_act_fwd", _f_dn_act, [(y, CONV_W, 0), (proj, 128, CB_AB)], [p["alog"], p["dtb"]],
                      [(CONV_W, f32), (128, f32)], 256)
    u, w, inv = _dn_prep_fwd(qkv, gb)
    o, states = _dn_scan_fwd(qkv, gb, u, w)
    (ya,) = _ew_fwd("dn_out_fwd", _f_dn_out, [(o, DN_W, 0), (proj, DN_W, CB_Z)], [p["wn"]], [(DN_W, bf16)], 256)
    (yb,) = _ew_fwd("gm_fwd", _f_gm, [(proj, GM_W, CB_GM), (proj, GM_W, CB_GM + 1)],
                    [p["lng"], p["lnb"], p["ws"], p["bst"]], [(GM_W, bf16)], GM_C)
    qk = _ew_fwd("sw_pre_fwd", _f_sw_pre, _sw_pre_tiles(proj, cs, sn), [p["wq"], p["wk"]], [(SW_W, f32)] * 6, 512)
    ol = [[], []]
    for g, d in enumerate(SW_DIL):
        o_l, o_h, l_l, l_h = _sw_attn_fwd(qk[2 * g], qk[2 * g + 1], proj, CB_SW + 3 * g + 2, d)
        ol[0] += [o_l, l_l]
        ol[1] += [o_h, l_h]
    yc = [_ew_fwd(f"sw_merge_fwd_{h}", _f_sw_merge, [(a, HALF, 0) for a in ol[h]], [], [(HALF, bf16)], 512)[0]
          for h in range(2)]
    ycat = jnp.concatenate([ya, yb] + yc, axis=1)
    p.update(late_weights(ycat))
    m1 = _mm("out_proj", ycat, p["wo"], "nn", 1024, 1024)
    (x2,) = _ew_fwd("resid1_fwd", _f_resid, [(x, D, 0), (m1, D, 0)], [g1], [(D, f32)], 512)
    (h2,) = _ew_fwd("modnorm2_fwd", _f_modnorm, [(x2, D, 0)], [p["ffnw"], sc2, sh2], [(D, bf16)], 512)
    gu = _mm("ffn_in", h2, p["wfi"], "nn", 1024, 1408, bf16, b_outer=True)
    (act,) = _ew_fwd("swiglu_fwd", _f_swiglu, [(gu, 2 * FFN, 0)], [], [(FFN, bf16)], 256)
    m2 = _mm("ffn_out", act, p["wfo"], "nn", 1024, 1024)
    (x3,) = _ew_fwd("resid2_fwd", _f_resid, [(x2, D, 0), (m2, D, 0)], [g2], [(D, f32)], 512)
    res = dict(x=x, h1=h1, proj=proj, y=y, qkv=qkv, gb=gb, u=u, w=w, inv=inv, states=states, o=o, qk=list(qk), ol=ol,
               ycat=ycat, m1=m1, x2=x2, h2=h2, gu=gu, act=act, m2=m2)
    return x3, res


def _layer_bwd(dx3, p, r, tabs, ffn_grads_ready):
    cs, sn = tabs
    sh1, sc1, g1, sh2, sc2, g2 = (p["mod"][k] for k in range(6))
    proj = r["proj"]
    (dx2a, dm2), (dg2,) = _ew_bwd("resid2_bwd", _f_resid, [(r["x2"], D, 0), (r["m2"], D, 0)], [g2], [(dx3, D, 0)],
                                  [(0, f32), (1, bf16)], 512)
    dact = _mm("ffn_out_dx", dm2, p["wfo"], "nt", 512, FFN, bf16)
    dwfo = _mm("ffn_out_dw", r["act"], dm2, "tn", 256, 1024, bf16)
    (dgu_cat,), _ = _ew_bwd("swiglu_bwd", _f_swiglu, [(r["gu"], 2 * FFN, 0)], [], [(dact, FFN, 0)], [(0, bf16)], 256)
    dh2 = _mm("ffn_in_dx", dgu_cat, p["wfi"], "nt", 512, 1024)
    dwfi = _mm("ffn_in_dw", r["h2"], dgu_cat, "tn", 1024, 512, bf16)
    g1 = g1 + ffn_grads_ready(dwfi, dwfo)[0, 0]
    (dx2,), (dffnw, dsc2, dsh2) = _ew_bwd("modnorm2_bwd", _f_modnorm, [(r["x2"], D, 0)], [p["ffnw"], sc2, sh2],
                                          [(dh2, D, 0)], [(0, f32)], 512, adds=[dx2a])
    (dxa, dm1), (dg1,) = _ew_bwd("resid1_bwd", _f_resid, [(r["x"], D, 0), (r["m1"], D, 0)], [g1], [(dx2, D, 0)],
                                 [(0, f32), (1, bf16)], 512)
    dycat = _mm("out_proj_dx", dm1, p["wo"], "nt", 1024, 1024)
    dwo = _mm("out_proj_dw", r["ycat"], dm1, "tn", 1024, 512, bf16)
    dol = [_ew_bwd(f"sw_merge_bwd_{h}", _f_sw_merge, [(a, HALF, 0) for a in r["ol"][h]], [], [(dycat, HALF, 6 + h)],
                   [(k, f32) for k in range(6)], 512)[0] for h in range(2)]
    dqk, dvs = [], []
    for g, d in enumerate(SW_DIL):
        dq_l, dq_h, dk_l, dk_h, dv_l, dv_h = _sw_attn_bwd(
            r["qk"][2 * g], r["qk"][2 * g + 1], proj, CB_SW + 3 * g + 2,
            dol[0][2 * g], dol[1][2 * g], dol[0][2 * g + 1], dol[1][2 * g + 1], d)
        dqk += [[(dq_l, HALF, 0), (dq_h, HALF, 0)], [(dk_l, HALF, 0), (dk_h, HALF, 0)]]
        dvs += [dv_l.astype(bf16), dv_h.astype(bf16)]
    dqk_raw, (dwq, dwk) = _ew_bwd("sw_pre_bwd", _f_sw_pre, _sw_pre_tiles(proj, cs, sn), [p["wq"], p["wk"]],
                                  dqk, [(k, bf16) for k in range(6)], 512)
    (dgm_u, dgm_v), (dlng, dlnb, dws, dbst) = _ew_bwd(
        "gm_bwd", _f_gm, [(proj, GM_W, CB_GM), (proj, GM_W, CB_GM + 1)], [p["lng"], p["lnb"], p["ws"], p["bst"]],
        [(dycat, GM_W, 2)], [(0, bf16), (1, bf16)], GM_C)
    (do, dz), (dwn,) = _ew_bwd("dn_out_bwd", _f_dn_out, [(r["o"], DN_W, 0), (proj, DN_W, CB_Z)], [p["wn"]],
                               [(dycat, DN_W, 0)], [(0, f32), (1, bf16)], 256)
    dqk1, du, dw, dgb1 = _dn_scan_bwd(r["qkv"], r["gb"], r["u"], r["w"], r["states"], do)
    dqkv, dgb = _dn_prep_bwd(r["qkv"], r["gb"], r["inv"], du, dw, dqk1, dgb1)
    (dy, dab), (dalog, ddtb) = _ew_bwd("dn_act_bwd", _f_dn_act, [(r["y"], CONV_W, 0), (proj, 128, CB_AB)],
                                       [p["alog"], p["dtb"]], [(dqkv, CONV_W, 0), (dgb, 128, 0)],
                                       [(0, f32), (1, bf16)], 256)
    dxc, dconv = _conv_bwd(proj, dy, p["conv"], 256)
    dproj = jnp.concatenate([dxc, dz, dgm_u, dgm_v, dqk_raw[0], dqk_raw[1], dvs[0], dvs[1], dqk_raw[2], dqk_raw[3],
                             dvs[2], dvs[3], dqk_raw[4], dqk_raw[5], dvs[4], dvs[5], dab], axis=1)
    dh1 = _mm("in_proj_dx", dproj, p["wi"], "nt", 512, 1024)
    dwi = _mm("in_proj_dw", r["h1"], dproj, "tn", 1024, 384, bf16)
    (dx,), (dmixw, dsc1, dsh1) = _ew_bwd("modnorm1_bwd", _f_modnorm, [(r["x"], D, 0)], [p["mixw"], sc1, sh1],
                                         [(dh1, D, 0)], [(0, f32)], 512, adds=[dxa])
    grads = dict(wi=dwi, wo=dwo, wfi=dwfi, wfo=dwfo, conv=dconv, mixw=dmixw, ffnw=dffnw,
                 mod=jnp.stack([dsh1, dsc1, dg1, dsh2, dsc2, dg2]), alog=dalog, dtb=ddtb, wn=dwn, lng=dlng, lnb=dlnb,
                 ws=dws, bst=dbst, wq=dwq, wk=dwk)
    return dx, grads


def _rope_tables(t):
    inv = ROPE_THETA ** (-jnp.arange(0, ROPE_DIM, 2, dtype=f32) / ROPE_DIM)
    ang = jnp.arange(t, dtype=f32)[:, None] * inv[None, :]
    cos, sin = jnp.cos(ang), jnp.sin(ang)
    rest = SW_D - ROPE_DIM
    ch = jnp.concatenate([cos, cos, jnp.ones((t, rest), f32)], axis=1)
    sh = jnp.concatenate([sin, sin, jnp.zeros((t, rest), f32)], axis=1)
    return jnp.tile(ch, (1, SW_H)), jnp.tile(sh, (1, SW_H))


def _pad_last(a, n):
    return jnp.pad(a, [(0, 0)] * (a.ndim - 1) + [(0, n - a.shape[-1])])


def _scatter_cols(name, gs, cs, pad_to):
    r = gs[0].shape[0]
    parts = jnp.stack([jnp.transpose(_pad_last(g.reshape(r, NDEV, cs), pad_to), (1, 0, 2)) for g in gs], axis=1)
    return _all_to_all(name, parts).reshape(NDEV, len(gs) * r, pad_to)


_SMALL = ("b_mod", "mix_norm_w", "ffn_norm_w", "dn_a_log", "dn_dt_bias", "dn_out_norm_w", "gm_ln_g", "gm_ln_b",
          "gm_w_s", "gm_b_s", "sw_q_norm_w", "sw_k_norm_w")
PACK_ROWS = 800


def _pack_rows(a):
    return -(-a.size // 1024) * 8


def _pack(arrs):
    parts = [jnp.pad(a.reshape(-1), (0, _pack_rows(a) * 128 - a.size)).reshape(_pack_rows(a), 128) for a in arrs]
    rows = sum(p.shape[0] for p in parts)
    parts.append(jnp.zeros((-(-rows // PACK_ROWS) * PACK_ROWS - rows, 128), parts[0].dtype))
    return jnp.concatenate(parts, axis=0)


def _unpack(buf, like):
    out, off = [], 0
    for a in like:
        rows = _pack_rows(a)
        flat = buf[off:off + rows].reshape(-1)
        out.append((flat if flat.size == a.size else flat[:a.size]).reshape(a.shape))
        off += rows
    return out


def kernel(x, c, w_mod, b_mod, mix_norm_w, ffn_norm_w, w_in, w_out, dn_conv_w, dn_a_log, dn_dt_bias, dn_out_norm_w, gm_ln_g, gm_ln_b, gm_w_s, gm_b_s, sw_q_norm_w, sw_k_norm_w, w_ffn_in, w_ffn_out, loss_target, m_w_mod, m_b_mod, m_mix_norm_w, m_ffn_norm_w, m_w_in, m_w_out, m_dn_conv_w, m_dn_a_log, m_dn_dt_bias, m_dn_out_norm_w, m_gm_ln_g, m_gm_ln_b, m_gm_w_s, m_gm_b_s, m_sw_q_norm_w, m_sw_k_norm_w, m_w_ffn_in, m_w_ffn_out, v_w_mod, v_b_mod, v_mix_norm_w, v_ffn_norm_w, v_w_in, v_w_out, v_dn_conv_w, v_dn_a_log, v_dn_dt_bias, v_dn_out_norm_w, v_gm_ln_g, v_gm_ln_b, v_gm_w_s, v_gm_b_s, v_sw_q_norm_w, v_sw_k_norm_w, v_w_ffn_in, v_w_ffn_out):
    weights = dict(w_mod=w_mod, b_mod=b_mod, mix_norm_w=mix_norm_w, ffn_norm_w=ffn_norm_w, w_in=w_in, w_out=w_out,
                   dn_conv_w=dn_conv_w, dn_a_log=dn_a_log, dn_dt_bias=dn_dt_bias, dn_out_norm_w=dn_out_norm_w,
                   gm_ln_g=gm_ln_g, gm_ln_b=gm_ln_b, gm_w_s=gm_w_s, gm_b_s=gm_b_s, sw_q_norm_w=sw_q_norm_w,
                   sw_k_norm_w=sw_k_norm_w, w_ffn_in=w_ffn_in, w_ffn_out=w_ffn_out)
    mom = dict(w_mod=m_w_mod, b_mod=m_b_mod, mix_norm_w=m_mix_norm_w, ffn_norm_w=m_ffn_norm_w, w_in=m_w_in,
               w_out=m_w_out, dn_conv_w=m_dn_conv_w, dn_a_log=m_dn_a_log, dn_dt_bias=m_dn_dt_bias,
               dn_out_norm_w=m_dn_out_norm_w, gm_ln_g=m_gm_ln_g, gm_ln_b=m_gm_ln_b, gm_w_s=m_gm_w_s, gm_b_s=m_gm_b_s,
               sw_q_norm_w=m_sw_q_norm_w, sw_k_norm_w=m_sw_k_norm_w, w_ffn_in=m_w_ffn_in, w_ffn_out=m_w_ffn_out)
    var = dict(w_mod=v_w_mod, b_mod=v_b_mod, mix_norm_w=v_mix_norm_w, ffn_norm_w=v_ffn_norm_w, w_in=v_w_in,
               w_out=v_w_out, dn_conv_w=v_dn_conv_w, dn_a_log=v_dn_a_log, dn_dt_bias=v_dn_dt_bias,
               dn_out_norm_w=v_dn_out_norm_w, gm_ln_g=v_gm_ln_g, gm_ln_b=v_gm_ln_b, gm_w_s=v_gm_w_s, gm_b_s=v_gm_b_s,
               sw_q_norm_w=v_sw_q_norm_w, sw_k_norm_w=v_sw_k_norm_w, w_ffn_in=v_w_ffn_in, w_ffn_out=v_w_ffn_out)
    names = list(weights)
    xi, tgt = x[0], loss_target[0]
    t = xi.shape[0]
    nl = w_mod.shape[0]
    ax, ay, ac = (lax.axis_index(a) for a in AXES)
    me = 4 * ax + 2 * ay + ac
    mod_cs = w_mod.shape[2]

    shards_a = [_pad_last(w_in.astype(bf16), 640)]
    shards_b = [w_out.astype(bf16), _pad_last(w_ffn_in.astype(bf16), 768), w_ffn_out.astype(bf16)]

    def start_gather(l, which, tok):
        shards_l = [a[l] + tok.astype(bf16) for a in (shards_a if which == "a" else shards_b)]
        return _exchange_start(f"ag_start_{which}{l}", shards_l, [_own_slot(a, me) for a in shards_l], False)

    gather_a = start_gather(0, "a", jnp.zeros((), f32))

    c = c + gather_a[4][0, 0]
    (c_g, conv_g), _ = _all_gather("ag_c_conv", [jnp.broadcast_to(c, (NDEV, D)), _pad_last(dn_conv_w, 256)])
    c_all = c_g[:, 0, :]
    conv = jnp.transpose(conv_g, (1, 2, 0, 3))[..., :dn_conv_w.shape[2]].reshape(nl, CONV_K, CONV_W)
    b_cols = lax.dynamic_slice_in_dim(b_mod, me * mod_cs, mod_cs, axis=1)[:, None, :]
    modc = _mod_cols(c_all, w_mod, b_cols)
    mod_tx = jnp.pad(jnp.transpose(modc, (1, 0, 2)), ((0, 0), (0, 8 - nl), (0, 0)))
    mod_rx = _all_to_all("a2a_mod", mod_tx)[:, :nl]
    mod = jnp.transpose(mod_rx, (1, 0, 2)).reshape(nl, 6, 1, D)

    def w_in_full(landed):
        wi = jnp.transpose(landed[0], (1, 0, 2))[..., :w_in.shape[2]].reshape(D, IN_W)
        return jnp.concatenate([wi[:, :2048], wi[:, 2056:], wi[:, 2048:2056], jnp.zeros((D, IN_WA - IN_W), bf16)], axis=-1)

    def late_full(landed):
        go, gfi, gfo = landed
        wfi = jnp.transpose(gfi, (1, 0, 2))[..., :w_ffn_in.shape[2]].reshape(D, 2 * FFN)
        return dict(wo=go.reshape(D, D), wfi=wfi, wfo=gfo.reshape(FFN, D))

    pad128 = lambda a: _pad_last(a, 128)[:, None, :]
    params = dict(
        mod=mod, conv=conv,
        mixw=mix_norm_w[:, None, :], ffnw=ffn_norm_w[:, None, :], alog=pad128(dn_a_log), dtb=pad128(dn_dt_bias),
        wn=dn_out_norm_w[:, None, :], lng=gm_ln_g[:, None, :], lnb=gm_ln_b[:, None, :], ws=gm_w_s,
        bst=_pad_last(jnp.transpose(gm_b_s, (0, 2, 1)), 128),
        wq=jnp.pad(sw_q_norm_w[:, None, :], ((0, 0), (0, 7), (0, 128 - SW_D))),
        wk=jnp.pad(sw_k_norm_w[:, None, :], ((0, 0), (0, 7), (0, 128 - SW_D))))
    tabs = _rope_tables(t)

    layer_p, res = [], []
    xc, after, gather_b = xi, mod, None
    for l in range(nl):
        landed_a, tok = _exchange_wait(f"ag_wait_a{l}", gather_a, after, False)
        p = {k: v[l] for k, v in params.items()}
        p["wi"] = w_in_full(landed_a)
        hb = gather_b if l else start_gather(0, "b", tok[0, 0])
        tok = tok if l else hb[4]
        if l + 1 < nl:
            gather_a = start_gather(l + 1, "a", tok[0, 0])
            gather_b = start_gather(l + 1, "b", gather_a[4][0, 0])
            tok = gather_b[4]
        p["mod"] = p["mod"] + tok[0, 0]
        late = lambda ycat, hb=hb, l=l: late_full(_exchange_wait(f"ag_wait_b{l}", hb, ycat, False)[0])
        layer_p.append(p)
        xc, r = _layer_fwd(xc, p, tabs, late)
        res.append(r)
        after = xc
    dy, lpart = _loss_head(xc, tgt, 512)
    loss = lax.psum(lpart[0, 0], AXES)

    slot = lax.broadcasted_iota(jnp.int32, (NDEV, 1, 1), 0)

    def start_scatter(name, parts):
        return _exchange_start(name, parts, [jnp.where(slot == me, a, jnp.zeros_like(a)) for a in parts], True)

    dxi, gl, h_ffn, h_mix = dy, [None] * nl, [None] * nl, [None] * nl
    for l in reversed(range(nl)):
        def ffn_ready(dwfi, dwfo, l=l):
            h_ffn[l] = start_scatter(f"a2a_start_f{l}", [
                jnp.transpose(_pad_last(dwfi.reshape(D, NDEV, w_ffn_in.shape[2]), 768), (1, 0, 2)),
                dwfo.reshape(NDEV, w_ffn_out.shape[1], D)])
            return h_ffn[l][4]

        dxi, gl[l] = _layer_bwd(dxi, layer_p[l], res[l], tabs, ffn_ready)
        d = gl[l]["wi"]
        d = jnp.concatenate([d[:, :2048], d[:, 4864:4872], d[:, 2048:4864]], axis=-1)
        mix_parts = [jnp.transpose(_pad_last(d.reshape(D, NDEV, w_in.shape[2]), 640), (1, 0, 2)),
                     gl[l]["wo"].reshape(NDEV, w_out.shape[1], D)]
        if l > 0:
            h_mix[l] = start_scatter(f"a2a_start_m{l}", mix_parts)
            layer_p[l - 1]["mod"] = layer_p[l - 1]["mod"] + h_mix[l][4][0, 0]
    g = {k: jnp.stack([gl[l][k] for l in range(nl)]) for k in gl[0] if k not in ("wi", "wo", "wfi", "wfo")}

    dmod = g["mod"].reshape(nl, 6 * D)
    small_g = dict(b_mod=dmod, mix_norm_w=g["mixw"][:, 0], ffn_norm_w=g["ffnw"][:, 0], dn_a_log=g["alog"][:, 0, :DN_H],
                   dn_dt_bias=g["dtb"][:, 0, :DN_H], dn_out_norm_w=g["wn"][:, 0], gm_ln_g=g["lng"][:, 0],
                   gm_ln_b=g["lnb"][:, 0], gm_w_s=g["ws"], gm_b_s=jnp.transpose(g["bst"][:, :, :GM_G], (0, 2, 1)),
                   sw_q_norm_w=g["wq"][:, 0, :SW_D], sw_k_norm_w=g["wk"][:, 0, :SW_D])
    (parts,), tok = _all_gather("ag_small_grads", [_pack([small_g[n] for n in _SMALL])])
    h_mix[0] = start_scatter("a2a_start_m0", [a + tok[0, 0].astype(bf16) for a in mix_parts])
    like = [weights[n] for n in _SMALL]
    sm = _adamw("adamw_small", _pack(like) + h_mix[0][4][0, 0], _pack([mom[n] for n in _SMALL]),
                _pack([var[n] for n in _SMALL]), parts, PACK_ROWS)
    out = {n: vals for n, vals in zip(_SMALL, zip(*[_unpack(b, like) for b in sm]))}

    dmod_all = parts[:, :nl * 6 * D // 128, :].reshape(NDEV, nl, 6 * D)
    dmod_cols = jnp.transpose(lax.dynamic_slice_in_dim(dmod_all, me * mod_cs, mod_cs, axis=2), (1, 0, 2))
    gw_mod = _wmod_grad(c_all, dmod_cols).reshape(1, nl * D, mod_cs)

    def adamw_sharded(n, gp, cols, padc):
        shp = weights[n].shape
        rows = gp.shape[1]
        prep = lambda a: _pad_last(a.reshape(rows, cols), padc)
        tr = 256 if rows % 256 == 0 else rows // 4 if rows % 32 == 0 else rows
        res4 = _adamw("adamw_" + n, prep(weights[n]), prep(mom[n]), prep(var[n]), gp, tr)
        out[n] = tuple(a[:, :cols].reshape(shp) for a in res4)

    adamw_sharded("w_mod", gw_mod, mod_cs, mod_cs)
    adamw_sharded("dn_conv_w", _scatter_cols("a2a_conv", [gl[l]["conv"] for l in range(nl)], dn_conv_w.shape[2], 256),
                  dn_conv_w.shape[2], 256)
    after, land_f, land_m = sm[0], [None] * nl, [None] * nl
    for l in reversed(range(nl)):
        land_f[l], after = _exchange_wait(f"a2a_wait_f{l}", h_ffn[l], after, True)
        if l > 0:
            land_m[l], after = _exchange_wait(f"a2a_wait_m{l}", h_mix[l], after, True)
    adamw_sharded("w_ffn_in", jnp.concatenate([land_f[l][0] for l in range(nl)], axis=1), w_ffn_in.shape[2], 768)
    adamw_sharded("w_ffn_out", jnp.concatenate([land_f[l][1] for l in range(nl)], axis=1), D, D)
    both = out["w_ffn_in"][1][0, 0, :1] + out["w_ffn_out"][1][0, 0, :1]
    land_m[0], _ = _exchange_wait("a2a_wait_m0", h_mix[0], both, True)
    adamw_sharded("w_in", jnp.concatenate([land_m[l][0] for l in range(nl)], axis=1), w_in.shape[2], 640)
    adamw_sharded("w_out", jnp.concatenate([land_m[l][1] for l in range(nl)], axis=1), D, D)

    return (loss, dxi[None], *[out[n][0] for n in names], *[out[n][1] for n in names],
            *[out[n][2] for n in names], *[out[n][3] for n in names])
```

```python
import functools
import math

import jax
import jax.numpy as jnp
from jax import lax
from jax.experimental import pallas as pl
from jax.experimental.pallas import tpu as pltpu

f32 = jnp.float32
bf16 = jnp.bfloat16
HI = lax.Precision.HIGH
AXES = ("x", "y", "c")
NDEV = 8
SDS = jax.ShapeDtypeStruct

D = 1024
NORM_EPS = 1e-6
DN_W, DN_H, DN_D, DN_C = 512, 4, 128, 64
GM_W, GM_G, GM_C = 256, 4, 128
SW_W, SW_H, SW_D, SW_B = 256, 4, 64, 128
SW_DIL = (1, 4, 16)
SW_SPAN = 128
ROPE_DIM, ROPE_THETA = 16, 500000.0
IN_W = 4872
IN_WA = 4992
FFN = 2816
ADAM_LR, ADAM_B1, ADAM_B2, ADAM_EPS, ADAM_WD, ADAM_STEP = 0.001, 0.9, 0.999, 1e-08, 0.01, 10

VMEM_LIMIT = 52 * 1024 * 1024


def _cp(sem=None):
    return pltpu.CompilerParams(vmem_limit_bytes=VMEM_LIMIT, dimension_semantics=sem)


_DIMS = {"nn": (((1,), (0,)), ((), ())), "nt": (((1,), (1,)), ((), ())), "tn": (((0,), (0,)), ((), ()))}


def _raw_dot(a, b, mode, hi):
    if hi:
        return lax.dot_general(a, b, _DIMS[mode], precision=HI, preferred_element_type=f32)
    return lax.dot_general(a.astype(bf16), b.astype(bf16), _DIMS[mode], preferred_element_type=f32)


@functools.partial(jax.custom_vjp, nondiff_argnums=(2, 3))
def _dot(a, b, mode, hi):
    return _raw_dot(a, b, mode, hi)


def _dot_fwd(a, b, mode, hi):
    return _raw_dot(a, b, mode, hi), (a, b)


def _dot_bwd(mode, hi, res, g):
    a, b = res
    if mode == "nn":
        return _raw_dot(g, b, "nt", hi), _raw_dot(a, g, "tn", hi)
    if mode == "nt":
        return _raw_dot(g, b, "nn", hi), _raw_dot(g, a, "tn", hi)
    return _raw_dot(b, g, "nt", hi), _raw_dot(a, g, "nn", hi)


_dot.defvjp(_dot_fwd, _dot_bwd)


def _iota(shape, dim):
    return lax.broadcasted_iota(jnp.int32, shape, dim)


def _f_modnorm(x, w, scale, shift):
    y = x * lax.rsqrt(jnp.mean(x * x, axis=-1, keepdims=True) + NORM_EPS) * w
    return (y * (1.0 + scale) + shift,)


def _f_resid(x, m, gate):
    return (x + gate * m,)


def _f_swiglu(gu):
    return (jax.nn.silu(gu[:, :FFN]) * gu[:, FFN:],)


def _softplus(x):
    return jnp.maximum(x, 0.0) + jnp.log1p(jnp.exp(-jnp.abs(x)))


def _f_dn_act(y, ab, alog, dtb):
    c = jax.nn.silu(y)
    parts = []
    for j in range(3 * DN_H):
        p = c[:, j * DN_D:(j + 1) * DN_D]
        if j < 2 * DN_H:
            p = p * lax.rsqrt(jnp.sum(p * p, axis=-1, keepdims=True) + NORM_EPS)
        parts.append(p)
    lane = _iota(ab.shape, 1)
    g = -jnp.exp(alog) * _softplus(ab + dtb)
    beta = jax.nn.sigmoid(ab)
    gb = jnp.where(lane < DN_H, g, jnp.where(lane < 2 * DN_H, beta, 0.0))
    return jnp.concatenate(parts, axis=1), gb


def _f_dn_out(o, z, wn):
    parts = []
    for h in range(DN_H):
        oh = o[:, h * DN_D:(h + 1) * DN_D]
        zh = z[:, h * DN_D:(h + 1) * DN_D]
        n = oh * lax.rsqrt(jnp.mean(oh * oh, axis=-1, keepdims=True) + NORM_EPS) * wn
        parts.append(n * jax.nn.silu(zh))
    return (jnp.concatenate(parts, axis=1),)


def _gelu(x):
    return 0.5 * x * (1.0 + lax.erf(x * (1.0 / math.sqrt(2.0))))


def _f_gm(u_raw, v_raw, ln_g, ln_b, w_s, b_st):
    u = _gelu(u_raw)
    v = _gelu(v_raw)
    mu = jnp.mean(v, axis=-1, keepdims=True)
    vc = v - mu
    var = jnp.mean(vc * vc, axis=-1, keepdims=True)
    v = vc * lax.rsqrt(var + NORM_EPS) * ln_g + ln_b
    r = _iota((GM_C, GM_C), 0)
    c = _iota((GM_C, GM_C), 1)
    grp = _iota((GM_C, GM_W), 1) // (GM_W // GM_G)
    expand = jnp.where(_iota((GM_C, GM_W), 0) == grp, 1.0, 0.0)
    sv = _dot(b_st, expand, "nn", True)
    for g in range(GM_G):
        wg = jnp.where(r >= c, w_s[g], 0.0)
        sv = sv + jnp.where(grp == g, _dot(wg, v, "nn", False), 0.0)
    return (u * sv,)


def _head_lanes(shape):
    return _iota(shape, 1) // SW_D


def _f_sw_pre(q0, k0, q1, k1, q2, k2, cs, sn, wq, wk):
    r = _iota((SW_W, SW_W), 0)
    c = _iota((SW_W, SW_W), 1)
    same_head = jnp.where(r // SW_D == c // SW_D, 1.0, 0.0)
    hc = c % SW_D
    half = ROPE_DIM // 2
    perm = jnp.where((hc < half) & (r == c + half), -1.0, jnp.where((hc >= half) & (hc < ROPE_DIM) & (r == c - half), 1.0, 0.0))
    tile = jnp.where((_iota((128, SW_W), 1) % SW_D == _iota((128, SW_W), 0)) & (_iota((128, SW_W), 0) < SW_D), 1.0, 0.0)
    wq_full = _dot(wq, tile, "nn", True)[0:1, :]
    wk_full = _dot(wk, tile, "nn", True)[0:1, :]

    def one(t, w):
        ms = _dot(t * t, same_head, "nn", False) * (1.0 / SW_D)
        n = t * lax.rsqrt(ms + NORM_EPS) * w
        return n * cs + _dot(n, perm, "nn", False) * sn

    return one(q0, wq_full), one(k0, wk_full), one(q1, wq_full), one(k1, wk_full), one(q2, wq_full), one(k2, wk_full)


def _f_sw_merge(o0, l0, o1, l1, o2, l2):
    m = jnp.maximum(jnp.maximum(l0, l1), l2)
    e0, e1, e2 = jnp.exp(l0 - m), jnp.exp(l1 - m), jnp.exp(l2 - m)
    return ((e0 * o0 + e1 * o1 + e2 * o2) / (e0 + e1 + e2),)


def _f_attn(q, kp, kc, vp, vc, has_prev):
    kk = jnp.concatenate([kp, kc], axis=0)
    vv = jnp.concatenate([vp, vc], axis=0)
    i = _iota((SW_B, 2 * SW_B), 0)
    j = _iota((SW_B, 2 * SW_B), 1)
    dist = i + SW_B - j
    valid = (dist >= 0) & (dist <= SW_SPAN) & ((j >= SW_B) | has_prev)
    hl = _head_lanes(q.shape)
    heads = range(SW_H)
    ss = [_dot(jnp.where(hl == h, q, 0.0), kk, "nt", False) * (SW_D ** -0.5) for h in heads]
    ss = [jnp.where(valid, s, -1e30) for s in ss]
    ms = [jnp.max(s, axis=-1, keepdims=True) for s in ss]
    ps = [jnp.where(valid, jnp.exp(s - m), 0.0) for s, m in zip(ss, ms)]
    ls = [jnp.sum(p, axis=-1, keepdims=True) for p in ps]
    ohs = [_dot(p, vv, "nn", False) * (1.0 / l) for p, l in zip(ps, ls)]
    o = jnp.zeros(q.shape, f32)
    lse = jnp.zeros(q.shape, f32)
    for h in heads:
        o = o + jnp.where(hl == h, ohs[h], 0.0)
        lse = lse + jnp.where(hl == h, ms[h] + jnp.log(ls[h]), 0.0)
    return o, lse


def _lane_col(tile, lane_idx):
    return jnp.sum(jnp.where(_iota(tile.shape, 1) == lane_idx, tile, 0.0), axis=1, keepdims=True)


def _chunk_decays(gbv):
    n = gbv.shape[0]
    r = _iota((n, n), 0)
    c = _iota((n, n), 1)
    gc_all = _dot(jnp.where(r >= c, 1.0, 0.0), gbv, "nn", True)
    gc_rows = gc_all.T
    out = []
    for h in range(DN_H):
        gcol = _lane_col(gc_all, h)
        diff = jnp.where(r >= c, gcol - gc_rows[h:h + 1, :], 0.0)
        out.append((gcol, jnp.where(r >= c, jnp.exp(diff), 0.0)))
    return out, r, c


def _f_dn_pre(ks, vs, gbvs):
    decs, betas = [], []
    for gbv in gbvs:
        d, r, c = _chunk_decays(gbv)
        decs += d
        betas += [_lane_col(gbv, DN_H + h) for h in range(DN_H)]
    kbs = [k * b for k, b in zip(ks, betas)]
    grams = [_dot(kb, k, "nt", True) for kb, k in zip(kbs, ks)]
    mats = [jnp.where(r > c, g * dec[1], 0.0) for g, dec in zip(grams, decs)]
    rhss = [jnp.concatenate([v * b, kb * jnp.exp(dec[0])], axis=1) for v, b, kb, dec in zip(vs, betas, kbs, decs)]
    return mats, rhss


def _inv_unit_lower(mats):
    n = mats[0].shape[0]
    eye = jnp.where(_iota((n, n), 0) == _iota((n, n), 1), 1.0, 0.0)
    invs = [eye - a for a in mats]
    pws = list(mats)
    for _ in range(n.bit_length() - 2):
        pws = [_raw_dot(p, p, "nn", True) for p in pws]
        invs = [_raw_dot(i, eye + p, "nn", True) for i, p in zip(invs, pws)]
    return invs


def _f_dn_scan(states, qs, ks, us, ws, gbv):
    decs, _, _ = _chunk_decays(gbv)
    n = gbv.shape[0]
    last = _iota((n, 1), 0) == n - 1
    g_last = [jnp.sum(jnp.where(last, gc, 0.0), axis=0, keepdims=True) for gc, _ in decs]
    qs = [q * (DN_D ** -0.5) for q in qs]
    a_qk = [_dot(q, k, "nt", False) * dec[1] for q, k, dec in zip(qs, ks, decs)]
    q_dec = [q * jnp.exp(dec[0]) for q, dec in zip(qs, decs)]
    k_dec = [k * jnp.exp(gl - dec[0]) for k, gl, dec in zip(ks, g_last, decs)]
    ws_ = [_dot(w, s, "nn", False) for w, s in zip(ws, states)]
    o_st = [_dot(qd, s, "nn", False) for qd, s in zip(q_dec, states)]
    v_new = [u - x for u, x in zip(us, ws_)]
    o_in = [_dot(a, vn, "nn", False) for a, vn in zip(a_qk, v_new)]
    upd = [_dot(kd, vn, "tn", False) for kd, vn in zip(k_dec, v_new)]
    outs = [x + y for x, y in zip(o_st, o_in)]
    new_states = [s * jnp.exp(gl) + x for s, gl, x in zip(states, g_last, upd)]
    return outs, new_states


def _cspec(tt, w, cb):
    return pl.BlockSpec((tt, w), lambda i, cb=cb: (i, cb))


def _pspec(shape):
    nd = len(shape)
    return pl.BlockSpec(tuple(shape), lambda i, nd=nd: (0,) * nd)


def _ew_fwd(name, f, tiled, params, outs, tt):
    t = tiled[0][0].shape[0]
    nt, npar = len(tiled), len(params)

    def body(*refs):
        tv = [r[...].astype(f32) for r in refs[:nt]]
        pv = [r[...] for r in refs[nt:nt + npar]]
        res = f(*tv, *pv)
        for o, r in zip(refs[nt + npar:], res):
            o[...] = r.astype(o.dtype)

    res = pl.pallas_call(
        body, grid=(t // tt,), name=name,
        in_specs=[_cspec(tt, w, cb) for _, w, cb in tiled] + [_pspec(p.shape) for p in params],
        out_specs=[_cspec(tt, w, 0) for w, _ in outs],
        out_shape=[SDS((t, w), dt) for w, dt in outs],
        compiler_params=_cp(("arbitrary",)),
    )(*[a for a, _, _ in tiled], *params)
    return res


def _ew_bwd(name, f, tiled, params, cots, diff, tt, adds=None):
    t = tiled[0][0].shape[0]
    cots = [c if isinstance(c, list) else [c] for c in cots]
    pieces = [pc for c in cots for pc in c]
    nt, npar, nc, nd = len(tiled), len(params), len(pieces), len(diff)
    adds = adds or [None] * nd
    add_arrs = [a for a in adds if a is not None]
    na = len(add_arrs)
    dwidth = [tiled[k][1] for k, _ in diff]

    def body(*refs):
        tin = refs[:nt]
        pin = refs[nt:nt + npar]
        cin = refs[nt + npar:nt + npar + nc]
        ain = list(refs[nt + npar + nc:nt + npar + nc + na])
        dts = refs[nt + npar + nc + na:nt + npar + nc + na + nd]
        dps = refs[nt + npar + nc + na + nd:]
        tv = [r[...].astype(f32) for r in tin]
        pv = [r[...] for r in pin]

        def g(*dv):
            full = list(tv)
            for n_, (k, _) in enumerate(diff):
                full[k] = dv[n_]
            return tuple(f(*full, *dv[nd:]))

        _, vjp = jax.vjp(g, *[tv[k] for k, _ in diff], *pv)
        cin = list(cin)
        cvals = [jnp.concatenate([cin.pop(0)[...].astype(f32) for _ in c], axis=1) if len(c) > 1
                 else cin.pop(0)[...].astype(f32) for c in cots]
        grads = vjp(tuple(cvals))
        for n_ in range(nd):
            val = grads[n_]
            if adds[n_] is not None:
                val = val + ain.pop(0)[...].astype(f32)
            dts[n_][...] = val.astype(dts[n_].dtype)

        @pl.when(pl.program_id(0) == 0)
        def _():
            for r in dps:
                r[...] = jnp.zeros(r.shape, f32)

        for r, gp in zip(dps, grads[nd:]):
            r[...] += gp

    res = pl.pallas_call(
        body, grid=(t // tt,), name=name,
        in_specs=[_cspec(tt, w, cb) for _, w, cb in tiled] + [_pspec(p.shape) for p in params]
        + [_cspec(tt, w, cb) for _, w, cb in pieces] + [_cspec(tt, a.shape[1], 0) for a in add_arrs],
        out_specs=[_cspec(tt, w, 0) for w in dwidth] + [_pspec(p.shape) for p in params],
        out_shape=[SDS((t, w), dt) for w, (_, dt) in zip(dwidth, diff)] + [SDS(p.shape, f32) for p in params],
        compiler_params=_cp(("arbitrary",)),
    )(*[a for a, _, _ in tiled], *params, *[a for a, _, _ in pieces], *add_arrs)
    return res[:nd], res[nd:]


def _mm(name, a, b, mode, tm, tn, out_dtype=f32, b_outer=False):
    ij = (lambda g0, g1: (g1, g0)) if b_outer else (lambda g0, g1: (g0, g1))
    if mode == "nn":
        (m, k), (k2, n) = a.shape, b.shape
        a_spec = pl.BlockSpec((tm, k), lambda g0, g1: (ij(g0, g1)[0], 0))
        b_spec = pl.BlockSpec((k, tn), lambda g0, g1: (0, ij(g0, g1)[1]))
    elif mode == "nt":
        (m, k), (n, k2) = a.shape, b.shape
        a_spec = pl.BlockSpec((tm, k), lambda g0, g1: (ij(g0, g1)[0], 0))
        b_spec = pl.BlockSpec((tn, k), lambda g0, g1: (ij(g0, g1)[1], 0))
    else:
        (k, m), (k2, n) = a.shape, b.shape
        a_spec = pl.BlockSpec((k, tm), lambda g0, g1: (0, ij(g0, g1)[0]))
        b_spec = pl.BlockSpec((k, tn), lambda g0, g1: (0, ij(g0, g1)[1]))
    assert k == k2 and m % tm == 0 and n % tn == 0, (name, a.shape, b.shape, mode)
    assert a.dtype == bf16 and b.dtype == bf16, name

    def body(a_ref, b_ref, o_ref):
        o_ref[...] = lax.dot_general(a_ref[...], b_ref[...], _DIMS[mode], preferred_element_type=f32).astype(o_ref.dtype)

    return pl.pallas_call(
        body, grid=(n // tn, m // tm) if b_outer else (m // tm, n // tn), name=name,
        in_specs=[a_spec, b_spec], out_specs=pl.BlockSpec((tm, tn), lambda g0, g1: ij(g0, g1)),
        out_shape=SDS((m, n), out_dtype), compiler_params=_cp(("parallel", "parallel")),
    )(a, b)


CONV_K = 4
CONV_W = 3 * DN_W
HALO = 8


def _conv_fwd(proj, w, tt):
    t = proj.shape[0]
    nb8 = tt // HALO

    def body(x_ref, h_ref, w_ref, y_ref, xe):
        i = pl.program_id(0)
        xe[0:HALO, :] = jnp.where(i == 0, 0.0, h_ref[...])
        xe[HALO:, :] = x_ref[...]
        wv = w_ref[...]
        acc = jnp.zeros((tt, CONV_W), f32)
        for k in range(CONV_K):
            acc = acc + wv[k:k + 1, :] * xe[pl.ds(HALO - (CONV_K - 1) + k, tt), :]
        y_ref[...] = acc

    return pl.pallas_call(
        body, grid=(t // tt,), name="conv_fwd",
        in_specs=[pl.BlockSpec((tt, CONV_W), lambda i: (i, 0)),
                  pl.BlockSpec((HALO, CONV_W), lambda i: (jnp.maximum(i * nb8 - 1, 0), 0)),
                  _pspec(w.shape)],
        out_specs=pl.BlockSpec((tt, CONV_W), lambda i: (i, 0)),
        out_shape=SDS((t, CONV_W), f32),
        scratch_shapes=[pltpu.VMEM((tt + HALO, CONV_W), f32)],
        compiler_params=_cp(("arbitrary",)),
    )(proj, proj, w)


def _conv_bwd(proj, dy, w, tt):
    t = proj.shape[0]
    nb8 = tt // HALO
    last8 = t // HALO - 1
    nsteps = t // tt

    def body(x_ref, h_ref, dy_ref, n_ref, w_ref, dx_ref, dw_ref, xe, dye):
        i = pl.program_id(0)
        xe[0:HALO, :] = jnp.where(i == 0, 0.0, h_ref[...])
        xe[HALO:, :] = x_ref[...]
        dye[0:tt, :] = dy_ref[...]
        dye[tt:, :] = jnp.where(i == nsteps - 1, 0.0, n_ref[...])
        wv = w_ref[...]
        dyv = dy_ref[...]
        acc = jnp.zeros((tt, CONV_W), f32)

        @pl.when(i == 0)
        def _():
            dw_ref[...] = jnp.zeros(dw_ref.shape, f32)

        for k in range(CONV_K):
            acc = acc + wv[k:k + 1, :] * dye[pl.ds(CONV_K - 1 - k, tt), :]
            dw_ref[k:k + 1, :] += jnp.sum(dyv * xe[pl.ds(HALO - (CONV_K - 1) + k, tt), :], axis=0, keepdims=True)
        dx_ref[...] = acc.astype(dx_ref.dtype)

    return pl.pallas_call(
        body, grid=(nsteps,), name="conv_bwd",
        in_specs=[pl.BlockSpec((tt, CONV_W), lambda i: (i, 0)),
                  pl.BlockSpec((HALO, CONV_W), lambda i: (jnp.maximum(i * nb8 - 1, 0), 0)),
                  pl.BlockSpec((tt, CONV_W), lambda i: (i, 0)),
                  pl.BlockSpec((HALO, CONV_W), lambda i: (jnp.minimum((i + 1) * nb8, last8), 0)),
                  _pspec(w.shape)],
        out_specs=[pl.BlockSpec((tt, CONV_W), lambda i: (i, 0)), _pspec(w.shape)],
        out_shape=[SDS((t, CONV_W), bf16), SDS(w.shape, f32)],
        scratch_shapes=[pltpu.VMEM((tt + HALO, CONV_W), f32), pltpu.VMEM((tt + HALO, CONV_W), f32)],
        compiler_params=_cp(("arbitrary",)),
    )(proj, proj, dy, dy, w)


PREP_CHUNKS = 4


def _head_cols(part, h):
    return slice(part * DN_W + h * DN_D, part * DN_W + (h + 1) * DN_D)


def _prep_operands(qkv_ref, gb_ref):
    inst = [(slice(ch * DN_C, (ch + 1) * DN_C), h) for ch in range(PREP_CHUNKS) for h in range(DN_H)]
    ks = [qkv_ref[rs, _head_cols(1, h)] for rs, h in inst]
    vs = [qkv_ref[rs, _head_cols(2, h)] for rs, h in inst]
    gbvs = [gb_ref[ch * DN_C:(ch + 1) * DN_C, :] for ch in range(PREP_CHUNKS)]
    return inst, ks, vs, gbvs


def _dn_prep_fwd(qkv, gb):
    t = qkv.shape[0]
    rows = PREP_CHUNKS * DN_C

    def body(qkv_ref, gb_ref, u_ref, w_ref, inv_ref):
        inv_ref[...] = jnp.zeros(inv_ref.shape, f32)
        inst, ks, vs, gbvs = _prep_operands(qkv_ref, gb_ref)
        mats, rhss = _f_dn_pre(ks, vs, gbvs)
        invs = _inv_unit_lower(mats)
        uws = [_raw_dot(inv, rhs, "nn", True) for inv, rhs in zip(invs, rhss)]
        for (rs, h), inv, uw in zip(inst, invs, uws):
            u_ref[rs, _head_cols(0, h)] = uw[:, :DN_D]
            w_ref[rs, _head_cols(0, h)] = uw[:, DN_D:]
            inv_ref[rs, h * DN_D:h * DN_D + DN_C] = inv

    return pl.pallas_call(
        body, grid=(t // rows,), name="dn_prep_fwd",
        in_specs=[pl.BlockSpec((rows, CONV_W), lambda i: (i, 0)), pl.BlockSpec((rows, 128), lambda i: (i, 0))],
        out_specs=[pl.BlockSpec((rows, DN_W), lambda i: (i, 0))] * 3,
        out_shape=[SDS((t, DN_W), f32)] * 3,
        compiler_params=_cp(("arbitrary",)),
    )(qkv, gb)


def _dn_prep_bwd(qkv, gb, inv_all, du, dw, dqk1, dgb1):
    t = qkv.shape[0]
    rows = PREP_CHUNKS * DN_C

    def body(qkv_ref, gb_ref, inv_ref, du_ref, dw_ref, dqk1_ref, dgb1_ref, dqkv_ref, dgb_ref):
        inst, ks, vs, gbvs = _prep_operands(qkv_ref, gb_ref)
        (_, rhss), vjp = jax.vjp(_f_dn_pre, ks, vs, gbvs)
        invs = [inv_ref[rs, h * DN_D:h * DN_D + DN_C] for rs, h in inst]
        dxs = [jnp.concatenate([du_ref[rs, _head_cols(0, h)], dw_ref[rs, _head_cols(0, h)]], axis=1) for rs, h in inst]
        uws = [_raw_dot(inv, rhs, "nn", True) for inv, rhs in zip(invs, rhss)]
        drhss = [_raw_dot(inv, dx, "tn", True) for inv, dx in zip(invs, dxs)]
        das = [-_raw_dot(dr, uw, "nt", True) for dr, uw in zip(drhss, uws)]
        dks, dvs, dgbvs = vjp((das, drhss))
        for (rs, h), dk, dv in zip(inst, dks, dvs):
            dqkv_ref[rs, _head_cols(0, h)] = dqk1_ref[rs, _head_cols(0, h)]
            dqkv_ref[rs, _head_cols(1, h)] = dk + dqk1_ref[rs, _head_cols(1, h)]
            dqkv_ref[rs, _head_cols(2, h)] = dv
        for ch, dgbv in enumerate(dgbvs):
            rs = slice(ch * DN_C, (ch + 1) * DN_C)
            dgb_ref[rs, :] = dgbv + dgb1_ref[rs, :]

    return pl.pallas_call(
        body, grid=(t // rows,), name="dn_prep_bwd",
        in_specs=[pl.BlockSpec((rows, CONV_W), lambda i: (i, 0)), pl.BlockSpec((rows, 128), lambda i: (i, 0)),
                  pl.BlockSpec((rows, DN_W), lambda i: (i, 0)),
                  pl.BlockSpec((rows, DN_W), lambda i: (i, 0)), pl.BlockSpec((rows, DN_W), lambda i: (i, 0)),
                  pl.BlockSpec((rows, 2 * DN_W), lambda i: (i, 0)), pl.BlockSpec((rows, 128), lambda i: (i, 0))],
        out_specs=[pl.BlockSpec((rows, CONV_W), lambda i: (i, 0)), pl.BlockSpec((rows, 128), lambda i: (i, 0))],
        out_shape=[SDS((t, CONV_W), f32), SDS((t, 128), f32)],
        compiler_params=_cp(("arbitrary",)),
    )(qkv, gb, inv_all, du, dw, dqk1, dgb1)


def _dn_scan_fwd(qkv, gb, u, w):
    t = qkv.shape[0]
    n = t // DN_C
    heads = range(DN_H)

    def body(qkv_ref, gb_ref, u_ref, w_ref, o_ref, s_ref, state):
        @pl.when(pl.program_id(0) == 0)
        def _():
            state[...] = jnp.zeros(state.shape, f32)

        states = [state[h] for h in heads]
        for h in heads:
            s_ref[0, h] = states[h]
        outs, new = _f_dn_scan(states, [qkv_ref[:, _head_cols(0, h)] for h in heads],
                               [qkv_ref[:, _head_cols(1, h)] for h in heads],
                               [u_ref[:, _head_cols(0, h)] for h in heads],
                               [w_ref[:, _head_cols(0, h)] for h in heads], gb_ref[...])
        for h in heads:
            o_ref[:, _head_cols(0, h)] = outs[h]
            state[h] = new[h]

    return pl.pallas_call(
        body, grid=(n,), name="dn_scan_fwd",
        in_specs=[pl.BlockSpec((DN_C, 2 * DN_W), lambda i: (i, 0)), pl.BlockSpec((DN_C, 128), lambda i: (i, 0)),
                  pl.BlockSpec((DN_C, DN_W), lambda i: (i, 0)), pl.BlockSpec((DN_C, DN_W), lambda i: (i, 0))],
        out_specs=[pl.BlockSpec((DN_C, DN_W), lambda i: (i, 0)),
                   pl.BlockSpec((1, DN_H, DN_D, DN_D), lambda i: (i, 0, 0, 0))],
        out_shape=[SDS((t, DN_W), f32), SDS((n, DN_H, DN_D, DN_D), f32)],
        scratch_shapes=[pltpu.VMEM((DN_H, DN_D, DN_D), f32)],
        compiler_params=_cp(("arbitrary",)),
    )(qkv, gb, u, w)


def _dn_scan_bwd(qkv, gb, u, w, states, do):
    t = qkv.shape[0]
    n = t // DN_C
    rev = lambda i: (n - 1 - i, 0)
    heads = range(DN_H)

    def body(qkv_ref, gb_ref, u_ref, w_ref, s_ref, do_ref, dqk_ref, du_ref, dw_ref, dgb_ref, dstate):
        @pl.when(pl.program_id(0) == 0)
        def _():
            dstate[...] = jnp.zeros(dstate.shape, f32)

        _, vjp = jax.vjp(_f_dn_scan, [s_ref[0, h] for h in heads], [qkv_ref[:, _head_cols(0, h)] for h in heads],
                         [qkv_ref[:, _head_cols(1, h)] for h in heads], [u_ref[:, _head_cols(0, h)] for h in heads],
                         [w_ref[:, _head_cols(0, h)] for h in heads], gb_ref[...])
        ds, dq, dk, du, dw, dgbv = vjp(([do_ref[:, _head_cols(0, h)] for h in heads], [dstate[h] for h in heads]))
        for h in heads:
            dstate[h] = ds[h]
            dqk_ref[:, _head_cols(0, h)] = dq[h]
            dqk_ref[:, _head_cols(1, h)] = dk[h]
            du_ref[:, _head_cols(0, h)] = du[h]
            dw_ref[:, _head_cols(0, h)] = dw[h]
        dgb_ref[...] = dgbv

    return pl.pallas_call(
        body, grid=(n,), name="dn_scan_bwd",
        in_specs=[pl.BlockSpec((DN_C, 2 * DN_W), rev), pl.BlockSpec((DN_C, 128), rev),
                  pl.BlockSpec((DN_C, DN_W), rev), pl.BlockSpec((DN_C, DN_W), rev),
                  pl.BlockSpec((1, DN_H, DN_D, DN_D), lambda i: (n - 1 - i, 0, 0, 0)),
                  pl.BlockSpec((DN_C, DN_W), rev)],
        out_specs=[pl.BlockSpec((DN_C, 2 * DN_W), rev), pl.BlockSpec((DN_C, DN_W), rev),
                   pl.BlockSpec((DN_C, DN_W), rev), pl.BlockSpec((DN_C, 128), rev)],
        out_shape=[SDS((t, 2 * DN_W), f32), SDS((t, DN_W), f32), SDS((t, DN_W), f32), SDS((t, 128), f32)],
        scratch_shapes=[pltpu.VMEM((DN_H, DN_D, DN_D), f32)],
        compiler_params=_cp(("arbitrary",)),
    )(qkv, gb, u, w, states, do)


HALF = SW_W // 2


def _attn_specs(rows, cb, lag, nb):
    def one(c):
        if lag:
            return pl.BlockSpec((rows, HALF), lambda n, r: (jnp.maximum(jnp.minimum(n, nb - 1) - 1, 0), c))
        return pl.BlockSpec((rows, HALF), lambda n, r: (jnp.minimum(n, nb - 1), c))
    return [one(2 * cb), one(2 * cb + 1)]


def _sw_attn_fwd(q, k, vsrc, vcb, d):
    t = q.shape[0]
    rows = SW_B * d
    nb = t // rows
    cur = lambda cb: _attn_specs(rows, cb, False, nb)
    prev = lambda cb: _attn_specs(rows, cb, True, nb)

    def body(ql, qh, kpl, kph, kcl, kch, vpl, vph, vcl, vch, ol, oh, ll, lh):
        rs = pl.ds(pl.program_id(1), SW_B, stride=d) if d > 1 else pl.ds(0, SW_B)
        ld = lambda lo, hi: jnp.concatenate([lo[rs, :], hi[rs, :]], axis=1)
        o, lse = _f_attn(ld(ql, qh), ld(kpl, kph), ld(kcl, kch), ld(vpl, vph), ld(vcl, vch), pl.program_id(0) > 0)
        ol[rs, :] = o[:, :HALF]
        oh[rs, :] = o[:, HALF:]
        ll[rs, :] = lse[:, :HALF]
        lh[rs, :] = lse[:, HALF:]

    out = pl.BlockSpec((rows, HALF), lambda n, r: (n, 0))
    return pl.pallas_call(
        body, grid=(nb, d), name=f"sw_attn_fwd_d{d}",
        in_specs=cur(0) + prev(0) + cur(0) + prev(vcb) + cur(vcb), out_specs=[out] * 4,
        out_shape=[SDS((t, HALF), f32)] * 4, compiler_params=_cp(("arbitrary", "arbitrary")),
    )(q, q, k, k, k, k, vsrc, vsrc, vsrc, vsrc)


def _sw_attn_bwd(q, k, vsrc, vcb, do_l, do_h, dl_l, dl_h, d):
    t = q.shape[0]
    rows = SW_B * d
    nb = t // rows
    cur = lambda cb: _attn_specs(rows, cb, False, nb)
    prev = lambda cb: _attn_specs(rows, cb, True, nb)
    lag = pl.BlockSpec((rows, HALF), lambda n, r: (jnp.maximum(n - 1, 0), 0))
    here = pl.BlockSpec((rows, HALF), lambda n, r: (jnp.minimum(n, nb - 1), 0))

    def body(ql, qh, kpl, kph, kcl, kch, vpl, vph, vcl, vch, dol, doh, dll, dlh,
             dql, dqh, dkl, dkh, dvl, dvh, dk_hold, dv_hold):
        n = pl.program_id(0)
        r = pl.program_id(1)
        rs = pl.ds(r, SW_B, stride=d) if d > 1 else pl.ds(0, SW_B)
        hs = pl.ds(pl.multiple_of(r * SW_B, SW_B), SW_B)
        ld = lambda lo, hi: jnp.concatenate([lo[rs, :], hi[rs, :]], axis=1)

        def put(lo, hi, val):
            lo[rs, :] = val[:, :HALF]
            hi[rs, :] = val[:, HALF:]

        @pl.when(n < nb)
        def _():
            has_prev = n > 0
            _, vjp = jax.vjp(lambda q_, kp, kc, vp, vc: _f_attn(q_, kp, kc, vp, vc, has_prev),
                             ld(ql, qh), ld(kpl, kph), ld(kcl, kch), ld(vpl, vph), ld(vcl, vch))
            dq, dkp, dkc, dvp, dvc = vjp((ld(dol, doh), ld(dll, dlh)))
            put(dql, dqh, dq)
            put(dkl, dkh, dk_hold[hs, :] + dkp)
            put(dvl, dvh, dv_hold[hs, :] + dvp)
            dk_hold[hs, :] = dkc
            dv_hold[hs, :] = dvc

        @pl.when(n == nb)
        def _():
            put(dkl, dkh, dk_hold[hs, :])
            put(dvl, dvh, dv_hold[hs, :])

    return pl.pallas_call(
        body, grid=(nb + 1, d), name=f"sw_attn_bwd_d{d}",
        in_specs=cur(0) + prev(0) + cur(0) + prev(vcb) + cur(vcb) + [here] * 4,
        out_specs=[here, here, lag, lag, lag, lag],
        out_shape=[SDS((t, HALF), f32)] * 6,
        scratch_shapes=[pltpu.VMEM((rows, SW_W), f32)] * 2,
        compiler_params=_cp(("arbitrary", "arbitrary")),
    )(q, q, k, k, k, k, vsrc, vsrc, vsrc, vsrc, do_l, do_h, dl_l, dl_h)


def _loss_head(y, target, tt):
    t = y.shape[0]

    def body(y_ref, t_ref, dy_ref, l_ref):
        @pl.when(pl.program_id(0) == 0)
        def _():
            l_ref[...] = jnp.zeros(l_ref.shape, f32)

        err = y_ref[...] - t_ref[...]
        dy_ref[...] = err * (1.0 / D)
        l_ref[...] += 0.5 * jnp.sum(jnp.sum(err * err, axis=1, keepdims=True) * (1.0 / D), axis=0, keepdims=True)

    return pl.pallas_call(
        body, grid=(t // tt,), name="loss_head",
        in_specs=[pl.BlockSpec((tt, D), lambda i: (i, 0))] * 2,
        out_specs=[pl.BlockSpec((tt, D), lambda i: (i, 0)), pl.BlockSpec((8, 128), lambda i: (0, 0))],
        out_shape=[SDS((t, D), f32), SDS((8, 128), f32)],
        compiler_params=_cp(("arbitrary",)),
    )(y, target)


def _adamw(name, w, m, v, gparts, tr):
    r, c = w.shape
    p = gparts.shape[0]
    assert r % tr == 0, (name, w.shape, tr)

    def body(w_ref, m_ref, v_ref, g_ref, go_ref, d_ref, mo_ref, vo_ref):
        g = g_ref[0].astype(f32)
        for k in range(1, p):
            g = g + g_ref[k].astype(f32)
        wv = w_ref[...]
        mn = ADAM_B1 * m_ref[...] + (1.0 - ADAM_B1) * g
        vn = ADAM_B2 * v_ref[...] + (1.0 - ADAM_B2) * jnp.square(g)
        m_hat = mn / (1.0 - ADAM_B1 ** ADAM_STEP)
        v_hat = vn / (1.0 - ADAM_B2 ** ADAM_STEP)
        go_ref[...] = g
        d_ref[...] = -ADAM_LR * (m_hat / (jnp.sqrt(v_hat) + ADAM_EPS) + ADAM_WD * wv)
        mo_ref[...] = mn
        vo_ref[...] = vn

    spec = pl.BlockSpec((tr, c), lambda i: (i, 0))
    return pl.pallas_call(
        body, grid=(r // tr,), name=name,
        in_specs=[spec, spec, spec, pl.BlockSpec((p, tr, c), lambda i: (0, i, 0))],
        out_specs=[spec] * 4, out_shape=[SDS((r, c), f32)] * 4,
        compiler_params=_cp(("arbitrary",)),
    )(w, m, v, gparts)


def _mod_cols(c_all, w_mod, b_cols):
    nl, _, wc = w_mod.shape

    def body(c_ref, w_ref, b_ref, o_ref):
        o_ref[0] = _raw_dot(jax.nn.silu(c_ref[...]), w_ref[0], "nn", True) + b_ref[0]

    return pl.pallas_call(
        body, grid=(nl,), name="mod_cols",
        in_specs=[pl.BlockSpec((NDEV, D), lambda l: (0, 0)), pl.BlockSpec((1, D, wc), lambda l: (l, 0, 0)),
                  pl.BlockSpec((1, 1, wc), lambda l: (l, 0, 0))],
        out_specs=pl.BlockSpec((1, NDEV, wc), lambda l: (l, 0, 0)),
        out_shape=SDS((nl, NDEV, wc), f32), compiler_params=_cp(("arbitrary",)),
    )(c_all, w_mod, b_cols)


def _wmod_grad(c_all, dmod_cols):
    nl, _, wc = dmod_cols.shape

    def body(c_ref, d_ref, o_ref):
        o_ref[0, 0] = _raw_dot(jax.nn.silu(c_ref[...]), d_ref[0], "tn", True)

    return pl.pallas_call(
        body, grid=(nl,), name="wmod_grad",
        in_specs=[pl.BlockSpec((NDEV, D), lambda l: (0, 0)), pl.BlockSpec((1, NDEV, wc), lambda l: (l, 0, 0))],
        out_specs=pl.BlockSpec((1, 1, D, wc), lambda l: (0, l, 0, 0)),
        out_shape=SDS((1, nl, D, wc), f32), compiler_params=_cp(("arbitrary",)),
    )(c_all, dmod_cols)


def _me_and_peers():
    x, y, c = (lax.axis_index(a) for a in AXES)
    peers = []
    for k in range(1, NDEV):
        px = 1 - x if (k >> 2) & 1 else x
        py = 1 - y if (k >> 1) & 1 else y
        pc = 1 - c if k & 1 else c
        peers.append(((px, py, pc), 4 * px + 2 * py + pc))
    return 4 * x + 2 * y + c, peers


_ANY = pl.BlockSpec(memory_space=pl.ANY)


def _all_gather(name, arrs):
    n = len(arrs)

    def body(*refs):
        ins, outs, token = refs[:n], refs[n:2 * n], refs[2 * n]
        send_sems, recv_sems, local_sems = refs[2 * n + 1:]
        me, peers = _me_and_peers()
        mine = [pltpu.make_async_copy(ins[a], outs[a].at[me], local_sems.at[a]) for a in range(n)]
        copies = [pltpu.make_async_remote_copy(ins[a], outs[a].at[me], send_sems.at[a * (NDEV - 1) + k],
                                               recv_sems.at[a * (NDEV - 1) + k], device_id=dev,
                                               device_id_type=pl.DeviceIdType.MESH)
                  for a in range(n) for k, (dev, _) in enumerate(peers)]
        for cp in mine + copies:
            cp.start()
        token[...] = jnp.zeros(token.shape, token.dtype)
        for cp in copies + mine:
            cp.wait()

    res = pl.pallas_call(
        body, name=name, in_specs=[_ANY] * n, out_specs=[_ANY] * n + [pl.BlockSpec(memory_space=pltpu.VMEM)],
        out_shape=[SDS((NDEV,) + a.shape, a.dtype) for a in arrs] + [SDS((8, 128), f32)],
        scratch_shapes=[pltpu.SemaphoreType.DMA((n * (NDEV - 1),)), pltpu.SemaphoreType.DMA((n * (NDEV - 1),)),
                        pltpu.SemaphoreType.DMA((n,))],
        compiler_params=pltpu.CompilerParams(has_side_effects=True),
    )(*arrs)
    return list(res[:n]), res[n]


def _all_to_all(name, a):
    def body(a_ref, o_ref, send_sems, recv_sems, local_sem):
        me, peers = _me_and_peers()
        mine = pltpu.make_async_copy(a_ref.at[me], o_ref.at[me], local_sem)
        mine.start()
        copies = [pltpu.make_async_remote_copy(a_ref.at[pid], o_ref.at[me], send_sems.at[k], recv_sems.at[k],
                                               device_id=dev, device_id_type=pl.DeviceIdType.MESH)
                  for k, (dev, pid) in enumerate(peers)]
        for cp in copies:
            cp.start()
        for cp in copies:
            cp.wait()
        mine.wait()

    return pl.pallas_call(
        body, name=name, in_specs=[_ANY], out_specs=_ANY, out_shape=SDS(a.shape, a.dtype),
        scratch_shapes=[pltpu.SemaphoreType.DMA((NDEV - 1,)), pltpu.SemaphoreType.DMA((NDEV - 1,)), pltpu.SemaphoreType.DMA],
        compiler_params=pltpu.CompilerParams(has_side_effects=True),
    )(a)


_HBM = pl.BlockSpec(memory_space=pltpu.HBM)
_SEM = pl.BlockSpec(memory_space=pltpu.SEMAPHORE)
_FLOW = pltpu.SideEffectType.DATAFLOW_SIDE_EFFECTING


def _exchange_copies(srcs, lands, send_sems, recv_sems, scatter):
    me, peers = _me_and_peers()
    copies = []
    for a, (src, land) in enumerate(zip(srcs, lands)):
        for k, (dev, pid) in enumerate(peers):
            copies.append(pltpu.make_async_remote_copy(
                src.at[pid] if scatter else src, land.at[me], send_sems.at[a * (NDEV - 1) + k],
                recv_sems.at[a * (NDEV - 1) + k], device_id=dev, device_id_type=pl.DeviceIdType.MESH))
    return copies


def _exchange_start(name, srcs, lands, scatter):
    n = len(srcs)
    nsem = n * (NDEV - 1)

    def body(*refs):
        for cp in _exchange_copies(refs[:n], refs[n:2 * n], refs[2 * n], refs[2 * n + 1], scatter):
            cp.start()
        token = refs[-1]
        token[...] = jnp.zeros(token.shape, token.dtype)

    hbm = lambda a: pltpu.with_memory_space_constraint(a, pltpu.HBM)
    res = pl.pallas_call(
        body, name=name,
        out_shape=(pltpu.SemaphoreType.DMA((nsem,)), pltpu.SemaphoreType.DMA((nsem,)),
                   *[pltpu.HBM(a.shape, a.dtype) for a in srcs], *[pltpu.HBM(a.shape, a.dtype) for a in lands],
                   SDS((8, 128), f32)),
        in_specs=[_HBM] * (2 * n), out_specs=(_SEM, _SEM, *([_HBM] * (2 * n)), pl.BlockSpec(memory_space=pltpu.VMEM)),
        input_output_aliases={i: 2 + i for i in range(2 * n)},
        compiler_params=pltpu.CompilerParams(has_side_effects=_FLOW),
    )(*[hbm(a) for a in srcs], *[hbm(a) for a in lands])
    return res[0], res[1], list(res[2:2 + n]), list(res[2 + n:2 + 2 * n]), res[-1]


def _exchange_wait(name, handle, after, scatter):
    send_sems, recv_sems, srcs, lands, _ = handle
    n = len(srcs)

    def body(*refs):
        for cp in _exchange_copies(refs[:n], refs[n:2 * n], refs[2 * n], refs[2 * n + 1], scatter):
            cp.wait_send()
            cp.wait_recv()
        token = refs[-1]
        token[...] = jnp.zeros(token.shape, token.dtype)

    res = pl.pallas_call(
        body, name=name,
        out_shape=(*[pltpu.HBM(a.shape, a.dtype) for a in srcs + lands], SDS((8, 128), f32)),
        in_specs=[_HBM] * (2 * n) + [_SEM, _SEM, _ANY],
        out_specs=(*([_HBM] * (2 * n)), pl.BlockSpec(memory_space=pltpu.VMEM)),
        input_output_aliases={i: i for i in range(2 * n)},
        compiler_params=pltpu.CompilerParams(has_side_effects=_FLOW),
    )(*srcs, *lands, send_sems, recv_sems, after)
    return list(res[n:2 * n]), res[-1]


def _own_slot(a, me):
    return lax.dynamic_update_slice(jnp.zeros((NDEV,) + a.shape, a.dtype), a[None], (me,) + (0,) * a.ndim)


CB_Z = 3
CB_GM = 8
CB_SW = 10
CB_AB = 38


def _sw_pre_tiles(proj, cs, sn):
    return [(proj, SW_W, CB_SW + 3 * g + j) for g in range(3) for j in range(2)] + [(cs, SW_W, 0), (sn, SW_W, 0)]


def _layer_fwd(x, p, tabs, late_weights):
    cs, sn = tabs
    sh1, sc1, g1, sh2, sc2, g2 = (p["mod"][k] for k in range(6))
    (h1,) = _ew_fwd("modnorm1_fwd", _f_modnorm, [(x, D, 0)], [p["mixw"], sc1, sh1], [(D, bf16)], 512)
    proj = _mm("in_proj", h1, p["wi"], "nn", 1024, 1664, b_outer=True)
    y = _conv_fwd(proj, p["conv"], 256)
    qkv, gb = _ew_fwd("dn_act_fwd", _f_dn_act, [(y, CONV_W, 0), (proj, 128, CB_AB)], [p["alog"], p["dtb"]],
                      [(CONV_W, f32), (128, f32)], 256)
    u, w, inv = _dn_prep_fwd(qkv, gb)
    o, states = _dn_scan_fwd(qkv, gb, u, w)
    (ya,) = _ew_fwd("dn_out_fwd", _f_dn_out, [(o, DN_W, 0), (proj, DN_W, CB_Z)], [p["wn"]], [(DN_W, bf16)], 256)
    (yb,) = _ew_fwd("gm_fwd", _f_gm, [(proj, GM_W, CB_GM), (proj, GM_W, CB_GM + 1)],
                    [p["lng"], p["lnb"], p["ws"], p["bst"]], [(GM_W, bf16)], GM_C)
    qk = _ew_fwd("sw_pre_fwd", _f_sw_pre, _sw_pre_tiles(proj, cs, sn), [p["wq"], p["wk"]], [(SW_W, f32)] * 6, 512)
    ol = [[], []]
    for g, d in enumerate(SW_DIL):
        o_l, o_h, l_l, l_h = _sw_attn_fwd(qk[2 * g], qk[2 * g + 1], proj, CB_SW + 3 * g + 2, d)
        ol[0] += [o_l, l_l]
        ol[1] += [o_h, l_h]
    yc = [_ew_fwd(f"sw_merge_fwd_{h}", _f_sw_merge, [(a, HALF, 0) for a in ol[h]], [], [(HALF, bf16)], 512)[0]
          for h in range(2)]
    ycat = jnp.concatenate([ya, yb] + yc, axis=1)
    p.update(late_weights(ycat))
    m1 = _mm("out_proj", ycat, p["wo"], "nn", 1024, 1024)
    (x2,) = _ew_fwd("resid1_fwd", _f_resid, [(x, D, 0), (m1, D, 0)], [g1], [(D, f32)], 512)
    (h2,) = _ew_fwd("modnorm2_fwd", _f_modnorm, [(x2, D, 0)], [p["ffnw"], sc2, sh2], [(D, bf16)], 512)
    gu = _mm("ffn_in", h2, p["wfi"], "nn", 1024, 1408, bf16, b_outer=True)
    (act,) = _ew_fwd("swiglu_fwd", _f_swiglu, [(gu, 2 * FFN, 0)], [], [(FFN, bf16)], 256)
    m2 = _mm("ffn_out", act, p["wfo"], "nn", 1024, 1024)
    (x3,) = _ew_fwd("resid2_fwd", _f_resid, [(x2, D, 0), (m2, D, 0)], [g2], [(D, f32)], 512)
    res = dict(x=x, h1=h1, proj=proj, y=y, qkv=qkv, gb=gb, u=u, w=w, inv=inv, states=states, o=o, qk=list(qk), ol=ol,
               ycat=ycat, m1=m1, x2=x2, h2=h2, gu=gu, act=act, m2=m2)
    return x3, res


def _layer_bwd(dx3, p, r, tabs, ffn_grads_ready):
    cs, sn = tabs
    sh1, sc1, g1, sh2, sc2, g2 = (p["mod"][k] for k in range(6))
    proj = r["proj"]
    (dx2a, dm2), (dg2,) = _ew_bwd("resid2_bwd", _f_resid, [(r["x2"], D, 0), (r["m2"], D, 0)], [g2], [(dx3, D, 0)],
                                  [(0, f32), (1, bf16)], 512)
    dact = _mm("ffn_out_dx", dm2, p["wfo"], "nt", 512, FFN, bf16)
    dwfo = _mm("ffn_out_dw", r["act"], dm2, "tn", 256, 1024, bf16)
    (dgu_cat,), _ = _ew_bwd("swiglu_bwd", _f_swiglu, [(r["gu"], 2 * FFN, 0)], [], [(dact, FFN, 0)], [(0, bf16)], 256)
    dh2 = _mm("ffn_in_dx", dgu_cat, p["wfi"], "nt", 512, 1024)
    dwfi = _mm("ffn_in_dw", r["h2"], dgu_cat, "tn", 1024, 512, bf16)
    g1 = g1 + ffn_grads_ready(dwfi, dwfo)[0, 0]
    (dx2,), (dffnw, dsc2, dsh2) = _ew_bwd("modnorm2_bwd", _f_modnorm, [(r["x2"], D, 0)], [p["ffnw"], sc2, sh2],
                                          [(dh2, D, 0)], [(0, f32)], 512, adds=[dx2a])
    (dxa, dm1), (dg1,) = _ew_bwd("resid1_bwd", _f_resid, [(r["x"], D, 0), (r["m1"], D, 0)], [g1], [(dx2, D, 0)],
                                 [(0, f32), (1, bf16)], 512)
    dycat = _mm("out_proj_dx", dm1, p["wo"], "nt", 1024, 1024)
    dwo = _mm("out_proj_dw", r["ycat"], dm1, "tn", 1024, 512, bf16)
    dol = [_ew_bwd(f"sw_merge_bwd_{h}", _f_sw_merge, [(a, HALF, 0) for a in r["ol"][h]], [], [(dycat, HALF, 6 + h)],
                   [(k, f32) for k in range(6)], 512)[0] for h in range(2)]
    dqk, dvs = [], []
    for g, d in enumerate(SW_DIL):
        dq_l, dq_h, dk_l, dk_h, dv_l, dv_h = _sw_attn_bwd(
            r["qk"][2 * g], r["qk"][2 * g + 1], proj, CB_SW + 3 * g + 2,
            dol[0][2 * g], dol[1][2 * g], dol[0][2 * g + 1], dol[1][2 * g + 1], d)
        dqk += [[(dq_l, HALF, 0), (dq_h, HALF, 0)], [(dk_l, HALF, 0), (dk_h, HALF, 0)]]
        dvs += [dv_l.astype(bf16), dv_h.astype(bf16)]
    dqk_raw, (dwq, dwk) = _ew_bwd("sw_pre_bwd", _f_sw_pre, _sw_pre_tiles(proj, cs, sn), [p["wq"], p["wk"]],
                                  dqk, [(k, bf16) for k in range(6)], 512)
    (dgm_u, dgm_v), (dlng, dlnb, dws, dbst) = _ew_bwd(
        "gm_bwd", _f_gm, [(proj, GM_W, CB_GM), (proj, GM_W, CB_GM + 1)], [p["lng"], p["lnb"], p["ws"], p["bst"]],
        [(dycat, GM_W, 2)], [(0, bf16), (1, bf16)], GM_C)
    (do, dz), (dwn,) = _ew_bwd("dn_out_bwd", _f_dn_out, [(r["o"], DN_W, 0), (proj, DN_W, CB_Z)], [p["wn"]],
                               [(dycat, DN_W, 0)], [(0, f32), (1, bf16)], 256)
    dqk1, du, dw, dgb1 = _dn_scan_bwd(r["qkv"], r["gb"], r["u"], r["w"], r["states"], do)
    dqkv, dgb = _dn_prep_bwd(r["qkv"], r["gb"], r["inv"], du, dw, dqk1, dgb1)
    (dy, dab), (dalog, ddtb) = _ew_bwd("dn_act_bwd", _f_dn_act, [(r["y"], CONV_W, 0), (proj, 128, CB_AB)],
                                       [p["alog"], p["dtb"]], [(dqkv, CONV_W, 0), (dgb, 128, 0)],
                                       [(0, f32), (1, bf16)], 256)
    dxc, dconv = _conv_bwd(proj, dy, p["conv"], 256)
    dproj = jnp.concatenate([dxc, dz, dgm_u, dgm_v, dqk_raw[0], dqk_raw[1], dvs[0], dvs[1], dqk_raw[2], dqk_raw[3],
                             dvs[2], dvs[3], dqk_raw[4], dqk_raw[5], dvs[4], dvs[5], dab], axis=1)
    dh1 = _mm("in_proj_dx", dproj, p["wi"], "nt", 512, 1024)
    dwi = _mm("in_proj_dw", r["h1"], dproj, "tn", 1024, 384, bf16)
    (dx,), (dmixw, dsc1, dsh1) = _ew_bwd("modnorm1_bwd", _f_modnorm, [(r["x"], D, 0)], [p["mixw"], sc1, sh1],
                                         [(dh1, D, 0)], [(0, f32)], 512, adds=[dxa])
    grads = dict(wi=dwi, wo=dwo, wfi=dwfi, wfo=dwfo, conv=dconv, mixw=dmixw, ffnw=dffnw,
                 mod=jnp.stack([dsh1, dsc1, dg1, dsh2, dsc2, dg2]), alog=dalog, dtb=ddtb, wn=dwn, lng=dlng, lnb=dlnb,
                 ws=dws, bst=dbst, wq=dwq, wk=dwk)
    return dx, grads


def _rope_tables(t):
    inv = ROPE_THETA ** (-jnp.arange(0, ROPE_DIM, 2, dtype=f32) / ROPE_DIM)
    ang = jnp.arange(t, dtype=f32)[:, None] * inv[None, :]
    cos, sin = jnp.cos(ang), jnp.sin(ang)
    rest = SW_D - ROPE_DIM
    ch = jnp.concatenate([cos, cos, jnp.ones((t, rest), f32)], axis=1)
    sh = jnp.concatenate([sin, sin, jnp.zeros((t, rest), f32)], axis=1)
    return jnp.tile(ch, (1, SW_H)), jnp.tile(sh, (1, SW_H))


def _pad_last(a, n):
    return jnp.pad(a, [(0, 0)] * (a.ndim - 1) + [(0, n - a.shape[-1])])


def _scatter_cols(name, gs, cs, pad_to):
    r = gs[0].shape[0]
    parts = jnp.stack([jnp.transpose(_pad_last(g.reshape(r, NDEV, cs), pad_to), (1, 0, 2)) for g in gs], axis=1)
    return _all_to_all(name, parts).reshape(NDEV, len(gs) * r, pad_to)


_SMALL = ("b_mod", "mix_norm_w", "ffn_norm_w", "dn_a_log", "dn_dt_bias", "dn_out_norm_w", "gm_ln_g", "gm_ln_b",
          "gm_w_s", "gm_b_s", "sw_q_norm_w", "sw_k_norm_w")
PACK_ROWS = 800


def _pack_rows(a):
    return -(-a.size // 1024) * 8


def _pack(arrs):
    parts = [jnp.pad(a.reshape(-1), (0, _pack_rows(a) * 128 - a.size)).reshape(_pack_rows(a), 128) for a in arrs]
    rows = sum(p.shape[0] for p in parts)
    parts.append(jnp.zeros((-(-rows // PACK_ROWS) * PACK_ROWS - rows, 128), parts[0].dtype))
    return jnp.concatenate(parts, axis=0)


def _unpack(buf, like):
    out, off = [], 0
    for a in like:
        rows = _pack_rows(a)
        flat = buf[off:off + rows].reshape(-1)
        out.append((flat if flat.size == a.size else flat[:a.size]).reshape(a.shape))
        off += rows
    return out


def kernel(x, c, w_mod, b_mod, mix_norm_w, ffn_norm_w, w_in, w_out, dn_conv_w, dn_a_log, dn_dt_bias, dn_out_norm_w, gm_ln_g, gm_ln_b, gm_w_s, gm_b_s, sw_q_norm_w, sw_k_norm_w, w_ffn_in, w_ffn_out, loss_target, m_w_mod, m_b_mod, m_mix_norm_w, m_ffn_norm_w, m_w_in, m_w_out, m_dn_conv_w, m_dn_a_log, m_dn_dt_bias, m_dn_out_norm_w, m_gm_ln_g, m_gm_ln_b, m_gm_w_s, m_gm_b_s, m_sw_q_norm_w, m_sw_k_norm_w, m_w_ffn_in, m_w_ffn_out, v_w_mod, v_b_mod, v_mix_norm_w, v_ffn_norm_w, v_w_in, v_w_out, v_dn_conv_w, v_dn_a_log, v_dn_dt_bias, v_dn_out_norm_w, v_gm_ln_g, v_gm_ln_b, v_gm_w_s, v_gm_b_s, v_sw_q_norm_w, v_sw_k_norm_w, v_w_ffn_in, v_w_ffn_out):
    weights = dict(w_mod=w_mod, b_mod=b_mod, mix_norm_w=mix_norm_w, ffn_norm_w=ffn_norm_w, w_in=w_in, w_out=w_out,
                   dn_conv_w=dn_conv_w, dn_a_log=dn_a_log, dn_dt_bias=dn_dt_bias, dn_out_norm_w=dn_out_norm_w,
                   gm_ln_g=gm_ln_g, gm_ln_b=gm_ln_b, gm_w_s=gm_w_s, gm_b_s=gm_b_s, sw_q_norm_w=sw_q_norm_w,
                   sw_k_norm_w=sw_k_norm_w, w_ffn_in=w_ffn_in, w_ffn_out=w_ffn_out)
    mom = dict(w_mod=m_w_mod, b_mod=m_b_mod, mix_norm_w=m_mix_norm_w, ffn_norm_w=m_ffn_norm_w, w_in=m_w_in,
               w_out=m_w_out, dn_conv_w=m_dn_conv_w, dn_a_log=m_dn_a_log, dn_dt_bias=m_dn_dt_bias,
               dn_out_norm_w=m_dn_out_norm_w, gm_ln_g=m_gm_ln_g, gm_ln_b=m_gm_ln_b, gm_w_s=m_gm_w_s, gm_b_s=m_gm_b_s,
               sw_q_norm_w=m_sw_q_norm_w, sw_k_norm_w=m_sw_k_norm_w, w_ffn_in=m_w_ffn_in, w_ffn_out=m_w_ffn_out)
    var = dict(w_mod=v_w_mod, b_mod=v_b_mod, mix_norm_w=v_mix_norm_w, ffn_norm_w=v_ffn_norm_w, w_in=v_w_in,
               w_out=v_w_out, dn_conv_w=v_dn_conv_w, dn_a_log=v_dn_a_log, dn_dt_bias=v_dn_dt_bias,
               dn_out_norm_w=v_dn_out_norm_w, gm_ln_g=v_gm_ln_g, gm_ln_b=v_gm_ln_b, gm_w_s=v_gm_w_s, gm_b_s=v_gm_b_s,
               sw_q_norm_w=v_sw_q_norm_w, sw_k_norm_w=v_sw_k_norm_w, w_ffn_in=v_w_ffn_in, w_ffn_out=v_w_ffn_out)
    names = list(weights)
    xi, tgt = x[0], loss_target[0]
    t = xi.shape[0]
    nl = w_mod.shape[0]
    ax, ay, ac = (lax.axis_index(a) for a in AXES)
    me = 4 * ax + 2 * ay + ac
    mod_cs = w_mod.shape[2]

    shards_a = [_pad_last(w_in.astype(bf16), 640)]
    shards_b = [w_out.astype(bf16), _pad_last(w_ffn_in.astype(bf16), 768), w_ffn_out.astype(bf16)]

    def start_gather(l, which, tok):
        shards_l = [a[l] + tok.astype(bf16) for a in (shards_a if which == "a" else shards_b)]
        return _exchange_start(f"ag_start_{which}{l}", shards_l, [_own_slot(a, me) for a in shards_l], False)

    gather_a = start_gather(0, "a", jnp.zeros((), f32))

    c = c + gather_a[4][0, 0]
    (c_g, conv_g), _ = _all_gather("ag_c_conv", [jnp.broadcast_to(c, (NDEV, D)), _pad_last(dn_conv_w, 256)])
    c_all = c_g[:, 0, :]
    conv = jnp.transpose(conv_g, (1, 2, 0, 3))[..., :dn_conv_w.shape[2]].reshape(nl, CONV_K, CONV_W)
    b_cols = lax.dynamic_slice_in_dim(b_mod, me * mod_cs, mod_cs, axis=1)[:, None, :]
    modc = _mod_cols(c_all, w_mod, b_cols)
    mod_tx = jnp.pad(jnp.transpose(modc, (1, 0, 2)), ((0, 0), (0, 8 - nl), (0, 0)))
    mod_rx = _all_to_all("a2a_mod", mod_tx)[:, :nl]
    mod = jnp.transpose(mod_rx, (1, 0, 2)).reshape(nl, 6, 1, D)

    def w_in_full(landed):
        wi = jnp.transpose(landed[0], (1, 0, 2))[..., :w_in.shape[2]].reshape(D, IN_W)
        return jnp.concatenate([wi[:, :2048], wi[:, 2056:], wi[:, 2048:2056], jnp.zeros((D, IN_WA - IN_W), bf16)], axis=-1)

    def late_full(landed):
        go, gfi, gfo = landed
        wfi = jnp.transpose(gfi, (1, 0, 2))[..., :w_ffn_in.shape[2]].reshape(D, 2 * FFN)
        return dict(wo=go.reshape(D, D), wfi=wfi, wfo=gfo.reshape(FFN, D))

    pad128 = lambda a: _pad_last(a, 128)[:, None, :]
    params = dict(
        mod=mod, conv=conv,
        mixw=mix_norm_w[:, None, :], ffnw=ffn_norm_w[:, None, :], alog=pad128(dn_a_log), dtb=pad128(dn_dt_bias),
        wn=dn_out_norm_w[:, None, :], lng=gm_ln_g[:, None, :], lnb=gm_ln_b[:, None, :], ws=gm_w_s,
        bst=_pad_last(jnp.transpose(gm_b_s, (0, 2, 1)), 128),
        wq=jnp.pad(sw_q_norm_w[:, None, :], ((0, 0), (0, 7), (0, 128 - SW_D))),
        wk=jnp.pad(sw_k_norm_w[:, None, :], ((0, 0), (0, 7), (0, 128 - SW_D))))
    tabs = _rope_tables(t)

    layer_p, res = [], []
    xc, after, gather_b = xi, mod, None
    for l in range(nl):
        landed_a, tok = _exchange_wait(f"ag_wait_a{l}", gather_a, after, False)
        p = {k: v[l] for k, v in params.items()}
        p["wi"] = w_in_full(landed_a)
        hb = gather_b if l else start_gather(0, "b", tok[0, 0])
        tok = tok if l else hb[4]
        if l + 1 < nl:
            gather_a = start_gather(l + 1, "a", tok[0, 0])
            gather_b = start_gather(l + 1, "b", gather_a[4][0, 0])
            tok = gather_b[4]
        p["mod"] = p["mod"] + tok[0, 0]
        late = lambda ycat, hb=hb, l=l: late_full(_exchange_wait(f"ag_wait_b{l}", hb, ycat, False)[0])
        layer_p.append(p)
        xc, r = _layer_fwd(xc, p, tabs, late)
        res.append(r)
        after = xc
    dy, lpart = _loss_head(xc, tgt, 512)
    loss = lax.psum(lpart[0, 0], AXES)

    slot = lax.broadcasted_iota(jnp.int32, (NDEV, 1, 1), 0)

    def start_scatter(name, parts):
        return _exchange_start(name, parts, [jnp.where(slot == me, a, jnp.zeros_like(a)) for a in parts], True)

    dxi, gl, h_ffn, h_mix = dy, [None] * nl, [None] * nl, [None] * nl
    for l in reversed(range(nl)):
        def ffn_ready(dwfi, dwfo, l=l):
            h_ffn[l] = start_scatter(f"a2a_start_f{l}", [
                jnp.transpose(_pad_last(dwfi.reshape(D, NDEV, w_ffn_in.shape[2]), 768), (1, 0, 2)),
                dwfo.reshape(NDEV, w_ffn_out.shape[1], D)])
            return h_ffn[l][4]

        dxi, gl[l] = _layer_bwd(dxi, layer_p[l], res[l], tabs, ffn_ready)
        d = gl[l]["wi"]
        d = jnp.concatenate([d[:, :2048], d[:, 4864:4872], d[:, 2048:4864]], axis=-1)
        mix_parts = [jnp.transpose(_pad_last(d.reshape(D, NDEV, w_in.shape[2]), 640), (1, 0, 2)),
                     gl[l]["wo"].reshape(NDEV, w_out.shape[1], D)]
        if l > 0:
            h_mix[l] = start_scatter(f"a2a_start_m{l}", mix_parts)
            layer_p[l - 1]["mod"] = layer_p[l - 1]["mod"] + h_mix[l][4][0, 0]
    g = {k: jnp.stack([gl[l][k] for l in range(nl)]) for k in gl[0] if k not in ("wi", "wo", "wfi", "wfo")}

    dmod = g["mod"].reshape(nl, 6 * D)
    small_g = dict(b_mod=dmod, mix_norm_w=g["mixw"][:, 0], ffn_norm_w=g["ffnw"][:, 0], dn_a_log=g["alog"][:, 0, :DN_H],
                   dn_dt_bias=g["dtb"][:, 0, :DN_H], dn_out_norm_w=g["wn"][:, 0], gm_ln_g=g["lng"][:, 0],
                   gm_ln_b=g["lnb"][:, 0], gm_w_s=g["ws"], gm_b_s=jnp.transpose(g["bst"][:, :, :GM_G], (0, 2, 1)),
                   sw_q_norm_w=g["wq"][:, 0, :SW_D], sw_k_norm_w=g["wk"][:, 0, :SW_D])
    (parts,), tok = _all_gather("ag_small_grads", [_pack([small_g[n] for n in _SMALL])])
    h_mix[0] = start_scatter("a2a_start_m0", [a + tok[0, 0].astype(bf16) for a in mix_parts])
    like = [weights[n] for n in _SMALL]
    sm = _adamw("adamw_small", _pack(like) + h_mix[0][4][0, 0], _pack([mom[n] for n in _SMALL]),
                _pack([var[n] for n in _SMALL]), parts, PACK_ROWS)
    out = {n: vals for n, vals in zip(_SMALL, zip(*[_unpack(b, like) for b in sm]))}

    dmod_all = parts[:, :nl * 6 * D // 128, :].reshape(NDEV, nl, 6 * D)
    dmod_cols = jnp.transpose(lax.dynamic_slice_in_dim(dmod_all, me * mod_cs, mod_cs, axis=2), (1, 0, 2))
    gw_mod = _wmod_grad(c_all, dmod_cols).reshape(1, nl * D, mod_cs)

    def adamw_sharded(n, gp, cols, padc):
        shp = weights[n].shape
        rows = gp.shape[1]
        prep = lambda a: _pad_last(a.reshape(rows, cols), padc)
        tr = 256 if rows % 256 == 0 else rows // 4 if rows % 32 == 0 else rows
        res4 = _adamw("adamw_" + n, prep(weights[n]), prep(mom[n]), prep(var[n]), gp, tr)
        out[n] = tuple(a[:, :cols].reshape(shp) for a in res4)

    adamw_sharded("w_mod", gw_mod, mod_cs, mod_cs)
    adamw_sharded("dn_conv_w", _scatter_cols("a2a_conv", [gl[l]["conv"] for l in range(nl)], dn_conv_w.shape[2], 256),
                  dn_conv_w.shape[2], 256)
    after, land_f, land_m = sm[0], [None] * nl, [None] * nl
    for l in reversed(range(nl)):
        land_f[l], after = _exchange_wait(f"a2a_wait_f{l}", h_ffn[l], after, True)
        if l > 0:
            land_m[l], after = _exchange_wait(f"a2a_wait_m{l}", h_mix[l], after, True)
    adamw_sharded("w_ffn_in", jnp.concatenate([land_f[l][0] for l in range(nl)], axis=1), w_ffn_in.shape[2], 768)
    adamw_sharded("w_ffn_out", jnp.concatenate([land_f[l][1] for l in range(nl)], axis=1), D, D)
    both = out["w_ffn_in"][1][0, 0, :1] + out["w_ffn_out"][1][0, 0, :1]
    land_m[0], _ = _exchange_wait("a2a_wait_m0", h_mix[0], both, True)
    adamw_sharded("w_in", jnp.concatenate([land_m[l][0] for l in range(nl)], axis=1), w_in.shape[2], 640)
    adamw_sharded("w_out", jnp.concatenate([land_m[l][1] for l in range(nl)], axis=1), D, D)

    return (loss, dxi[None], *[out[n][0] for n in names], *[out[n][1] for n in names],
            *[out[n][2] for n in names], *[out[n][3] for n in names])
```
